```python
import math, functools
import jax, jax.numpy as jnp
from jax import lax
import numpy as np

D_MODEL = 1024
BATCH = 2
SEQ = 16384
DEPTH = 2

HEAD_DIM = 64
N_META = 16
BLOCK_Q = 128
PAD = BLOCK_Q - N_META
SEG_BLOCKS = 8
SUB = 64
RWKV_HEADS = 6
SB_HEADS = 5
FOX_HEADS = 5
RWKV_W = RWKV_HEADS * HEAD_DIM
SB_W = SB_HEADS * HEAD_DIM
FOX_W = FOX_HEADS * HEAD_DIM
DECAY_RANK = 64
ICLR_RANK = 64
N_BRANCH = 3
SHIFT_W = 3 * RWKV_W + DECAY_RANK + ICLR_RANK
REST_W = 3 * SB_W + 3 * FOX_W + FOX_HEADS + N_BRANCH * D_MODEL
D_IN = SHIFT_W + REST_W
N_GROUPS = 4
EXPERTS_PER_GROUP = 4
N_EXPERTS = N_GROUPS * EXPERTS_PER_GROUP
TOP_K = 2
D_EXPERT = 256
NORM_EPS = 1e-6
RWKV_LN_EPS = 64e-5
NEG_INF = -1e30

kernel_name = "hybrid_rwkv7_stickbreak_fox_hiermoe"


def rms_norm(x, g, eps=NORM_EPS):
    xf = x.astype(jnp.float32)
    y = xf * lax.rsqrt(jnp.mean(xf * xf, axis=-1, keepdims=True) + eps)
    return (y * g.astype(jnp.float32)).astype(x.dtype)


def token_shift(u, mu):
    prev = jnp.pad(u, ((0, 0), (1, 0), (0, 0)))[:, :-1]
    return u + (prev - u) * mu


def rwkv7_time_mix(r, k, v, wd, ad, w_up, w0, a_up, a0, k_k, k_a, r_k, ln_g, ln_b):
    dt = r.dtype
    f32 = jnp.float32
    B, L, _ = r.shape
    logw = -jax.nn.softplus(-(w0 + jnp.tanh(wd) @ w_up).astype(f32)) - 0.5
    decay = jnp.exp(-jnp.exp(logw))
    a = jax.nn.sigmoid((a0 + ad @ a_up).astype(f32))
    hs = lambda u: u.astype(f32).reshape(B, L, RWKV_HEADS, HEAD_DIM)
    r, k, v, decay, a = hs(r), hs(k), hs(v), hs(decay), hs(a)
    kk = k * k_k.astype(f32)
    kk = kk / jnp.maximum(jnp.sqrt(jnp.sum(kk * kk, axis=-1, keepdims=True)), 1e-12)
    k = k * (1.0 + (a - 1.0) * k_a.astype(f32))

    def step(S, inp):
        r_t, w_t, k_t, v_t, kk_t, a_t = inp
        sa = jnp.einsum('bhvk,bhk->bhv', S, -kk_t)
        S = (S * w_t[:, :, None, :]
             + sa[..., None] * (kk_t * a_t)[:, :, None, :]
             + v_t[..., None] * k_t[:, :, None, :])
        return S, jnp.einsum('bhvk,bhk->bhv', S, r_t)

    xs = (jnp.moveaxis(r, 1, 0), jnp.moveaxis(decay, 1, 0), jnp.moveaxis(k, 1, 0),
          jnp.moveaxis(v, 1, 0), jnp.moveaxis(kk, 1, 0), jnp.moveaxis(a, 1, 0))
    S0 = jnp.zeros((B, RWKV_HEADS, HEAD_DIM, HEAD_DIM), f32)
    _, y = lax.scan(step, S0, xs)
    y = jnp.moveaxis(y, 0, 1)
    mean = jnp.mean(y, axis=-1, keepdims=True)
    var = jnp.mean(jnp.square(y - mean), axis=-1, keepdims=True)
    y = (y - mean) * lax.rsqrt(var + RWKV_LN_EPS)
    y = y.reshape(B, L, RWKV_W) * ln_g.astype(f32) + ln_b.astype(f32)
    bonus = jnp.sum(r * k * r_k.astype(f32), axis=-1, keepdims=True) * v
    return (y + bonus.reshape(B, L, RWKV_W)).astype(dt)


def to_padded_heads(u, n_heads):
    B, L, _ = u.shape
    u = u.reshape(B, L, n_heads, HEAD_DIM).transpose(0, 2, 1, 3)
    return jnp.pad(u, ((0, 0), (0, 0), (PAD, 0), (0, 0)))


def from_padded_heads(o):
    B, H, T, d = o.shape
    return o[:, :, PAD:].transpose(0, 2, 1, 3).reshape(B, T - PAD, H * d)


def causal_sweep(block_fn, n_blocks):
    outs = []
    for b0 in range(0, n_blocks, SEG_BLOCKS):
        b1 = min(b0 + SEG_BLOCKS, n_blocks)
        outs.append(lax.map(functools.partial(block_fn, key_len=b1 * BLOCK_Q), jnp.arange(b0, b1)))
    return jnp.concatenate(outs, axis=0)


def stick_breaking_attention(q, k, v):
    B, H, T, d = q.shape
    scale = 1.0 / math.sqrt(d)
    f32 = jnp.float32

    def block(i, key_len):
        nsub = key_len // SUB
        kb, vb = k[:, :, :key_len], v[:, :, :key_len]
        q0 = i * BLOCK_Q
        qb = lax.dynamic_slice_in_dim(q, q0, BLOCK_Q, axis=2)
        z = jnp.einsum('bhqd,bhkd->bhqk', qb, kb, preferred_element_type=f32) * scale
        key_pos = jnp.arange(key_len)
        qpos = q0 + jnp.arange(BLOCK_Q)
        mask = (key_pos[None, :] < qpos[:, None]) & (key_pos[None, :] >= PAD)
        log_1mb = jnp.where(mask, jax.nn.log_sigmoid(-z), 0.0)
        lr = log_1mb.reshape(B, H, BLOCK_Q, nsub, SUB)
        j_in = jnp.arange(SUB)
        upper_in = (j_in[:, None] > j_in[None, :]).astype(f32)
        tail_in = jnp.einsum('bhqnj,js->bhqns', lr, upper_in)
        m_blk = jnp.arange(nsub)
        upper_blk = (m_blk[:, None] > m_blk[None, :]).astype(f32)
        later = jnp.einsum('bhqm,mn->bhqn', jnp.sum(lr, axis=-1), upper_blk)
        tail = (tail_in + later[..., None]).reshape(B, H, BLOCK_Q, key_len)
        A = jnp.where(mask, jnp.exp(jax.nn.log_sigmoid(z) + tail), 0.0)
        return jnp.einsum('bhqk,bhkd->bhqd', A.astype(v.dtype), vb)

    out = causal_sweep(block, T // BLOCK_Q)
    return out.transpose(1, 2, 0, 3, 4).reshape(B, H, T, d)


def forgetting_attention(q, k, v, logf_cum):
    B, H, T, d = q.shape
    scale = 1.0 / math.sqrt(d)

    def block(i, key_len):
        kb, vb, fk = k[:, :, :key_len], v[:, :, :key_len], logf_cum[:, :, :key_len]
        q0 = i * BLOCK_Q
        qb = lax.dynamic_slice_in_dim(q, q0, BLOCK_Q, axis=2)
        fq = lax.dynamic_slice_in_dim(logf_cum, q0, BLOCK_Q, axis=2)
        s = jnp.einsum('bhqd,bhkd->bhqk', qb, kb, preferred_element_type=jnp.float32) * scale
        s = s + fq[..., :, None] - fk[:, :, None, :]
        key_pos = jnp.arange(key_len)
        qpos = q0 + jnp.arange(BLOCK_Q)
        mask = (key_pos[None, :] <= qpos[:, None]) & (key_pos[None, :] >= PAD)
        p = jax.nn.softmax(jnp.where(mask, s, NEG_INF), axis=-1)
        return jnp.einsum('bhqk,bhkd->bhqd', p.astype(v.dtype), vb)

    out = causal_sweep(block, T // BLOCK_Q)
    return out.transpose(1, 2, 0, 3, 4).reshape(B, H, T, d)


def hybrid_mixer(n, w_in, rwkv_mu, rwkv_w_up, rwkv_w0, rwkv_a_up, rwkv_a0, rwkv_k_k, rwkv_k_a,
                 rwkv_r_k, rwkv_ln_g, rwkv_ln_b, fox_f_b, fox_q_g, fox_k_g,
                 w_p_rwkv, w_p_sb, w_p_fox, w_out):
    B, L, D = n.shape
    u = n @ w_in
    u_shift = token_shift(u[..., :SHIFT_W], rwkv_mu)
    u_rest = u[..., SHIFT_W:]
    r, k, v, wd, ad = jnp.split(
        u_shift, [RWKV_W, 2 * RWKV_W, 3 * RWKV_W, 3 * RWKV_W + DECAY_RANK], axis=-1)
    sbq, sbk, sbv, fq, fk, fv, f_logit, gates = jnp.split(
        u_rest, [SB_W, 2 * SB_W, 3 * SB_W, 3 * SB_W + FOX_W, 3 * SB_W + 2 * FOX_W,
                 3 * SB_W + 3 * FOX_W, 3 * SB_W + 3 * FOX_W + FOX_HEADS], axis=-1)

    y_a = rwkv7_time_mix(r, k, v, wd, ad, rwkv_w_up, rwkv_w0, rwkv_a_up, rwkv_a0,
                         rwkv_k_k, rwkv_k_a, rwkv_r_k, rwkv_ln_g, rwkv_ln_b)

    y_b = from_padded_heads(stick_breaking_attention(
        to_padded_heads(sbq, SB_HEADS), to_padded_heads(sbk, SB_HEADS), to_padded_heads(sbv, SB_HEADS)))

    fq = rms_norm(fq.reshape(B, L, FOX_HEADS, HEAD_DIM), fox_q_g).reshape(B, L, FOX_W)
    fk = rms_norm(fk.reshape(B, L, FOX_HEADS, HEAD_DIM), fox_k_g).reshape(B, L, FOX_W)
    logf = jax.nn.log_sigmoid((f_logit + fox_f_b).astype(jnp.float32))
    logf = jnp.pad(logf, ((0, 0), (PAD, 0), (0, 0)))
    logf_cum = jnp.cumsum(logf, axis=1).transpose(0, 2, 1)
    y_c = from_padded_heads(forgetting_attention(
        to_padded_heads(fq, FOX_HEADS), to_padded_heads(fk, FOX_HEADS),
        to_padded_heads(fv, FOX_HEADS), logf_cum))

    g = jax.nn.sigmoid(gates).reshape(B, L, N_BRANCH, D)
    merged = (g[:, :, 0] * (y_a @ w_p_rwkv)
              + g[:, :, 1] * (y_b @ w_p_sb)
              + g[:, :, 2] * (y_c @ w_p_fox))
    return merged @ w_out


def hierarchical_moe(n, wg, bg, we, be, w_gate, w_up, w_down):
    B, L, D = n.shape
    t = n.reshape(B * L, D)
    N = t.shape[0]
    lg = (t @ wg + bg).astype(jnp.float32)
    pg = jax.nn.softmax(lg, axis=-1)
    g_idx = jnp.argmax(lg, axis=-1)
    p_group = jnp.take_along_axis(pg, g_idx[:, None], axis=-1)
    le = (t @ we + be).astype(jnp.float32).reshape(N, N_GROUPS, EXPERTS_PER_GROUP)
    le_sel = jnp.take_along_axis(le, g_idx[:, None, None], axis=1)[:, 0]
    top_v, top_i = lax.top_k(le_sel, TOP_K)
    p_top = jax.nn.softmax(top_v, axis=-1)
    expert_id = g_idx[:, None] * EXPERTS_PER_GROUP + top_i
    combine = jnp.sum(jax.nn.one_hot(expert_id, N_EXPERTS, dtype=jnp.float32)
                      * (p_group * p_top)[..., None], axis=1).astype(t.dtype)
    y = jnp.zeros_like(t)
    for e in range(N_EXPERTS):
        he = jax.nn.silu(t @ w_gate[e]) * (t @ w_up[e])
        y = y + combine[:, e:e + 1] * (he @ w_down[e])
    return y.reshape(B, L, D)


def setup_inputs(seed: int = 0) -> dict:
    key = jax.random.key(seed)
    ks = jax.random.split(key, 32)
    f32 = jnp.float32
    nrm = lambda k, shape, s: jax.random.normal(k, shape, f32) * s
    unif = lambda k, shape, lo, hi: jax.random.uniform(k, shape, f32, lo, hi)
    return {
        "x": nrm(ks[0], (BATCH, SEQ, D_MODEL), 1.0),
        "meta_tokens": nrm(ks[1], (N_META, D_MODEL), 1.0),
        "norm1_g": 1.0 + nrm(ks[2], (DEPTH, D_MODEL), 0.02),
        "w_in": nrm(ks[3], (DEPTH, D_MODEL, D_IN), D_MODEL ** -0.5),
        "rwkv_mu": unif(ks[4], (DEPTH, SHIFT_W), 0.0, 1.0),
        "rwkv_w_up": nrm(ks[5], (DEPTH, DECAY_RANK, RWKV_W), 0.1),
        "rwkv_w0": unif(ks[6], (DEPTH, RWKV_W), -6.0, 1.0),
        "rwkv_a_up": nrm(ks[7], (DEPTH, ICLR_RANK, RWKV_W), 0.1),
        "rwkv_a0": nrm(ks[8], (DEPTH, RWKV_W), 0.5),
        "rwkv_k_k": 0.85 + nrm(ks[9], (DEPTH, RWKV_HEADS, HEAD_DIM), 0.05),
        "rwkv_k_a": 1.0 + nrm(ks[10], (DEPTH, RWKV_HEADS, HEAD_DIM), 0.05),
        "rwkv_r_k": nrm(ks[11], (DEPTH, RWKV_HEADS, HEAD_DIM), 0.1),
        "rwkv_ln_g": 1.0 + nrm(ks[12], (DEPTH, RWKV_W), 0.02),
        "rwkv_ln_b": nrm(ks[13], (DEPTH, RWKV_W), 0.02),
        "fox_f_b": unif(ks[14], (DEPTH, FOX_HEADS), 2.0, 6.0),
        "fox_q_g": 1.0 + nrm(ks[15], (DEPTH, HEAD_DIM), 0.02),
        "fox_k_g": 1.0 + nrm(ks[16], (DEPTH, HEAD_DIM), 0.02),
        "w_p_rwkv": nrm(ks[17], (DEPTH, RWKV_W, D_MODEL), RWKV_W ** -0.5),
        "w_p_sb": nrm(ks[18], (DEPTH, SB_W, D_MODEL), SB_W ** -0.5),
        "w_p_fox": nrm(ks[19], (DEPTH, FOX_W, D_MODEL), FOX_W ** -0.5),
        "w_out": nrm(ks[20], (DEPTH, D_MODEL, D_MODEL), D_MODEL ** -0.5),
        "norm2_g": 1.0 + nrm(ks[21], (DEPTH, D_MODEL), 0.02),
        "moe_wg": nrm(ks[22], (DEPTH, D_MODEL, N_GROUPS), D_MODEL ** -0.5),
        "moe_bg": nrm(ks[23], (DEPTH, N_GROUPS), 0.01),
        "moe_we": nrm(ks[24], (DEPTH, D_MODEL, N_EXPERTS), D_MODEL ** -0.5),
        "moe_be": nrm(ks[25], (DEPTH, N_EXPERTS), 0.01),
        "moe_w_gate": nrm(ks[26], (DEPTH, N_EXPERTS, D_MODEL, D_EXPERT), D_MODEL ** -0.5),
        "moe_w_up": nrm(ks[27], (DEPTH, N_EXPERTS, D_MODEL, D_EXPERT), D_MODEL ** -0.5),
        "moe_w_down": nrm(ks[28], (DEPTH, N_EXPERTS, D_EXPERT, D_MODEL), D_EXPERT ** -0.5),
    }


def reference(x, meta_tokens, norm1_g, w_in, rwkv_mu, rwkv_w_up, rwkv_w0, rwkv_a_up, rwkv_a0,
              rwkv_k_k, rwkv_k_a, rwkv_r_k, rwkv_ln_g, rwkv_ln_b, fox_f_b, fox_q_g, fox_k_g,
              w_p_rwkv, w_p_sb, w_p_fox, w_out, norm2_g, moe_wg, moe_bg, moe_we, moe_be,
              moe_w_gate, moe_w_up, moe_w_down):
    B = x.shape[0]
    meta = jnp.broadcast_to(meta_tokens[None].astype(x.dtype), (B, N_META, D_MODEL))
    h = jnp.concatenate([meta, x], axis=1)
    for l in range(DEPTH):
        h = h + hybrid_mixer(rms_norm(h, norm1_g[l]), w_in[l], rwkv_mu[l], rwkv_w_up[l],
                             rwkv_w0[l], rwkv_a_up[l], rwkv_a0[l], rwkv_k_k[l], rwkv_k_a[l],
                             rwkv_r_k[l], rwkv_ln_g[l], rwkv_ln_b[l], fox_f_b[l], fox_q_g[l],
                             fox_k_g[l], w_p_rwkv[l], w_p_sb[l], w_p_fox[l], w_out[l])
        h = h + hierarchical_moe(rms_norm(h, norm2_g[l]), moe_wg[l], moe_bg[l], moe_we[l],
                                 moe_be[l], moe_w_gate[l], moe_w_up[l], moe_w_down[l])
    return h[:, N_META:]
```

```python
import functools
import math

import jax
import jax.numpy as jnp
from jax import lax
from jax.experimental import pallas as pl
from jax.experimental.pallas import tpu as pltpu

D_MODEL = 1024
HEAD_DIM = 64
N_META = 16
PAD = 128 - N_META
RWKV_HEADS = 6
SB_HEADS = 5
FOX_HEADS = 5
RWKV_W = RWKV_HEADS * HEAD_DIM
SB_W = SB_HEADS * HEAD_DIM
FOX_W = FOX_HEADS * HEAD_DIM
DECAY_RANK = 64
ICLR_RANK = 64
SHIFT_W = 3 * RWKV_W + DECAY_RANK + ICLR_RANK
N_GROUPS = 4
EXPERTS_PER_GROUP = 4
N_EXPERTS = 16
D_EXPERT = 256
NORM_EPS = 1e-6
RWKV_LN_EPS = 64e-5
NEG_INF = -1e30
ATT_SCALE = 1.0 / math.sqrt(HEAD_DIM)

LANE = 128
ROW_TILE = 512
SEG_SB = SHIFT_W
SEG_FOX = SEG_SB + 1024
SEG_GATE = SEG_FOX + 1024
D_IN_PAD = SEG_GATE + 3 * D_MODEL
FLOGIT_SLOT = (3 * FOX_W // LANE) * LANE
FLOGIT_LANE = 3 * FOX_W - FLOGIT_SLOT

CHUNK = 64
VMEM_LIMIT = 56 * 1024 * 1024

F32 = jnp.float32
BF16 = jnp.bfloat16
HI = lax.Precision.HIGHEST


def _log_sigmoid(x):
    return jnp.minimum(x, 0.0) - jnp.log1p(jnp.exp(-jnp.abs(x)))


def _sigmoid(x):
    return 1.0 / (1.0 + jnp.exp(-x))


def _dot(a, b, **kw):
    return jnp.dot(a, b, preferred_element_type=F32, **kw)


def _dot_nt(a, b):
    return lax.dot_general(a, b, (((1,), (1,)), ((), ())), preferred_element_type=F32)


def _dot_tn(a, b):
    return lax.dot_general(a, b, (((0,), (0,)), ((), ())), preferred_element_type=F32)


def _const_spec(shape):
    n = len(shape)
    return pl.BlockSpec(shape, lambda *_: (0,) * n, pipeline_mode=pl.Buffered(1))


def _inproj_kernel(h_ref, g_ref, w_ref, mu_ref, fb_ref, fqg_ref, fkg_ref,
                   rkv_ref, wdad_ref, sb_ref, fx_ref, fcum_ref, gate_ref,
                   carry_u, carry_f, *, tm):
    i = pl.program_id(1)

    @pl.when(i == 0)
    def _():
        carry_u[...] = jnp.zeros_like(carry_u)
        carry_f[...] = jnp.zeros_like(carry_f)

    x = h_ref[0]
    ms = jnp.mean(x * x, axis=-1, keepdims=True)
    n = (x * lax.rsqrt(ms + NORM_EPS) * g_ref[...]).astype(BF16)
    row = lax.broadcasted_iota(jnp.int32, (tm, 1), 0)

    us = _dot(n, w_ref[:, 0:SHIFT_W])
    prev = pltpu.roll(us, 1, axis=0)
    prev = jnp.where(row == 0, carry_u[...], prev)
    carry_u[...] = us[tm - 1:tm, :]
    ush = us + (prev - us) * mu_ref[...]
    for j in range(3 * RWKV_HEADS):
        rkv_ref[0, j] = ush[:, j * HEAD_DIM:(j + 1) * HEAD_DIM]
    wdad_ref[0] = ush[:, 3 * RWKV_W:SHIFT_W]

    usb = _dot(n, w_ref[:, SEG_SB:SEG_SB + 1024])
    for j in range(3 * SB_HEADS):
        piece = usb[:, j * HEAD_DIM:(j + 1) * HEAD_DIM]
        if j < SB_HEADS:
            piece = piece * ATT_SCALE
        sb_ref[0, j] = piece.astype(BF16)

    uf = _dot(n, w_ref[:, SEG_FOX:SEG_FOX + 1024])
    for j in range(3 * FOX_HEADS):
        piece = uf[:, j * HEAD_DIM:(j + 1) * HEAD_DIM]
        if j < 2 * FOX_HEADS:
            gain = fqg_ref[...] if j < FOX_HEADS else fkg_ref[...]
            pms = jnp.mean(piece * piece, axis=-1, keepdims=True)
            piece = piece * lax.rsqrt(pms + NORM_EPS) * gain
            if j < FOX_HEADS:
                piece = piece * ATT_SCALE
        fx_ref[0, j] = piece.astype(BF16)
    lane = lax.broadcasted_iota(jnp.int32, (tm, LANE), 1)
    t_glob = i * tm + row
    logf = _log_sigmoid(uf[:, FLOGIT_SLOT:FLOGIT_SLOT + LANE] + fb_ref[...])
    is_f = (lane >= FLOGIT_LANE) & (lane < FLOGIT_LANE + FOX_HEADS)
    logf = jnp.where(is_f & (t_glob >= PAD), logf, 0.0)
    tri = (lax.broadcasted_iota(jnp.int32, (tm, tm), 0)
           >= lax.broadcasted_iota(jnp.int32, (tm, tm), 1)).astype(F32)
    cum = _dot(tri, logf, precision=HI) + carry_f[...]
    carry_f[...] = cum[tm - 1:tm, :]
    fcum_ref[0] = cum

    ug = _dot(n, w_ref[:, SEG_GATE:D_IN_PAD])
    gate_ref[0] = _sigmoid(ug).astype(BF16)


def _inproj(h, g1, w_in_p, mu, fb, fqg, fkg, *, tm=256):
    B, TP, D = h.shape
    nb = TP // tm
    kern = functools.partial(_inproj_kernel, tm=tm)
    out_shape = (
        jax.ShapeDtypeStruct((B, 3 * RWKV_HEADS, TP, HEAD_DIM), F32),
        jax.ShapeDtypeStruct((B, TP, LANE), F32),
        jax.ShapeDtypeStruct((B, 3 * SB_HEADS, TP, HEAD_DIM), BF16),
        jax.ShapeDtypeStruct((B, 3 * FOX_HEADS, TP, HEAD_DIM), BF16),
        jax.ShapeDtypeStruct((B, TP, LANE), F32),
        jax.ShapeDtypeStruct((B, TP, 3 * D_MODEL), BF16),
    )
    in_specs = [
        pl.BlockSpec((1, tm, D), lambda b, i: (b, i, 0)),
        _const_spec((1, D)),
        _const_spec((D, D_IN_PAD)),
        _const_spec((1, SHIFT_W)),
        _const_spec((1, LANE)),
        _const_spec((1, HEAD_DIM)),
        _const_spec((1, HEAD_DIM)),
    ]
    out_specs = (
        pl.BlockSpec((1, 3 * RWKV_HEADS, tm, HEAD_DIM), lambda b, i: (b, 0, i, 0)),
        pl.BlockSpec((1, tm, LANE), lambda b, i: (b, i, 0)),
        pl.BlockSpec((1, 3 * SB_HEADS, tm, HEAD_DIM), lambda b, i: (b, 0, i, 0)),
        pl.BlockSpec((1, 3 * FOX_HEADS, tm, HEAD_DIM), lambda b, i: (b, 0, i, 0)),
        pl.BlockSpec((1, tm, LANE), lambda b, i: (b, i, 0)),
        pl.BlockSpec((1, tm, 3 * D_MODEL), lambda b, i: (b, i, 0)),
    )
    return pl.pallas_call(
        kern, out_shape=out_shape, grid=(B, nb), in_specs=in_specs, out_specs=out_specs,
        scratch_shapes=[pltpu.VMEM((1, SHIFT_W), F32), pltpu.VMEM((1, LANE), F32)],
        compiler_params=pltpu.CompilerParams(
            dimension_semantics=("arbitrary", "arbitrary"), vmem_limit_bytes=VMEM_LIMIT),
        name="inproj",
    )(h, g1, w_in_p, mu, fb, fqg, fkg)


def _rwkv_chunk_kernel(r_ref, k_ref, v_ref, wdad_ref, wup_ref, w0_ref, aup_ref, a0_ref,
                       kk_ref, ka_ref, rk_ref,
                       g_ref, hc_ref, r2_ref, y0_ref, bonus_ref, *, rows):
    r = r_ref[0, 0]
    k = k_ref[0, 0]
    v = v_ref[0, 0]
    wd = wdad_ref[0][:, 0:DECAY_RANK]
    ad = wdad_ref[0][:, DECAY_RANK:DECAY_RANK + ICLR_RANK]

    pre = w0_ref[0] + _dot(jnp.tanh(wd), wup_ref[0], precision=HI)
    logw = _log_sigmoid(pre) - 0.5
    lw = -jnp.exp(logw)
    iclr = _sigmoid(a0_ref[0] + _dot(ad, aup_ref[0], precision=HI))
    kk = k * kk_ref[0]
    nrm = jnp.sqrt(jnp.sum(kk * kk, axis=-1, keepdims=True))
    kk = kk / jnp.maximum(nrm, 1e-12)
    k2 = k * (1.0 + (iclr - 1.0) * ka_ref[0])
    a = -kk
    b = kk * iclr
    bonus_ref[0, 0] = jnp.sum(r * k2 * rk_ref[0], axis=-1, keepdims=True) * v

    ri = lax.broadcasted_iota(jnp.int32, (CHUNK, CHUNK), 0)
    ci = lax.broadcasted_iota(jnp.int32, (CHUNK, CHUNK), 1)
    low_incl = ri >= ci
    low_strict = ri > ci
    eye = ri == ci
    tri = low_incl.astype(F32)
    eye_f = eye.astype(F32)

    for c in range(rows // CHUNK):
        sl = slice(c * CHUNK, (c + 1) * CHUNK)
        lwc = lw[sl]
        cum = _dot(tri, lwc, precision=HI)
        cum_end = cum[CHUNK - 1:CHUNK, :]
        e_pos = jnp.exp(cum)
        e_neg = jnp.exp(-cum)
        e_prev = jnp.exp(cum - lwc)
        e_rem = jnp.exp(cum_end - cum)
        vv = v[sl].astype(BF16)
        at = a[sl] * e_prev
        rt = r[sl] * e_pos
        bt = (b[sl] * e_neg).astype(BF16)
        kt = (k2[sl] * e_neg).astype(BF16)
        bq = (b[sl] * e_rem).astype(BF16)
        kq = (k2[sl] * e_rem).astype(BF16)
        at16 = at.astype(BF16)
        rt16 = rt.astype(BF16)
        m_ab = jnp.where(low_strict, _dot_nt(at16, bt), 0.0)
        m_ak = jnp.where(low_strict, _dot_nt(at16, kt), 0.0)
        m_rb = jnp.where(low_incl, _dot_nt(rt16, bt), 0.0)
        m_rk = jnp.where(low_incl, _dot_nt(rt16, kt), 0.0)
        p = m_ab
        tinv = eye_f + p
        for _ in range(5):
            p16 = p.astype(BF16)
            p = _dot(p16, p16)
            tinv = tinv + _dot(tinv.astype(BF16), p.astype(BF16))
        tinv16 = tinv.astype(BF16)
        a2 = _dot(tinv16, at16)
        u0 = _dot(tinv16, _dot(m_ak.astype(BF16), vv).astype(BF16))
        a216 = a2.astype(BF16)
        u016 = u0.astype(BF16)
        m_rb16 = m_rb.astype(BF16)
        r2_ref[0, 0, sl, :] = rt + _dot(m_rb16, a216)
        y0_ref[0, 0, sl, :] = _dot(m_rb16, u016) + _dot(m_rk.astype(BF16), vv)
        g_ref[0, 0, sl, :] = jnp.where(eye, jnp.exp(cum_end), 0.0) + _dot_tn(bq, a216)
        hc_ref[0, 0, sl, :] = _dot_tn(bq, u016) + _dot_tn(kq, vv)


def _rwkv_chunks(rkv, wdad, wup, w0, aup, a0, k_k, k_a, r_k, *, rows=256):
    B, _, TP, _ = rkv.shape
    nb = TP // rows
    H = RWKV_HEADS
    kern = functools.partial(_rwkv_chunk_kernel, rows=rows)
    head_spec = lambda off: pl.BlockSpec((1, 1, rows, HEAD_DIM), lambda b, h, i: (b, h + off, i, 0))
    par_mat = pl.BlockSpec((1, HEAD_DIM, HEAD_DIM), lambda b, h, i: (h, 0, 0))
    par_vec = pl.BlockSpec((1, 1, HEAD_DIM), lambda b, h, i: (h, 0, 0))
    out = jax.ShapeDtypeStruct((B, H, TP, HEAD_DIM), F32)
    out_spec = pl.BlockSpec((1, 1, rows, HEAD_DIM), lambda b, h, i: (b, h, i, 0))
    return pl.pallas_call(
        kern, out_shape=(out,) * 5, grid=(B, H, nb),
        in_specs=[head_spec(0), head_spec(H), head_spec(2 * H),
                  pl.BlockSpec((1, rows, LANE), lambda b, h, i: (b, i, 0)),
                  par_mat, par_vec, par_mat, par_vec, par_vec, par_vec, par_vec],
        out_specs=(out_spec,) * 5,
        compiler_params=pltpu.CompilerParams(
            dimension_semantics=("arbitrary",) * 3, vmem_limit_bytes=VMEM_LIMIT),
        name="rwkv_chunks",
    )(rkv, rkv, rkv, wdad, wup, w0, aup, a0, k_k, k_a, r_k)


def _rwkv_scan_kernel(g_ref, hc_ref, r2_ref, y0_ref, bonus_ref, lng_ref, lnb_ref, y_ref,
                      state, *, rows):
    i = pl.program_id(1)

    @pl.when(i == 0)
    def _():
        state[...] = jnp.zeros_like(state)

    for h in range(RWKV_HEADS):
        hs = state[h]
        for c in range(rows // CHUNK):
            sl = slice(c * CHUNK, (c + 1) * CHUNK)
            y = _dot(r2_ref[0, h, sl, :], hs, precision=HI) + y0_ref[0, h, sl, :]
            hs = _dot(g_ref[0, h, sl, :], hs, precision=HI) + hc_ref[0, h, sl, :]
            mean = jnp.mean(y, axis=-1, keepdims=True)
            yc = y - mean
            var = jnp.mean(yc * yc, axis=-1, keepdims=True)
            out = yc * lax.rsqrt(var + RWKV_LN_EPS) * lng_ref[h] + lnb_ref[h]
            y_ref[0, h, sl, :] = (out + bonus_ref[0, h, sl, :]).astype(BF16)
        state[h] = hs


def _rwkv_scan(gm, hc, r2, y0, bonus, ln_g, ln_b, *, rows=512):
    B, H, TP, _ = gm.shape
    nb = TP // rows
    kern = functools.partial(_rwkv_scan_kernel, rows=rows)
    blk = pl.BlockSpec((1, H, rows, HEAD_DIM), lambda b, i: (b, 0, i, 0))
    par = pl.BlockSpec((H, 1, HEAD_DIM), lambda b, i: (0, 0, 0))
    return pl.pallas_call(
        kern, out_shape=jax.ShapeDtypeStruct((B, H, TP, HEAD_DIM), BF16), grid=(B, nb),
        in_specs=[blk] * 5 + [par, par], out_specs=blk,
        scratch_shapes=[pltpu.VMEM((H, HEAD_DIM, HEAD_DIM), F32)],
        compiler_params=pltpu.CompilerParams(
            dimension_semantics=("arbitrary", "arbitrary"), vmem_limit_bytes=VMEM_LIMIT),
        name="rwkv_scan",
    )(gm, hc, r2, y0, bonus, ln_g, ln_b)


def _sb_kernel(q_ref, k_ref, v_ref, o_ref, *, tq, tk):
    qi = pl.program_id(2)
    q = q_ref[0, 0]
    qpos = qi * tq + lax.broadcasted_iota(jnp.int32, (tq, 1), 0)
    kloc = lax.broadcasted_iota(jnp.int32, (1, tk), 1)
    upper = (lax.broadcasted_iota(jnp.int32, (tk, tk), 0)
             > lax.broadcasted_iota(jnp.int32, (tk, tk), 1)).astype(BF16)
    nkb = (qi * tq + tq) // tk

    def body(step, carry):
        acc, csum = carry
        kb = nkb - 1 - step
        start = pl.multiple_of(kb * tk, tk)
        kblk = k_ref[0, 0, pl.ds(start, tk), :]
        vblk = v_ref[0, 0, pl.ds(start, tk), :]
        z = _dot_nt(q, kblk)
        kpos = start + kloc
        mask = (kpos < qpos) & (kpos >= PAD)
        ls = _log_sigmoid(z)
        l1 = jnp.where(mask, ls - z, 0.0)
        tail = _dot(l1.astype(BF16), upper) + csum
        a = jnp.where(mask, jnp.exp(ls + tail), 0.0)
        acc = acc + _dot(a.astype(BF16), vblk)
        csum = csum + jnp.sum(l1, axis=-1, keepdims=True)
        return acc, csum

    acc, _ = lax.fori_loop(0, nkb, body,
                           (jnp.zeros((tq, HEAD_DIM), F32), jnp.zeros((tq, 1), F32)))
    o_ref[0, 0] = acc.astype(BF16)


def _sb_attention(sb, *, tq=256, tk=256):
    B, _, TP, _ = sb.shape
    H = SB_HEADS
    kern = functools.partial(_sb_kernel, tq=tq, tk=tk)
    return pl.pallas_call(
        kern, out_shape=jax.ShapeDtypeStruct((B, H, TP, HEAD_DIM), BF16),
        grid=(B, H, TP // tq),
        in_specs=[pl.BlockSpec((1, 1, tq, HEAD_DIM), lambda b, h, i: (b, h, i, 0)),
                  pl.BlockSpec((1, 1, TP, HEAD_DIM), lambda b, h, i: (b, h + H, 0, 0)),
                  pl.BlockSpec((1, 1, TP, HEAD_DIM), lambda b, h, i: (b, h + 2 * H, 0, 0))],
        out_specs=pl.BlockSpec((1, 1, tq, HEAD_DIM), lambda b, h, i: (b, h, i, 0)),
        compiler_params=pltpu.CompilerParams(
            dimension_semantics=("arbitrary",) * 3, vmem_limit_bytes=VMEM_LIMIT),
        name="sb_attention",
    )(sb, sb, sb)


def _fox_kernel(q_ref, k_ref, v_ref, fq_ref, fk_ref, o_ref, *, tq, tk):
    h = pl.program_id(1)
    qi = pl.program_id(2)
    q = q_ref[0, 0]
    lane = lax.broadcasted_iota(jnp.int32, (tq, LANE), 1)
    fq = jnp.sum(jnp.where(lane == FLOGIT_LANE + h, fq_ref[0], 0.0), axis=-1, keepdims=True)
    qpos = qi * tq + lax.broadcasted_iota(jnp.int32, (tq, 1), 0)
    kloc = lax.broadcasted_iota(jnp.int32, (1, tk), 1)
    nkb = (qi * tq + tq) // tk

    def body(kb, carry):
        acc, m, l = carry
        start = pl.multiple_of(kb * tk, tk)
        kblk = k_ref[0, 0, pl.ds(start, tk), :]
        vblk = v_ref[0, 0, pl.ds(start, tk), :]
        fk = fk_ref[0, 0, :, pl.ds(start, tk)]
        s = _dot_nt(q, kblk) + fq - fk
        kpos = start + kloc
        mask = (kpos <= qpos) & (kpos >= PAD)
        s = jnp.where(mask, s, NEG_INF)
        m_new = jnp.maximum(m, jnp.max(s, axis=-1, keepdims=True))
        alpha = jnp.exp(m - m_new)
        p = jnp.exp(s - m_new)
        l = alpha * l + jnp.sum(p, axis=-1, keepdims=True)
        acc = alpha * acc + _dot(p.astype(BF16), vblk)
        return acc, m_new, l

    acc, _, l = lax.fori_loop(
        0, nkb, body,
        (jnp.zeros((tq, HEAD_DIM), F32), jnp.full((tq, 1), NEG_INF, F32), jnp.zeros((tq, 1), F32)))
    o_ref[0, 0] = (acc / l).astype(BF16)


def _fox_attention(fx, fcum, fcum_t, *, tq=256, tk=256):
    B, _, TP, _ = fx.shape
    H = FOX_HEADS
    kern = functools.partial(_fox_kernel, tq=tq, tk=tk)
    return pl.pallas_call(
        kern, out_shape=jax.ShapeDtypeStruct((B, H, TP, HEAD_DIM), BF16),
        grid=(B, H, TP // tq),
        in_specs=[pl.BlockSpec((1, 1, tq, HEAD_DIM), lambda b, h, i: (b, h, i, 0)),
                  pl.BlockSpec((1, 1, TP, HEAD_DIM), lambda b, h, i: (b, h + H, 0, 0)),
                  pl.BlockSpec((1, 1, TP, HEAD_DIM), lambda b, h, i: (b, h + 2 * H, 0, 0)),
                  pl.BlockSpec((1, tq, LANE), lambda b, h, i: (b, i, 0)),
                  pl.BlockSpec((1, 1, 1, TP), lambda b, h, i: (b, h, 0, 0))],
        out_specs=pl.BlockSpec((1, 1, tq, HEAD_DIM), lambda b, h, i: (b, h, i, 0)),
        compiler_params=pltpu.CompilerParams(
            dimension_semantics=("arbitrary",) * 3, vmem_limit_bytes=VMEM_LIMIT),
        name="fox_attention",
    )(fx, fx, fx, fcum, fcum_t)


def _merge_kernel(h_ref, ya_ref, yb_ref, yc_ref, gate_ref, wpa_ref, wpb_ref, wpc_ref, wout_ref,
                  o_ref, *, tm, n_valid):
    i = pl.program_id(1)

    def proj(y_ref, w_ref, heads):
        acc = _dot(y_ref[0, 0], w_ref[0])
        for hh in range(1, heads):
            acc = acc + _dot(y_ref[0, hh], w_ref[hh])
        return acc

    merged = gate_ref[0, :, 0:D_MODEL].astype(F32) * proj(ya_ref, wpa_ref, RWKV_HEADS)
    merged = merged + gate_ref[0, :, D_MODEL:2 * D_MODEL].astype(F32) * proj(yb_ref, wpb_ref, SB_HEADS)
    merged = merged + gate_ref[0, :, 2 * D_MODEL:3 * D_MODEL].astype(F32) * proj(yc_ref, wpc_ref, FOX_HEADS)
    out = h_ref[0] + _dot(merged.astype(BF16), wout_ref[...])
    t = i * tm + lax.broadcasted_iota(jnp.int32, (tm, 1), 0)
    o_ref[0] = jnp.where((t >= PAD) & (t < PAD + n_valid), out, 0.0)


def _merge(h, ya, yb, yc, gates, wpa, wpb, wpc, wout, *, n_valid, tm=256):
    B, TP, D = h.shape
    kern = functools.partial(_merge_kernel, tm=tm, n_valid=n_valid)
    head_blk = lambda nh: pl.BlockSpec((1, nh, tm, HEAD_DIM), lambda b, i: (b, 0, i, 0))
    return pl.pallas_call(
        kern, out_shape=jax.ShapeDtypeStruct((B, TP, D), F32), grid=(B, TP // tm),
        in_specs=[pl.BlockSpec((1, tm, D), lambda b, i: (b, i, 0)),
                  head_blk(RWKV_HEADS), head_blk(SB_HEADS), head_blk(FOX_HEADS),
                  pl.BlockSpec((1, tm, 3 * D), lambda b, i: (b, i, 0)),
                  _const_spec((RWKV_HEADS, HEAD_DIM, D)),
                  _const_spec((SB_HEADS, HEAD_DIM, D)),
                  _const_spec((FOX_HEADS, HEAD_DIM, D)),
                  _const_spec((D, D))],
        out_specs=pl.BlockSpec((1, tm, D), lambda b, i: (b, i, 0)),
        compiler_params=pltpu.CompilerParams(
            dimension_semantics=("arbitrary", "arbitrary"), vmem_limit_bytes=VMEM_LIMIT),
        name="merge",
    )(h, ya, yb, yc, gates, wpa, wpb, wpc, wout)


def _moe_kernel(h_ref, g_ref, wr_ref, br_ref, wg_ref, wu_ref, wd_ref, o_ref, he_ref,
                *, tm, n_valid):
    i = pl.program_id(1)
    x = h_ref[0]
    ms = jnp.mean(x * x, axis=-1, keepdims=True)
    n = x * lax.rsqrt(ms + NORM_EPS) * g_ref[...]

    lg = _dot(n, wr_ref[...], precision=HI) + br_ref[...]
    lane = lax.broadcasted_iota(jnp.int32, (tm, LANE), 1)
    big = jnp.int32(LANE)
    is_group = lane < N_GROUPS
    gl = jnp.where(is_group, lg, -jnp.inf)
    gmax = jnp.max(gl, axis=-1, keepdims=True)
    g_idx = jnp.min(jnp.where(is_group & (gl == gmax), lane, big), axis=-1, keepdims=True)
    p_group = 1.0 / jnp.sum(jnp.where(is_group, jnp.exp(gl - gmax), 0.0), axis=-1, keepdims=True)
    lo = N_GROUPS + g_idx * EXPERTS_PER_GROUP
    in_grp = (lane >= lo) & (lane < lo + EXPERTS_PER_GROUP)
    el = jnp.where(in_grp, lg, -jnp.inf)
    top1 = jnp.max(el, axis=-1, keepdims=True)
    i1 = jnp.min(jnp.where(in_grp & (el == top1), lane, big), axis=-1, keepdims=True)
    rest = in_grp & (lane != i1)
    el2 = jnp.where(rest, lg, -jnp.inf)
    top2 = jnp.max(el2, axis=-1, keepdims=True)
    i2 = jnp.min(jnp.where(rest & (el2 == top2), lane, big), axis=-1, keepdims=True)
    e2 = jnp.exp(top2 - top1)
    p1 = 1.0 / (1.0 + e2)
    p2 = e2 / (1.0 + e2)
    comb = p_group * (jnp.where(lane == i1, p1, 0.0) + jnp.where(lane == i2, p2, 0.0))

    n16 = n.astype(BF16)
    gate = _dot(n16, wg_ref[...])
    up = _dot(n16, wu_ref[...])
    he = gate * _sigmoid(gate) * up
    for e in range(N_EXPERTS):
        c_e = comb[:, N_GROUPS + e:N_GROUPS + e + 1]
        he_ref[:, e * D_EXPERT:(e + 1) * D_EXPERT] = (
            he[:, e * D_EXPERT:(e + 1) * D_EXPERT] * c_e).astype(BF16)
    out = x + _dot(he_ref[...], wd_ref[...])
    t = i * tm + lax.broadcasted_iota(jnp.int32, (tm, 1), 0)
    o_ref[0] = jnp.where((t >= PAD) & (t < PAD + n_valid), out, 0.0)


def _moe(h, g2, wr, br, wg, wu, wd, *, n_valid, tm=256):
    B, TP, D = h.shape
    kern = functools.partial(_moe_kernel, tm=tm, n_valid=n_valid)
    EW = N_EXPERTS * D_EXPERT
    return pl.pallas_call(
        kern, out_shape=jax.ShapeDtypeStruct((B, TP, D), F32), grid=(B, TP // tm),
        in_specs=[pl.BlockSpec((1, tm, D), lambda b, i: (b, i, 0)),
                  _const_spec((1, D)), _const_spec((D, LANE)), _const_spec((1, LANE)),
                  _const_spec((D, EW)), _const_spec((D, EW)), _const_spec((EW, D))],
        out_specs=pl.BlockSpec((1, tm, D), lambda b, i: (b, i, 0)),
        scratch_shapes=[pltpu.VMEM((tm, EW), BF16)],
        compiler_params=pltpu.CompilerParams(
            dimension_semantics=("arbitrary", "arbitrary"), vmem_limit_bytes=VMEM_LIMIT),
        name="moe",
    )(h, g2, wr, br, wg, wu, wd)


def _pack_w_in(w_in):
    D = w_in.shape[0]
    rest = w_in[:, SHIFT_W:]
    qkv_w = 3 * SB_W + 3 * FOX_W
    z = lambda n: jnp.zeros((D, n), w_in.dtype)
    cols = [w_in[:, :SHIFT_W],
            rest[:, 0:3 * SB_W], z(1024 - 3 * SB_W),
            rest[:, 3 * SB_W:qkv_w + FOX_HEADS], z(1024 - 3 * FOX_W - FOX_HEADS),
            rest[:, qkv_w + FOX_HEADS:]]
    return jnp.concatenate(cols, axis=1).astype(BF16)


def _per_head_cols(w, heads):
    return w.reshape(w.shape[0], heads, HEAD_DIM).transpose(1, 0, 2)


def kernel(x, meta_tokens, norm1_g, w_in, rwkv_mu, rwkv_w_up, rwkv_w0, rwkv_a_up, rwkv_a0,
           rwkv_k_k, rwkv_k_a, rwkv_r_k, rwkv_ln_g, rwkv_ln_b, fox_f_b, fox_q_g, fox_k_g,
           w_p_rwkv, w_p_sb, w_p_fox, w_out, norm2_g, moe_wg, moe_bg, moe_we, moe_be,
           moe_w_gate, moe_w_up, moe_w_down):
    B, S, D = x.shape
    depth = w_in.shape[0]
    L = N_META + S
    TP = -(-(PAD + L) // ROW_TILE) * ROW_TILE
    meta = jnp.broadcast_to(meta_tokens[None].astype(x.dtype), (B, N_META, D))
    h = jnp.concatenate([jnp.zeros((B, PAD, D), x.dtype), meta, x,
                         jnp.zeros((B, TP - PAD - L, D), x.dtype)], axis=1)
    H = RWKV_HEADS
    EW = N_EXPERTS * D_EXPERT
    for l in range(depth):
        fb = jnp.zeros((1, LANE), F32).at[0, FLOGIT_LANE:FLOGIT_LANE + FOX_HEADS].set(fox_f_b[l])
        rkv, wdad, sb, fx, fcum, gates = _inproj(
            h, norm1_g[l][None], _pack_w_in(w_in[l]), rwkv_mu[l][None], fb,
            fox_q_g[l][None], fox_k_g[l][None])
        hv = lambda p: p.reshape(H, 1, HEAD_DIM)
        gm, hc, r2, y0, bonus = _rwkv_chunks(
            rkv, wdad, _per_head_cols(rwkv_w_up[l], H), hv(rwkv_w0[l]),
            _per_head_cols(rwkv_a_up[l], H), hv(rwkv_a0[l]),
            hv(rwkv_k_k[l]), hv(rwkv_k_a[l]), hv(rwkv_r_k[l]))
        ya = _rwkv_scan(gm, hc, r2, y0, bonus, hv(rwkv_ln_g[l]), hv(rwkv_ln_b[l]))
        yb = _sb_attention(sb)
        fcum_t = jnp.transpose(fcum[:, :, FLOGIT_LANE:FLOGIT_LANE + FOX_HEADS], (0, 2, 1))[:, :, None, :]
        yc = _fox_attention(fx, fcum, fcum_t)
        ph = lambda w, nh: w.reshape(nh, HEAD_DIM, D).astype(BF16)
        h = _merge(h, ya, yb, yc, gates, ph(w_p_rwkv[l], H), ph(w_p_sb[l], SB_HEADS),
                   ph(w_p_fox[l], FOX_HEADS), w_out[l].astype(BF16), n_valid=L)
        wr = jnp.zeros((D, LANE), F32).at[:, :N_GROUPS].set(moe_wg[l])
        wr = wr.at[:, N_GROUPS:N_GROUPS + N_EXPERTS].set(moe_we[l])
        br = jnp.zeros((1, LANE), F32).at[0, :N_GROUPS].set(moe_bg[l])
        br = br.at[0, N_GROUPS:N_GROUPS + N_EXPERTS].set(moe_be[l])
        wg = moe_w_gate[l].transpose(1, 0, 2).reshape(D, EW).astype(BF16)
        wu = moe_w_up[l].transpose(1, 0, 2).reshape(D, EW).astype(BF16)
        wd = moe_w_down[l].reshape(EW, D).astype(BF16)
        h = _moe(h, norm2_g[l][None], wr, br, wg, wu, wd, n_valid=L)
    return h[:, PAD + N_META:PAD + L]
```

```python
import functools
import math

import jax
import jax.numpy as jnp
from jax import lax
from jax.experimental import pallas as pl
from jax.experimental.pallas import tpu as pltpu

D_MODEL = 1024
HEAD_DIM = 64
N_META = 16
PAD = 128 - N_META
RWKV_HEADS = 6
SB_HEADS = 5
FOX_HEADS = 5
RWKV_W = RWKV_HEADS * HEAD_DIM
SB_W = SB_HEADS * HEAD_DIM
FOX_W = FOX_HEADS * HEAD_DIM
DECAY_RANK = 64
ICLR_RANK = 64
SHIFT_W = 3 * RWKV_W + DECAY_RANK + ICLR_RANK
N_GROUPS = 4
EXPERTS_PER_GROUP = 4
N_EXPERTS = 16
D_EXPERT = 256
NORM_EPS = 1e-6
RWKV_LN_EPS = 64e-5
NEG_INF = -1e30
ATT_SCALE = 1.0 / math.sqrt(HEAD_DIM)

LANE = 128
ROW_TILE = 512
PAIRS = 3
QKV_W = 3 * PAIRS * LANE
SEG_SB = SHIFT_W
SEG_FOX = SEG_SB + QKV_W
SEG_GATE = SEG_FOX + QKV_W + LANE
D_IN_PAD = SEG_GATE + 3 * D_MODEL
SB_CUTOFF = 110.0

CHUNK = 64
VMEM_LIMIT = 56 * 1024 * 1024

F32 = jnp.float32
BF16 = jnp.bfloat16
HI = lax.Precision.HIGHEST


def _log_sigmoid(x):
    return jnp.minimum(x, 0.0) - jnp.log1p(jnp.exp(-jnp.abs(x)))


def _sigmoid(x):
    return 1.0 / (1.0 + jnp.exp(-x))


def _dot(a, b, **kw):
    return jnp.dot(a, b, preferred_element_type=F32, **kw)


def _dot_nt(a, b):
    return lax.dot_general(a, b, (((1,), (1,)), ((), ())), preferred_element_type=F32)


def _dot_tn(a, b):
    return lax.dot_general(a, b, (((0,), (0,)), ((), ())), preferred_element_type=F32)


def _const_spec(shape):
    n = len(shape)
    return pl.BlockSpec(shape, lambda *_: (0,) * n, pipeline_mode=pl.Buffered(1))


def _inproj_kernel(h_ref, g_ref, w_ref, mu_ref, fb_ref, fqg_ref, fkg_ref,
                   rkv_ref, wdad_ref, sb_ref, fx_ref, fcum_ref, gate_ref,
                   carry_u, carry_f, *, tm):
    i = pl.program_id(1)

    @pl.when(i == 0)
    def _():
        carry_u[...] = jnp.zeros_like(carry_u)
        carry_f[...] = jnp.zeros_like(carry_f)

    x = h_ref[0]
    ms = jnp.mean(x * x, axis=-1, keepdims=True)
    n = (x * lax.rsqrt(ms + NORM_EPS) * g_ref[...]).astype(BF16)
    row = lax.broadcasted_iota(jnp.int32, (tm, 1), 0)

    us = _dot(n, w_ref[:, 0:SHIFT_W])
    prev = pltpu.roll(us, 1, axis=0)
    prev = jnp.where(row == 0, carry_u[...], prev)
    carry_u[...] = us[tm - 1:tm, :]
    ush = us + (prev - us) * mu_ref[...]
    for j in range(3 * RWKV_HEADS):
        rkv_ref[0, j] = ush[:, j * HEAD_DIM:(j + 1) * HEAD_DIM]
    wdad_ref[0] = ush[:, 3 * RWKV_W:SHIFT_W]

    usb = _dot(n, w_ref[:, SEG_SB:SEG_SB + QKV_W])
    for j in range(3 * PAIRS):
        piece = usb[:, j * LANE:(j + 1) * LANE]
        if j < PAIRS:
            piece = piece * ATT_SCALE
        sb_ref[0, j] = piece.astype(BF16)

    uf = _dot(n, w_ref[:, SEG_FOX:SEG_FOX + QKV_W + LANE])
    lane = lax.broadcasted_iota(jnp.int32, (tm, LANE), 1)
    first = lane < HEAD_DIM
    for j in range(3 * PAIRS):
        piece = uf[:, j * LANE:(j + 1) * LANE]
        if j < 2 * PAIRS:
            gain = fqg_ref[...] if j < PAIRS else fkg_ref[...]
            sq = piece * piece
            ms0 = jnp.sum(jnp.where(first, sq, 0.0), axis=-1, keepdims=True) * (1.0 / HEAD_DIM)
            ms1 = jnp.sum(jnp.where(first, 0.0, sq), axis=-1, keepdims=True) * (1.0 / HEAD_DIM)
            inv = jnp.where(first, lax.rsqrt(ms0 + NORM_EPS), lax.rsqrt(ms1 + NORM_EPS))
            piece = piece * inv * gain
            if j < PAIRS:
                piece = piece * ATT_SCALE
        fx_ref[0, j] = piece.astype(BF16)
    t_glob = i * tm + row
    logf = _log_sigmoid(uf[:, QKV_W:QKV_W + LANE] + fb_ref[...])
    logf = jnp.where((lane < FOX_HEADS) & (t_glob >= PAD), logf, 0.0)
    tri = (lax.broadcasted_iota(jnp.int32, (tm, tm), 0)
           >= lax.broadcasted_iota(jnp.int32, (tm, tm), 1)).astype(F32)
    cum = _dot(tri, logf, precision=HI) + carry_f[...]
    carry_f[...] = cum[tm - 1:tm, :]
    fcum_ref[0] = cum

    ug = _dot(n, w_ref[:, SEG_GATE:D_IN_PAD])
    gate_ref[0] = _sigmoid(ug).astype(BF16)


def _inproj(h, g1, w_in_p, mu, fb, fqg, fkg, *, tm=256):
    B, TP, D = h.shape
    nb = TP // tm
    kern = functools.partial(_inproj_kernel, tm=tm)
    out_shape = (
        jax.ShapeDtypeStruct((B, 3 * RWKV_HEADS, TP, HEAD_DIM), F32),
        jax.ShapeDtypeStruct((B, TP, LANE), F32),
        jax.ShapeDtypeStruct((B, 3 * PAIRS, TP, LANE), BF16),
        jax.ShapeDtypeStruct((B, 3 * PAIRS, TP, LANE), BF16),
        jax.ShapeDtypeStruct((B, TP, LANE), F32),
        jax.ShapeDtypeStruct((B, TP, 3 * D_MODEL), BF16),
    )
    in_specs = [
        pl.BlockSpec((1, tm, D), lambda b, i: (b, i, 0)),
        _const_spec((1, D)),
        _const_spec((D, D_IN_PAD)),
        _const_spec((1, SHIFT_W)),
        _const_spec((1, LANE)),
        _const_spec((1, LANE)),
        _const_spec((1, LANE)),
    ]
    out_specs = (
        pl.BlockSpec((1, 3 * RWKV_HEADS, tm, HEAD_DIM), lambda b, i: (b, 0, i, 0)),
        pl.BlockSpec((1, tm, LANE), lambda b, i: (b, i, 0)),
        pl.BlockSpec((1, 3 * PAIRS, tm, LANE), lambda b, i: (b, 0, i, 0)),
        pl.BlockSpec((1, 3 * PAIRS, tm, LANE), lambda b, i: (b, 0, i, 0)),
        pl.BlockSpec((1, tm, LANE), lambda b, i: (b, i, 0)),
        pl.BlockSpec((1, tm, 3 * D_MODEL), lambda b, i: (b, i, 0)),
    )
    return pl.pallas_call(
        kern, out_shape=out_shape, grid=(B, nb), in_specs=in_specs, out_specs=out_specs,
        scratch_shapes=[pltpu.VMEM((1, SHIFT_W), F32), pltpu.VMEM((1, LANE), F32)],
        compiler_params=pltpu.CompilerParams(
            dimension_semantics=("arbitrary", "arbitrary"), vmem_limit_bytes=VMEM_LIMIT),
        name="inproj",
    )(h, g1, w_in_p, mu, fb, fqg, fkg)


def _rwkv_chunk_kernel(r_ref, k_ref, v_ref, wdad_ref, wup_ref, w0_ref, aup_ref, a0_ref,
                       kk_ref, ka_ref, rk_ref,
                       g_ref, hc_ref, r2_ref, y0_ref, bonus_ref, *, rows):
    r = r_ref[0, 0]
    k = k_ref[0, 0]
    v = v_ref[0, 0]
    wd = wdad_ref[0][:, 0:DECAY_RANK]
    ad = wdad_ref[0][:, DECAY_RANK:DECAY_RANK + ICLR_RANK]

    pre = w0_ref[0] + _dot(jnp.tanh(wd), wup_ref[0], precision=HI)
    logw = _log_sigmoid(pre) - 0.5
    lw = -jnp.exp(logw)
    iclr = _sigmoid(a0_ref[0] + _dot(ad, aup_ref[0], precision=HI))
    kk = k * kk_ref[0]
    nrm = jnp.sqrt(jnp.sum(kk * kk, axis=-1, keepdims=True))
    kk = kk / jnp.maximum(nrm, 1e-12)
    k2 = k * (1.0 + (iclr - 1.0) * ka_ref[0])
    a = -kk
    b = kk * iclr
    bonus_ref[0, 0] = jnp.sum(r * k2 * rk_ref[0], axis=-1, keepdims=True) * v

    ri = lax.broadcasted_iota(jnp.int32, (CHUNK, CHUNK), 0)
    ci = lax.broadcasted_iota(jnp.int32, (CHUNK, CHUNK), 1)
    low_incl = ri >= ci
    low_strict = ri > ci
    eye = ri == ci
    tri = low_incl.astype(F32)
    eye_f = eye.astype(F32)

    for c in range(rows // CHUNK):
        sl = slice(c * CHUNK, (c + 1) * CHUNK)
        lwc = lw[sl]
        cum = _dot(tri, lwc, precision=HI)
        cum_end = cum[CHUNK - 1:CHUNK, :]
        e_pos = jnp.exp(cum)
        e_neg = jnp.exp(-cum)
        e_prev = jnp.exp(cum - lwc)
        e_rem = jnp.exp(cum_end - cum)
        vv = v[sl].astype(BF16)
        at = a[sl] * e_prev
        rt = r[sl] * e_pos
        bt = (b[sl] * e_neg).astype(BF16)
        kt = (k2[sl] * e_neg).astype(BF16)
        bq = (b[sl] * e_rem).astype(BF16)
        kq = (k2[sl] * e_rem).astype(BF16)
        at16 = at.astype(BF16)
        rt16 = rt.astype(BF16)
        m_ab = jnp.where(low_strict, _dot_nt(at16, bt), 0.0)
        m_ak = jnp.where(low_strict, _dot_nt(at16, kt), 0.0)
        m_rb = jnp.where(low_incl, _dot_nt(rt16, bt), 0.0)
        m_rk = jnp.where(low_incl, _dot_nt(rt16, kt), 0.0)
        p = m_ab
        tinv = eye_f + p
        for _ in range(5):
            p16 = p.astype(BF16)
            p = _dot(p16, p16)
            tinv = tinv + _dot(tinv.astype(BF16), p.astype(BF16))
        tinv16 = tinv.astype(BF16)
        a2 = _dot(tinv16, at16)
        u0 = _dot(tinv16, _dot(m_ak.astype(BF16), vv).astype(BF16))
        a216 = a2.astype(BF16)
        u016 = u0.astype(BF16)
        m_rb16 = m_rb.astype(BF16)
        r2_ref[0, 0, sl, :] = rt + _dot(m_rb16, a216)
        y0_ref[0, 0, sl, :] = _dot(m_rb16, u016) + _dot(m_rk.astype(BF16), vv)
        g_ref[0, 0, sl, :] = jnp.where(eye, jnp.exp(cum_end), 0.0) + _dot_tn(bq, a216)
        hc_ref[0, 0, sl, :] = _dot_tn(bq, u016) + _dot_tn(kq, vv)


def _rwkv_chunks(rkv, wdad, wup, w0, aup, a0, k_k, k_a, r_k, *, rows=256):
    B, _, TP, _ = rkv.shape
    nb = TP // rows
    H = RWKV_HEADS
    kern = functools.partial(_rwkv_chunk_kernel, rows=rows)
    head_spec = lambda off: pl.BlockSpec((1, 1, rows, HEAD_DIM), lambda b, h, i: (b, h + off, i, 0))
    par_mat = pl.BlockSpec((1, HEAD_DIM, HEAD_DIM), lambda b, h, i: (h, 0, 0))
    par_vec = pl.BlockSpec((1, 1, HEAD_DIM), lambda b, h, i: (h, 0, 0))
    out = jax.ShapeDtypeStruct((B, H, TP, HEAD_DIM), F32)
    out_spec = pl.BlockSpec((1, 1, rows, HEAD_DIM), lambda b, h, i: (b, h, i, 0))
    return pl.pallas_call(
        kern, out_shape=(out,) * 5, grid=(B, H, nb),
        in_specs=[head_spec(0), head_spec(H), head_spec(2 * H),
                  pl.BlockSpec((1, rows, LANE), lambda b, h, i: (b, i, 0)),
                  par_mat, par_vec, par_mat, par_vec, par_vec, par_vec, par_vec],
        out_specs=(out_spec,) * 5,
        compiler_params=pltpu.CompilerParams(
            dimension_semantics=("arbitrary",) * 3, vmem_limit_bytes=VMEM_LIMIT),
        name="rwkv_chunks",
    )(rkv, rkv, rkv, wdad, wup, w0, aup, a0, k_k, k_a, r_k)


def _rwkv_scan_kernel(g_ref, hc_ref, r2_ref, y0_ref, bonus_ref, lng_ref, lnb_ref, y_ref,
                      state, *, rows):
    i = pl.program_id(1)

    @pl.when(i == 0)
    def _():
        state[...] = jnp.zeros_like(state)

    for h in range(RWKV_HEADS):
        hs = state[h]
        for c in range(rows // CHUNK):
            sl = slice(c * CHUNK, (c + 1) * CHUNK)
            y = _dot(r2_ref[0, h, sl, :], hs, precision=HI) + y0_ref[0, h, sl, :]
            hs = _dot(g_ref[0, h, sl, :], hs, precision=HI) + hc_ref[0, h, sl, :]
            mean = jnp.mean(y, axis=-1, keepdims=True)
            yc = y - mean
            var = jnp.mean(yc * yc, axis=-1, keepdims=True)
            out = yc * lax.rsqrt(var + RWKV_LN_EPS) * lng_ref[h] + lnb_ref[h]
            y_ref[0, h, sl, :] = (out + bonus_ref[0, h, sl, :]).astype(BF16)
        state[h] = hs


def _rwkv_scan(gm, hc, r2, y0, bonus, ln_g, ln_b, *, rows=512):
    B, H, TP, _ = gm.shape
    nb = TP // rows
    kern = functools.partial(_rwkv_scan_kernel, rows=rows)
    blk = pl.BlockSpec((1, H, rows, HEAD_DIM), lambda b, i: (b, 0, i, 0))
    par = pl.BlockSpec((H, 1, HEAD_DIM), lambda b, i: (0, 0, 0))
    return pl.pallas_call(
        kern, out_shape=jax.ShapeDtypeStruct((B, H, TP, HEAD_DIM), BF16), grid=(B, nb),
        in_specs=[blk] * 5 + [par, par], out_specs=blk,
        scratch_shapes=[pltpu.VMEM((H, HEAD_DIM, HEAD_DIM), F32)],
        compiler_params=pltpu.CompilerParams(
            dimension_semantics=("arbitrary", "arbitrary"), vmem_limit_bytes=VMEM_LIMIT),
        name="rwkv_scan",
    )(gm, hc, r2, y0, bonus, ln_g, ln_b)


def _split_pair(qp, nh):
    lane = lax.broadcasted_iota(jnp.int32, qp.shape, 1)
    first = lane < HEAD_DIM
    zero = jnp.zeros_like(qp)
    qs = [jnp.where(first, qp, zero), jnp.where(first, zero, qp)]
    return qs[:nh], first


def _sb_kernel(q_ref, k_ref, v_ref, o_ref, *, t, nh):
    qi = pl.program_id(2)
    qs, first = _split_pair(q_ref[0, 0], nh)
    qpos = qi * t + lax.broadcasted_iota(jnp.int32, (t, 1), 0)
    kloc = lax.broadcasted_iota(jnp.int32, (1, t), 1)
    upper = (lax.broadcasted_iota(jnp.int32, (t, t), 0)
             > lax.broadcasted_iota(jnp.int32, (t, t), 1)).astype(BF16)

    def block(kb, accs, cs, causal):
        start = pl.multiple_of(kb * t, t)
        kblk = k_ref[0, 0, pl.ds(start, t), :]
        vblk = v_ref[0, 0, pl.ds(start, t), :]
        kpos = start + kloc
        mask = kpos >= PAD
        if causal:
            mask = mask & (kpos < qpos)
        new_accs, new_cs = [], []
        for h in range(nh):
            z = _dot_nt(qs[h], kblk)
            ls = jnp.minimum(z, 0.0) - jnp.log(1.0 + jnp.exp(-jnp.abs(z)))
            l1 = jnp.where(mask, ls - z, 0.0)
            tail = _dot(l1.astype(BF16), upper) + cs[h]
            a = jnp.where(mask, jnp.exp(ls + tail), 0.0)
            new_accs.append(accs[h] + _dot(a.astype(BF16), vblk))
            new_cs.append(cs[h] + jnp.sum(l1, axis=-1, keepdims=True))
        return new_accs, new_cs

    def live(cs):
        top = cs[0]
        for c in cs[1:]:
            top = jnp.maximum(top, c)
        return (jnp.max(top) > -SB_CUTOFF).astype(jnp.int32)

    zero_acc = [jnp.zeros((t, LANE), F32) for _ in range(nh)]
    zero_c = [jnp.zeros((t, 1), F32) for _ in range(nh)]
    accs, cs = block(qi, zero_acc, zero_c, True)

    def cond(carry):
        kb, alive = carry[0], carry[1]
        return (kb >= 0) & (alive > 0)

    def body(carry):
        kb = carry[0]
        accs, cs = block(kb, list(carry[2:2 + nh]), list(carry[2 + nh:]), False)
        return (kb - 1, live(cs), *accs, *cs)

    carry = lax.while_loop(cond, body, (qi - 1, live(cs), *accs, *cs))
    accs = carry[2:2 + nh]
    out = jnp.where(first, accs[0], accs[1] if nh == 2 else 0.0)
    o_ref[0, 0] = out.astype(BF16)


def _sb_attention(sb, *, slot0, nslots, nh, t=256):
    B, _, TP, _ = sb.shape
    kern = functools.partial(_sb_kernel, t=t, nh=nh)
    blk = lambda off: pl.BlockSpec((1, 1, t, LANE), lambda b, p, i: (b, slot0 + off + p, i, 0))
    full = lambda off: pl.BlockSpec((1, 1, TP, LANE), lambda b, p, i: (b, slot0 + off + p, 0, 0))
    return pl.pallas_call(
        kern, out_shape=jax.ShapeDtypeStruct((B, nslots, TP, LANE), BF16),
        grid=(B, nslots, TP // t),
        in_specs=[blk(0), full(PAIRS), full(2 * PAIRS)],
        out_specs=pl.BlockSpec((1, 1, t, LANE), lambda b, p, i: (b, p, i, 0)),
        compiler_params=pltpu.CompilerParams(
            dimension_semantics=("arbitrary",) * 3, vmem_limit_bytes=VMEM_LIMIT),
        name="sb_attention",
    )(sb, sb, sb)


def _fox_kernel(q_ref, k_ref, v_ref, fq_ref, *rest, t, nh, head0):
    fk_refs, o_ref = rest[:nh], rest[nh]
    p = pl.program_id(1)
    qi = pl.program_id(2)
    qs, first = _split_pair(q_ref[0, 0], nh)
    lane = lax.broadcasted_iota(jnp.int32, (t, LANE), 1)
    fqs = [jnp.sum(jnp.where(lane == head0 + 2 * p + h, fq_ref[0], 0.0), axis=-1, keepdims=True)
           for h in range(nh)]
    qpos = qi * t + lax.broadcasted_iota(jnp.int32, (t, 1), 0)
    kloc = lax.broadcasted_iota(jnp.int32, (1, t), 1)

    def block(kb, carry, masked):
        start = pl.multiple_of(kb * t, t)
        kblk = k_ref[0, 0, pl.ds(start, t), :]
        vblk = v_ref[0, 0, pl.ds(start, t), :]
        if masked:
            kpos = start + kloc
            mask = (kpos <= qpos) & (kpos >= PAD)
        out = []
        for h in range(nh):
            acc, m, l = carry[3 * h:3 * h + 3]
            fk = fk_refs[h][0, 0, :, pl.ds(start, t)]
            s = _dot_nt(qs[h], kblk) + fqs[h] - fk
            if masked:
                s = jnp.where(mask, s, NEG_INF)
            m_new = jnp.maximum(m, jnp.max(s, axis=-1, keepdims=True))
            alpha = jnp.exp(m - m_new)
            pr = jnp.exp(s - m_new)
            l = alpha * l + jnp.sum(pr, axis=-1, keepdims=True)
            acc = alpha * acc + _dot(pr.astype(BF16), vblk)
            out += [acc, m_new, l]
        return tuple(out)

    init = []
    for h in range(nh):
        init += [jnp.zeros((t, LANE), F32), jnp.full((t, 1), NEG_INF, F32), jnp.zeros((t, 1), F32)]
    carry = block(0, tuple(init), True)
    carry = lax.fori_loop(1, qi, lambda kb, c: block(kb, c, False), carry)
    carry = lax.cond(qi > 0, lambda c: block(qi, c, True), lambda c: c, carry)
    outs = [carry[3 * h] / carry[3 * h + 2] for h in range(nh)]
    out = jnp.where(first, outs[0], outs[1] if nh == 2 else 0.0)
    o_ref[0, 0] = out.astype(BF16)


def _fox_attention(fx, fcum, fcum_t, *, slot0, nslots, nh, t=512):
    B, _, TP, _ = fx.shape
    head0 = 2 * slot0
    kern = functools.partial(_fox_kernel, t=t, nh=nh, head0=head0)
    blk = lambda off: pl.BlockSpec((1, 1, t, LANE), lambda b, p, i: (b, slot0 + off + p, i, 0))
    full = lambda off: pl.BlockSpec((1, 1, TP, LANE), lambda b, p, i: (b, slot0 + off + p, 0, 0))
    fk_spec = lambda h: pl.BlockSpec((1, 1, 1, TP), lambda b, p, i: (b, head0 + 2 * p + h, 0, 0))
    return pl.pallas_call(
        kern, out_shape=jax.ShapeDtypeStruct((B, nslots, TP, LANE), BF16),
        grid=(B, nslots, TP // t),
        in_specs=[blk(0), full(PAIRS), full(2 * PAIRS),
                  pl.BlockSpec((1, t, LANE), lambda b, p, i: (b, i, 0))]
                 + [fk_spec(h) for h in range(nh)],
        out_specs=pl.BlockSpec((1, 1, t, LANE), lambda b, p, i: (b, p, i, 0)),
        compiler_params=pltpu.CompilerParams(
            dimension_semantics=("arbitrary",) * 3, vmem_limit_bytes=VMEM_LIMIT),
        name="fox_attention",
    )(fx, fx, fx, fcum, *([fcum_t] * nh))


def _merge_kernel(h_ref, ya_ref, yb2_ref, yb1_ref, yc2_ref, yc1_ref, gate_ref,
                  wpa_ref, wpb_ref, wpc_ref, wout_ref, o_ref, *, tm, n_valid):
    i = pl.program_id(1)

    pa = _dot(ya_ref[0, 0], wpa_ref[0])
    for hh in range(1, RWKV_HEADS):
        pa = pa + _dot(ya_ref[0, hh], wpa_ref[hh])

    def pair_proj(y2_ref, y1_ref, w_ref):
        return (_dot(y2_ref[0, 0], w_ref[0]) + _dot(y2_ref[0, 1], w_ref[1])
                + _dot(y1_ref[0, 0], w_ref[2]))

    merged = gate_ref[0, :, 0:D_MODEL].astype(F32) * pa
    merged = merged + (gate_ref[0, :, D_MODEL:2 * D_MODEL].astype(F32)
                       * pair_proj(yb2_ref, yb1_ref, wpb_ref))
    merged = merged + (gate_ref[0, :, 2 * D_MODEL:3 * D_MODEL].astype(F32)
                       * pair_proj(yc2_ref, yc1_ref, wpc_ref))
    out = h_ref[0] + _dot(merged.astype(BF16), wout_ref[...])
    t = i * tm + lax.broadcasted_iota(jnp.int32, (tm, 1), 0)
    o_ref[0] = jnp.where((t >= PAD) & (t < PAD + n_valid), out, 0.0)


def _merge(h, ya, yb2, yb1, yc2, yc1, gates, wpa, wpb, wpc, wout, *, n_valid, tm=256):
    B, TP, D = h.shape
    kern = functools.partial(_merge_kernel, tm=tm, n_valid=n_valid)
    pair_blk = lambda ns: pl.BlockSpec((1, ns, tm, LANE), lambda b, i: (b, 0, i, 0))
    return pl.pallas_call(
        kern, out_shape=jax.ShapeDtypeStruct((B, TP, D), F32), grid=(B, TP // tm),
        in_specs=[pl.BlockSpec((1, tm, D), lambda b, i: (b, i, 0)),
                  pl.BlockSpec((1, RWKV_HEADS, tm, HEAD_DIM), lambda b, i: (b, 0, i, 0)),
                  pair_blk(2), pair_blk(1), pair_blk(2), pair_blk(1),
                  pl.BlockSpec((1, tm, 3 * D), lambda b, i: (b, i, 0)),
                  _const_spec((RWKV_HEADS, HEAD_DIM, D)),
                  _const_spec((PAIRS, LANE, D)),
                  _const_spec((PAIRS, LANE, D)),
                  _const_spec((D, D))],
        out_specs=pl.BlockSpec((1, tm, D), lambda b, i: (b, i, 0)),
        compiler_params=pltpu.CompilerParams(
            dimension_semantics=("arbitrary", "arbitrary"), vmem_limit_bytes=VMEM_LIMIT),
        name="merge",
    )(h, ya, yb2, yb1, yc2, yc1, gates, wpa, wpb, wpc, wout)


def _moe_kernel(h_ref, g_ref, wr_ref, br_ref, wg_ref, wu_ref, wd_ref, o_ref, he_ref,
                *, tm, n_valid):
    i = pl.program_id(1)
    x = h_ref[0]
    ms = jnp.mean(x * x, axis=-1, keepdims=True)
    n = x * lax.rsqrt(ms + NORM_EPS) * g_ref[...]

    lg = _dot(n, wr_ref[...], precision=HI) + br_ref[...]
    lane = lax.broadcasted_iota(jnp.int32, (tm, LANE), 1)
    big = jnp.int32(LANE)
    is_group = lane < N_GROUPS
    gl = jnp.where(is_group, lg, -jnp.inf)
    gmax = jnp.max(gl, axis=-1, keepdims=True)
    g_idx = jnp.min(jnp.where(is_group & (gl == gmax), lane, big), axis=-1, keepdims=True)
    p_group = 1.0 / jnp.sum(jnp.where(is_group, jnp.exp(gl - gmax), 0.0), axis=-1, keepdims=True)
    lo = N_GROUPS + g_idx * EXPERTS_PER_GROUP
    in_grp = (lane >= lo) & (lane < lo + EXPERTS_PER_GROUP)
    el = jnp.where(in_grp, lg, -jnp.inf)
    top1 = jnp.max(el, axis=-1, keepdims=True)
    i1 = jnp.min(jnp.where(in_grp & (el == top1), lane, big), axis=-1, keepdims=True)
    rest = in_grp & (lane != i1)
    el2 = jnp.where(rest, lg, -jnp.inf)
    top2 = jnp.max(el2, axis=-1, keepdims=True)
    i2 = jnp.min(jnp.where(rest & (el2 == top2), lane, big), axis=-1, keepdims=True)
    e2 = jnp.exp(top2 - top1)
    p1 = 1.0 / (1.0 + e2)
    p2 = e2 / (1.0 + e2)
    comb = p_group * (jnp.where(lane == i1, p1, 0.0) + jnp.where(lane == i2, p2, 0.0))

    n16 = n.astype(BF16)
    gate = _dot(n16, wg_ref[...])
    up = _dot(n16, wu_ref[...])
    he = gate * _sigmoid(gate) * up
    for e in range(N_EXPERTS):
        c_e = comb[:, N_GROUPS + e:N_GROUPS + e + 1]
        he_ref[:, e * D_EXPERT:(e + 1) * D_EXPERT] = (
            he[:, e * D_EXPERT:(e + 1) * D_EXPERT] * c_e).astype(BF16)
    out = x + _dot(he_ref[...], wd_ref[...])
    t = i * tm + lax.broadcasted_iota(jnp.int32, (tm, 1), 0)
    o_ref[0] = jnp.where((t >= PAD) & (t < PAD + n_valid), out, 0.0)


def _moe(h, g2, wr, br, wg, wu, wd, *, n_valid, tm=256):
    B, TP, D = h.shape
    kern = functools.partial(_moe_kernel, tm=tm, n_valid=n_valid)
    EW = N_EXPERTS * D_EXPERT
    return pl.pallas_call(
        kern, out_shape=jax.ShapeDtypeStruct((B, TP, D), F32), grid=(B, TP // tm),
        in_specs=[pl.BlockSpec((1, tm, D), lambda b, i: (b, i, 0)),
                  _const_spec((1, D)), _const_spec((D, LANE)), _const_spec((1, LANE)),
                  _const_spec((D, EW)), _const_spec((D, EW)), _const_spec((EW, D))],
        out_specs=pl.BlockSpec((1, tm, D), lambda b, i: (b, i, 0)),
        scratch_shapes=[pltpu.VMEM((tm, EW), BF16)],
        compiler_params=pltpu.CompilerParams(
            dimension_semantics=("arbitrary", "arbitrary"), vmem_limit_bytes=VMEM_LIMIT),
        name="moe",
    )(h, g2, wr, br, wg, wu, wd)


def _pad_pairs(w, heads):
    return jnp.pad(w, ((0, 0), (0, PAIRS * LANE - heads * HEAD_DIM)))


def _pack_w_in(w_in):
    D = w_in.shape[0]
    rest = w_in[:, SHIFT_W:]
    cols = [w_in[:, :SHIFT_W]]
    off = 0
    for heads in (SB_HEADS,) * 3 + (FOX_HEADS,) * 3:
        cols.append(_pad_pairs(rest[:, off:off + heads * HEAD_DIM], heads))
        off += heads * HEAD_DIM
    cols.append(jnp.pad(rest[:, off:off + FOX_HEADS], ((0, 0), (0, LANE - FOX_HEADS))))
    cols.append(rest[:, off + FOX_HEADS:])
    return jnp.concatenate(cols, axis=1).astype(BF16)


def _per_head_cols(w, heads):
    return w.reshape(w.shape[0], heads, HEAD_DIM).transpose(1, 0, 2)


def kernel(x, meta_tokens, norm1_g, w_in, rwkv_mu, rwkv_w_up, rwkv_w0, rwkv_a_up, rwkv_a0,
           rwkv_k_k, rwkv_k_a, rwkv_r_k, rwkv_ln_g, rwkv_ln_b, fox_f_b, fox_q_g, fox_k_g,
           w_p_rwkv, w_p_sb, w_p_fox, w_out, norm2_g, moe_wg, moe_bg, moe_we, moe_be,
           moe_w_gate, moe_w_up, moe_w_down):
    B, S, D = x.shape
    depth = w_in.shape[0]
    L = N_META + S
    TP = -(-(PAD + L) // ROW_TILE) * ROW_TILE
    meta = jnp.broadcast_to(meta_tokens[None].astype(x.dtype), (B, N_META, D))
    h = jnp.concatenate([jnp.zeros((B, PAD, D), x.dtype), meta, x,
                         jnp.zeros((B, TP - PAD - L, D), x.dtype)], axis=1)
    H = RWKV_HEADS
    EW = N_EXPERTS * D_EXPERT
    for l in range(depth):
        fb = jnp.zeros((1, LANE), F32).at[0, :FOX_HEADS].set(fox_f_b[l])
        rkv, wdad, sb, fx, fcum, gates = _inproj(
            h, norm1_g[l][None], _pack_w_in(w_in[l]), rwkv_mu[l][None], fb,
            jnp.tile(fox_q_g[l], 2)[None], jnp.tile(fox_k_g[l], 2)[None])
        hv = lambda p: p.reshape(H, 1, HEAD_DIM)
        gm, hc, r2, y0, bonus = _rwkv_chunks(
            rkv, wdad, _per_head_cols(rwkv_w_up[l], H), hv(rwkv_w0[l]),
            _per_head_cols(rwkv_a_up[l], H), hv(rwkv_a0[l]),
            hv(rwkv_k_k[l]), hv(rwkv_k_a[l]), hv(rwkv_r_k[l]))
        ya = _rwkv_scan(gm, hc, r2, y0, bonus, hv(rwkv_ln_g[l]), hv(rwkv_ln_b[l]))
        yb2 = _sb_attention(sb, slot0=0, nslots=2, nh=2)
        yb1 = _sb_attention(sb, slot0=2, nslots=1, nh=1)
        fcum_t = jnp.transpose(fcum[:, :, :FOX_HEADS], (0, 2, 1))[:, :, None, :]
        yc2 = _fox_attention(fx, fcum, fcum_t, slot0=0, nslots=2, nh=2)
        yc1 = _fox_attention(fx, fcum, fcum_t, slot0=2, nslots=1, nh=1)
        ph = lambda w, nh: w.reshape(nh, HEAD_DIM, D).astype(BF16)
        pp = lambda w, nh: jnp.pad(w, ((0, PAIRS * LANE - nh * HEAD_DIM), (0, 0))).reshape(
            PAIRS, LANE, D).astype(BF16)
        h = _merge(h, ya, yb2, yb1, yc2, yc1, gates, ph(w_p_rwkv[l], H), pp(w_p_sb[l], SB_HEADS),
                   pp(w_p_fox[l], FOX_HEADS), w_out[l].astype(BF16), n_valid=L)
        wr = jnp.zeros((D, LANE), F32).at[:, :N_GROUPS].set(moe_wg[l])
        wr = wr.at[:, N_GROUPS:N_GROUPS + N_EXPERTS].set(moe_we[l])
        br = jnp.zeros((1, LANE), F32).at[0, :N_GROUPS].set(moe_bg[l])
        br = br.at[0, N_GROUPS:N_GROUPS + N_EXPERTS].set(moe_be[l])
        wg = moe_w_gate[l].transpose(1, 0, 2).reshape(D, EW).astype(BF16)
        wu = moe_w_up[l].transpose(1, 0, 2).reshape(D, EW).astype(BF16)
        wd = moe_w_down[l].reshape(EW, D).astype(BF16)
        h = _moe(h, norm2_g[l][None], wr, br, wg, wu, wd, n_valid=L)
    return h[:, PAD + N_META:PAD + L]
```

```python
import functools
import math

import jax
import jax.numpy as jnp
from jax import lax
from jax.experimental import pallas as pl
from jax.experimental.pallas import tpu as pltpu

D_MODEL = 1024
HEAD_DIM = 64
N_META = 16
PAD = 128 - N_META
RWKV_HEADS = 6
SB_HEADS = 5
FOX_HEADS = 5
RWKV_W = RWKV_HEADS * HEAD_DIM
SB_W = SB_HEADS * HEAD_DIM
FOX_W = FOX_HEADS * HEAD_DIM
DECAY_RANK = 64
ICLR_RANK = 64
SHIFT_W = 3 * RWKV_W + DECAY_RANK + ICLR_RANK
N_GROUPS = 4
EXPERTS_PER_GROUP = 4
N_EXPERTS = 16
D_EXPERT = 256
NORM_EPS = 1e-6
RWKV_LN_EPS = 64e-5
NEG_INF = -1e30
ATT_SCALE = 1.0 / math.sqrt(HEAD_DIM)

LANE = 128
ROW_TILE = 512
PAIRS = 3
QKV_W = 3 * PAIRS * LANE
SEG_SB = SHIFT_W
SEG_FOX = SEG_SB + QKV_W
SEG_GATE = SEG_FOX + QKV_W + LANE
D_IN_PAD = SEG_GATE + 3 * D_MODEL
SB_CUTOFF = 110.0

CHUNK = 64
VMEM_LIMIT = 56 * 1024 * 1024

F32 = jnp.float32
BF16 = jnp.bfloat16
HI = lax.Precision.HIGHEST


def _log_sigmoid(x):
    return jnp.minimum(x, 0.0) - jnp.log1p(jnp.exp(-jnp.abs(x)))


def _sigmoid(x):
    return 1.0 / (1.0 + jnp.exp(-x))


def _dot(a, b, **kw):
    return jnp.dot(a, b, preferred_element_type=F32, **kw)


def _dot_nt(a, b):
    return lax.dot_general(a, b, (((1,), (1,)), ((), ())), preferred_element_type=F32)


def _dot_tn(a, b):
    return lax.dot_general(a, b, (((0,), (0,)), ((), ())), preferred_element_type=F32)


def _const_spec(shape):
    n = len(shape)
    return pl.BlockSpec(shape, lambda *_: (0,) * n, pipeline_mode=pl.Buffered(1))


def _inproj_kernel(h_ref, g_ref, w_ref, mu_ref, fb_ref, fqg_ref, fkg_ref,
                   rkv_ref, wdad_ref, sb_ref, fx_ref, fcum_ref, gate_ref,
                   carry_u, carry_f, *, tm):
    i = pl.program_id(1)

    @pl.when(i == 0)
    def _():
        carry_u[...] = jnp.zeros_like(carry_u)
        carry_f[...] = jnp.zeros_like(carry_f)

    x = h_ref[0]
    ms = jnp.mean(x * x, axis=-1, keepdims=True)
    n = (x * lax.rsqrt(ms + NORM_EPS) * g_ref[...]).astype(BF16)
    row = lax.broadcasted_iota(jnp.int32, (tm, 1), 0)

    us = _dot(n, w_ref[:, 0:SHIFT_W])
    prev = pltpu.roll(us, 1, axis=0)
    prev = jnp.where(row == 0, carry_u[...], prev)
    carry_u[...] = us[tm - 1:tm, :]
    ush = us + (prev - us) * mu_ref[...]
    for j in range(3 * PAIRS):
        rkv_ref[0, j] = ush[:, j * LANE:(j + 1) * LANE]
    wdad_ref[0] = ush[:, 3 * RWKV_W:SHIFT_W]

    usb = _dot(n, w_ref[:, SEG_SB:SEG_SB + QKV_W])
    for j in range(3 * PAIRS):
        piece = usb[:, j * LANE:(j + 1) * LANE]
        if j < PAIRS:
            piece = piece * ATT_SCALE
        sb_ref[0, j] = piece.astype(BF16)

    uf = _dot(n, w_ref[:, SEG_FOX:SEG_FOX + QKV_W + LANE])
    lane = lax.broadcasted_iota(jnp.int32, (tm, LANE), 1)
    first = lane < HEAD_DIM
    for j in range(3 * PAIRS):
        piece = uf[:, j * LANE:(j + 1) * LANE]
        if j < 2 * PAIRS:
            gain = fqg_ref[...] if j < PAIRS else fkg_ref[...]
            sq = piece * piece
            ms0 = jnp.sum(jnp.where(first, sq, 0.0), axis=-1, keepdims=True) * (1.0 / HEAD_DIM)
            ms1 = jnp.sum(jnp.where(first, 0.0, sq), axis=-1, keepdims=True) * (1.0 / HEAD_DIM)
            inv = jnp.where(first, lax.rsqrt(ms0 + NORM_EPS), lax.rsqrt(ms1 + NORM_EPS))
            piece = piece * inv * gain
            if j < PAIRS:
                piece = piece * ATT_SCALE
        fx_ref[0, j] = piece.astype(BF16)
    t_glob = i * tm + row
    logf = _log_sigmoid(uf[:, QKV_W:QKV_W + LANE] + fb_ref[...])
    logf = jnp.where((lane < FOX_HEADS) & (t_glob >= PAD), logf, 0.0)
    tri = (lax.broadcasted_iota(jnp.int32, (tm, tm), 0)
           >= lax.broadcasted_iota(jnp.int32, (tm, tm), 1)).astype(F32)
    cum = _dot(tri, logf, precision=HI) + carry_f[...]
    carry_f[...] = cum[tm - 1:tm, :]
    fcum_ref[0] = cum

    ug = _dot(n, w_ref[:, SEG_GATE:D_IN_PAD])
    gate_ref[0] = _sigmoid(ug).astype(BF16)


def _inproj(h, g1, w_in_p, mu, fb, fqg, fkg, *, tm=256):
    B, TP, D = h.shape
    nb = TP // tm
    kern = functools.partial(_inproj_kernel, tm=tm)
    out_shape = (
        jax.ShapeDtypeStruct((B, 3 * PAIRS, TP, LANE), F32),
        jax.ShapeDtypeStruct((B, TP, LANE), F32),
        jax.ShapeDtypeStruct((B, 3 * PAIRS, TP, LANE), BF16),
        jax.ShapeDtypeStruct((B, 3 * PAIRS, TP, LANE), BF16),
        jax.ShapeDtypeStruct((B, TP, LANE), F32),
        jax.ShapeDtypeStruct((B, TP, 3 * D_MODEL), BF16),
    )
    in_specs = [
        pl.BlockSpec((1, tm, D), lambda b, i: (b, i, 0)),
        _const_spec((1, D)),
        _const_spec((D, D_IN_PAD)),
        _const_spec((1, SHIFT_W)),
        _const_spec((1, LANE)),
        _const_spec((1, LANE)),
        _const_spec((1, LANE)),
    ]
    out_specs = (
        pl.BlockSpec((1, 3 * PAIRS, tm, LANE), lambda b, i: (b, 0, i, 0)),
        pl.BlockSpec((1, tm, LANE), lambda b, i: (b, i, 0)),
        pl.BlockSpec((1, 3 * PAIRS, tm, LANE), lambda b, i: (b, 0, i, 0)),
        pl.BlockSpec((1, 3 * PAIRS, tm, LANE), lambda b, i: (b, 0, i, 0)),
        pl.BlockSpec((1, tm, LANE), lambda b, i: (b, i, 0)),
        pl.BlockSpec((1, tm, 3 * D_MODEL), lambda b, i: (b, i, 0)),
    )
    return pl.pallas_call(
        kern, out_shape=out_shape, grid=(B, nb), in_specs=in_specs, out_specs=out_specs,
        scratch_shapes=[pltpu.VMEM((1, SHIFT_W), F32), pltpu.VMEM((1, LANE), F32)],
        compiler_params=pltpu.CompilerParams(
            dimension_semantics=("arbitrary", "arbitrary"), vmem_limit_bytes=VMEM_LIMIT),
        name="inproj",
    )(h, g1, w_in_p, mu, fb, fqg, fkg)


def _bdot(a, b):
    return lax.dot_general(a, b, (((2,), (1,)), ((0,), (0,))), preferred_element_type=F32)


def _bdot_nt(a, b):
    return lax.dot_general(a, b, (((2,), (2,)), ((0,), (0,))), preferred_element_type=F32)


def _bdot_tn(a, b):
    return lax.dot_general(a, b, (((1,), (1,)), ((0,), (0,))), preferred_element_type=F32)


def _head_sum(x, first):
    s0 = jnp.sum(jnp.where(first, x, 0.0), axis=-1, keepdims=True)
    s1 = jnp.sum(jnp.where(first, 0.0, x), axis=-1, keepdims=True)
    return jnp.where(first, s0, s1)


def _rwkv_chunk_kernel(r_ref, k_ref, v_ref, wdad_ref, wup_ref, w0_ref, aup_ref, a0_ref,
                       kk_ref, ka_ref, rk_ref,
                       g_ref, hc_ref, r2_ref, y0_ref, bonus_ref, *, rows):
    nc = rows // CHUNK
    r = r_ref[0, 0]
    k = k_ref[0, 0]
    v = v_ref[0, 0]
    wd = wdad_ref[0][:, 0:DECAY_RANK]
    ad = wdad_ref[0][:, DECAY_RANK:DECAY_RANK + ICLR_RANK]
    first = lax.broadcasted_iota(jnp.int32, (rows, LANE), 1) < HEAD_DIM

    pre = w0_ref[0] + _dot(jnp.tanh(wd), wup_ref[0], precision=HI)
    lw = -jnp.exp(_log_sigmoid(pre) - 0.5)
    iclr = _sigmoid(a0_ref[0] + _dot(ad, aup_ref[0], precision=HI))
    kk = k * kk_ref[0]
    kk = kk / jnp.maximum(jnp.sqrt(_head_sum(kk * kk, first)), 1e-12)
    k2 = k * (1.0 + (iclr - 1.0) * ka_ref[0])
    b = kk * iclr
    bonus_ref[0, 0] = _head_sum(r * k2 * rk_ref[0], first) * v

    to3 = lambda x: x.reshape(nc, CHUNK, LANE)
    ri = lax.broadcasted_iota(jnp.int32, (nc, CHUNK, CHUNK), 1)
    ci = lax.broadcasted_iota(jnp.int32, (nc, CHUNK, CHUNK), 2)
    low_incl = ri >= ci
    low_strict = ri > ci
    first3 = lax.broadcasted_iota(jnp.int32, (nc, CHUNK, LANE), 2) < HEAD_DIM

    lw3 = to3(lw)
    cum = lax.dot_general(low_incl.astype(F32), lw3, (((2,), (1,)), ((0,), (0,))),
                          preferred_element_type=F32, precision=HI)
    cum_end = cum[:, CHUNK - 1:CHUNK, :]
    e_neg = jnp.exp(-cum)
    at = to3(-kk) * jnp.exp(cum - lw3)
    rt = to3(r) * jnp.exp(cum)
    bt = (to3(b) * e_neg).astype(BF16)
    kt = (to3(k2) * e_neg).astype(BF16)
    e_rem = jnp.exp(cum_end - cum)
    bq = (to3(b) * e_rem).astype(BF16)
    kq = (to3(k2) * e_rem).astype(BF16)
    vv = to3(v).astype(BF16)

    xs, r2s, y0s = [], [], []
    for hd in range(2):
        sel = first3 if hd == 0 else jnp.logical_not(first3)
        lhs = jnp.concatenate([jnp.where(sel, at, 0.0), jnp.where(sel, rt, 0.0)],
                              axis=1).astype(BF16)
        mb = _bdot_nt(lhs, bt)
        mk = _bdot_nt(lhs, kt)
        m_ab = jnp.where(low_strict, mb[:, :CHUNK], 0.0)
        m_ak = jnp.where(low_strict, mk[:, :CHUNK], 0.0).astype(BF16)
        m_rb = jnp.where(low_incl, mb[:, CHUNK:], 0.0).astype(BF16)
        m_rk = jnp.where(low_incl, mk[:, CHUNK:], 0.0).astype(BF16)
        x = jnp.concatenate([at, _bdot(m_ak, vv)], axis=-1)
        p = m_ab
        for j in range(6):
            p16 = p.astype(BF16)
            x = x + _bdot(p16, x.astype(BF16))
            if j < 5:
                p = _bdot(p16, p16)
        ru = _bdot(m_rb, x.astype(BF16))
        xs.append(x)
        r2s.append(rt + ru[..., :LANE])
        y0s.append(ru[..., LANE:] + _bdot(m_rk, vv))

    first3w = jnp.concatenate([first3, first3], axis=-1)
    x = jnp.where(first3w, xs[0], xs[1])
    r2_ref[0, 0] = jnp.where(first3, r2s[0], r2s[1]).reshape(rows, LANE)
    y0_ref[0, 0] = jnp.where(first3, y0s[0], y0s[1]).reshape(rows, LANE)

    pg = _bdot_tn(bq, x.astype(BF16))
    ph = pg[..., LANE:] + _bdot_tn(kq, vv)
    rr = lax.broadcasted_iota(jnp.int32, (nc, LANE, LANE), 1)
    cc = lax.broadcasted_iota(jnp.int32, (nc, LANE, LANE), 2)
    same_head = (rr < HEAD_DIM) == (cc < HEAD_DIM)
    g = jnp.where(same_head, pg[..., :LANE], 0.0) + jnp.where(rr == cc, jnp.exp(cum_end), 0.0)
    g_ref[0, 0] = g.reshape(nc * LANE, LANE)
    hc_ref[0, 0] = jnp.where(same_head, ph, 0.0).reshape(nc * LANE, LANE)


def _rwkv_chunks(rkv, wdad, wup, w0, aup, a0, k_k, k_a, r_k, *, rows=256):
    B, _, TP, _ = rkv.shape
    nb = TP // rows
    kern = functools.partial(_rwkv_chunk_kernel, rows=rows)
    slot_spec = lambda off: pl.BlockSpec((1, 1, rows, LANE), lambda b, p, i: (b, p + off, i, 0))
    par_mat = pl.BlockSpec((1, DECAY_RANK, LANE), lambda b, p, i: (p, 0, 0))
    par_vec = pl.BlockSpec((1, 1, LANE), lambda b, p, i: (p, 0, 0))
    row_out = jax.ShapeDtypeStruct((B, PAIRS, TP, LANE), F32)
    mat_out = jax.ShapeDtypeStruct((B, PAIRS, 2 * TP, LANE), F32)
    row_spec = pl.BlockSpec((1, 1, rows, LANE), lambda b, p, i: (b, p, i, 0))
    mat_spec = pl.BlockSpec((1, 1, 2 * rows, LANE), lambda b, p, i: (b, p, i, 0))
    return pl.pallas_call(
        kern, out_shape=(mat_out, mat_out, row_out, row_out, row_out), grid=(B, PAIRS, nb),
        in_specs=[slot_spec(0), slot_spec(PAIRS), slot_spec(2 * PAIRS),
                  pl.BlockSpec((1, rows, LANE), lambda b, p, i: (b, i, 0)),
                  par_mat, par_vec, par_mat, par_vec, par_vec, par_vec, par_vec],
        out_specs=(mat_spec, mat_spec, row_spec, row_spec, row_spec),
        compiler_params=pltpu.CompilerParams(
            dimension_semantics=("arbitrary",) * 3, vmem_limit_bytes=VMEM_LIMIT),
        name="rwkv_chunks",
    )(rkv, rkv, rkv, wdad, wup, w0, aup, a0, k_k, k_a, r_k)


def _rwkv_scan_kernel(g_ref, hc_ref, r2_ref, y0_ref, bonus_ref, lng_ref, lnb_ref, y_ref,
                      state, *, rows):
    i = pl.program_id(1)

    @pl.when(i == 0)
    def _():
        state[...] = jnp.zeros_like(state)

    first = lax.broadcasted_iota(jnp.int32, (CHUNK, LANE), 1) < HEAD_DIM
    hs = [state[p] for p in range(PAIRS)]
    for c in range(rows // CHUNK):
        sl = slice(c * CHUNK, (c + 1) * CHUNK)
        sm = slice(c * LANE, (c + 1) * LANE)
        for p in range(PAIRS):
            y = _dot(r2_ref[0, p, sl, :], hs[p], precision=HI) + y0_ref[0, p, sl, :]
            hs[p] = _dot(g_ref[0, p, sm, :], hs[p], precision=HI) + hc_ref[0, p, sm, :]
            yc = y - _head_sum(y, first) * (1.0 / HEAD_DIM)
            var = _head_sum(yc * yc, first) * (1.0 / HEAD_DIM)
            out = yc * lax.rsqrt(var + RWKV_LN_EPS) * lng_ref[p] + lnb_ref[p]
            y_ref[0, p, sl, :] = (out + bonus_ref[0, p, sl, :]).astype(BF16)
    for p in range(PAIRS):
        state[p] = hs[p]


def _rwkv_scan(gm, hc, r2, y0, bonus, ln_g, ln_b, *, rows=512):
    B, P, TP, _ = r2.shape
    nb = TP // rows
    kern = functools.partial(_rwkv_scan_kernel, rows=rows)
    blk = pl.BlockSpec((1, P, rows, LANE), lambda b, i: (b, 0, i, 0))
    mat = pl.BlockSpec((1, P, 2 * rows, LANE), lambda b, i: (b, 0, i, 0))
    par = pl.BlockSpec((P, 1, LANE), lambda b, i: (0, 0, 0))
    return pl.pallas_call(
        kern, out_shape=jax.ShapeDtypeStruct((B, P, TP, LANE), BF16), grid=(B, nb),
        in_specs=[mat, mat, blk, blk, blk, par, par], out_specs=blk,
        scratch_shapes=[pltpu.VMEM((P, LANE, LANE), F32)],
        compiler_params=pltpu.CompilerParams(
            dimension_semantics=("arbitrary", "arbitrary"), vmem_limit_bytes=VMEM_LIMIT),
        name="rwkv_scan",
    )(gm, hc, r2, y0, bonus, ln_g, ln_b)


def _split_pair(qp, nh):
    lane = lax.broadcasted_iota(jnp.int32, qp.shape, 1)
    first = lane < HEAD_DIM
    zero = jnp.zeros_like(qp)
    qs = [jnp.where(first, qp, zero), jnp.where(first, zero, qp)]
    return qs[:nh], first


def _sb_kernel(q_ref, k_ref, v_ref, o_ref, *, t, nh):
    qi = pl.program_id(2)
    qs, first = _split_pair(q_ref[0, 0], nh)
    qpos = qi * t + lax.broadcasted_iota(jnp.int32, (t, 1), 0)
    kloc = lax.broadcasted_iota(jnp.int32, (1, t), 1)
    upper = (lax.broadcasted_iota(jnp.int32, (t, t), 0)
             > lax.broadcasted_iota(jnp.int32, (t, t), 1)).astype(BF16)

    def block(kb, accs, cs, causal):
        start = pl.multiple_of(kb * t, t)
        kblk = k_ref[0, 0, pl.ds(start, t), :]
        vblk = v_ref[0, 0, pl.ds(start, t), :]
        kpos = start + kloc
        mask = kpos >= PAD
        if causal:
            mask = mask & (kpos < qpos)
        new_accs, new_cs = [], []
        for h in range(nh):
            z = _dot_nt(qs[h], kblk)
            ls = jnp.minimum(z, 0.0) - jnp.log(1.0 + jnp.exp(-jnp.abs(z)))
            l1 = jnp.where(mask, ls - z, 0.0)
            tail = _dot(l1.astype(BF16), upper) + cs[h]
            a = jnp.where(mask, jnp.exp(ls + tail), 0.0)
            new_accs.append(accs[h] + _dot(a.astype(BF16), vblk))
            new_cs.append(cs[h] + jnp.sum(l1, axis=-1, keepdims=True))
        return new_accs, new_cs

    def live(cs):
        top = cs[0]
        for c in cs[1:]:
            top = jnp.maximum(top, c)
        return (jnp.max(top) > -SB_CUTOFF).astype(jnp.int32)

    zero_acc = [jnp.zeros((t, LANE), F32) for _ in range(nh)]
    zero_c = [jnp.zeros((t, 1), F32) for _ in range(nh)]
    accs, cs = block(qi, zero_acc, zero_c, True)

    def cond(carry):
        kb, alive = carry[0], carry[1]
        return (kb >= 0) & (alive > 0)

    def body(carry):
        kb = carry[0]
        accs, cs = block(kb, list(carry[2:2 + nh]), list(carry[2 + nh:]), False)
        return (kb - 1, live(cs), *accs, *cs)

    carry = lax.while_loop(cond, body, (qi - 1, live(cs), *accs, *cs))
    accs = carry[2:2 + nh]
    out = jnp.where(first, accs[0], accs[1] if nh == 2 else 0.0)
    o_ref[0, 0] = out.astype(BF16)


def _sb_attention(sb, *, slot0, nslots, nh, t=256):
    B, _, TP, _ = sb.shape
    kern = functools.partial(_sb_kernel, t=t, nh=nh)
    blk = lambda off: pl.BlockSpec((1, 1, t, LANE), lambda b, p, i: (b, slot0 + off + p, i, 0))
    full = lambda off: pl.BlockSpec((1, 1, TP, LANE), lambda b, p, i: (b, slot0 + off + p, 0, 0))
    return pl.pallas_call(
        kern, out_shape=jax.ShapeDtypeStruct((B, nslots, TP, LANE), BF16),
        grid=(B, nslots, TP // t),
        in_specs=[blk(0), full(PAIRS), full(2 * PAIRS)],
        out_specs=pl.BlockSpec((1, 1, t, LANE), lambda b, p, i: (b, p, i, 0)),
        compiler_params=pltpu.CompilerParams(
            dimension_semantics=("arbitrary",) * 3, vmem_limit_bytes=VMEM_LIMIT),
        name="sb_attention",
    )(sb, sb, sb)


def _fox_kernel(q_ref, k_ref, v_ref, fq_ref, *rest, t, nh, head0):
    fk_refs, o_ref = rest[:nh], rest[nh]
    p = pl.program_id(1)
    qi = pl.program_id(2)
    qs, first = _split_pair(q_ref[0, 0], nh)
    lane = lax.broadcasted_iota(jnp.int32, (t, LANE), 1)
    fqs = [jnp.sum(jnp.where(lane == head0 + 2 * p + h, fq_ref[0], 0.0), axis=-1, keepdims=True)
           for h in range(nh)]
    qpos = qi * t + lax.broadcasted_iota(jnp.int32, (t, 1), 0)
    kloc = lax.broadcasted_iota(jnp.int32, (1, t), 1)

    def block(kb, carry, masked):
        start = pl.multiple_of(kb * t, t)
        kblk = k_ref[0, 0, pl.ds(start, t), :]
        vblk = v_ref[0, 0, pl.ds(start, t), :]
        if masked:
            kpos = start + kloc
            mask = (kpos <= qpos) & (kpos >= PAD)
        out = []
        for h in range(nh):
            acc, m, l = carry[3 * h:3 * h + 3]
            fk = fk_refs[h][0, 0, :, pl.ds(start, t)]
            s = _dot_nt(qs[h], kblk) + fqs[h] - fk
            if masked:
                s = jnp.where(mask, s, NEG_INF)
            m_new = jnp.maximum(m, jnp.max(s, axis=-1, keepdims=True))
            alpha = jnp.exp(m - m_new)
            pr = jnp.exp(s - m_new)
            l = alpha * l + jnp.sum(pr, axis=-1, keepdims=True)
            acc = alpha * acc + _dot(pr.astype(BF16), vblk)
            out += [acc, m_new, l]
        return tuple(out)

    init = []
    for h in range(nh):
        init += [jnp.zeros((t, LANE), F32), jnp.full((t, 1), NEG_INF, F32), jnp.zeros((t, 1), F32)]
    carry = block(0, tuple(init), True)
    carry = lax.fori_loop(1, qi, lambda kb, c: block(kb, c, False), carry)
    carry = lax.cond(qi > 0, lambda c: block(qi, c, True), lambda c: c, carry)
    outs = [carry[3 * h] / carry[3 * h + 2] for h in range(nh)]
    out = jnp.where(first, outs[0], outs[1] if nh == 2 else 0.0)
    o_ref[0, 0] = out.astype(BF16)


def _fox_attention(fx, fcum, fcum_t, *, slot0, nslots, nh, t=512):
    B, _, TP, _ = fx.shape
    head0 = 2 * slot0
    kern = functools.partial(_fox_kernel, t=t, nh=nh, head0=head0)
    blk = lambda off: pl.BlockSpec((1, 1, t, LANE), lambda b, p, i: (b, slot0 + off + p, i, 0))
    full = lambda off: pl.BlockSpec((1, 1, TP, LANE), lambda b, p, i: (b, slot0 + off + p, 0, 0))
    fk_spec = lambda h: pl.BlockSpec((1, 1, 1, TP), lambda b, p, i: (b, head0 + 2 * p + h, 0, 0))
    return pl.pallas_call(
        kern, out_shape=jax.ShapeDtypeStruct((B, nslots, TP, LANE), BF16),
        grid=(B, nslots, TP // t),
        in_specs=[blk(0), full(PAIRS), full(2 * PAIRS),
                  pl.BlockSpec((1, t, LANE), lambda b, p, i: (b, i, 0))]
                 + [fk_spec(h) for h in range(nh)],
        out_specs=pl.BlockSpec((1, 1, t, LANE), lambda b, p, i: (b, p, i, 0)),
        compiler_params=pltpu.CompilerParams(
            dimension_semantics=("arbitrary",) * 3, vmem_limit_bytes=VMEM_LIMIT),
        name="fox_attention",
    )(fx, fx, fx, fcum, *([fcum_t] * nh))


def _merge_kernel(h_ref, ya_ref, yb2_ref, yb1_ref, yc2_ref, yc1_ref, gate_ref,
                  wpa_ref, wpb_ref, wpc_ref, wout_ref, o_ref, *, tm, n_valid):
    i = pl.program_id(1)

    pa = _dot(ya_ref[0, 0], wpa_ref[0])
    for hh in range(1, PAIRS):
        pa = pa + _dot(ya_ref[0, hh], wpa_ref[hh])

    def pair_proj(y2_ref, y1_ref, w_ref):
        return (_dot(y2_ref[0, 0], w_ref[0]) + _dot(y2_ref[0, 1], w_ref[1])
                + _dot(y1_ref[0, 0], w_ref[2]))

    merged = gate_ref[0, :, 0:D_MODEL].astype(F32) * pa
    merged = merged + (gate_ref[0, :, D_MODEL:2 * D_MODEL].astype(F32)
                       * pair_proj(yb2_ref, yb1_ref, wpb_ref))
    merged = merged + (gate_ref[0, :, 2 * D_MODEL:3 * D_MODEL].astype(F32)
                       * pair_proj(yc2_ref, yc1_ref, wpc_ref))
    out = h_ref[0] + _dot(merged.astype(BF16), wout_ref[...])
    t = i * tm + lax.broadcasted_iota(jnp.int32, (tm, 1), 0)
    o_ref[0] = jnp.where((t >= PAD) & (t < PAD + n_valid), out, 0.0)


def _merge(h, ya, yb2, yb1, yc2, yc1, gates, wpa, wpb, wpc, wout, *, n_valid, tm=256):
    B, TP, D = h.shape
    kern = functools.partial(_merge_kernel, tm=tm, n_valid=n_valid)
    pair_blk = lambda ns: pl.BlockSpec((1, ns, tm, LANE), lambda b, i: (b, 0, i, 0))
    return pl.pallas_call(
        kern, out_shape=jax.ShapeDtypeStruct((B, TP, D), F32), grid=(B, TP // tm),
        in_specs=[pl.BlockSpec((1, tm, D), lambda b, i: (b, i, 0)),
                  pair_blk(PAIRS),
                  pair_blk(2), pair_blk(1), pair_blk(2), pair_blk(1),
                  pl.BlockSpec((1, tm, 3 * D), lambda b, i: (b, i, 0)),
                  _const_spec((PAIRS, LANE, D)),
                  _const_spec((PAIRS, LANE, D)),
                  _const_spec((PAIRS, LANE, D)),
                  _const_spec((D, D))],
        out_specs=pl.BlockSpec((1, tm, D), lambda b, i: (b, i, 0)),
        compiler_params=pltpu.CompilerParams(
            dimension_semantics=("arbitrary", "arbitrary"), vmem_limit_bytes=VMEM_LIMIT),
        name="merge",
    )(h, ya, yb2, yb1, yc2, yc1, gates, wpa, wpb, wpc, wout)


def _moe_kernel(h_ref, g_ref, wr_ref, br_ref, wg_ref, wu_ref, wd_ref, o_ref, he_ref,
                *, tm, n_valid):
    i = pl.program_id(1)
    x = h_ref[0]
    ms = jnp.mean(x * x, axis=-1, keepdims=True)
    n = x * lax.rsqrt(ms + NORM_EPS) * g_ref[...]

    lg = _dot(n, wr_ref[...], precision=HI) + br_ref[...]
    lane = lax.broadcasted_iota(jnp.int32, (tm, LANE), 1)
    big = jnp.int32(LANE)
    is_group = lane < N_GROUPS
    gl = jnp.where(is_group, lg, -jnp.inf)
    gmax = jnp.max(gl, axis=-1, keepdims=True)
    g_idx = jnp.min(jnp.where(is_group & (gl == gmax), lane, big), axis=-1, keepdims=True)
    p_group = 1.0 / jnp.sum(jnp.where(is_group, jnp.exp(gl - gmax), 0.0), axis=-1, keepdims=True)
    lo = N_GROUPS + g_idx * EXPERTS_PER_GROUP
    in_grp = (lane >= lo) & (lane < lo + EXPERTS_PER_GROUP)
    el = jnp.where(in_grp, lg, -jnp.inf)
    top1 = jnp.max(el, axis=-1, keepdims=True)
    i1 = jnp.min(jnp.where(in_grp & (el == top1), lane, big), axis=-1, keepdims=True)
    rest = in_grp & (lane != i1)
    el2 = jnp.where(rest, lg, -jnp.inf)
    top2 = jnp.max(el2, axis=-1, keepdims=True)
    i2 = jnp.min(jnp.where(rest & (el2 == top2), lane, big), axis=-1, keepdims=True)
    e2 = jnp.exp(top2 - top1)
    p1 = 1.0 / (1.0 + e2)
    p2 = e2 / (1.0 + e2)
    comb = p_group * (jnp.where(lane == i1, p1, 0.0) + jnp.where(lane == i2, p2, 0.0))

    n16 = n.astype(BF16)
    gate = _dot(n16, wg_ref[...])
    up = _dot(n16, wu_ref[...])
    he = gate * _sigmoid(gate) * up
    for e in range(N_EXPERTS):
        c_e = comb[:, N_GROUPS + e:N_GROUPS + e + 1]
        he_ref[:, e * D_EXPERT:(e + 1) * D_EXPERT] = (
            he[:, e * D_EXPERT:(e + 1) * D_EXPERT] * c_e).astype(BF16)
    out = x + _dot(he_ref[...], wd_ref[...])
    t = i * tm + lax.broadcasted_iota(jnp.int32, (tm, 1), 0)
    o_ref[0] = jnp.where((t >= PAD) & (t < PAD + n_valid), out, 0.0)


def _moe(h, g2, wr, br, wg, wu, wd, *, n_valid, tm=256):
    B, TP, D = h.shape
    kern = functools.partial(_moe_kernel, tm=tm, n_valid=n_valid)
    EW = N_EXPERTS * D_EXPERT
    return pl.pallas_call(
        kern, out_shape=jax.ShapeDtypeStruct((B, TP, D), F32), grid=(B, TP // tm),
        in_specs=[pl.BlockSpec((1, tm, D), lambda b, i: (b, i, 0)),
                  _const_spec((1, D)), _const_spec((D, LANE)), _const_spec((1, LANE)),
                  _const_spec((D, EW)), _const_spec((D, EW)), _const_spec((EW, D))],
        out_specs=pl.BlockSpec((1, tm, D), lambda b, i: (b, i, 0)),
        scratch_shapes=[pltpu.VMEM((tm, EW), BF16)],
        compiler_params=pltpu.CompilerParams(
            dimension_semantics=("arbitrary", "arbitrary"), vmem_limit_bytes=VMEM_LIMIT),
        name="moe",
    )(h, g2, wr, br, wg, wu, wd)


def _pad_pairs(w, heads):
    return jnp.pad(w, ((0, 0), (0, PAIRS * LANE - heads * HEAD_DIM)))


def _pack_w_in(w_in):
    D = w_in.shape[0]
    rest = w_in[:, SHIFT_W:]
    cols = [w_in[:, :SHIFT_W]]
    off = 0
    for heads in (SB_HEADS,) * 3 + (FOX_HEADS,) * 3:
        cols.append(_pad_pairs(rest[:, off:off + heads * HEAD_DIM], heads))
        off += heads * HEAD_DIM
    cols.append(jnp.pad(rest[:, off:off + FOX_HEADS], ((0, 0), (0, LANE - FOX_HEADS))))
    cols.append(rest[:, off + FOX_HEADS:])
    return jnp.concatenate(cols, axis=1).astype(BF16)


def _per_pair_cols(w):
    return w.reshape(w.shape[0], PAIRS, LANE).transpose(1, 0, 2)


def kernel(x, meta_tokens, norm1_g, w_in, rwkv_mu, rwkv_w_up, rwkv_w0, rwkv_a_up, rwkv_a0,
           rwkv_k_k, rwkv_k_a, rwkv_r_k, rwkv_ln_g, rwkv_ln_b, fox_f_b, fox_q_g, fox_k_g,
           w_p_rwkv, w_p_sb, w_p_fox, w_out, norm2_g, moe_wg, moe_bg, moe_we, moe_be,
           moe_w_gate, moe_w_up, moe_w_down):
    B, S, D = x.shape
    depth = w_in.shape[0]
    L = N_META + S
    TP = -(-(PAD + L) // ROW_TILE) * ROW_TILE
    meta = jnp.broadcast_to(meta_tokens[None].astype(x.dtype), (B, N_META, D))
    h = jnp.concatenate([jnp.zeros((B, PAD, D), x.dtype), meta, x,
                         jnp.zeros((B, TP - PAD - L, D), x.dtype)], axis=1)
    H = RWKV_HEADS
    EW = N_EXPERTS * D_EXPERT
    for l in range(depth):
        fb = jnp.zeros((1, LANE), F32).at[0, :FOX_HEADS].set(fox_f_b[l])
        rkv, wdad, sb, fx, fcum, gates = _inproj(
            h, norm1_g[l][None], _pack_w_in(w_in[l]), rwkv_mu[l][None], fb,
            jnp.tile(fox_q_g[l], 2)[None], jnp.tile(fox_k_g[l], 2)[None])
        hv = lambda p: p.reshape(PAIRS, 1, LANE)
        gm, hc, r2, y0, bonus = _rwkv_chunks(
            rkv, wdad, _per_pair_cols(rwkv_w_up[l]), hv(rwkv_w0[l]),
            _per_pair_cols(rwkv_a_up[l]), hv(rwkv_a0[l]),
            hv(rwkv_k_k[l]), hv(rwkv_k_a[l]), hv(rwkv_r_k[l]))
        ya = _rwkv_scan(gm, hc, r2, y0, bonus, hv(rwkv_ln_g[l]), hv(rwkv_ln_b[l]))
        yb2 = _sb_attention(sb, slot0=0, nslots=2, nh=2)
        yb1 = _sb_attention(sb, slot0=2, nslots=1, nh=1)
        fcum_t = jnp.transpose(fcum[:, :, :FOX_HEADS], (0, 2, 1))[:, :, None, :]
        yc2 = _fox_attention(fx, fcum, fcum_t, slot0=0, nslots=2, nh=2)
        yc1 = _fox_attention(fx, fcum, fcum_t, slot0=2, nslots=1, nh=1)
        pp = lambda w, nh: jnp.pad(w, ((0, PAIRS * LANE - nh * HEAD_DIM), (0, 0))).reshape(
            PAIRS, LANE, D).astype(BF16)
        h = _merge(h, ya, yb2, yb1, yc2, yc1, gates, pp(w_p_rwkv[l], H), pp(w_p_sb[l], SB_HEADS),
                   pp(w_p_fox[l], FOX_HEADS), w_out[l].astype(BF16), n_valid=L)
        wr = jnp.zeros((D, LANE), F32).at[:, :N_GROUPS].set(moe_wg[l])
        wr = wr.at[:, N_GROUPS:N_GROUPS + N_EXPERTS].set(moe_we[l])
        br = jnp.zeros((1, LANE), F32).at[0, :N_GROUPS].set(moe_bg[l])
        br = br.at[0, N_GROUPS:N_GROUPS + N_EXPERTS].set(moe_be[l])
        wg = moe_w_gate[l].transpose(1, 0, 2).reshape(D, EW).astype(BF16)
        wu = moe_w_up[l].transpose(1, 0, 2).reshape(D, EW).astype(BF16)
        wd = moe_w_down[l].reshape(EW, D).astype(BF16)
        h = _moe(h, norm2_g[l][None], wr, br, wg, wu, wd, n_valid=L)
    return h[:, PAD + N_META:PAD + L]
```

```python
import functools
import math

import jax
import jax.numpy as jnp
from jax import lax
from jax.experimental import pallas as pl
from jax.experimental.pallas import tpu as pltpu

D_MODEL = 1024
HEAD_DIM = 64
N_META = 16
PAD = 128 - N_META
RWKV_HEADS = 6
SB_HEADS = 5
FOX_HEADS = 5
RWKV_W = RWKV_HEADS * HEAD_DIM
SB_W = SB_HEADS * HEAD_DIM
FOX_W = FOX_HEADS * HEAD_DIM
DECAY_RANK = 64
ICLR_RANK = 64
SHIFT_W = 3 * RWKV_W + DECAY_RANK + ICLR_RANK
N_GROUPS = 4
EXPERTS_PER_GROUP = 4
N_EXPERTS = 16
D_EXPERT = 256
NORM_EPS = 1e-6
RWKV_LN_EPS = 64e-5
NEG_INF = -1e30
ATT_SCALE = 1.0 / math.sqrt(HEAD_DIM)

LANE = 128
ROW_TILE = 512
PAIRS = 3
QKV_W = 3 * PAIRS * LANE
SEG_SB = SHIFT_W
SEG_FOX = SEG_SB + QKV_W
SEG_GATE = SEG_FOX + QKV_W + LANE
D_IN_PAD = SEG_GATE + 3 * D_MODEL
SB_CUTOFF = 110.0
LOG2E = 1.4426950408889634
FOX_FIXED_SHIFT_MAX = 50.0

CHUNK = 64
VMEM_LIMIT = 56 * 1024 * 1024

F32 = jnp.float32
BF16 = jnp.bfloat16
HI = lax.Precision.HIGHEST


def _log_sigmoid(x):
    return jnp.minimum(x, 0.0) - jnp.log1p(jnp.exp(-jnp.abs(x)))


def _sigmoid(x):
    return 1.0 / (1.0 + jnp.exp(-x))


def _dot(a, b, **kw):
    return jnp.dot(a, b, preferred_element_type=F32, **kw)


def _dot_nt(a, b):
    return lax.dot_general(a, b, (((1,), (1,)), ((), ())), preferred_element_type=F32)


def _dot_tn(a, b):
    return lax.dot_general(a, b, (((0,), (0,)), ((), ())), preferred_element_type=F32)


def _const_spec(shape):
    n = len(shape)
    return pl.BlockSpec(shape, lambda *_: (0,) * n, pipeline_mode=pl.Buffered(1))


def _inproj_kernel(h_ref, g_ref, w_ref, mu_ref, fb_ref, fqg_ref, fkg_ref,
                   rkv_ref, wdad_ref, sb_ref, fx_ref, gate_ref,
                   carry_u, carry_f, *, tm):
    i = pl.program_id(1)

    @pl.when(i == 0)
    def _():
        carry_u[...] = jnp.zeros_like(carry_u)
        carry_f[...] = jnp.zeros_like(carry_f)

    x = h_ref[0]
    ms = jnp.mean(x * x, axis=-1, keepdims=True)
    n = (x * lax.rsqrt(ms + NORM_EPS) * g_ref[...]).astype(BF16)
    row = lax.broadcasted_iota(jnp.int32, (tm, 1), 0)

    us = _dot(n, w_ref[:, 0:SHIFT_W])
    prev = pltpu.roll(us, 1, axis=0)
    prev = jnp.where(row == 0, carry_u[...], prev)
    carry_u[...] = us[tm - 1:tm, :]
    ush = us + (prev - us) * mu_ref[...]
    for j in range(3 * PAIRS):
        rkv_ref[0, j] = ush[:, j * LANE:(j + 1) * LANE]
    wdad_ref[0] = ush[:, 3 * RWKV_W:SHIFT_W]

    usb = _dot(n, w_ref[:, SEG_SB:SEG_SB + QKV_W])
    for j in range(3 * PAIRS):
        piece = usb[:, j * LANE:(j + 1) * LANE]
        if j < PAIRS:
            piece = piece * ATT_SCALE
        sb_ref[0, j] = piece.astype(BF16)

    uf = _dot(n, w_ref[:, SEG_FOX:SEG_FOX + QKV_W + LANE])
    lane = lax.broadcasted_iota(jnp.int32, (tm, LANE), 1)
    first = lane < HEAD_DIM
    t_glob = i * tm + row
    logf = _log_sigmoid(uf[:, QKV_W:QKV_W + LANE] + fb_ref[...])
    logf = jnp.where((lane < FOX_HEADS) & (t_glob >= PAD), logf, 0.0)
    tri = (lax.broadcasted_iota(jnp.int32, (tm, tm), 0)
           >= lax.broadcasted_iota(jnp.int32, (tm, tm), 1)).astype(F32)
    cum = _dot(tri, logf, precision=HI) + carry_f[...]
    carry_f[...] = cum[tm - 1:tm, :]

    def split3(hd):
        f2 = jnp.sum(jnp.where(lane == hd, cum, 0.0), axis=-1, keepdims=True) * LOG2E
        hi = f2.astype(BF16).astype(F32)
        mid = (f2 - hi).astype(BF16).astype(F32)
        return hi, mid, f2 - hi - mid

    def tail_cols(vals):
        out = jnp.zeros((tm, LANE), F32)
        for o, val in enumerate(vals):
            out = jnp.where(lane == HEAD_DIM + o, val, out)
        return out

    splits = [split3(hd) for hd in range(FOX_HEADS)]
    for j in range(3 * PAIRS):
        piece = uf[:, j * LANE:(j + 1) * LANE]
        kind = j // PAIRS
        if kind < 2:
            gain = fqg_ref[...] if kind == 0 else fkg_ref[...]
            sq = piece * piece
            ms0 = jnp.sum(jnp.where(first, sq, 0.0), axis=-1, keepdims=True) * (1.0 / HEAD_DIM)
            ms1 = jnp.sum(jnp.where(first, 0.0, sq), axis=-1, keepdims=True) * (1.0 / HEAD_DIM)
            inv = jnp.where(first, lax.rsqrt(ms0 + NORM_EPS), lax.rsqrt(ms1 + NORM_EPS))
            piece = piece * inv * gain
            if kind == 0:
                piece = piece * (ATT_SCALE * LOG2E)
        swapped = pltpu.roll(piece, HEAD_DIM, axis=1)
        for half in range(2):
            hd = 2 * (j % PAIRS) + half
            if hd >= FOX_HEADS:
                continue
            if kind == 2:
                extra = tail_cols([1.0])
            else:
                hi, mid, lo = splits[hd]
                extra = tail_cols([hi, mid, lo, 1.0, 1.0, 1.0] if kind == 0
                                  else [1.0, 1.0, 1.0, -hi, -mid, -lo])
            body = piece if half == 0 else swapped
            fx_ref[0, kind * FOX_HEADS + hd] = jnp.where(first, body, extra).astype(BF16)

    ug = _dot(n, w_ref[:, SEG_GATE:D_IN_PAD])
    gate_ref[0] = _sigmoid(ug).astype(BF16)


def _inproj(h, g1, w_in_p, mu, fb, fqg, fkg, *, tm=256):
    B, TP, D = h.shape
    nb = TP // tm
    kern = functools.partial(_inproj_kernel, tm=tm)
    out_shape = (
        jax.ShapeDtypeStruct((B, 3 * PAIRS, TP, LANE), F32),
        jax.ShapeDtypeStruct((B, TP, LANE), F32),
        jax.ShapeDtypeStruct((B, 3 * PAIRS, TP, LANE), BF16),
        jax.ShapeDtypeStruct((B, 3 * FOX_HEADS, TP, LANE), BF16),
        jax.ShapeDtypeStruct((B, TP, 3 * D_MODEL), BF16),
    )
    in_specs = [
        pl.BlockSpec((1, tm, D), lambda b, i: (b, i, 0)),
        _const_spec((1, D)),
        _const_spec((D, D_IN_PAD)),
        _const_spec((1, SHIFT_W)),
        _const_spec((1, LANE)),
        _const_spec((1, LANE)),
        _const_spec((1, LANE)),
    ]
    out_specs = (
        pl.BlockSpec((1, 3 * PAIRS, tm, LANE), lambda b, i: (b, 0, i, 0)),
        pl.BlockSpec((1, tm, LANE), lambda b, i: (b, i, 0)),
        pl.BlockSpec((1, 3 * PAIRS, tm, LANE), lambda b, i: (b, 0, i, 0)),
        pl.BlockSpec((1, 3 * FOX_HEADS, tm, LANE), lambda b, i: (b, 0, i, 0)),
        pl.BlockSpec((1, tm, 3 * D_MODEL), lambda b, i: (b, i, 0)),
    )
    return pl.pallas_call(
        kern, out_shape=out_shape, grid=(B, nb), in_specs=in_specs, out_specs=out_specs,
        scratch_shapes=[pltpu.VMEM((1, SHIFT_W), F32), pltpu.VMEM((1, LANE), F32)],
        compiler_params=pltpu.CompilerParams(
            dimension_semantics=("arbitrary", "arbitrary"), vmem_limit_bytes=VMEM_LIMIT),
        name="inproj",
    )(h, g1, w_in_p, mu, fb, fqg, fkg)


def _bdot(a, b):
    return lax.dot_general(a, b, (((2,), (1,)), ((0,), (0,))), preferred_element_type=F32)


def _bdot_nt(a, b):
    return lax.dot_general(a, b, (((2,), (2,)), ((0,), (0,))), preferred_element_type=F32)


def _bdot_tn(a, b):
    return lax.dot_general(a, b, (((1,), (1,)), ((0,), (0,))), preferred_element_type=F32)


def _head_sum(x, first):
    s0 = jnp.sum(jnp.where(first, x, 0.0), axis=-1, keepdims=True)
    s1 = jnp.sum(jnp.where(first, 0.0, x), axis=-1, keepdims=True)
    return jnp.where(first, s0, s1)


def _rwkv_chunk_kernel(r_ref, k_ref, v_ref, wdad_ref, wup_ref, w0_ref, aup_ref, a0_ref,
                       kk_ref, ka_ref, rk_ref,
                       g_ref, hc_ref, r2_ref, y0_ref, bonus_ref, *, rows):
    nc = rows // CHUNK
    r = r_ref[0, 0]
    k = k_ref[0, 0]
    v = v_ref[0, 0]
    wd = wdad_ref[0][:, 0:DECAY_RANK]
    ad = wdad_ref[0][:, DECAY_RANK:DECAY_RANK + ICLR_RANK]
    first = lax.broadcasted_iota(jnp.int32, (rows, LANE), 1) < HEAD_DIM

    pre = w0_ref[0] + _dot(jnp.tanh(wd), wup_ref[0], precision=HI)
    lw = -jnp.exp(_log_sigmoid(pre) - 0.5)
    iclr = _sigmoid(a0_ref[0] + _dot(ad, aup_ref[0], precision=HI))
    kk = k * kk_ref[0]
    kk = kk / jnp.maximum(jnp.sqrt(_head_sum(kk * kk, first)), 1e-12)
    k2 = k * (1.0 + (iclr - 1.0) * ka_ref[0])
    b = kk * iclr
    bonus_ref[0, 0] = _head_sum(r * k2 * rk_ref[0], first) * v

    to3 = lambda x: x.reshape(nc, CHUNK, LANE)
    ri = lax.broadcasted_iota(jnp.int32, (nc, CHUNK, CHUNK), 1)
    ci = lax.broadcasted_iota(jnp.int32, (nc, CHUNK, CHUNK), 2)
    low_incl = ri >= ci
    low_strict = ri > ci
    first3 = lax.broadcasted_iota(jnp.int32, (nc, CHUNK, LANE), 2) < HEAD_DIM

    lw3 = to3(lw)
    cum = lax.dot_general(low_incl.astype(F32), lw3, (((2,), (1,)), ((0,), (0,))),
                          preferred_element_type=F32, precision=HI)
    cum_end = cum[:, CHUNK - 1:CHUNK, :]
    e_neg = jnp.exp(-cum)
    at = to3(-kk) * jnp.exp(cum - lw3)
    rt = to3(r) * jnp.exp(cum)
    bt = (to3(b) * e_neg).astype(BF16)
    kt = (to3(k2) * e_neg).astype(BF16)
    e_rem = jnp.exp(cum_end - cum)
    bq = (to3(b) * e_rem).astype(BF16)
    kq = (to3(k2) * e_rem).astype(BF16)
    vv = to3(v).astype(BF16)

    xs, r2s, y0s = [], [], []
    for hd in range(2):
        sel = first3 if hd == 0 else jnp.logical_not(first3)
        lhs = jnp.concatenate([jnp.where(sel, at, 0.0), jnp.where(sel, rt, 0.0)],
                              axis=1).astype(BF16)
        mb = _bdot_nt(lhs, bt)
        mk = _bdot_nt(lhs, kt)
        m_ab = jnp.where(low_strict, mb[:, :CHUNK], 0.0)
        m_ak = jnp.where(low_strict, mk[:, :CHUNK], 0.0).astype(BF16)
        m_rb = jnp.where(low_incl, mb[:, CHUNK:], 0.0).astype(BF16)
        m_rk = jnp.where(low_incl, mk[:, CHUNK:], 0.0).astype(BF16)
        x = jnp.concatenate([at, _bdot(m_ak, vv)], axis=-1)
        p = m_ab
        for j in range(6):
            p16 = p.astype(BF16)
            x = x + _bdot(p16, x.astype(BF16))
            if j < 5:
                p = _bdot(p16, p16)
        ru = _bdot(m_rb, x.astype(BF16))
        xs.append(x)
        r2s.append(rt + ru[..., :LANE])
        y0s.append(ru[..., LANE:] + _bdot(m_rk, vv))

    first3w = jnp.concatenate([first3, first3], axis=-1)
    x = jnp.where(first3w, xs[0], xs[1])
    r2_ref[0, 0] = jnp.where(first3, r2s[0], r2s[1]).reshape(rows, LANE)
    y0_ref[0, 0] = jnp.where(first3, y0s[0], y0s[1]).reshape(rows, LANE)

    pg = _bdot_tn(bq, x.astype(BF16))
    ph = pg[..., LANE:] + _bdot_tn(kq, vv)
    rr = lax.broadcasted_iota(jnp.int32, (nc, LANE, LANE), 1)
    cc = lax.broadcasted_iota(jnp.int32, (nc, LANE, LANE), 2)
    same_head = (rr < HEAD_DIM) == (cc < HEAD_DIM)
    g = jnp.where(same_head, pg[..., :LANE], 0.0) + jnp.where(rr == cc, jnp.exp(cum_end), 0.0)
    g_ref[0, 0] = g.reshape(nc * LANE, LANE)
    hc_ref[0, 0] = jnp.where(same_head, ph, 0.0).reshape(nc * LANE, LANE)


def _rwkv_chunks(rkv, wdad, wup, w0, aup, a0, k_k, k_a, r_k, *, rows=256):
    B, _, TP, _ = rkv.shape
    nb = TP // rows
    kern = functools.partial(_rwkv_chunk_kernel, rows=rows)
    slot_spec = lambda off: pl.BlockSpec((1, 1, rows, LANE), lambda b, p, i: (b, p + off, i, 0))
    par_mat = pl.BlockSpec((1, DECAY_RANK, LANE), lambda b, p, i: (p, 0, 0))
    par_vec = pl.BlockSpec((1, 1, LANE), lambda b, p, i: (p, 0, 0))
    row_out = jax.ShapeDtypeStruct((B, PAIRS, TP, LANE), F32)
    mat_out = jax.ShapeDtypeStruct((B, PAIRS, 2 * TP, LANE), F32)
    row_spec = pl.BlockSpec((1, 1, rows, LANE), lambda b, p, i: (b, p, i, 0))
    mat_spec = pl.BlockSpec((1, 1, 2 * rows, LANE), lambda b, p, i: (b, p, i, 0))
    return pl.pallas_call(
        kern, out_shape=(mat_out, mat_out, row_out, row_out, row_out), grid=(B, PAIRS, nb),
        in_specs=[slot_spec(0), slot_spec(PAIRS), slot_spec(2 * PAIRS),
                  pl.BlockSpec((1, rows, LANE), lambda b, p, i: (b, i, 0)),
                  par_mat, par_vec, par_mat, par_vec, par_vec, par_vec, par_vec],
        out_specs=(mat_spec, mat_spec, row_spec, row_spec, row_spec),
        compiler_params=pltpu.CompilerParams(
            dimension_semantics=("arbitrary",) * 3, vmem_limit_bytes=VMEM_LIMIT),
        name="rwkv_chunks",
    )(rkv, rkv, rkv, wdad, wup, w0, aup, a0, k_k, k_a, r_k)


def _rwkv_scan_kernel(g_ref, hc_ref, r2_ref, y0_ref, bonus_ref, lng_ref, lnb_ref, y_ref,
                      state, *, rows):
    i = pl.program_id(1)

    @pl.when(i == 0)
    def _():
        state[...] = jnp.zeros_like(state)

    first = lax.broadcasted_iota(jnp.int32, (CHUNK, LANE), 1) < HEAD_DIM
    hs = [state[p] for p in range(PAIRS)]
    for c in range(rows // CHUNK):
        sl = slice(c * CHUNK, (c + 1) * CHUNK)
        sm = slice(c * LANE, (c + 1) * LANE)
        for p in range(PAIRS):
            y = _dot(r2_ref[0, p, sl, :], hs[p], precision=HI) + y0_ref[0, p, sl, :]
            hs[p] = _dot(g_ref[0, p, sm, :], hs[p], precision=HI) + hc_ref[0, p, sm, :]
            yc = y - _head_sum(y, first) * (1.0 / HEAD_DIM)
            var = _head_sum(yc * yc, first) * (1.0 / HEAD_DIM)
            out = yc * lax.rsqrt(var + RWKV_LN_EPS) * lng_ref[p] + lnb_ref[p]
            y_ref[0, p, sl, :] = (out + bonus_ref[0, p, sl, :]).astype(BF16)
    for p in range(PAIRS):
        state[p] = hs[p]


def _rwkv_scan(gm, hc, r2, y0, bonus, ln_g, ln_b, *, rows=512):
    B, P, TP, _ = r2.shape
    nb = TP // rows
    kern = functools.partial(_rwkv_scan_kernel, rows=rows)
    blk = pl.BlockSpec((1, P, rows, LANE), lambda b, i: (b, 0, i, 0))
    mat = pl.BlockSpec((1, P, 2 * rows, LANE), lambda b, i: (b, 0, i, 0))
    par = pl.BlockSpec((P, 1, LANE), lambda b, i: (0, 0, 0))
    return pl.pallas_call(
        kern, out_shape=jax.ShapeDtypeStruct((B, P, TP, LANE), BF16), grid=(B, nb),
        in_specs=[mat, mat, blk, blk, blk, par, par], out_specs=blk,
        scratch_shapes=[pltpu.VMEM((P, LANE, LANE), F32)],
        compiler_params=pltpu.CompilerParams(
            dimension_semantics=("arbitrary", "arbitrary"), vmem_limit_bytes=VMEM_LIMIT),
        name="rwkv_scan",
    )(gm, hc, r2, y0, bonus, ln_g, ln_b)


def _split_pair(qp, nh):
    lane = lax.broadcasted_iota(jnp.int32, qp.shape, 1)
    first = lane < HEAD_DIM
    zero = jnp.zeros_like(qp)
    qs = [jnp.where(first, qp, zero), jnp.where(first, zero, qp)]
    return qs[:nh], first


def _sb_kernel(q_ref, k_ref, v_ref, o_ref, *, t, nh):
    qi = pl.program_id(2)
    qs, first = _split_pair(q_ref[0, 0], nh)
    qpos = qi * t + lax.broadcasted_iota(jnp.int32, (t, 1), 0)
    kloc = lax.broadcasted_iota(jnp.int32, (1, t), 1)
    upper = (lax.broadcasted_iota(jnp.int32, (t, t), 0)
             > lax.broadcasted_iota(jnp.int32, (t, t), 1)).astype(BF16)

    def block(kb, accs, cs, causal):
        start = pl.multiple_of(kb * t, t)
        kblk = k_ref[0, 0, pl.ds(start, t), :]
        vblk = v_ref[0, 0, pl.ds(start, t), :]
        kpos = start + kloc
        mask = kpos >= PAD
        if causal:
            mask = mask & (kpos < qpos)
        new_accs, new_cs = [], []
        for h in range(nh):
            z = _dot_nt(qs[h], kblk)
            ls = jnp.minimum(z, 0.0) - jnp.log(1.0 + jnp.exp(-jnp.abs(z)))
            l1 = jnp.where(mask, ls - z, 0.0)
            tail = _dot(l1.astype(BF16), upper) + cs[h]
            a = jnp.where(mask, jnp.exp(ls + tail), 0.0)
            new_accs.append(accs[h] + _dot(a.astype(BF16), vblk))
            new_cs.append(cs[h] + jnp.sum(l1, axis=-1, keepdims=True))
        return new_accs, new_cs

    def live(cs):
        top = cs[0]
        for c in cs[1:]:
            top = jnp.maximum(top, c)
        return (jnp.max(top) > -SB_CUTOFF).astype(jnp.int32)

    zero_acc = [jnp.zeros((t, LANE), F32) for _ in range(nh)]
    zero_c = [jnp.zeros((t, 1), F32) for _ in range(nh)]
    accs, cs = block(qi, zero_acc, zero_c, True)

    def cond(carry):
        kb, alive = carry[0], carry[1]
        return (kb >= 0) & (alive > 0)

    def body(carry):
        kb = carry[0]
        accs, cs = block(kb, list(carry[2:2 + nh]), list(carry[2 + nh:]), False)
        return (kb - 1, live(cs), *accs, *cs)

    carry = lax.while_loop(cond, body, (qi - 1, live(cs), *accs, *cs))
    accs = carry[2:2 + nh]
    out = jnp.where(first, accs[0], accs[1] if nh == 2 else 0.0)
    o_ref[0, 0] = out.astype(BF16)


def _sb_attention(sb, *, slot0, nslots, nh, t=256):
    B, _, TP, _ = sb.shape
    kern = functools.partial(_sb_kernel, t=t, nh=nh)
    blk = lambda off: pl.BlockSpec((1, 1, t, LANE), lambda b, p, i: (b, slot0 + off + p, i, 0))
    full = lambda off: pl.BlockSpec((1, 1, TP, LANE), lambda b, p, i: (b, slot0 + off + p, 0, 0))
    return pl.pallas_call(
        kern, out_shape=jax.ShapeDtypeStruct((B, nslots, TP, LANE), BF16),
        grid=(B, nslots, TP // t),
        in_specs=[blk(0), full(PAIRS), full(2 * PAIRS)],
        out_specs=pl.BlockSpec((1, 1, t, LANE), lambda b, p, i: (b, p, i, 0)),
        compiler_params=pltpu.CompilerParams(
            dimension_semantics=("arbitrary",) * 3, vmem_limit_bytes=VMEM_LIMIT),
        name="sb_attention",
    )(sb, sb, sb)


def _fox_kernel(shift_ref, q_ref, k_ref, v_ref, o_ref, *, t):
    qi = pl.program_id(2)
    q = q_ref[0, 0]
    shift = shift_ref[0]
    qpos = qi * t + lax.broadcasted_iota(jnp.int32, (t, 1), 0)
    kloc = lax.broadcasted_iota(jnp.int32, (1, t), 1)

    def block(kb, carry, masked, online):
        acc, m = carry
        start = pl.multiple_of(kb * t, t)
        kblk = k_ref[0, 0, pl.ds(start, t), :]
        vblk = v_ref[0, 0, pl.ds(start, t), :]
        s = _dot_nt(q, kblk)
        if masked:
            kpos = start + kloc
            s = jnp.where((kpos <= qpos) & (kpos >= PAD), s, NEG_INF)
        if online:
            m_new = jnp.maximum(m, jnp.max(s, axis=-1, keepdims=True))
            acc = jnp.exp2(m - m_new) * acc
            m = m_new
        pr = jnp.exp2(s - m)
        return acc + _dot(pr.astype(BF16), vblk), m

    def sweep(online):
        def run(carry):
            carry = block(0, carry, True, online)
            carry = lax.fori_loop(1, qi, lambda kb, c: block(kb, c, False, online), carry)
            return lax.cond(qi > 0, lambda c: block(qi, c, True, online), lambda c: c, carry)
        return run

    acc0 = jnp.zeros((t, LANE), F32)
    fixed = shift <= FOX_FIXED_SHIFT_MAX
    m0 = jnp.where(fixed, jnp.full((t, 1), shift, F32), jnp.full((t, 1), NEG_INF, F32))
    acc, _ = lax.cond(fixed, sweep(False), sweep(True), (acc0, m0))
    lane = lax.broadcasted_iota(jnp.int32, (t, LANE), 1)
    denom = jnp.sum(jnp.where(lane == HEAD_DIM, acc, 0.0), axis=-1, keepdims=True)
    o_ref[0, 0] = (acc / jnp.where(denom > 0.0, denom, 1.0)).astype(BF16)


def _fox_attention(shift, fx, *, t=512):
    B, _, TP, _ = fx.shape
    H = FOX_HEADS
    kern = functools.partial(_fox_kernel, t=t)
    return pl.pallas_call(
        kern, out_shape=jax.ShapeDtypeStruct((B, H, TP, LANE), BF16),
        grid=(B, H, TP // t),
        in_specs=[pl.BlockSpec(memory_space=pltpu.SMEM),
                  pl.BlockSpec((1, 1, t, LANE), lambda b, h, i: (b, h, i, 0)),
                  pl.BlockSpec((1, 1, TP, LANE), lambda b, h, i: (b, h + H, 0, 0)),
                  pl.BlockSpec((1, 1, TP, LANE), lambda b, h, i: (b, h + 2 * H, 0, 0))],
        out_specs=pl.BlockSpec((1, 1, t, LANE), lambda b, h, i: (b, h, i, 0)),
        compiler_params=pltpu.CompilerParams(
            dimension_semantics=("arbitrary",) * 3, vmem_limit_bytes=VMEM_LIMIT),
        name="fox_attention",
    )(shift, fx, fx, fx)


def _merge_kernel(h_ref, ya_ref, yb2_ref, yb1_ref, yc_ref, gate_ref,
                  wpa_ref, wpb_ref, wpc_ref, wout_ref, o_ref, *, tm, n_valid):
    i = pl.program_id(1)

    pa = _dot(ya_ref[0, 0], wpa_ref[0])
    for hh in range(1, PAIRS):
        pa = pa + _dot(ya_ref[0, hh], wpa_ref[hh])

    def pair_proj(y2_ref, y1_ref, w_ref):
        return (_dot(y2_ref[0, 0], w_ref[0]) + _dot(y2_ref[0, 1], w_ref[1])
                + _dot(y1_ref[0, 0], w_ref[2]))

    merged = gate_ref[0, :, 0:D_MODEL].astype(F32) * pa
    merged = merged + (gate_ref[0, :, D_MODEL:2 * D_MODEL].astype(F32)
                       * pair_proj(yb2_ref, yb1_ref, wpb_ref))
    pc = _dot(yc_ref[0, 0], wpc_ref[0])
    for hh in range(1, FOX_HEADS):
        pc = pc + _dot(yc_ref[0, hh], wpc_ref[hh])
    merged = merged + gate_ref[0, :, 2 * D_MODEL:3 * D_MODEL].astype(F32) * pc
    out = h_ref[0] + _dot(merged.astype(BF16), wout_ref[...])
    t = i * tm + lax.broadcasted_iota(jnp.int32, (tm, 1), 0)
    o_ref[0] = jnp.where((t >= PAD) & (t < PAD + n_valid), out, 0.0)


def _merge(h, ya, yb2, yb1, yc, gates, wpa, wpb, wpc, wout, *, n_valid, tm=256):
    B, TP, D = h.shape
    kern = functools.partial(_merge_kernel, tm=tm, n_valid=n_valid)
    pair_blk = lambda ns: pl.BlockSpec((1, ns, tm, LANE), lambda b, i: (b, 0, i, 0))
    return pl.pallas_call(
        kern, out_shape=jax.ShapeDtypeStruct((B, TP, D), F32), grid=(B, TP // tm),
        in_specs=[pl.BlockSpec((1, tm, D), lambda b, i: (b, i, 0)),
                  pair_blk(PAIRS),
                  pair_blk(2), pair_blk(1), pair_blk(FOX_HEADS),
                  pl.BlockSpec((1, tm, 3 * D), lambda b, i: (b, i, 0)),
                  _const_spec((PAIRS, LANE, D)),
                  _const_spec((PAIRS, LANE, D)),
                  _const_spec((FOX_HEADS, LANE, D)),
                  _const_spec((D, D))],
        out_specs=pl.BlockSpec((1, tm, D), lambda b, i: (b, i, 0)),
        compiler_params=pltpu.CompilerParams(
            dimension_semantics=("arbitrary", "arbitrary"), vmem_limit_bytes=VMEM_LIMIT),
        name="merge",
    )(h, ya, yb2, yb1, yc, gates, wpa, wpb, wpc, wout)


def _moe_kernel(h_ref, g_ref, wr_ref, br_ref, wg_ref, wu_ref, wd_ref, o_ref, he_ref,
                *, tm, n_valid):
    i = pl.program_id(1)
    x = h_ref[0]
    ms = jnp.mean(x * x, axis=-1, keepdims=True)
    n = x * lax.rsqrt(ms + NORM_EPS) * g_ref[...]

    lg = _dot(n, wr_ref[...], precision=HI) + br_ref[...]
    lane = lax.broadcasted_iota(jnp.int32, (tm, LANE), 1)
    big = jnp.int32(LANE)
    is_group = lane < N_GROUPS
    gl = jnp.where(is_group, lg, -jnp.inf)
    gmax = jnp.max(gl, axis=-1, keepdims=True)
    g_idx = jnp.min(jnp.where(is_group & (gl == gmax), lane, big), axis=-1, keepdims=True)
    p_group = 1.0 / jnp.sum(jnp.where(is_group, jnp.exp(gl - gmax), 0.0), axis=-1, keepdims=True)
    lo = N_GROUPS + g_idx * EXPERTS_PER_GROUP
    in_grp = (lane >= lo) & (lane < lo + EXPERTS_PER_GROUP)
    el = jnp.where(in_grp, lg, -jnp.inf)
    top1 = jnp.max(el, axis=-1, keepdims=True)
    i1 = jnp.min(jnp.where(in_grp & (el == top1), lane, big), axis=-1, keepdims=True)
    rest = in_grp & (lane != i1)
    el2 = jnp.where(rest, lg, -jnp.inf)
    top2 = jnp.max(el2, axis=-1, keepdims=True)
    i2 = jnp.min(jnp.where(rest & (el2 == top2), lane, big), axis=-1, keepdims=True)
    e2 = jnp.exp(top2 - top1)
    p1 = 1.0 / (1.0 + e2)
    p2 = e2 / (1.0 + e2)
    comb = p_group * (jnp.where(lane == i1, p1, 0.0) + jnp.where(lane == i2, p2, 0.0))

    n16 = n.astype(BF16)
    gate = _dot(n16, wg_ref[...])
    up = _dot(n16, wu_ref[...])
    he = gate * _sigmoid(gate) * up
    for e in range(N_EXPERTS):
        c_e = comb[:, N_GROUPS + e:N_GROUPS + e + 1]
        he_ref[:, e * D_EXPERT:(e + 1) * D_EXPERT] = (
            he[:, e * D_EXPERT:(e + 1) * D_EXPERT] * c_e).astype(BF16)
    out = x + _dot(he_ref[...], wd_ref[...])
    t = i * tm + lax.broadcasted_iota(jnp.int32, (tm, 1), 0)
    o_ref[0] = jnp.where((t >= PAD) & (t < PAD + n_valid), out, 0.0)


def _moe(h, g2, wr, br, wg, wu, wd, *, n_valid, tm=256):
    B, TP, D = h.shape
    kern = functools.partial(_moe_kernel, tm=tm, n_valid=n_valid)
    EW = N_EXPERTS * D_EXPERT
    return pl.pallas_call(
        kern, out_shape=jax.ShapeDtypeStruct((B, TP, D), F32), grid=(B, TP // tm),
        in_specs=[pl.BlockSpec((1, tm, D), lambda b, i: (b, i, 0)),
                  _const_spec((1, D)), _const_spec((D, LANE)), _const_spec((1, LANE)),
                  _const_spec((D, EW)), _const_spec((D, EW)), _const_spec((EW, D))],
        out_specs=pl.BlockSpec((1, tm, D), lambda b, i: (b, i, 0)),
        scratch_shapes=[pltpu.VMEM((tm, EW), BF16)],
        compiler_params=pltpu.CompilerParams(
            dimension_semantics=("arbitrary", "arbitrary"), vmem_limit_bytes=VMEM_LIMIT),
        name="moe",
    )(h, g2, wr, br, wg, wu, wd)


def _pad_pairs(w, heads):
    return jnp.pad(w, ((0, 0), (0, PAIRS * LANE - heads * HEAD_DIM)))


def _pack_w_in(w_in):
    D = w_in.shape[0]
    rest = w_in[:, SHIFT_W:]
    cols = [w_in[:, :SHIFT_W]]
    off = 0
    for heads in (SB_HEADS,) * 3 + (FOX_HEADS,) * 3:
        cols.append(_pad_pairs(rest[:, off:off + heads * HEAD_DIM], heads))
        off += heads * HEAD_DIM
    cols.append(jnp.pad(rest[:, off:off + FOX_HEADS], ((0, 0), (0, LANE - FOX_HEADS))))
    cols.append(rest[:, off + FOX_HEADS:])
    return jnp.concatenate(cols, axis=1).astype(BF16)


def _per_pair_cols(w):
    return w.reshape(w.shape[0], PAIRS, LANE).transpose(1, 0, 2)


def kernel(x, meta_tokens, norm1_g, w_in, rwkv_mu, rwkv_w_up, rwkv_w0, rwkv_a_up, rwkv_a0,
           rwkv_k_k, rwkv_k_a, rwkv_r_k, rwkv_ln_g, rwkv_ln_b, fox_f_b, fox_q_g, fox_k_g,
           w_p_rwkv, w_p_sb, w_p_fox, w_out, norm2_g, moe_wg, moe_bg, moe_we, moe_be,
           moe_w_gate, moe_w_up, moe_w_down):
    B, S, D = x.shape
    depth = w_in.shape[0]
    L = N_META + S
    TP = -(-(PAD + L) // ROW_TILE) * ROW_TILE
    meta = jnp.broadcast_to(meta_tokens[None].astype(x.dtype), (B, N_META, D))
    h = jnp.concatenate([jnp.zeros((B, PAD, D), x.dtype), meta, x,
                         jnp.zeros((B, TP - PAD - L, D), x.dtype)], axis=1)
    H = RWKV_HEADS
    EW = N_EXPERTS * D_EXPERT
    for l in range(depth):
        fb = jnp.zeros((1, LANE), F32).at[0, :FOX_HEADS].set(fox_f_b[l])
        rkv, wdad, sb, fx, gates = _inproj(
            h, norm1_g[l][None], _pack_w_in(w_in[l]), rwkv_mu[l][None], fb,
            jnp.tile(fox_q_g[l], 2)[None], jnp.tile(fox_k_g[l], 2)[None])
        hv = lambda p: p.reshape(PAIRS, 1, LANE)
        gm, hc, r2, y0, bonus = _rwkv_chunks(
            rkv, wdad, _per_pair_cols(rwkv_w_up[l]), hv(rwkv_w0[l]),
            _per_pair_cols(rwkv_a_up[l]), hv(rwkv_a0[l]),
            hv(rwkv_k_k[l]), hv(rwkv_k_a[l]), hv(rwkv_r_k[l]))
        ya = _rwkv_scan(gm, hc, r2, y0, bonus, hv(rwkv_ln_g[l]), hv(rwkv_ln_b[l]))
        yb2 = _sb_attention(sb, slot0=0, nslots=2, nh=2)
        yb1 = _sb_attention(sb, slot0=2, nslots=1, nh=1)
        shift = (8.0 * LOG2E * jnp.max(jnp.abs(fox_q_g[l])) * jnp.max(jnp.abs(fox_k_g[l]))).reshape(1)
        yc = _fox_attention(shift, fx)
        pc = jnp.pad(w_p_fox[l].reshape(FOX_HEADS, HEAD_DIM, D),
                     ((0, 0), (0, LANE - HEAD_DIM), (0, 0))).astype(BF16)
        pp = lambda w, nh: jnp.pad(w, ((0, PAIRS * LANE - nh * HEAD_DIM), (0, 0))).reshape(
            PAIRS, LANE, D).astype(BF16)
        h = _merge(h, ya, yb2, yb1, yc, gates, pp(w_p_rwkv[l], H), pp(w_p_sb[l], SB_HEADS),
                   pc, w_out[l].astype(BF16), n_valid=L)
        wr = jnp.zeros((D, LANE), F32).at[:, :N_GROUPS].set(moe_wg[l])
        wr = wr.at[:, N_GROUPS:N_GROUPS + N_EXPERTS].set(moe_we[l])
        br = jnp.zeros((1, LANE), F32).at[0, :N_GROUPS].set(moe_bg[l])
        br = br.at[0, N_GROUPS:N_GROUPS + N_EXPERTS].set(moe_be[l])
        wg = moe_w_gate[l].transpose(1, 0, 2).reshape(D, EW).astype(BF16)
        wu = moe_w_up[l].transpose(1, 0, 2).reshape(D, EW).astype(BF16)
        wd = moe_w_down[l].reshape(EW, D).astype(BF16)
        h = _moe(h, norm2_g[l][None], wr, br, wg, wu, wd, n_valid=L)
    return h[:, PAD + N_META:PAD + L]
```

```python
import functools
import math

import jax
import jax.numpy as jnp
from jax import lax
from jax.experimental import pallas as pl
from jax.experimental.pallas import tpu as pltpu

D_MODEL = 1024
HEAD_DIM = 64
N_META = 16
PAD = 128 - N_META
RWKV_HEADS = 6
SB_HEADS = 5
FOX_HEADS = 5
RWKV_W = RWKV_HEADS * HEAD_DIM
SB_W = SB_HEADS * HEAD_DIM
FOX_W = FOX_HEADS * HEAD_DIM
DECAY_RANK = 64
ICLR_RANK = 64
SHIFT_W = 3 * RWKV_W + DECAY_RANK + ICLR_RANK
N_GROUPS = 4
EXPERTS_PER_GROUP = 4
N_EXPERTS = 16
D_EXPERT = 256
NORM_EPS = 1e-6
RWKV_LN_EPS = 64e-5
NEG_INF = -1e30
ATT_SCALE = 1.0 / math.sqrt(HEAD_DIM)

LANE = 128
ROW_TILE = 512
PAIRS = 3
QKV_W = 3 * PAIRS * LANE
SEG_SB = SHIFT_W
SEG_FOX = SEG_SB + QKV_W
SEG_GATE = SEG_FOX + QKV_W + LANE
D_IN_PAD = SEG_GATE + 3 * D_MODEL
SB_CUTOFF = 110.0
LOG2E = 1.4426950408889634
FOX_FIXED_SHIFT_MAX = 50.0
FOX_SKIP_LOG2 = 150.0
FOX_TILE = 512
INPROJ_TILE = 256

CHUNK = 64
VMEM_LIMIT = 56 * 1024 * 1024

F32 = jnp.float32
BF16 = jnp.bfloat16
HI = lax.Precision.HIGHEST


def _log_sigmoid(x):
    return jnp.minimum(x, 0.0) - jnp.log1p(jnp.exp(-jnp.abs(x)))


def _sigmoid(x):
    return 1.0 / (1.0 + jnp.exp(-x))


def _dot(a, b, **kw):
    return jnp.dot(a, b, preferred_element_type=F32, **kw)


def _dot_nt(a, b):
    return lax.dot_general(a, b, (((1,), (1,)), ((), ())), preferred_element_type=F32)


def _dot_tn(a, b):
    return lax.dot_general(a, b, (((0,), (0,)), ((), ())), preferred_element_type=F32)


def _dot3(a, b):
    ah = a.astype(BF16)
    al = (a - ah.astype(F32)).astype(BF16)
    bh = b.astype(BF16)
    bl = (b - bh.astype(F32)).astype(BF16)
    return _dot(ah, bh) + (_dot(ah, bl) + _dot(al, bh))


def _const_spec(shape):
    n = len(shape)
    return pl.BlockSpec(shape, lambda *_: (0,) * n, pipeline_mode=pl.Buffered(1))


def _inproj_kernel(h_ref, g_ref, w_ref, mu_ref, fb_ref, fqg_ref, fkg_ref,
                   rkv_ref, wdad_ref, sb_ref, fx_ref, fend_ref, gate_ref,
                   carry_u, carry_f, *, tm):
    i = pl.program_id(1)

    @pl.when(i == 0)
    def _():
        carry_u[...] = jnp.zeros_like(carry_u)
        carry_f[...] = jnp.zeros_like(carry_f)

    x = h_ref[0]
    ms = jnp.mean(x * x, axis=-1, keepdims=True)
    n = (x * lax.rsqrt(ms + NORM_EPS) * g_ref[...]).astype(BF16)
    row = lax.broadcasted_iota(jnp.int32, (tm, 1), 0)

    us = _dot(n, w_ref[:, 0:SHIFT_W])
    prev = pltpu.roll(us, 1, axis=0)
    prev = jnp.where(row == 0, carry_u[...], prev)
    carry_u[...] = us[tm - 1:tm, :]
    ush = us + (prev - us) * mu_ref[...]
    for j in range(3 * PAIRS):
        rkv_ref[0, j] = ush[:, j * LANE:(j + 1) * LANE]
    wdad_ref[0] = ush[:, 3 * RWKV_W:SHIFT_W]

    usb = _dot(n, w_ref[:, SEG_SB:SEG_SB + QKV_W])
    for j in range(3 * PAIRS):
        piece = usb[:, j * LANE:(j + 1) * LANE]
        if j < PAIRS:
            piece = piece * ATT_SCALE
        sb_ref[0, j] = piece.astype(BF16)

    uf = _dot(n, w_ref[:, SEG_FOX:SEG_FOX + QKV_W + LANE])
    lane = lax.broadcasted_iota(jnp.int32, (tm, LANE), 1)
    first = lane < HEAD_DIM
    t_glob = i * tm + row
    logf = _log_sigmoid(uf[:, QKV_W:QKV_W + LANE] + fb_ref[...])
    logf = jnp.where((lane < FOX_HEADS) & (t_glob >= PAD), logf, 0.0)
    tri = (lax.broadcasted_iota(jnp.int32, (tm, tm), 0)
           >= lax.broadcasted_iota(jnp.int32, (tm, tm), 1)).astype(F32)
    cum = _dot(tri, logf, precision=HI) + carry_f[...]
    carry_f[...] = cum[tm - 1:tm, :]
    fend_ref[0, 0] = cum[tm - 1:tm, :] * LOG2E

    def split3(hd):
        f2 = jnp.sum(jnp.where(lane == hd, cum, 0.0), axis=-1, keepdims=True) * LOG2E
        hi = f2.astype(BF16).astype(F32)
        mid = (f2 - hi).astype(BF16).astype(F32)
        return hi, mid, f2 - hi - mid

    def tail_cols(vals):
        out = jnp.zeros((tm, LANE), F32)
        for o, val in enumerate(vals):
            out = jnp.where(lane == HEAD_DIM + o, val, out)
        return out

    splits = [split3(hd) for hd in range(FOX_HEADS)]
    for j in range(3 * PAIRS):
        piece = uf[:, j * LANE:(j + 1) * LANE]
        kind = j // PAIRS
        if kind < 2:
            gain = fqg_ref[...] if kind == 0 else fkg_ref[...]
            sq = piece * piece
            ms0 = jnp.sum(jnp.where(first, sq, 0.0), axis=-1, keepdims=True) * (1.0 / HEAD_DIM)
            ms1 = jnp.sum(jnp.where(first, 0.0, sq), axis=-1, keepdims=True) * (1.0 / HEAD_DIM)
            inv = jnp.where(first, lax.rsqrt(ms0 + NORM_EPS), lax.rsqrt(ms1 + NORM_EPS))
            piece = piece * inv * gain
            if kind == 0:
                piece = piece * (ATT_SCALE * LOG2E)
        swapped = pltpu.roll(piece, HEAD_DIM, axis=1)
        for half in range(2):
            hd = 2 * (j % PAIRS) + half
            if hd >= FOX_HEADS:
                continue
            if kind == 2:
                extra = tail_cols([1.0])
            else:
                hi, mid, lo = splits[hd]
                extra = tail_cols([hi, mid, lo, 1.0, 1.0, 1.0] if kind == 0
                                  else [1.0, 1.0, 1.0, -hi, -mid, -lo])
            body = piece if half == 0 else swapped
            fx_ref[0, kind * FOX_HEADS + hd] = jnp.where(first, body, extra).astype(BF16)

    ug = _dot(n, w_ref[:, SEG_GATE:D_IN_PAD])
    gate_ref[0] = _sigmoid(ug).astype(BF16)


def _inproj(h, g1, w_in_p, mu, fb, fqg, fkg, *, tm=INPROJ_TILE):
    B, TP, D = h.shape
    nb = TP // tm
    kern = functools.partial(_inproj_kernel, tm=tm)
    out_shape = (
        jax.ShapeDtypeStruct((B, 3 * PAIRS, TP, LANE), F32),
        jax.ShapeDtypeStruct((B, TP, LANE), F32),
        jax.ShapeDtypeStruct((B, 3 * PAIRS, TP, LANE), BF16),
        jax.ShapeDtypeStruct((B, 3 * FOX_HEADS, TP, LANE), BF16),
        jax.ShapeDtypeStruct((B, nb, 1, LANE), F32),
        jax.ShapeDtypeStruct((B, TP, 3 * D_MODEL), BF16),
    )
    in_specs = [
        pl.BlockSpec((1, tm, D), lambda b, i: (b, i, 0)),
        _const_spec((1, D)),
        _const_spec((D, D_IN_PAD)),
        _const_spec((1, SHIFT_W)),
        _const_spec((1, LANE)),
        _const_spec((1, LANE)),
        _const_spec((1, LANE)),
    ]
    out_specs = (
        pl.BlockSpec((1, 3 * PAIRS, tm, LANE), lambda b, i: (b, 0, i, 0)),
        pl.BlockSpec((1, tm, LANE), lambda b, i: (b, i, 0)),
        pl.BlockSpec((1, 3 * PAIRS, tm, LANE), lambda b, i: (b, 0, i, 0)),
        pl.BlockSpec((1, 3 * FOX_HEADS, tm, LANE), lambda b, i: (b, 0, i, 0)),
        pl.BlockSpec((1, 1, 1, LANE), lambda b, i: (b, i, 0, 0)),
        pl.BlockSpec((1, tm, 3 * D_MODEL), lambda b, i: (b, i, 0)),
    )
    return pl.pallas_call(
        kern, out_shape=out_shape, grid=(B, nb), in_specs=in_specs, out_specs=out_specs,
        scratch_shapes=[pltpu.VMEM((1, SHIFT_W), F32), pltpu.VMEM((1, LANE), F32)],
        compiler_params=pltpu.CompilerParams(
            dimension_semantics=("arbitrary", "arbitrary"), vmem_limit_bytes=VMEM_LIMIT),
        name="inproj",
    )(h, g1, w_in_p, mu, fb, fqg, fkg)


def _bdot(a, b):
    return lax.dot_general(a, b, (((2,), (1,)), ((0,), (0,))), preferred_element_type=F32)


def _bdot_nt(a, b):
    return lax.dot_general(a, b, (((2,), (2,)), ((0,), (0,))), preferred_element_type=F32)


def _bdot_tn(a, b):
    return lax.dot_general(a, b, (((1,), (1,)), ((0,), (0,))), preferred_element_type=F32)


def _head_sum(x, first):
    s0 = jnp.sum(jnp.where(first, x, 0.0), axis=-1, keepdims=True)
    s1 = jnp.sum(jnp.where(first, 0.0, x), axis=-1, keepdims=True)
    return jnp.where(first, s0, s1)


def _rwkv_chunk_kernel(r_ref, k_ref, v_ref, wdad_ref, wup_ref, w0_ref, aup_ref, a0_ref,
                       kk_ref, ka_ref, rk_ref,
                       g_ref, hc_ref, r2_ref, y0_ref, bonus_ref, *, rows):
    nc = rows // CHUNK
    r = r_ref[0, 0]
    k = k_ref[0, 0]
    v = v_ref[0, 0]
    wd = wdad_ref[0][:, 0:DECAY_RANK]
    ad = wdad_ref[0][:, DECAY_RANK:DECAY_RANK + ICLR_RANK]
    first = lax.broadcasted_iota(jnp.int32, (rows, LANE), 1) < HEAD_DIM

    pre = w0_ref[0] + _dot3(jnp.tanh(wd), wup_ref[0])
    lw = -jnp.exp(_log_sigmoid(pre) - 0.5)
    iclr = _sigmoid(a0_ref[0] + _dot3(ad, aup_ref[0]))
    kk = k * kk_ref[0]
    kk = kk / jnp.maximum(jnp.sqrt(_head_sum(kk * kk, first)), 1e-12)
    k2 = k * (1.0 + (iclr - 1.0) * ka_ref[0])
    b = kk * iclr
    bonus_ref[0, 0] = _head_sum(r * k2 * rk_ref[0], first) * v

    to3 = lambda x: x.reshape(nc, CHUNK, LANE)
    ri = lax.broadcasted_iota(jnp.int32, (nc, CHUNK, CHUNK), 1)
    ci = lax.broadcasted_iota(jnp.int32, (nc, CHUNK, CHUNK), 2)
    low_incl = ri >= ci
    low_strict = ri > ci
    first3 = lax.broadcasted_iota(jnp.int32, (nc, CHUNK, LANE), 2) < HEAD_DIM

    lw3 = to3(lw)
    tri = low_incl.astype(BF16)
    lw_hi = lw3.astype(BF16)
    cum = _bdot(tri, lw_hi) + _bdot(tri, (lw3 - lw_hi.astype(F32)).astype(BF16))
    cum_end = cum[:, CHUNK - 1:CHUNK, :]
    e_neg = jnp.exp(-cum)
    at = to3(-kk) * jnp.exp(cum - lw3)
    rt = to3(r) * jnp.exp(cum)
    bt = (to3(b) * e_neg).astype(BF16)
    kt = (to3(k2) * e_neg).astype(BF16)
    e_rem = jnp.exp(cum_end - cum)
    bq = (to3(b) * e_rem).astype(BF16)
    kq = (to3(k2) * e_rem).astype(BF16)
    vv = to3(v).astype(BF16)

    xs, r2s, y0s = [], [], []
    for hd in range(2):
        sel = first3 if hd == 0 else jnp.logical_not(first3)
        lhs = jnp.concatenate([jnp.where(sel, at, 0.0), jnp.where(sel, rt, 0.0)],
                              axis=1).astype(BF16)
        mb = _bdot_nt(lhs, bt)
        mk = _bdot_nt(lhs, kt)
        m_ab = jnp.where(low_strict, mb[:, :CHUNK], 0.0)
        m_ak = jnp.where(low_strict, mk[:, :CHUNK], 0.0).astype(BF16)
        m_rb = jnp.where(low_incl, mb[:, CHUNK:], 0.0).astype(BF16)
        m_rk = jnp.where(low_incl, mk[:, CHUNK:], 0.0).astype(BF16)
        x = jnp.concatenate([at, _bdot(m_ak, vv)], axis=-1)
        p = m_ab
        for j in range(6):
            p16 = p.astype(BF16)
            x = x + _bdot(p16, x.astype(BF16))
            if j < 5:
                p = _bdot(p16, p16)
        ru = _bdot(m_rb, x.astype(BF16))
        xs.append(x)
        r2s.append(rt + ru[..., :LANE])
        y0s.append(ru[..., LANE:] + _bdot(m_rk, vv))

    first3w = jnp.concatenate([first3, first3], axis=-1)
    x = jnp.where(first3w, xs[0], xs[1])
    r2_ref[0, 0] = jnp.where(first3, r2s[0], r2s[1]).reshape(rows, LANE)
    y0_ref[0, 0] = jnp.where(first3, y0s[0], y0s[1]).reshape(rows, LANE)

    pg = _bdot_tn(bq, x.astype(BF16))
    ph = pg[..., LANE:] + _bdot_tn(kq, vv)
    rr = lax.broadcasted_iota(jnp.int32, (nc, LANE, LANE), 1)
    cc = lax.broadcasted_iota(jnp.int32, (nc, LANE, LANE), 2)
    same_head = (rr < HEAD_DIM) == (cc < HEAD_DIM)
    g = jnp.where(same_head, pg[..., :LANE], 0.0) + jnp.where(rr == cc, jnp.exp(cum_end), 0.0)
    g_ref[0, 0] = g.reshape(nc * LANE, LANE)
    hc_ref[0, 0] = jnp.where(same_head, ph, 0.0).reshape(nc * LANE, LANE)


def _rwkv_chunks(rkv, wdad, wup, w0, aup, a0, k_k, k_a, r_k, *, rows=512):
    B, _, TP, _ = rkv.shape
    nb = TP // rows
    kern = functools.partial(_rwkv_chunk_kernel, rows=rows)
    slot_spec = lambda off: pl.BlockSpec((1, 1, rows, LANE), lambda b, p, i: (b, p + off, i, 0))
    par_mat = pl.BlockSpec((1, DECAY_RANK, LANE), lambda b, p, i: (p, 0, 0))
    par_vec = pl.BlockSpec((1, 1, LANE), lambda b, p, i: (p, 0, 0))
    row_out = jax.ShapeDtypeStruct((B, PAIRS, TP, LANE), F32)
    mat_out = jax.ShapeDtypeStruct((B, PAIRS, 2 * TP, LANE), F32)
    row_spec = pl.BlockSpec((1, 1, rows, LANE), lambda b, p, i: (b, p, i, 0))
    mat_spec = pl.BlockSpec((1, 1, 2 * rows, LANE), lambda b, p, i: (b, p, i, 0))
    return pl.pallas_call(
        kern, out_shape=(mat_out, mat_out, row_out, row_out, row_out), grid=(B, PAIRS, nb),
        in_specs=[slot_spec(0), slot_spec(PAIRS), slot_spec(2 * PAIRS),
                  pl.BlockSpec((1, rows, LANE), lambda b, p, i: (b, i, 0)),
                  par_mat, par_vec, par_mat, par_vec, par_vec, par_vec, par_vec],
        out_specs=(mat_spec, mat_spec, row_spec, row_spec, row_spec),
        compiler_params=pltpu.CompilerParams(
            dimension_semantics=("arbitrary",) * 3, vmem_limit_bytes=VMEM_LIMIT),
        name="rwkv_chunks",
    )(rkv, rkv, rkv, wdad, wup, w0, aup, a0, k_k, k_a, r_k)


def _rwkv_scan_kernel(g_ref, hc_ref, r2_ref, y0_ref, bonus_ref, lng_ref, lnb_ref, y_ref,
                      state, *, rows):
    i = pl.program_id(1)

    @pl.when(i == 0)
    def _():
        state[...] = jnp.zeros_like(state)

    first = lax.broadcasted_iota(jnp.int32, (CHUNK, LANE), 1) < HEAD_DIM
    hs = [state[p] for p in range(PAIRS)]
    for c in range(rows // CHUNK):
        sl = slice(c * CHUNK, (c + 1) * CHUNK)
        sm = slice(c * LANE, (c + 1) * LANE)
        for p in range(PAIRS):
            y = _dot3(r2_ref[0, p, sl, :], hs[p]) + y0_ref[0, p, sl, :]
            hs[p] = _dot3(g_ref[0, p, sm, :], hs[p]) + hc_ref[0, p, sm, :]
            yc = y - _head_sum(y, first) * (1.0 / HEAD_DIM)
            var = _head_sum(yc * yc, first) * (1.0 / HEAD_DIM)
            out = yc * lax.rsqrt(var + RWKV_LN_EPS) * lng_ref[p] + lnb_ref[p]
            y_ref[0, p, sl, :] = (out + bonus_ref[0, p, sl, :]).astype(BF16)
    for p in range(PAIRS):
        state[p] = hs[p]


def _rwkv_scan(gm, hc, r2, y0, bonus, ln_g, ln_b, *, rows=512):
    B, P, TP, _ = r2.shape
    nb = TP // rows
    kern = functools.partial(_rwkv_scan_kernel, rows=rows)
    blk = pl.BlockSpec((1, P, rows, LANE), lambda b, i: (b, 0, i, 0))
    mat = pl.BlockSpec((1, P, 2 * rows, LANE), lambda b, i: (b, 0, i, 0))
    par = pl.BlockSpec((P, 1, LANE), lambda b, i: (0, 0, 0))
    return pl.pallas_call(
        kern, out_shape=jax.ShapeDtypeStruct((B, P, TP, LANE), BF16), grid=(B, nb),
        in_specs=[mat, mat, blk, blk, blk, par, par], out_specs=blk,
        scratch_shapes=[pltpu.VMEM((P, LANE, LANE), F32)],
        compiler_params=pltpu.CompilerParams(
            dimension_semantics=("arbitrary", "arbitrary"), vmem_limit_bytes=VMEM_LIMIT),
        name="rwkv_scan",
    )(gm, hc, r2, y0, bonus, ln_g, ln_b)


def _split_pair(qp, nh):
    lane = lax.broadcasted_iota(jnp.int32, qp.shape, 1)
    first = lane < HEAD_DIM
    zero = jnp.zeros_like(qp)
    qs = [jnp.where(first, qp, zero), jnp.where(first, zero, qp)]
    return qs[:nh], first


def _sb_kernel(q_ref, k_ref, v_ref, o_ref, *, t, nh):
    qi = pl.program_id(2)
    qs, first = _split_pair(q_ref[0, 0], nh)
    qpos = qi * t + lax.broadcasted_iota(jnp.int32, (t, 1), 0)
    kloc = lax.broadcasted_iota(jnp.int32, (1, t), 1)
    upper = (lax.broadcasted_iota(jnp.int32, (t, t), 0)
             > lax.broadcasted_iota(jnp.int32, (t, t), 1)).astype(BF16)

    def block(kb, accs, cs, causal):
        start = pl.multiple_of(kb * t, t)
        kblk = k_ref[0, 0, pl.ds(start, t), :]
        vblk = v_ref[0, 0, pl.ds(start, t), :]
        kpos = start + kloc
        mask = kpos >= PAD
        if causal:
            mask = mask & (kpos < qpos)
        new_accs, new_cs = [], []
        for h in range(nh):
            z = _dot_nt(qs[h], kblk)
            ls = jnp.minimum(z, 0.0) - jnp.log(1.0 + jnp.exp(-jnp.abs(z)))
            l1 = jnp.where(mask, ls - z, 0.0)
            tail = _dot(l1.astype(BF16), upper) + cs[h]
            a = jnp.where(mask, jnp.exp(ls + tail), 0.0)
            new_accs.append(accs[h] + _dot(a.astype(BF16), vblk))
            new_cs.append(cs[h] + jnp.sum(l1, axis=-1, keepdims=True))
        return new_accs, new_cs

    def live(cs):
        top = cs[0]
        for c in cs[1:]:
            top = jnp.maximum(top, c)
        return (jnp.max(top) > -SB_CUTOFF).astype(jnp.int32)

    zero_acc = [jnp.zeros((t, LANE), F32) for _ in range(nh)]
    zero_c = [jnp.zeros((t, 1), F32) for _ in range(nh)]
    accs, cs = block(qi, zero_acc, zero_c, True)

    def cond(carry):
        kb, alive = carry[0], carry[1]
        return (kb >= 0) & (alive > 0)

    def body(carry):
        kb = carry[0]
        accs, cs = block(kb, list(carry[2:2 + nh]), list(carry[2 + nh:]), False)
        return (kb - 1, live(cs), *accs, *cs)

    carry = lax.while_loop(cond, body, (qi - 1, live(cs), *accs, *cs))
    accs = carry[2:2 + nh]
    out = jnp.where(first, accs[0], accs[1] if nh == 2 else 0.0)
    o_ref[0, 0] = out.astype(BF16)


def _sb_attention(sb, *, slot0, nslots, nh, t=256):
    B, _, TP, _ = sb.shape
    kern = functools.partial(_sb_kernel, t=t, nh=nh)
    blk = lambda off: pl.BlockSpec((1, 1, t, LANE), lambda b, p, i: (b, slot0 + off + p, i, 0))
    full = lambda off: pl.BlockSpec((1, 1, TP, LANE), lambda b, p, i: (b, slot0 + off + p, 0, 0))
    return pl.pallas_call(
        kern, out_shape=jax.ShapeDtypeStruct((B, nslots, TP, LANE), BF16),
        grid=(B, nslots, TP // t),
        in_specs=[blk(0), full(PAIRS), full(2 * PAIRS)],
        out_specs=pl.BlockSpec((1, 1, t, LANE), lambda b, p, i: (b, p, i, 0)),
        compiler_params=pltpu.CompilerParams(
            dimension_semantics=("arbitrary",) * 3, vmem_limit_bytes=VMEM_LIMIT),
        name="sb_attention",
    )(sb, sb, sb)


def _fox_kernel(shift_ref, fe_ref, q_ref, k_ref, v_ref, o_ref, *, t):
    b = pl.program_id(0)
    h = pl.program_id(1)
    qi = pl.program_id(2)
    q = q_ref[0, 0]
    shift = shift_ref[0]
    thr = FOX_SKIP_LOG2 + 2.0 * shift
    f_q = fe_ref[b, h, jnp.maximum(qi - 1, 0)]
    lo = lax.fori_loop(
        0, qi, lambda kb, n: n + (fe_ref[b, h, kb] - f_q > thr).astype(jnp.int32), jnp.int32(0))
    qpos = qi * t + lax.broadcasted_iota(jnp.int32, (t, 1), 0)
    kloc = lax.broadcasted_iota(jnp.int32, (1, t), 1)

    def block(kb, carry, masked, online):
        acc, m = carry
        start = pl.multiple_of(kb * t, t)
        kblk = k_ref[0, 0, pl.ds(start, t), :]
        vblk = v_ref[0, 0, pl.ds(start, t), :]
        s = _dot_nt(q, kblk)
        if masked:
            kpos = start + kloc
            s = jnp.where((kpos <= qpos) & (kpos >= PAD), s, NEG_INF)
        if online:
            m_new = jnp.maximum(m, jnp.max(s, axis=-1, keepdims=True))
            acc = jnp.exp2(m - m_new) * acc
            m = m_new
        pr = jnp.exp2(s - m)
        return acc + _dot(pr.astype(BF16), vblk), m

    def sweep(online):
        def run(carry):
            carry = lax.cond(lo == 0, lambda c: block(0, c, True, online), lambda c: c, carry)
            start = jnp.maximum(lo, 1)
            n = jnp.maximum(qi - start, 0)

            def two(i, c):
                kb = start + 2 * i
                return block(kb + 1, block(kb, c, False, online), False, online)

            carry = lax.fori_loop(0, n // 2, two, carry)
            carry = lax.cond(n % 2 == 1, lambda c: block(qi - 1, c, False, online),
                             lambda c: c, carry)
            return lax.cond(qi > 0, lambda c: block(qi, c, True, online), lambda c: c, carry)
        return run

    acc0 = jnp.zeros((t, LANE), F32)
    fixed = shift <= FOX_FIXED_SHIFT_MAX
    m0 = jnp.where(fixed, jnp.full((t, 1), shift, F32), jnp.full((t, 1), NEG_INF, F32))
    acc, _ = lax.cond(fixed, sweep(False), sweep(True), (acc0, m0))
    lane = lax.broadcasted_iota(jnp.int32, (t, LANE), 1)
    denom = jnp.sum(jnp.where(lane == HEAD_DIM, acc, 0.0), axis=-1, keepdims=True)
    o_ref[0, 0] = (acc / jnp.where(denom > 0.0, denom, 1.0)).astype(BF16)


def _fox_attention(shift, fe, fx, *, t=FOX_TILE):
    B, _, TP, _ = fx.shape
    H = FOX_HEADS
    kern = functools.partial(_fox_kernel, t=t)
    return pl.pallas_call(
        kern, out_shape=jax.ShapeDtypeStruct((B, H, TP, LANE), BF16),
        grid=(B, H, TP // t),
        in_specs=[pl.BlockSpec(memory_space=pltpu.SMEM),
                  pl.BlockSpec(memory_space=pltpu.SMEM),
                  pl.BlockSpec((1, 1, t, LANE), lambda b, h, i: (b, h, i, 0)),
                  pl.BlockSpec((1, 1, TP, LANE), lambda b, h, i: (b, h + H, 0, 0)),
                  pl.BlockSpec((1, 1, TP, LANE), lambda b, h, i: (b, h + 2 * H, 0, 0))],
        out_specs=pl.BlockSpec((1, 1, t, LANE), lambda b, h, i: (b, h, i, 0)),
        compiler_params=pltpu.CompilerParams(
            dimension_semantics=("arbitrary",) * 3, vmem_limit_bytes=VMEM_LIMIT),
        name="fox_attention",
    )(shift, fe, fx, fx, fx)


def _merge_kernel(h_ref, ya_ref, yb2_ref, yb1_ref, yc_ref, gate_ref,
                  wpa_ref, wpb_ref, wpc_ref, wout_ref, o_ref, *, tm, n_valid):
    i = pl.program_id(1)

    pa = _dot(ya_ref[0, 0], wpa_ref[0])
    for hh in range(1, PAIRS):
        pa = pa + _dot(ya_ref[0, hh], wpa_ref[hh])

    def pair_proj(y2_ref, y1_ref, w_ref):
        return (_dot(y2_ref[0, 0], w_ref[0]) + _dot(y2_ref[0, 1], w_ref[1])
                + _dot(y1_ref[0, 0], w_ref[2]))

    merged = gate_ref[0, :, 0:D_MODEL].astype(F32) * pa
    merged = merged + (gate_ref[0, :, D_MODEL:2 * D_MODEL].astype(F32)
                       * pair_proj(yb2_ref, yb1_ref, wpb_ref))
    pc = _dot(yc_ref[0, 0], wpc_ref[0])
    for hh in range(1, FOX_HEADS):
        pc = pc + _dot(yc_ref[0, hh], wpc_ref[hh])
    merged = merged + gate_ref[0, :, 2 * D_MODEL:3 * D_MODEL].astype(F32) * pc
    out = h_ref[0] + _dot(merged.astype(BF16), wout_ref[...])
    t = i * tm + lax.broadcasted_iota(jnp.int32, (tm, 1), 0)
    o_ref[0] = jnp.where((t >= PAD) & (t < PAD + n_valid), out, 0.0)


def _merge(h, ya, yb2, yb1, yc, gates, wpa, wpb, wpc, wout, *, n_valid, tm=256):
    B, TP, D = h.shape
    kern = functools.partial(_merge_kernel, tm=tm, n_valid=n_valid)
    pair_blk = lambda ns: pl.BlockSpec((1, ns, tm, LANE), lambda b, i: (b, 0, i, 0))
    return pl.pallas_call(
        kern, out_shape=jax.ShapeDtypeStruct((B, TP, D), F32), grid=(B, TP // tm),
        in_specs=[pl.BlockSpec((1, tm, D), lambda b, i: (b, i, 0)),
                  pair_blk(PAIRS),
                  pair_blk(2), pair_blk(1), pair_blk(FOX_HEADS),
                  pl.BlockSpec((1, tm, 3 * D), lambda b, i: (b, i, 0)),
                  _const_spec((PAIRS, LANE, D)),
                  _const_spec((PAIRS, LANE, D)),
                  _const_spec((FOX_HEADS, LANE, D)),
                  _const_spec((D, D))],
        out_specs=pl.BlockSpec((1, tm, D), lambda b, i: (b, i, 0)),
        compiler_params=pltpu.CompilerParams(
            dimension_semantics=("arbitrary", "arbitrary"), vmem_limit_bytes=VMEM_LIMIT),
        name="merge",
    )(h, ya, yb2, yb1, yc, gates, wpa, wpb, wpc, wout)


def _moe_kernel(h_ref, g_ref, wr_ref, br_ref, wg_ref, wu_ref, wd_ref, o_ref, he_ref,
                *, tm, n_valid):
    i = pl.program_id(1)
    x = h_ref[0]
    ms = jnp.mean(x * x, axis=-1, keepdims=True)
    n = x * lax.rsqrt(ms + NORM_EPS) * g_ref[...]

    lg = _dot(n, wr_ref[...], precision=HI) + br_ref[...]
    lane = lax.broadcasted_iota(jnp.int32, (tm, LANE), 1)
    big = jnp.int32(LANE)
    is_group = lane < N_GROUPS
    gl = jnp.where(is_group, lg, -jnp.inf)
    gmax = jnp.max(gl, axis=-1, keepdims=True)
    g_idx = jnp.min(jnp.where(is_group & (gl == gmax), lane, big), axis=-1, keepdims=True)
    p_group = 1.0 / jnp.sum(jnp.where(is_group, jnp.exp(gl - gmax), 0.0), axis=-1, keepdims=True)
    lo = N_GROUPS + g_idx * EXPERTS_PER_GROUP
    in_grp = (lane >= lo) & (lane < lo + EXPERTS_PER_GROUP)
    el = jnp.where(in_grp, lg, -jnp.inf)
    top1 = jnp.max(el, axis=-1, keepdims=True)
    i1 = jnp.min(jnp.where(in_grp & (el == top1), lane, big), axis=-1, keepdims=True)
    rest = in_grp & (lane != i1)
    el2 = jnp.where(rest, lg, -jnp.inf)
    top2 = jnp.max(el2, axis=-1, keepdims=True)
    i2 = jnp.min(jnp.where(rest & (el2 == top2), lane, big), axis=-1, keepdims=True)
    e2 = jnp.exp(top2 - top1)
    p1 = 1.0 / (1.0 + e2)
    p2 = e2 / (1.0 + e2)
    comb = p_group * (jnp.where(lane == i1, p1, 0.0) + jnp.where(lane == i2, p2, 0.0))

    n16 = n.astype(BF16)
    gate = _dot(n16, wg_ref[...])
    up = _dot(n16, wu_ref[...])
    he = gate * _sigmoid(gate) * up
    for e in range(N_EXPERTS):
        c_e = comb[:, N_GROUPS + e:N_GROUPS + e + 1]
        he_ref[:, e * D_EXPERT:(e + 1) * D_EXPERT] = (
            he[:, e * D_EXPERT:(e + 1) * D_EXPERT] * c_e).astype(BF16)
    out = x + _dot(he_ref[...], wd_ref[...])
    t = i * tm + lax.broadcasted_iota(jnp.int32, (tm, 1), 0)
    o_ref[0] = jnp.where((t >= PAD) & (t < PAD + n_valid), out, 0.0)


def _moe(h, g2, wr, br, wg, wu, wd, *, n_valid, tm=256):
    B, TP, D = h.shape
    kern = functools.partial(_moe_kernel, tm=tm, n_valid=n_valid)
    EW = N_EXPERTS * D_EXPERT
    return pl.pallas_call(
        kern, out_shape=jax.ShapeDtypeStruct((B, TP, D), F32), grid=(B, TP // tm),
        in_specs=[pl.BlockSpec((1, tm, D), lambda b, i: (b, i, 0)),
                  _const_spec((1, D)), _const_spec((D, LANE)), _const_spec((1, LANE)),
                  _const_spec((D, EW)), _const_spec((D, EW)), _const_spec((EW, D))],
        out_specs=pl.BlockSpec((1, tm, D), lambda b, i: (b, i, 0)),
        scratch_shapes=[pltpu.VMEM((tm, EW), BF16)],
        compiler_params=pltpu.CompilerParams(
            dimension_semantics=("arbitrary", "arbitrary"), vmem_limit_bytes=VMEM_LIMIT),
        name="moe",
    )(h, g2, wr, br, wg, wu, wd)


def _pad_pairs(w, heads):
    return jnp.pad(w, ((0, 0), (0, PAIRS * LANE - heads * HEAD_DIM)))


def _pack_w_in(w_in):
    D = w_in.shape[0]
    rest = w_in[:, SHIFT_W:]
    cols = [w_in[:, :SHIFT_W]]
    off = 0
    for heads in (SB_HEADS,) * 3 + (FOX_HEADS,) * 3:
        cols.append(_pad_pairs(rest[:, off:off + heads * HEAD_DIM], heads))
        off += heads * HEAD_DIM
    cols.append(jnp.pad(rest[:, off:off + FOX_HEADS], ((0, 0), (0, LANE - FOX_HEADS))))
    cols.append(rest[:, off + FOX_HEADS:])
    return jnp.concatenate(cols, axis=1).astype(BF16)


def _per_pair_cols(w):
    return w.reshape(w.shape[0], PAIRS, LANE).transpose(1, 0, 2)


def kernel(x, meta_tokens, norm1_g, w_in, rwkv_mu, rwkv_w_up, rwkv_w0, rwkv_a_up, rwkv_a0,
           rwkv_k_k, rwkv_k_a, rwkv_r_k, rwkv_ln_g, rwkv_ln_b, fox_f_b, fox_q_g, fox_k_g,
           w_p_rwkv, w_p_sb, w_p_fox, w_out, norm2_g, moe_wg, moe_bg, moe_we, moe_be,
           moe_w_gate, moe_w_up, moe_w_down):
    B, S, D = x.shape
    depth = w_in.shape[0]
    L = N_META + S
    TP = -(-(PAD + L) // ROW_TILE) * ROW_TILE
    meta = jnp.broadcast_to(meta_tokens[None].astype(x.dtype), (B, N_META, D))
    h = jnp.concatenate([jnp.zeros((B, PAD, D), x.dtype), meta, x,
                         jnp.zeros((B, TP - PAD - L, D), x.dtype)], axis=1)
    H = RWKV_HEADS
    EW = N_EXPERTS * D_EXPERT
    for l in range(depth):
        fb = jnp.zeros((1, LANE), F32).at[0, :FOX_HEADS].set(fox_f_b[l])
        rkv, wdad, sb, fx, fend, gates = _inproj(
            h, norm1_g[l][None], _pack_w_in(w_in[l]), rwkv_mu[l][None], fb,
            jnp.tile(fox_q_g[l], 2)[None], jnp.tile(fox_k_g[l], 2)[None])
        hv = lambda p: p.reshape(PAIRS, 1, LANE)
        gm, hc, r2, y0, bonus = _rwkv_chunks(
            rkv, wdad, _per_pair_cols(rwkv_w_up[l]), hv(rwkv_w0[l]),
            _per_pair_cols(rwkv_a_up[l]), hv(rwkv_a0[l]),
            hv(rwkv_k_k[l]), hv(rwkv_k_a[l]), hv(rwkv_r_k[l]))
        ya = _rwkv_scan(gm, hc, r2, y0, bonus, hv(rwkv_ln_g[l]), hv(rwkv_ln_b[l]))
        yb2 = _sb_attention(sb, slot0=0, nslots=2, nh=2)
        yb1 = _sb_attention(sb, slot0=2, nslots=1, nh=1)
        shift = (8.0 * LOG2E * jnp.max(jnp.abs(fox_q_g[l])) * jnp.max(jnp.abs(fox_k_g[l]))).reshape(1)
        step = FOX_TILE // INPROJ_TILE
        fe = jnp.transpose(fend[:, step - 1::step, 0, :FOX_HEADS], (0, 2, 1))
        yc = _fox_attention(shift, fe, fx)
        pc = jnp.pad(w_p_fox[l].reshape(FOX_HEADS, HEAD_DIM, D),
                     ((0, 0), (0, LANE - HEAD_DIM), (0, 0))).astype(BF16)
        pp = lambda w, nh: jnp.pad(w, ((0, PAIRS * LANE - nh * HEAD_DIM), (0, 0))).reshape(
            PAIRS, LANE, D).astype(BF16)
        h = _merge(h, ya, yb2, yb1, yc, gates, pp(w_p_rwkv[l], H), pp(w_p_sb[l], SB_HEADS),
                   pc, w_out[l].astype(BF16), n_valid=L)
        wr = jnp.zeros((D, LANE), F32).at[:, :N_GROUPS].set(moe_wg[l])
        wr = wr.at[:, N_GROUPS:N_GROUPS + N_EXPERTS].set(moe_we[l])
        br = jnp.zeros((1, LANE), F32).at[0, :N_GROUPS].set(moe_bg[l])
        br = br.at[0, N_GROUPS:N_GROUPS + N_EXPERTS].set(moe_be[l])
        wg = moe_w_gate[l].transpose(1, 0, 2).reshape(D, EW).astype(BF16)
        wu = moe_w_up[l].transpose(1, 0, 2).reshape(D, EW).astype(BF16)
        wd = moe_w_down[l].reshape(EW, D).astype(BF16)
        h = _moe(h, norm2_g[l][None], wr, br, wg, wu, wd, n_valid=L)
    return h[:, PAD + N_META:PAD + L]
```

```python
import functools
import math

import jax
import jax.numpy as jnp
from jax import lax
from jax.experimental import pallas as pl
from jax.experimental.pallas import tpu as pltpu

D_MODEL = 1024
HEAD_DIM = 64
N_META = 16
PAD = 128 - N_META
RWKV_HEADS = 6
SB_HEADS = 5
FOX_HEADS = 5
RWKV_W = RWKV_HEADS * HEAD_DIM
SB_W = SB_HEADS * HEAD_DIM
FOX_W = FOX_HEADS * HEAD_DIM
DECAY_RANK = 64
ICLR_RANK = 64
SHIFT_W = 3 * RWKV_W + DECAY_RANK + ICLR_RANK
N_GROUPS = 4
EXPERTS_PER_GROUP = 4
N_EXPERTS = 16
D_EXPERT = 256
NORM_EPS = 1e-6
RWKV_LN_EPS = 64e-5
NEG_INF = -1e30
ATT_SCALE = 1.0 / math.sqrt(HEAD_DIM)

LANE = 128
ROW_TILE = 512
PAIRS = 3
QKV_W = 3 * PAIRS * LANE
SEG_SB = SHIFT_W
SEG_FOX = SEG_SB + QKV_W
SEG_GATE = SEG_FOX + QKV_W + LANE
D_IN_PAD = SEG_GATE + 3 * D_MODEL
SB_CUTOFF_LOG2 = 160.0
LOG2E = 1.4426950408889634
FOX_FIXED_SHIFT_MAX = 50.0
FOX_SKIP_LOG2 = 150.0
FOX_TILE = 512
INPROJ_TILE = 256

CHUNK = 64
VMEM_LIMIT = 56 * 1024 * 1024

F32 = jnp.float32
BF16 = jnp.bfloat16
HI = lax.Precision.HIGHEST


def _log_sigmoid(x):
    return jnp.minimum(x, 0.0) - jnp.log1p(jnp.exp(-jnp.abs(x)))


def _sigmoid(x):
    return 1.0 / (1.0 + jnp.exp(-x))


def _dot(a, b, **kw):
    return jnp.dot(a, b, preferred_element_type=F32, **kw)


def _dot_nt(a, b):
    return lax.dot_general(a, b, (((1,), (1,)), ((), ())), preferred_element_type=F32)


def _dot_tn(a, b):
    return lax.dot_general(a, b, (((0,), (0,)), ((), ())), preferred_element_type=F32)


def _dot3(a, b):
    ah = a.astype(BF16)
    al = (a - ah.astype(F32)).astype(BF16)
    bh = b.astype(BF16)
    bl = (b - bh.astype(F32)).astype(BF16)
    return _dot(ah, bh) + (_dot(ah, bl) + _dot(al, bh))


def _const_spec(shape):
    n = len(shape)
    return pl.BlockSpec(shape, lambda *_: (0,) * n, pipeline_mode=pl.Buffered(1))


def _inproj_kernel(h_ref, g_ref, w_ref, mu_ref, fb_ref, fqg_ref, fkg_ref,
                   rkv_ref, wdad_ref, sb_ref, fx_ref, fend_ref, gate_ref,
                   carry_u, carry_f, *, tm):
    i = pl.program_id(1)

    @pl.when(i == 0)
    def _():
        carry_u[...] = jnp.zeros_like(carry_u)
        carry_f[...] = jnp.zeros_like(carry_f)

    x = h_ref[0]
    ms = jnp.mean(x * x, axis=-1, keepdims=True)
    n = (x * lax.rsqrt(ms + NORM_EPS) * g_ref[...]).astype(BF16)
    row = lax.broadcasted_iota(jnp.int32, (tm, 1), 0)

    us = _dot(n, w_ref[:, 0:SHIFT_W])
    prev = pltpu.roll(us, 1, axis=0)
    prev = jnp.where(row == 0, carry_u[...], prev)
    carry_u[...] = us[tm - 1:tm, :]
    ush = us + (prev - us) * mu_ref[...]
    for j in range(3 * PAIRS):
        rkv_ref[0, j] = ush[:, j * LANE:(j + 1) * LANE]
    wdad_ref[0] = ush[:, 3 * RWKV_W:SHIFT_W]

    usb = _dot(n, w_ref[:, SEG_SB:SEG_SB + QKV_W])
    for j in range(3 * PAIRS):
        piece = usb[:, j * LANE:(j + 1) * LANE]
        if j < PAIRS:
            piece = piece * (ATT_SCALE * LOG2E)
        sb_ref[0, j] = piece.astype(BF16)

    uf = _dot(n, w_ref[:, SEG_FOX:SEG_FOX + QKV_W + LANE])
    lane = lax.broadcasted_iota(jnp.int32, (tm, LANE), 1)
    first = lane < HEAD_DIM
    t_glob = i * tm + row
    logf = _log_sigmoid(uf[:, QKV_W:QKV_W + LANE] + fb_ref[...])
    logf = jnp.where((lane < FOX_HEADS) & (t_glob >= PAD), logf, 0.0)
    tri = (lax.broadcasted_iota(jnp.int32, (tm, tm), 0)
           >= lax.broadcasted_iota(jnp.int32, (tm, tm), 1)).astype(F32)
    cum = _dot(tri, logf, precision=HI) + carry_f[...]
    carry_f[...] = cum[tm - 1:tm, :]
    fend_ref[0, 0] = cum[tm - 1:tm, :] * LOG2E

    def split3(hd):
        f2 = jnp.sum(jnp.where(lane == hd, cum, 0.0), axis=-1, keepdims=True) * LOG2E
        hi = f2.astype(BF16).astype(F32)
        mid = (f2 - hi).astype(BF16).astype(F32)
        return hi, mid, f2 - hi - mid

    def tail_cols(vals):
        out = jnp.zeros((tm, LANE), F32)
        for o, val in enumerate(vals):
            out = jnp.where(lane == HEAD_DIM + o, val, out)
        return out

    splits = [split3(hd) for hd in range(FOX_HEADS)]
    for j in range(3 * PAIRS):
        piece = uf[:, j * LANE:(j + 1) * LANE]
        kind = j // PAIRS
        if kind < 2:
            gain = fqg_ref[...] if kind == 0 else fkg_ref[...]
            sq = piece * piece
            ms0 = jnp.sum(jnp.where(first, sq, 0.0), axis=-1, keepdims=True) * (1.0 / HEAD_DIM)
            ms1 = jnp.sum(jnp.where(first, 0.0, sq), axis=-1, keepdims=True) * (1.0 / HEAD_DIM)
            inv = jnp.where(first, lax.rsqrt(ms0 + NORM_EPS), lax.rsqrt(ms1 + NORM_EPS))
            piece = piece * inv * gain
            if kind == 0:
                piece = piece * (ATT_SCALE * LOG2E)
        swapped = pltpu.roll(piece, HEAD_DIM, axis=1)
        for half in range(2):
            hd = 2 * (j % PAIRS) + half
            if hd >= FOX_HEADS:
                continue
            if kind == 2:
                extra = tail_cols([1.0])
            else:
                hi, mid, lo = splits[hd]
                extra = tail_cols([hi, mid, lo, 1.0, 1.0, 1.0] if kind == 0
                                  else [1.0, 1.0, 1.0, -hi, -mid, -lo])
            body = piece if half == 0 else swapped
            fx_ref[0, kind * FOX_HEADS + hd] = jnp.where(first, body, extra).astype(BF16)

    ug = _dot(n, w_ref[:, SEG_GATE:D_IN_PAD])
    gate_ref[0] = _sigmoid(ug).astype(BF16)


def _inproj(h, g1, w_in_p, mu, fb, fqg, fkg, *, tm=INPROJ_TILE):
    B, TP, D = h.shape
    nb = TP // tm
    kern = functools.partial(_inproj_kernel, tm=tm)
    out_shape = (
        jax.ShapeDtypeStruct((B, 3 * PAIRS, TP, LANE), F32),
        jax.ShapeDtypeStruct((B, TP, LANE), F32),
        jax.ShapeDtypeStruct((B, 3 * PAIRS, TP, LANE), BF16),
        jax.ShapeDtypeStruct((B, 3 * FOX_HEADS, TP, LANE), BF16),
        jax.ShapeDtypeStruct((B, nb, 1, LANE), F32),
        jax.ShapeDtypeStruct((B, TP, 3 * D_MODEL), BF16),
    )
    in_specs = [
        pl.BlockSpec((1, tm, D), lambda b, i: (b, i, 0)),
        _const_spec((1, D)),
        _const_spec((D, D_IN_PAD)),
        _const_spec((1, SHIFT_W)),
        _const_spec((1, LANE)),
        _const_spec((1, LANE)),
        _const_spec((1, LANE)),
    ]
    out_specs = (
        pl.BlockSpec((1, 3 * PAIRS, tm, LANE), lambda b, i: (b, 0, i, 0)),
        pl.BlockSpec((1, tm, LANE), lambda b, i: (b, i, 0)),
        pl.BlockSpec((1, 3 * PAIRS, tm, LANE), lambda b, i: (b, 0, i, 0)),
        pl.BlockSpec((1, 3 * FOX_HEADS, tm, LANE), lambda b, i: (b, 0, i, 0)),
        pl.BlockSpec((1, 1, 1, LANE), lambda b, i: (b, i, 0, 0)),
        pl.BlockSpec((1, tm, 3 * D_MODEL), lambda b, i: (b, i, 0)),
    )
    return pl.pallas_call(
        kern, out_shape=out_shape, grid=(B, nb), in_specs=in_specs, out_specs=out_specs,
        scratch_shapes=[pltpu.VMEM((1, SHIFT_W), F32), pltpu.VMEM((1, LANE), F32)],
        compiler_params=pltpu.CompilerParams(
            dimension_semantics=("arbitrary", "arbitrary"), vmem_limit_bytes=VMEM_LIMIT),
        name="inproj",
    )(h, g1, w_in_p, mu, fb, fqg, fkg)


def _bdot(a, b):
    return lax.dot_general(a, b, (((2,), (1,)), ((0,), (0,))), preferred_element_type=F32)


def _bdot_nt(a, b):
    return lax.dot_general(a, b, (((2,), (2,)), ((0,), (0,))), preferred_element_type=F32)


def _bdot_tn(a, b):
    return lax.dot_general(a, b, (((1,), (1,)), ((0,), (0,))), preferred_element_type=F32)


def _head_sum(x, first):
    s0 = jnp.sum(jnp.where(first, x, 0.0), axis=-1, keepdims=True)
    s1 = jnp.sum(jnp.where(first, 0.0, x), axis=-1, keepdims=True)
    return jnp.where(first, s0, s1)


def _rwkv_chunk_kernel(r_ref, k_ref, v_ref, wdad_ref, wup_ref, w0_ref, aup_ref, a0_ref,
                       kk_ref, ka_ref, rk_ref,
                       g_ref, hc_ref, r2_ref, y0_ref, bonus_ref, *, rows):
    nc = rows // CHUNK
    r = r_ref[0, 0]
    k = k_ref[0, 0]
    v = v_ref[0, 0]
    wd = wdad_ref[0][:, 0:DECAY_RANK]
    ad = wdad_ref[0][:, DECAY_RANK:DECAY_RANK + ICLR_RANK]
    first = lax.broadcasted_iota(jnp.int32, (rows, LANE), 1) < HEAD_DIM

    pre = w0_ref[0] + _dot3(jnp.tanh(wd), wup_ref[0])
    lw = -jnp.exp(_log_sigmoid(pre) - 0.5)
    iclr = _sigmoid(a0_ref[0] + _dot3(ad, aup_ref[0]))
    kk = k * kk_ref[0]
    kk = kk / jnp.maximum(jnp.sqrt(_head_sum(kk * kk, first)), 1e-12)
    k2 = k * (1.0 + (iclr - 1.0) * ka_ref[0])
    b = kk * iclr
    bonus_ref[0, 0] = _head_sum(r * k2 * rk_ref[0], first) * v

    to3 = lambda x: x.reshape(nc, CHUNK, LANE)
    ri = lax.broadcasted_iota(jnp.int32, (nc, CHUNK, CHUNK), 1)
    ci = lax.broadcasted_iota(jnp.int32, (nc, CHUNK, CHUNK), 2)
    low_incl = ri >= ci
    low_strict = ri > ci
    first3 = lax.broadcasted_iota(jnp.int32, (nc, CHUNK, LANE), 2) < HEAD_DIM

    lw3 = to3(lw)
    tri = low_incl.astype(BF16)
    lw_hi = lw3.astype(BF16)
    cum = _bdot(tri, lw_hi) + _bdot(tri, (lw3 - lw_hi.astype(F32)).astype(BF16))
    cum_end = cum[:, CHUNK - 1:CHUNK, :]
    e_neg = jnp.exp(-cum)
    at = to3(-kk) * jnp.exp(cum - lw3)
    rt = to3(r) * jnp.exp(cum)
    bt = (to3(b) * e_neg).astype(BF16)
    kt = (to3(k2) * e_neg).astype(BF16)
    e_rem = jnp.exp(cum_end - cum)
    bq = (to3(b) * e_rem).astype(BF16)
    kq = (to3(k2) * e_rem).astype(BF16)
    vv = to3(v).astype(BF16)

    xs, r2s, y0s = [], [], []
    for hd in range(2):
        sel = first3 if hd == 0 else jnp.logical_not(first3)
        lhs = jnp.concatenate([jnp.where(sel, at, 0.0), jnp.where(sel, rt, 0.0)],
                              axis=1).astype(BF16)
        mb = _bdot_nt(lhs, bt)
        mk = _bdot_nt(lhs, kt)
        m_ab = jnp.where(low_strict, mb[:, :CHUNK], 0.0)
        m_ak = jnp.where(low_strict, mk[:, :CHUNK], 0.0).astype(BF16)
        m_rb = jnp.where(low_incl, mb[:, CHUNK:], 0.0).astype(BF16)
        m_rk = jnp.where(low_incl, mk[:, CHUNK:], 0.0).astype(BF16)
        x = jnp.concatenate([at, _bdot(m_ak, vv)], axis=-1)
        p = m_ab
        for j in range(6):
            p16 = p.astype(BF16)
            x = x + _bdot(p16, x.astype(BF16))
            if j < 5:
                p = _bdot(p16, p16)
        ru = _bdot(m_rb, x.astype(BF16))
        xs.append(x)
        r2s.append(rt + ru[..., :LANE])
        y0s.append(ru[..., LANE:] + _bdot(m_rk, vv))

    first3w = jnp.concatenate([first3, first3], axis=-1)
    x = jnp.where(first3w, xs[0], xs[1])
    r2_ref[0, 0] = jnp.where(first3, r2s[0], r2s[1]).reshape(rows, LANE)
    y0_ref[0, 0] = jnp.where(first3, y0s[0], y0s[1]).reshape(rows, LANE)

    pg = _bdot_tn(bq, x.astype(BF16))
    ph = pg[..., LANE:] + _bdot_tn(kq, vv)
    rr = lax.broadcasted_iota(jnp.int32, (nc, LANE, LANE), 1)
    cc = lax.broadcasted_iota(jnp.int32, (nc, LANE, LANE), 2)
    same_head = (rr < HEAD_DIM) == (cc < HEAD_DIM)
    g = jnp.where(same_head, pg[..., :LANE], 0.0) + jnp.where(rr == cc, jnp.exp(cum_end), 0.0)
    g_ref[0, 0] = g.reshape(nc * LANE, LANE)
    hc_ref[0, 0] = jnp.where(same_head, ph, 0.0).reshape(nc * LANE, LANE)


def _rwkv_chunks(rkv, wdad, wup, w0, aup, a0, k_k, k_a, r_k, *, rows=512):
    B, _, TP, _ = rkv.shape
    nb = TP // rows
    kern = functools.partial(_rwkv_chunk_kernel, rows=rows)
    slot_spec = lambda off: pl.BlockSpec((1, 1, rows, LANE), lambda b, p, i: (b, p + off, i, 0))
    par_mat = pl.BlockSpec((1, DECAY_RANK, LANE), lambda b, p, i: (p, 0, 0))
    par_vec = pl.BlockSpec((1, 1, LANE), lambda b, p, i: (p, 0, 0))
    row_out = jax.ShapeDtypeStruct((B, PAIRS, TP, LANE), F32)
    mat_out = jax.ShapeDtypeStruct((B, PAIRS, 2 * TP, LANE), F32)
    row_spec = pl.BlockSpec((1, 1, rows, LANE), lambda b, p, i: (b, p, i, 0))
    mat_spec = pl.BlockSpec((1, 1, 2 * rows, LANE), lambda b, p, i: (b, p, i, 0))
    return pl.pallas_call(
        kern, out_shape=(mat_out, mat_out, row_out, row_out, row_out), grid=(B, PAIRS, nb),
        in_specs=[slot_spec(0), slot_spec(PAIRS), slot_spec(2 * PAIRS),
                  pl.BlockSpec((1, rows, LANE), lambda b, p, i: (b, i, 0)),
                  par_mat, par_vec, par_mat, par_vec, par_vec, par_vec, par_vec],
        out_specs=(mat_spec, mat_spec, row_spec, row_spec, row_spec),
        compiler_params=pltpu.CompilerParams(
            dimension_semantics=("arbitrary",) * 3, vmem_limit_bytes=VMEM_LIMIT),
        name="rwkv_chunks",
    )(rkv, rkv, rkv, wdad, wup, w0, aup, a0, k_k, k_a, r_k)


def _rwkv_scan_kernel(g_ref, hc_ref, r2_ref, y0_ref, bonus_ref, lng_ref, lnb_ref, y_ref,
                      state, *, rows):
    i = pl.program_id(1)

    @pl.when(i == 0)
    def _():
        state[...] = jnp.zeros_like(state)

    first = lax.broadcasted_iota(jnp.int32, (CHUNK, LANE), 1) < HEAD_DIM
    hs = [state[p] for p in range(PAIRS)]
    for c in range(rows // CHUNK):
        sl = slice(c * CHUNK, (c + 1) * CHUNK)
        sm = slice(c * LANE, (c + 1) * LANE)
        for p in range(PAIRS):
            y = _dot3(r2_ref[0, p, sl, :], hs[p]) + y0_ref[0, p, sl, :]
            hs[p] = _dot3(g_ref[0, p, sm, :], hs[p]) + hc_ref[0, p, sm, :]
            yc = y - _head_sum(y, first) * (1.0 / HEAD_DIM)
            var = _head_sum(yc * yc, first) * (1.0 / HEAD_DIM)
            out = yc * lax.rsqrt(var + RWKV_LN_EPS) * lng_ref[p] + lnb_ref[p]
            y_ref[0, p, sl, :] = (out + bonus_ref[0, p, sl, :]).astype(BF16)
    for p in range(PAIRS):
        state[p] = hs[p]


def _rwkv_scan(gm, hc, r2, y0, bonus, ln_g, ln_b, *, rows=512):
    B, P, TP, _ = r2.shape
    nb = TP // rows
    kern = functools.partial(_rwkv_scan_kernel, rows=rows)
    blk = pl.BlockSpec((1, P, rows, LANE), lambda b, i: (b, 0, i, 0))
    mat = pl.BlockSpec((1, P, 2 * rows, LANE), lambda b, i: (b, 0, i, 0))
    par = pl.BlockSpec((P, 1, LANE), lambda b, i: (0, 0, 0))
    return pl.pallas_call(
        kern, out_shape=jax.ShapeDtypeStruct((B, P, TP, LANE), BF16), grid=(B, nb),
        in_specs=[mat, mat, blk, blk, blk, par, par], out_specs=blk,
        scratch_shapes=[pltpu.VMEM((P, LANE, LANE), F32)],
        compiler_params=pltpu.CompilerParams(
            dimension_semantics=("arbitrary", "arbitrary"), vmem_limit_bytes=VMEM_LIMIT),
        name="rwkv_scan",
    )(gm, hc, r2, y0, bonus, ln_g, ln_b)


def _split_pair(qp, nh):
    lane = lax.broadcasted_iota(jnp.int32, qp.shape, 1)
    first = lane < HEAD_DIM
    zero = jnp.zeros_like(qp)
    qs = [jnp.where(first, qp, zero), jnp.where(first, zero, qp)]
    return qs[:nh], first


def _sb_kernel(q_ref, k_ref, v_ref, o_ref, *, t, nh):
    qi = pl.program_id(2)
    qs, first = _split_pair(q_ref[0, 0], nh)
    qpos = qi * t + lax.broadcasted_iota(jnp.int32, (t, 1), 0)
    kloc = lax.broadcasted_iota(jnp.int32, (1, t), 1)
    upper = (lax.broadcasted_iota(jnp.int32, (t, t), 0)
             > lax.broadcasted_iota(jnp.int32, (t, t), 1)).astype(BF16)

    def block(kb, accs, cs, mask):
        start = pl.multiple_of(kb * t, t)
        kblk = k_ref[0, 0, pl.ds(start, t), :]
        vblk = v_ref[0, 0, pl.ds(start, t), :]
        if mask is not None:
            mask = mask(start + kloc)
        new_accs, new_cs = [], []
        for h in range(nh):
            z = _dot_nt(qs[h], kblk)
            sp = jnp.maximum(z, 0.0) + jnp.log2(1.0 + jnp.exp2(-jnp.abs(z)))
            spm = sp if mask is None else jnp.where(mask, sp, 0.0)
            later = _dot(spm.astype(BF16), upper)
            a = jnp.exp2(z - sp - later - cs[h])
            if mask is not None:
                a = jnp.where(mask, a, 0.0)
            new_accs.append(accs[h] + _dot(a.astype(BF16), vblk))
            new_cs.append(cs[h] + jnp.sum(spm, axis=-1, keepdims=True))
        return new_accs, new_cs

    def live(cs):
        low = cs[0]
        for c in cs[1:]:
            low = jnp.minimum(low, c)
        return (jnp.min(low) < SB_CUTOFF_LOG2).astype(jnp.int32)

    zero_acc = [jnp.zeros((t, LANE), F32) for _ in range(nh)]
    zero_c = [jnp.zeros((t, 1), F32) for _ in range(nh)]
    accs, cs = block(qi, zero_acc, zero_c, lambda kpos: (kpos >= PAD) & (kpos < qpos))
    accs, cs = block(jnp.maximum(qi - 1, 0), accs, cs, lambda kpos: (kpos >= PAD) & (qi >= 1))

    def cond(carry):
        kb, alive = carry[0], carry[1]
        return (kb >= 1) & (alive > 0)

    def body(carry):
        kb = carry[0]
        accs, cs = block(kb, list(carry[2:2 + nh]), list(carry[2 + nh:]), None)
        return (kb - 1, live(cs), *accs, *cs)

    carry = lax.while_loop(cond, body, (qi - 2, live(cs), *accs, *cs))

    def front(carry):
        accs, cs = block(0, list(carry[2:2 + nh]), list(carry[2 + nh:]), lambda kpos: kpos >= PAD)
        return (carry[0], carry[1], *accs, *cs)

    carry = lax.cond((carry[0] == 0) & (carry[1] > 0), front, lambda c: c, carry)
    accs = carry[2:2 + nh]
    out = jnp.where(first, accs[0], accs[1] if nh == 2 else 0.0)
    o_ref[0, 0] = out.astype(BF16)


def _sb_attention(sb, *, slot0, nslots, nh, t=256):
    B, _, TP, _ = sb.shape
    kern = functools.partial(_sb_kernel, t=t, nh=nh)
    blk = lambda off: pl.BlockSpec((1, 1, t, LANE), lambda b, p, i: (b, slot0 + off + p, i, 0))
    full = lambda off: pl.BlockSpec((1, 1, TP, LANE), lambda b, p, i: (b, slot0 + off + p, 0, 0))
    return pl.pallas_call(
        kern, out_shape=jax.ShapeDtypeStruct((B, nslots, TP, LANE), BF16),
        grid=(B, nslots, TP // t),
        in_specs=[blk(0), full(PAIRS), full(2 * PAIRS)],
        out_specs=pl.BlockSpec((1, 1, t, LANE), lambda b, p, i: (b, p, i, 0)),
        compiler_params=pltpu.CompilerParams(
            dimension_semantics=("arbitrary",) * 3, vmem_limit_bytes=VMEM_LIMIT),
        name="sb_attention",
    )(sb, sb, sb)


def _fox_kernel(shift_ref, fe_ref, q_ref, k_ref, v_ref, o_ref, *, t):
    b = pl.program_id(0)
    h = pl.program_id(1)
    qi = pl.program_id(2)
    q = q_ref[0, 0]
    shift = shift_ref[0]
    thr = FOX_SKIP_LOG2 + 2.0 * shift
    f_q = fe_ref[b, h, jnp.maximum(qi - 1, 0)]
    lo = lax.fori_loop(
        0, qi, lambda kb, n: n + (fe_ref[b, h, kb] - f_q > thr).astype(jnp.int32), jnp.int32(0))
    qpos = qi * t + lax.broadcasted_iota(jnp.int32, (t, 1), 0)
    kloc = lax.broadcasted_iota(jnp.int32, (1, t), 1)

    def block(kb, carry, masked, online):
        acc, m = carry
        start = pl.multiple_of(kb * t, t)
        kblk = k_ref[0, 0, pl.ds(start, t), :]
        vblk = v_ref[0, 0, pl.ds(start, t), :]
        s = _dot_nt(q, kblk)
        if masked:
            kpos = start + kloc
            s = jnp.where((kpos <= qpos) & (kpos >= PAD), s, NEG_INF)
        if online:
            m_new = jnp.maximum(m, jnp.max(s, axis=-1, keepdims=True))
            acc = jnp.exp2(m - m_new) * acc
            m = m_new
        pr = jnp.exp2(s - m)
        return acc + _dot(pr.astype(BF16), vblk), m

    def sweep(online):
        def run(carry):
            carry = lax.cond(lo == 0, lambda c: block(0, c, True, online), lambda c: c, carry)
            start = jnp.maximum(lo, 1)
            n = jnp.maximum(qi - start, 0)

            def two(i, c):
                kb = start + 2 * i
                return block(kb + 1, block(kb, c, False, online), False, online)

            carry = lax.fori_loop(0, n // 2, two, carry)
            carry = lax.cond(n % 2 == 1, lambda c: block(qi - 1, c, False, online),
                             lambda c: c, carry)
            return lax.cond(qi > 0, lambda c: block(qi, c, True, online), lambda c: c, carry)
        return run

    acc0 = jnp.zeros((t, LANE), F32)
    fixed = shift <= FOX_FIXED_SHIFT_MAX
    m0 = jnp.where(fixed, jnp.full((t, 1), shift, F32), jnp.full((t, 1), NEG_INF, F32))
    acc, _ = lax.cond(fixed, sweep(False), sweep(True), (acc0, m0))
    lane = lax.broadcasted_iota(jnp.int32, (t, LANE), 1)
    denom = jnp.sum(jnp.where(lane == HEAD_DIM, acc, 0.0), axis=-1, keepdims=True)
    o_ref[0, 0] = (acc / jnp.where(denom > 0.0, denom, 1.0)).astype(BF16)


def _fox_attention(shift, fe, fx, *, t=FOX_TILE):
    B, _, TP, _ = fx.shape
    H = FOX_HEADS
    kern = functools.partial(_fox_kernel, t=t)
    return pl.pallas_call(
        kern, out_shape=jax.ShapeDtypeStruct((B, H, TP, LANE), BF16),
        grid=(B, H, TP // t),
        in_specs=[pl.BlockSpec(memory_space=pltpu.SMEM),
                  pl.BlockSpec(memory_space=pltpu.SMEM),
                  pl.BlockSpec((1, 1, t, LANE), lambda b, h, i: (b, h, i, 0)),
                  pl.BlockSpec((1, 1, TP, LANE), lambda b, h, i: (b, h + H, 0, 0)),
                  pl.BlockSpec((1, 1, TP, LANE), lambda b, h, i: (b, h + 2 * H, 0, 0))],
        out_specs=pl.BlockSpec((1, 1, t, LANE), lambda b, h, i: (b, h, i, 0)),
        compiler_params=pltpu.CompilerParams(
            dimension_semantics=("arbitrary",) * 3, vmem_limit_bytes=VMEM_LIMIT),
        name="fox_attention",
    )(shift, fe, fx, fx, fx)


def _merge_kernel(h_ref, ya_ref, yb2_ref, yb1_ref, yc_ref, gate_ref,
                  wpa_ref, wpb_ref, wpc_ref, wout_ref, o_ref, *, tm, n_valid):
    i = pl.program_id(1)

    pa = _dot(ya_ref[0, 0], wpa_ref[0])
    for hh in range(1, PAIRS):
        pa = pa + _dot(ya_ref[0, hh], wpa_ref[hh])

    def pair_proj(y2_ref, y1_ref, w_ref):
        return (_dot(y2_ref[0, 0], w_ref[0]) + _dot(y2_ref[0, 1], w_ref[1])
                + _dot(y1_ref[0, 0], w_ref[2]))

    merged = gate_ref[0, :, 0:D_MODEL].astype(F32) * pa
    merged = merged + (gate_ref[0, :, D_MODEL:2 * D_MODEL].astype(F32)
                       * pair_proj(yb2_ref, yb1_ref, wpb_ref))
    pc = _dot(yc_ref[0, 0], wpc_ref[0])
    for hh in range(1, FOX_HEADS):
        pc = pc + _dot(yc_ref[0, hh], wpc_ref[hh])
    merged = merged + gate_ref[0, :, 2 * D_MODEL:3 * D_MODEL].astype(F32) * pc
    out = h_ref[0] + _dot(merged.astype(BF16), wout_ref[...])
    t = i * tm + lax.broadcasted_iota(jnp.int32, (tm, 1), 0)
    o_ref[0] = jnp.where((t >= PAD) & (t < PAD + n_valid), out, 0.0)


def _merge(h, ya, yb2, yb1, yc, gates, wpa, wpb, wpc, wout, *, n_valid, tm=256):
    B, TP, D = h.shape
    kern = functools.partial(_merge_kernel, tm=tm, n_valid=n_valid)
    pair_blk = lambda ns: pl.BlockSpec((1, ns, tm, LANE), lambda b, i: (b, 0, i, 0))
    return pl.pallas_call(
        kern, out_shape=jax.ShapeDtypeStruct((B, TP, D), F32), grid=(B, TP // tm),
        in_specs=[pl.BlockSpec((1, tm, D), lambda b, i: (b, i, 0)),
                  pair_blk(PAIRS),
                  pair_blk(2), pair_blk(1), pair_blk(FOX_HEADS),
                  pl.BlockSpec((1, tm, 3 * D), lambda b, i: (b, i, 0)),
                  _const_spec((PAIRS, LANE, D)),
                  _const_spec((PAIRS, LANE, D)),
                  _const_spec((FOX_HEADS, LANE, D)),
                  _const_spec((D, D))],
        out_specs=pl.BlockSpec((1, tm, D), lambda b, i: (b, i, 0)),
        compiler_params=pltpu.CompilerParams(
            dimension_semantics=("arbitrary", "arbitrary"), vmem_limit_bytes=VMEM_LIMIT),
        name="merge",
    )(h, ya, yb2, yb1, yc, gates, wpa, wpb, wpc, wout)


def _moe_kernel(h_ref, g_ref, wr_ref, br_ref, wg_ref, wu_ref, wd_ref, o_ref, he_ref,
                *, tm, n_valid):
    i = pl.program_id(1)
    x = h_ref[0]
    ms = jnp.mean(x * x, axis=-1, keepdims=True)
    n = x * lax.rsqrt(ms + NORM_EPS) * g_ref[...]

    lg = _dot(n, wr_ref[...], precision=HI) + br_ref[...]
    lane = lax.broadcasted_iota(jnp.int32, (tm, LANE), 1)
    big = jnp.int32(LANE)
    is_group = lane < N_GROUPS
    gl = jnp.where(is_group, lg, -jnp.inf)
    gmax = jnp.max(gl, axis=-1, keepdims=True)
    g_idx = jnp.min(jnp.where(is_group & (gl == gmax), lane, big), axis=-1, keepdims=True)
    p_group = 1.0 / jnp.sum(jnp.where(is_group, jnp.exp(gl - gmax), 0.0), axis=-1, keepdims=True)
    lo = N_GROUPS + g_idx * EXPERTS_PER_GROUP
    in_grp = (lane >= lo) & (lane < lo + EXPERTS_PER_GROUP)
    el = jnp.where(in_grp, lg, -jnp.inf)
    top1 = jnp.max(el, axis=-1, keepdims=True)
    i1 = jnp.min(jnp.where(in_grp & (el == top1), lane, big), axis=-1, keepdims=True)
    rest = in_grp & (lane != i1)
    el2 = jnp.where(rest, lg, -jnp.inf)
    top2 = jnp.max(el2, axis=-1, keepdims=True)
    i2 = jnp.min(jnp.where(rest & (el2 == top2), lane, big), axis=-1, keepdims=True)
    e2 = jnp.exp(top2 - top1)
    p1 = 1.0 / (1.0 + e2)
    p2 = e2 / (1.0 + e2)
    comb = p_group * (jnp.where(lane == i1, p1, 0.0) + jnp.where(lane == i2, p2, 0.0))

    n16 = n.astype(BF16)
    for e in range(N_EXPERTS):
        gate = _dot(n16, wg_ref[e])
        up = _dot(n16, wu_ref[e])
        c_e = comb[:, N_GROUPS + e:N_GROUPS + e + 1]
        he_ref[:, e * D_EXPERT:(e + 1) * D_EXPERT] = (
            gate * _sigmoid(gate) * up * c_e).astype(BF16)
    out = x + _dot(he_ref[...], wd_ref[...])
    t = i * tm + lax.broadcasted_iota(jnp.int32, (tm, 1), 0)
    o_ref[0] = jnp.where((t >= PAD) & (t < PAD + n_valid), out, 0.0)


def _moe(h, g2, wr, br, wg, wu, wd, *, n_valid, tm=256):
    B, TP, D = h.shape
    kern = functools.partial(_moe_kernel, tm=tm, n_valid=n_valid)
    EW = N_EXPERTS * D_EXPERT
    return pl.pallas_call(
        kern, out_shape=jax.ShapeDtypeStruct((B, TP, D), F32), grid=(B, TP // tm),
        in_specs=[pl.BlockSpec((1, tm, D), lambda b, i: (b, i, 0)),
                  _const_spec((1, D)), _const_spec((D, LANE)), _const_spec((1, LANE)),
                  _const_spec((N_EXPERTS, D, D_EXPERT)), _const_spec((N_EXPERTS, D, D_EXPERT)),
                  _const_spec((EW, D))],
        out_specs=pl.BlockSpec((1, tm, D), lambda b, i: (b, i, 0)),
        scratch_shapes=[pltpu.VMEM((tm, EW), BF16)],
        compiler_params=pltpu.CompilerParams(
            dimension_semantics=("arbitrary", "arbitrary"), vmem_limit_bytes=VMEM_LIMIT),
        name="moe",
    )(h, g2, wr, br, wg, wu, wd)


def _pad_pairs(w, heads):
    return jnp.pad(w, ((0, 0), (0, PAIRS * LANE - heads * HEAD_DIM)))


def _pack_w_in(w_in):
    D = w_in.shape[0]
    rest = w_in[:, SHIFT_W:]
    cols = [w_in[:, :SHIFT_W]]
    off = 0
    for heads in (SB_HEADS,) * 3 + (FOX_HEADS,) * 3:
        cols.append(_pad_pairs(rest[:, off:off + heads * HEAD_DIM], heads))
        off += heads * HEAD_DIM
    cols.append(jnp.pad(rest[:, off:off + FOX_HEADS], ((0, 0), (0, LANE - FOX_HEADS))))
    cols.append(rest[:, off + FOX_HEADS:])
    return jnp.concatenate(cols, axis=1).astype(BF16)


def _per_pair_cols(w):
    return w.reshape(w.shape[0], PAIRS, LANE).transpose(1, 0, 2)


def kernel(x, meta_tokens, norm1_g, w_in, rwkv_mu, rwkv_w_up, rwkv_w0, rwkv_a_up, rwkv_a0,
           rwkv_k_k, rwkv_k_a, rwkv_r_k, rwkv_ln_g, rwkv_ln_b, fox_f_b, fox_q_g, fox_k_g,
           w_p_rwkv, w_p_sb, w_p_fox, w_out, norm2_g, moe_wg, moe_bg, moe_we, moe_be,
           moe_w_gate, moe_w_up, moe_w_down):
    B, S, D = x.shape
    depth = w_in.shape[0]
    L = N_META + S
    TP = -(-(PAD + L) // ROW_TILE) * ROW_TILE
    meta = jnp.broadcast_to(meta_tokens[None].astype(x.dtype), (B, N_META, D))
    h = jnp.concatenate([jnp.zeros((B, PAD, D), x.dtype), meta, x,
                         jnp.zeros((B, TP - PAD - L, D), x.dtype)], axis=1)
    H = RWKV_HEADS
    EW = N_EXPERTS * D_EXPERT
    for l in range(depth):
        fb = jnp.zeros((1, LANE), F32).at[0, :FOX_HEADS].set(fox_f_b[l])
        rkv, wdad, sb, fx, fend, gates = _inproj(
            h, norm1_g[l][None], _pack_w_in(w_in[l]), rwkv_mu[l][None], fb,
            jnp.tile(fox_q_g[l], 2)[None], jnp.tile(fox_k_g[l], 2)[None])
        hv = lambda p: p.reshape(PAIRS, 1, LANE)
        gm, hc, r2, y0, bonus = _rwkv_chunks(
            rkv, wdad, _per_pair_cols(rwkv_w_up[l]), hv(rwkv_w0[l]),
            _per_pair_cols(rwkv_a_up[l]), hv(rwkv_a0[l]),
            hv(rwkv_k_k[l]), hv(rwkv_k_a[l]), hv(rwkv_r_k[l]))
        ya = _rwkv_scan(gm, hc, r2, y0, bonus, hv(rwkv_ln_g[l]), hv(rwkv_ln_b[l]))
        yb2 = _sb_attention(sb, slot0=0, nslots=2, nh=2)
        yb1 = _sb_attention(sb, slot0=2, nslots=1, nh=1)
        shift = (8.0 * LOG2E * jnp.max(jnp.abs(fox_q_g[l])) * jnp.max(jnp.abs(fox_k_g[l]))).reshape(1)
        step = FOX_TILE // INPROJ_TILE
        fe = jnp.transpose(fend[:, step - 1::step, 0, :FOX_HEADS], (0, 2, 1))
        yc = _fox_attention(shift, fe, fx)
        pc = jnp.pad(w_p_fox[l].reshape(FOX_HEADS, HEAD_DIM, D),
                     ((0, 0), (0, LANE - HEAD_DIM), (0, 0))).astype(BF16)
        pp = lambda w, nh: jnp.pad(w, ((0, PAIRS * LANE - nh * HEAD_DIM), (0, 0))).reshape(
            PAIRS, LANE, D).astype(BF16)
        h = _merge(h, ya, yb2, yb1, yc, gates, pp(w_p_rwkv[l], H), pp(w_p_sb[l], SB_HEADS),
                   pc, w_out[l].astype(BF16), n_valid=L)
        wr = jnp.zeros((D, LANE), F32).at[:, :N_GROUPS].set(moe_wg[l])
        wr = wr.at[:, N_GROUPS:N_GROUPS + N_EXPERTS].set(moe_we[l])
        br = jnp.zeros((1, LANE), F32).at[0, :N_GROUPS].set(moe_bg[l])
        br = br.at[0, N_GROUPS:N_GROUPS + N_EXPERTS].set(moe_be[l])
        wg = moe_w_gate[l].astype(BF16)
        wu = moe_w_up[l].astype(BF16)
        wd = moe_w_down[l].reshape(EW, D).astype(BF16)
        h = _moe(h, norm2_g[l][None], wr, br, wg, wu, wd, n_valid=L)
    return h[:, PAD + N_META:PAD + L]
```

```python
import functools
import math

import jax
import jax.numpy as jnp
from jax import lax
from jax.experimental import pallas as pl
from jax.experimental.pallas import tpu as pltpu

D_MODEL = 1024
HEAD_DIM = 64
N_META = 16
PAD = 128 - N_META
RWKV_HEADS = 6
SB_HEADS = 5
FOX_HEADS = 5
RWKV_W = RWKV_HEADS * HEAD_DIM
SB_W = SB_HEADS * HEAD_DIM
FOX_W = FOX_HEADS * HEAD_DIM
DECAY_RANK = 64
ICLR_RANK = 64
SHIFT_W = 3 * RWKV_W + DECAY_RANK + ICLR_RANK
N_GROUPS = 4
EXPERTS_PER_GROUP = 4
N_EXPERTS = 16
D_EXPERT = 256
NORM_EPS = 1e-6
RWKV_LN_EPS = 64e-5
NEG_INF = -1e30
ATT_SCALE = 1.0 / math.sqrt(HEAD_DIM)

LANE = 128
ROW_TILE = 512
PAIRS = 3
QKV_W = 3 * PAIRS * LANE
SEG_SB = SHIFT_W
SEG_FOX = SEG_SB + QKV_W
SEG_GATE = SEG_FOX + QKV_W + LANE
D_IN_PAD = SEG_GATE + 3 * D_MODEL
SB_CUTOFF_LOG2 = 160.0
LOG2E = 1.4426950408889634
FOX_FIXED_SHIFT_MAX = 50.0
FOX_SKIP_LOG2 = 150.0
FOX_TILE = 512
INPROJ_TILE = 256

CHUNK = 64
VMEM_LIMIT = 56 * 1024 * 1024

F32 = jnp.float32
BF16 = jnp.bfloat16
HI = lax.Precision.HIGHEST


def _log_sigmoid(x):
    return jnp.minimum(x, 0.0) - jnp.log1p(jnp.exp(-jnp.abs(x)))


def _sigmoid(x):
    return 1.0 / (1.0 + jnp.exp(-x))


def _dot(a, b, **kw):
    return jnp.dot(a, b, preferred_element_type=F32, **kw)


def _dot_nt(a, b):
    return lax.dot_general(a, b, (((1,), (1,)), ((), ())), preferred_element_type=F32)


def _dot_tn(a, b):
    return lax.dot_general(a, b, (((0,), (0,)), ((), ())), preferred_element_type=F32)


def _dot3(a, b):
    ah = a.astype(BF16)
    al = (a - ah.astype(F32)).astype(BF16)
    bh = b.astype(BF16)
    bl = (b - bh.astype(F32)).astype(BF16)
    return _dot(ah, bh) + (_dot(ah, bl) + _dot(al, bh))


def _const_spec(shape):
    n = len(shape)
    return pl.BlockSpec(shape, lambda *_: (0,) * n, pipeline_mode=pl.Buffered(1))


def _inproj_kernel(h_ref, g_ref, w_ref, mu_ref, fb_ref, fqg_ref, fkg_ref,
                   rkv_ref, wdad_ref, sb_ref, fx_ref, fend_ref, gate_ref,
                   carry_u, carry_f, *, tm):
    i = pl.program_id(1)

    @pl.when(i == 0)
    def _():
        carry_u[...] = jnp.zeros_like(carry_u)
        carry_f[...] = jnp.zeros_like(carry_f)

    x = h_ref[0]
    ms = jnp.mean(x * x, axis=-1, keepdims=True)
    n = (x * lax.rsqrt(ms + NORM_EPS) * g_ref[...]).astype(BF16)
    row = lax.broadcasted_iota(jnp.int32, (tm, 1), 0)

    us = _dot(n, w_ref[:, 0:SHIFT_W])
    prev = pltpu.roll(us, 1, axis=0)
    prev = jnp.where(row == 0, carry_u[...], prev)
    carry_u[...] = us[tm - 1:tm, :]
    ush = us + (prev - us) * mu_ref[...]
    for j in range(3 * PAIRS):
        rkv_ref[0, j] = ush[:, j * LANE:(j + 1) * LANE]
    wdad_ref[0] = ush[:, 3 * RWKV_W:SHIFT_W]

    usb = _dot(n, w_ref[:, SEG_SB:SEG_SB + QKV_W])
    for j in range(3 * PAIRS):
        piece = usb[:, j * LANE:(j + 1) * LANE]
        if j < PAIRS:
            piece = piece * (ATT_SCALE * LOG2E)
        sb_ref[0, j] = piece.astype(BF16)

    uf = _dot(n, w_ref[:, SEG_FOX:SEG_FOX + QKV_W + LANE])
    lane = lax.broadcasted_iota(jnp.int32, (tm, LANE), 1)
    first = lane < HEAD_DIM
    t_glob = i * tm + row
    logf = _log_sigmoid(uf[:, QKV_W:QKV_W + LANE] + fb_ref[...])
    logf = jnp.where((lane < FOX_HEADS) & (t_glob >= PAD), logf, 0.0)
    tri = (lax.broadcasted_iota(jnp.int32, (tm, tm), 0)
           >= lax.broadcasted_iota(jnp.int32, (tm, tm), 1)).astype(BF16)
    f_hi = logf.astype(BF16)
    f_mid = (logf - f_hi.astype(F32)).astype(BF16)
    f_lo = (logf - f_hi.astype(F32) - f_mid.astype(F32)).astype(BF16)
    cum = _dot(tri, f_hi) + _dot(tri, f_mid) + _dot(tri, f_lo) + carry_f[...]
    carry_f[...] = cum[tm - 1:tm, :]
    fend_ref[0, 0] = cum[tm - 1:tm, :] * LOG2E

    def split3(hd):
        f2 = jnp.sum(jnp.where(lane == hd, cum, 0.0), axis=-1, keepdims=True) * LOG2E
        hi = f2.astype(BF16).astype(F32)
        mid = (f2 - hi).astype(BF16).astype(F32)
        return hi, mid, f2 - hi - mid

    def tail_cols(vals):
        out = jnp.zeros((tm, LANE), F32)
        for o, val in enumerate(vals):
            out = jnp.where(lane == HEAD_DIM + o, val, out)
        return out

    splits = [split3(hd) for hd in range(FOX_HEADS)]
    for j in range(3 * PAIRS):
        piece = uf[:, j * LANE:(j + 1) * LANE]
        kind = j // PAIRS
        if kind < 2:
            gain = fqg_ref[...] if kind == 0 else fkg_ref[...]
            sq = piece * piece
            ms0 = jnp.sum(jnp.where(first, sq, 0.0), axis=-1, keepdims=True) * (1.0 / HEAD_DIM)
            ms1 = jnp.sum(jnp.where(first, 0.0, sq), axis=-1, keepdims=True) * (1.0 / HEAD_DIM)
            inv = jnp.where(first, lax.rsqrt(ms0 + NORM_EPS), lax.rsqrt(ms1 + NORM_EPS))
            piece = piece * inv * gain
            if kind == 0:
                piece = piece * (ATT_SCALE * LOG2E)
        swapped = pltpu.roll(piece, HEAD_DIM, axis=1)
        for half in range(2):
            hd = 2 * (j % PAIRS) + half
            if hd >= FOX_HEADS:
                continue
            if kind == 2:
                extra = tail_cols([1.0])
            else:
                hi, mid, lo = splits[hd]
                extra = tail_cols([hi, mid, lo, 1.0, 1.0, 1.0] if kind == 0
                                  else [1.0, 1.0, 1.0, -hi, -mid, -lo])
            body = piece if half == 0 else swapped
            fx_ref[0, kind * FOX_HEADS + hd] = jnp.where(first, body, extra).astype(BF16)

    ug = _dot(n, w_ref[:, SEG_GATE:D_IN_PAD])
    gate_ref[0] = _sigmoid(ug).astype(BF16)


def _inproj(h, g1, w_in_p, mu, fb, fqg, fkg, *, tm=INPROJ_TILE):
    B, TP, D = h.shape
    nb = TP // tm
    kern = functools.partial(_inproj_kernel, tm=tm)
    out_shape = (
        jax.ShapeDtypeStruct((B, 3 * PAIRS, TP, LANE), F32),
        jax.ShapeDtypeStruct((B, TP, LANE), F32),
        jax.ShapeDtypeStruct((B, 3 * PAIRS, TP, LANE), BF16),
        jax.ShapeDtypeStruct((B, 3 * FOX_HEADS, TP, LANE), BF16),
        jax.ShapeDtypeStruct((B, nb, 1, LANE), F32),
        jax.ShapeDtypeStruct((B, TP, 3 * D_MODEL), BF16),
    )
    in_specs = [
        pl.BlockSpec((1, tm, D), lambda b, i: (b, i, 0)),
        _const_spec((1, D)),
        _const_spec((D, D_IN_PAD)),
        _const_spec((1, SHIFT_W)),
        _const_spec((1, LANE)),
        _const_spec((1, LANE)),
        _const_spec((1, LANE)),
    ]
    out_specs = (
        pl.BlockSpec((1, 3 * PAIRS, tm, LANE), lambda b, i: (b, 0, i, 0)),
        pl.BlockSpec((1, tm, LANE), lambda b, i: (b, i, 0)),
        pl.BlockSpec((1, 3 * PAIRS, tm, LANE), lambda b, i: (b, 0, i, 0)),
        pl.BlockSpec((1, 3 * FOX_HEADS, tm, LANE), lambda b, i: (b, 0, i, 0)),
        pl.BlockSpec((1, 1, 1, LANE), lambda b, i: (b, i, 0, 0)),
        pl.BlockSpec((1, tm, 3 * D_MODEL), lambda b, i: (b, i, 0)),
    )
    return pl.pallas_call(
        kern, out_shape=out_shape, grid=(B, nb), in_specs=in_specs, out_specs=out_specs,
        scratch_shapes=[pltpu.VMEM((1, SHIFT_W), F32), pltpu.VMEM((1, LANE), F32)],
        compiler_params=pltpu.CompilerParams(
            dimension_semantics=("arbitrary", "arbitrary"), vmem_limit_bytes=VMEM_LIMIT),
        name="inproj",
    )(h, g1, w_in_p, mu, fb, fqg, fkg)


def _bdot(a, b):
    return lax.dot_general(a, b, (((2,), (1,)), ((0,), (0,))), preferred_element_type=F32)


def _bdot_nt(a, b):
    return lax.dot_general(a, b, (((2,), (2,)), ((0,), (0,))), preferred_element_type=F32)


def _bdot_tn(a, b):
    return lax.dot_general(a, b, (((1,), (1,)), ((0,), (0,))), preferred_element_type=F32)


def _head_sum(x, first):
    s0 = jnp.sum(jnp.where(first, x, 0.0), axis=-1, keepdims=True)
    s1 = jnp.sum(jnp.where(first, 0.0, x), axis=-1, keepdims=True)
    return jnp.where(first, s0, s1)


def _rwkv_chunk_kernel(r_ref, k_ref, v_ref, wdad_ref, wup_ref, w0_ref, aup_ref, a0_ref,
                       kk_ref, ka_ref, rk_ref,
                       g_ref, hc_ref, r2_ref, y0_ref, bonus_ref, *, rows):
    nc = rows // CHUNK
    r = r_ref[0, 0]
    k = k_ref[0, 0]
    v = v_ref[0, 0]
    wd = wdad_ref[0][:, 0:DECAY_RANK]
    ad = wdad_ref[0][:, DECAY_RANK:DECAY_RANK + ICLR_RANK]
    first = lax.broadcasted_iota(jnp.int32, (rows, LANE), 1) < HEAD_DIM

    pre = w0_ref[0] + _dot3(jnp.tanh(wd), wup_ref[0])
    lw = -jnp.exp(_log_sigmoid(pre) - 0.5)
    iclr = _sigmoid(a0_ref[0] + _dot3(ad, aup_ref[0]))
    kk = k * kk_ref[0]
    kk = kk / jnp.maximum(jnp.sqrt(_head_sum(kk * kk, first)), 1e-12)
    k2 = k * (1.0 + (iclr - 1.0) * ka_ref[0])
    b = kk * iclr
    bonus_ref[0, 0] = _head_sum(r * k2 * rk_ref[0], first) * v

    to3 = lambda x: x.reshape(nc, CHUNK, LANE)
    ri = lax.broadcasted_iota(jnp.int32, (nc, CHUNK, CHUNK), 1)
    ci = lax.broadcasted_iota(jnp.int32, (nc, CHUNK, CHUNK), 2)
    low_incl = ri >= ci
    low_strict = ri > ci
    first3 = lax.broadcasted_iota(jnp.int32, (nc, CHUNK, LANE), 2) < HEAD_DIM

    lw3 = to3(lw)
    tri = low_incl.astype(BF16)
    lw_hi = lw3.astype(BF16)
    cum = _bdot(tri, lw_hi) + _bdot(tri, (lw3 - lw_hi.astype(F32)).astype(BF16))
    cum_end = cum[:, CHUNK - 1:CHUNK, :]
    e_neg = jnp.exp(-cum)
    at = to3(-kk) * jnp.exp(cum - lw3)
    rt = to3(r) * jnp.exp(cum)
    bt = (to3(b) * e_neg).astype(BF16)
    kt = (to3(k2) * e_neg).astype(BF16)
    e_rem = jnp.exp(cum_end - cum)
    bq = (to3(b) * e_rem).astype(BF16)
    kq = (to3(k2) * e_rem).astype(BF16)
    vv = to3(v).astype(BF16)

    heads = range(2)
    sels = [first3, jnp.logical_not(first3)]
    lhs = [jnp.concatenate([jnp.where(sels[hd], at, 0.0), jnp.where(sels[hd], rt, 0.0)],
                           axis=1).astype(BF16) for hd in heads]
    mb = [_bdot_nt(lhs[hd], bt) for hd in heads]
    mk = [_bdot_nt(lhs[hd], kt) for hd in heads]
    m_ak = [jnp.where(low_strict, mk[hd][:, :CHUNK], 0.0).astype(BF16) for hd in heads]
    m_rb = [jnp.where(low_incl, mb[hd][:, CHUNK:], 0.0).astype(BF16) for hd in heads]
    m_rk = [jnp.where(low_incl, mk[hd][:, CHUNK:], 0.0).astype(BF16) for hd in heads]
    p = [jnp.where(low_strict, mb[hd][:, :CHUNK], 0.0) for hd in heads]
    xs = [jnp.concatenate([at, _bdot(m_ak[hd], vv)], axis=-1) for hd in heads]
    for j in range(6):
        p16 = [p[hd].astype(BF16) for hd in heads]
        xs = [xs[hd] + _bdot(p16[hd], xs[hd].astype(BF16)) for hd in heads]
        if j < 5:
            p = [_bdot(p16[hd], p16[hd]) for hd in heads]
    ru = [_bdot(m_rb[hd], xs[hd].astype(BF16)) for hd in heads]
    r2s = [rt + ru[hd][..., :LANE] for hd in heads]
    y0s = [ru[hd][..., LANE:] + _bdot(m_rk[hd], vv) for hd in heads]

    first3w = jnp.concatenate([first3, first3], axis=-1)
    x = jnp.where(first3w, xs[0], xs[1])
    r2_ref[0, 0] = jnp.where(first3, r2s[0], r2s[1]).reshape(rows, LANE)
    y0_ref[0, 0] = jnp.where(first3, y0s[0], y0s[1]).reshape(rows, LANE)

    pg = _bdot_tn(bq, x.astype(BF16))
    ph = pg[..., LANE:] + _bdot_tn(kq, vv)
    rr = lax.broadcasted_iota(jnp.int32, (nc, LANE, LANE), 1)
    cc = lax.broadcasted_iota(jnp.int32, (nc, LANE, LANE), 2)
    same_head = (rr < HEAD_DIM) == (cc < HEAD_DIM)
    g = jnp.where(same_head, pg[..., :LANE], 0.0) + jnp.where(rr == cc, jnp.exp(cum_end), 0.0)
    g_ref[0, 0] = g.reshape(nc * LANE, LANE)
    hc_ref[0, 0] = jnp.where(same_head, ph, 0.0).reshape(nc * LANE, LANE)


def _rwkv_chunks(rkv, wdad, wup, w0, aup, a0, k_k, k_a, r_k):
    B, _, TP, _ = rkv.shape
    rows = next(r for r in (768, 512, 256) if TP % r == 0)
    nb = TP // rows
    kern = functools.partial(_rwkv_chunk_kernel, rows=rows)
    slot_spec = lambda off: pl.BlockSpec((1, 1, rows, LANE), lambda b, p, i: (b, p + off, i, 0))
    par_mat = pl.BlockSpec((1, DECAY_RANK, LANE), lambda b, p, i: (p, 0, 0))
    par_vec = pl.BlockSpec((1, 1, LANE), lambda b, p, i: (p, 0, 0))
    row_out = jax.ShapeDtypeStruct((B, PAIRS, TP, LANE), F32)
    mat_out = jax.ShapeDtypeStruct((B, PAIRS, 2 * TP, LANE), F32)
    row_spec = pl.BlockSpec((1, 1, rows, LANE), lambda b, p, i: (b, p, i, 0))
    mat_spec = pl.BlockSpec((1, 1, 2 * rows, LANE), lambda b, p, i: (b, p, i, 0))
    return pl.pallas_call(
        kern, out_shape=(mat_out, mat_out, row_out, row_out, row_out), grid=(B, PAIRS, nb),
        in_specs=[slot_spec(0), slot_spec(PAIRS), slot_spec(2 * PAIRS),
                  pl.BlockSpec((1, rows, LANE), lambda b, p, i: (b, i, 0)),
                  par_mat, par_vec, par_mat, par_vec, par_vec, par_vec, par_vec],
        out_specs=(mat_spec, mat_spec, row_spec, row_spec, row_spec),
        compiler_params=pltpu.CompilerParams(
            dimension_semantics=("arbitrary",) * 3, vmem_limit_bytes=VMEM_LIMIT),
        name="rwkv_chunks",
    )(rkv, rkv, rkv, wdad, wup, w0, aup, a0, k_k, k_a, r_k)


def _rwkv_scan_kernel(g_ref, hc_ref, r2_ref, y0_ref, bonus_ref, lng_ref, lnb_ref, y_ref,
                      state, *, rows):
    i = pl.program_id(1)

    @pl.when(i == 0)
    def _():
        state[...] = jnp.zeros_like(state)

    first = lax.broadcasted_iota(jnp.int32, (CHUNK, LANE), 1) < HEAD_DIM
    hs = [state[p] for p in range(PAIRS)]
    for c in range(rows // CHUNK):
        sl = slice(c * CHUNK, (c + 1) * CHUNK)
        sm = slice(c * LANE, (c + 1) * LANE)
        for p in range(PAIRS):
            y = _dot3(r2_ref[0, p, sl, :], hs[p]) + y0_ref[0, p, sl, :]
            hs[p] = _dot3(g_ref[0, p, sm, :], hs[p]) + hc_ref[0, p, sm, :]
            yc = y - _head_sum(y, first) * (1.0 / HEAD_DIM)
            var = _head_sum(yc * yc, first) * (1.0 / HEAD_DIM)
            out = yc * lax.rsqrt(var + RWKV_LN_EPS) * lng_ref[p] + lnb_ref[p]
            y_ref[0, p, sl, :] = (out + bonus_ref[0, p, sl, :]).astype(BF16)
    for p in range(PAIRS):
        state[p] = hs[p]


def _rwkv_scan(gm, hc, r2, y0, bonus, ln_g, ln_b, *, rows=512):
    B, P, TP, _ = r2.shape
    nb = TP // rows
    kern = functools.partial(_rwkv_scan_kernel, rows=rows)
    blk = pl.BlockSpec((1, P, rows, LANE), lambda b, i: (b, 0, i, 0))
    mat = pl.BlockSpec((1, P, 2 * rows, LANE), lambda b, i: (b, 0, i, 0))
    par = pl.BlockSpec((P, 1, LANE), lambda b, i: (0, 0, 0))
    return pl.pallas_call(
        kern, out_shape=jax.ShapeDtypeStruct((B, P, TP, LANE), BF16), grid=(B, nb),
        in_specs=[mat, mat, blk, blk, blk, par, par], out_specs=blk,
        scratch_shapes=[pltpu.VMEM((P, LANE, LANE), F32)],
        compiler_params=pltpu.CompilerParams(
            dimension_semantics=("arbitrary", "arbitrary"), vmem_limit_bytes=VMEM_LIMIT),
        name="rwkv_scan",
    )(gm, hc, r2, y0, bonus, ln_g, ln_b)


def _split_pair(qp, nh):
    lane = lax.broadcasted_iota(jnp.int32, qp.shape, 1)
    first = lane < HEAD_DIM
    zero = jnp.zeros_like(qp)
    qs = [jnp.where(first, qp, zero), jnp.where(first, zero, qp)]
    return qs[:nh], first


def _sb_kernel(q_ref, k_ref, v_ref, o_ref, *, t, nh):
    qi = pl.program_id(2)
    qs, first = _split_pair(q_ref[0, 0], nh)
    qpos = qi * t + lax.broadcasted_iota(jnp.int32, (t, 1), 0)
    kloc = lax.broadcasted_iota(jnp.int32, (1, t), 1)
    upper = (lax.broadcasted_iota(jnp.int32, (t, t), 0)
             > lax.broadcasted_iota(jnp.int32, (t, t), 1)).astype(BF16)

    def sweep(blocks, accs, cs):
        units = []
        for kb, mask in blocks:
            start = pl.multiple_of(kb * t, t)
            kblk = k_ref[0, 0, pl.ds(start, t), :]
            vblk = v_ref[0, 0, pl.ds(start, t), :]
            m = None if mask is None else mask(start + kloc)
            units += [(h, kblk, vblk, m) for h in range(nh)]
        zs = [_dot_nt(qs[h], kblk) for h, kblk, _, _ in units]
        sps = [jnp.maximum(z, 0.0) + jnp.log2(1.0 + jnp.exp2(-jnp.abs(z))) for z in zs]
        spms = [sp if u[3] is None else jnp.where(u[3], sp, 0.0) for sp, u in zip(sps, units)]
        laters = [_dot(spm.astype(BF16), upper) for spm in spms]
        accs, cs = list(accs), list(cs)
        weights = []
        for (h, _, _, m), z, sp, spm, later in zip(units, zs, sps, spms, laters):
            a = jnp.exp2(z - sp - later - cs[h])
            weights.append(a if m is None else jnp.where(m, a, 0.0))
            cs[h] = cs[h] + jnp.sum(spm, axis=-1, keepdims=True)
        for (h, _, vblk, _), a in zip(units, weights):
            accs[h] = accs[h] + _dot(a.astype(BF16), vblk)
        return accs, cs

    def block(kb, accs, cs, mask):
        return sweep([(kb, mask)], accs, cs)

    def live(cs):
        low = cs[0]
        for c in cs[1:]:
            low = jnp.minimum(low, c)
        return (jnp.min(low) < SB_CUTOFF_LOG2).astype(jnp.int32)

    zero_acc = [jnp.zeros((t, LANE), F32) for _ in range(nh)]
    zero_c = [jnp.zeros((t, 1), F32) for _ in range(nh)]
    accs, cs = sweep([(qi, lambda kpos: (kpos >= PAD) & (kpos < qpos)),
                      (jnp.maximum(qi - 1, 0), lambda kpos: (kpos >= PAD) & (qi >= 1))],
                     zero_acc, zero_c)

    def cond(carry):
        kb, alive = carry[0], carry[1]
        return (kb >= 1) & (alive > 0)

    def body(carry):
        kb = carry[0]
        accs, cs = block(kb, list(carry[2:2 + nh]), list(carry[2 + nh:]), None)
        return (kb - 1, live(cs), *accs, *cs)

    carry = lax.while_loop(cond, body, (qi - 2, live(cs), *accs, *cs))

    def front(carry):
        accs, cs = block(0, list(carry[2:2 + nh]), list(carry[2 + nh:]), lambda kpos: kpos >= PAD)
        return (carry[0], carry[1], *accs, *cs)

    carry = lax.cond((carry[0] == 0) & (carry[1] > 0), front, lambda c: c, carry)
    accs = carry[2:2 + nh]
    out = jnp.where(first, accs[0], accs[1] if nh == 2 else 0.0)
    o_ref[0, 0] = out.astype(BF16)


def _sb_attention(sb, *, slot0, nslots, nh, t=256):
    B, _, TP, _ = sb.shape
    kern = functools.partial(_sb_kernel, t=t, nh=nh)
    blk = lambda off: pl.BlockSpec((1, 1, t, LANE), lambda b, p, i: (b, slot0 + off + p, i, 0))
    full = lambda off: pl.BlockSpec((1, 1, TP, LANE), lambda b, p, i: (b, slot0 + off + p, 0, 0))
    return pl.pallas_call(
        kern, out_shape=jax.ShapeDtypeStruct((B, nslots, TP, LANE), BF16),
        grid=(B, nslots, TP // t),
        in_specs=[blk(0), full(PAIRS), full(2 * PAIRS)],
        out_specs=pl.BlockSpec((1, 1, t, LANE), lambda b, p, i: (b, p, i, 0)),
        compiler_params=pltpu.CompilerParams(
            dimension_semantics=("arbitrary",) * 3, vmem_limit_bytes=VMEM_LIMIT),
        name="sb_attention",
    )(sb, sb, sb)


def _fox_kernel(shift_ref, fe_ref, q_ref, k_ref, v_ref, o_ref, *, t):
    b = pl.program_id(0)
    h = pl.program_id(1)
    qi = pl.program_id(2)
    q = q_ref[0, 0]
    shift = shift_ref[0]
    thr = FOX_SKIP_LOG2 + 2.0 * shift
    f_q = fe_ref[b, h, jnp.maximum(qi - 1, 0)]
    lo = lax.fori_loop(
        0, qi, lambda kb, n: n + (fe_ref[b, h, kb] - f_q > thr).astype(jnp.int32), jnp.int32(0))
    qpos = qi * t + lax.broadcasted_iota(jnp.int32, (t, 1), 0)
    kloc = lax.broadcasted_iota(jnp.int32, (1, t), 1)

    def block(kb, carry, masked, online):
        acc, m = carry
        start = pl.multiple_of(kb * t, t)
        kblk = k_ref[0, 0, pl.ds(start, t), :]
        vblk = v_ref[0, 0, pl.ds(start, t), :]
        s = _dot_nt(q, kblk)
        if masked:
            kpos = start + kloc
            s = jnp.where((kpos <= qpos) & (kpos >= PAD), s, NEG_INF)
        if online:
            m_new = jnp.maximum(m, jnp.max(s, axis=-1, keepdims=True))
            acc = jnp.exp2(m - m_new) * acc
            m = m_new
        pr = jnp.exp2(s - m)
        return acc + _dot(pr.astype(BF16), vblk), m

    def sweep(online):
        def run(carry):
            carry = lax.cond(lo == 0, lambda c: block(0, c, True, online), lambda c: c, carry)
            start = jnp.maximum(lo, 1)
            n = jnp.maximum(qi - start, 0)

            def two(i, c):
                kb = start + 2 * i
                if online:
                    return block(kb + 1, block(kb, c, False, True), False, True)
                acc, m = c
                kvs = []
                for j in range(2):
                    st = pl.multiple_of((kb + j) * t, t)
                    kvs.append((k_ref[0, 0, pl.ds(st, t), :], v_ref[0, 0, pl.ds(st, t), :]))
                ss = [_dot_nt(q, kblk) for kblk, _ in kvs]
                prs = [jnp.exp2(s - m).astype(BF16) for s in ss]
                for pr, (_, vblk) in zip(prs, kvs):
                    acc = acc + _dot(pr, vblk)
                return acc, m

            carry = lax.fori_loop(0, n // 2, two, carry)
            carry = lax.cond(n % 2 == 1, lambda c: block(qi - 1, c, False, online),
                             lambda c: c, carry)
            return lax.cond(qi > 0, lambda c: block(qi, c, True, online), lambda c: c, carry)
        return run

    acc0 = jnp.zeros((t, LANE), F32)
    fixed = shift <= FOX_FIXED_SHIFT_MAX
    m0 = jnp.where(fixed, jnp.full((t, 1), shift, F32), jnp.full((t, 1), NEG_INF, F32))
    acc, _ = lax.cond(fixed, sweep(False), sweep(True), (acc0, m0))
    lane = lax.broadcasted_iota(jnp.int32, (t, LANE), 1)
    denom = jnp.sum(jnp.where(lane == HEAD_DIM, acc, 0.0), axis=-1, keepdims=True)
    o_ref[0, 0] = (acc / jnp.where(denom > 0.0, denom, 1.0)).astype(BF16)


def _fox_attention(shift, fe, fx, *, t=FOX_TILE):
    B, _, TP, _ = fx.shape
    H = FOX_HEADS
    kern = functools.partial(_fox_kernel, t=t)
    return pl.pallas_call(
        kern, out_shape=jax.ShapeDtypeStruct((B, H, TP, LANE), BF16),
        grid=(B, H, TP // t),
        in_specs=[pl.BlockSpec(memory_space=pltpu.SMEM),
                  pl.BlockSpec(memory_space=pltpu.SMEM),
                  pl.BlockSpec((1, 1, t, LANE), lambda b, h, i: (b, h, i, 0)),
                  pl.BlockSpec((1, 1, TP, LANE), lambda b, h, i: (b, h + H, 0, 0)),
                  pl.BlockSpec((1, 1, TP, LANE), lambda b, h, i: (b, h + 2 * H, 0, 0))],
        out_specs=pl.BlockSpec((1, 1, t, LANE), lambda b, h, i: (b, h, i, 0)),
        compiler_params=pltpu.CompilerParams(
            dimension_semantics=("arbitrary",) * 3, vmem_limit_bytes=VMEM_LIMIT),
        name="fox_attention",
    )(shift, fe, fx, fx, fx)


def _merge_kernel(h_ref, ya_ref, yb2_ref, yb1_ref, yc_ref, gate_ref,
                  wpa_ref, wpb_ref, wpc_ref, wout_ref, o_ref, *, tm, n_valid):
    i = pl.program_id(1)

    pa = _dot(ya_ref[0, 0], wpa_ref[0])
    for hh in range(1, PAIRS):
        pa = pa + _dot(ya_ref[0, hh], wpa_ref[hh])

    def pair_proj(y2_ref, y1_ref, w_ref):
        return (_dot(y2_ref[0, 0], w_ref[0]) + _dot(y2_ref[0, 1], w_ref[1])
                + _dot(y1_ref[0, 0], w_ref[2]))

    merged = gate_ref[0, :, 0:D_MODEL].astype(F32) * pa
    merged = merged + (gate_ref[0, :, D_MODEL:2 * D_MODEL].astype(F32)
                       * pair_proj(yb2_ref, yb1_ref, wpb_ref))
    pc = _dot(yc_ref[0, 0], wpc_ref[0])
    for hh in range(1, FOX_HEADS):
        pc = pc + _dot(yc_ref[0, hh], wpc_ref[hh])
    merged = merged + gate_ref[0, :, 2 * D_MODEL:3 * D_MODEL].astype(F32) * pc
    out = h_ref[0] + _dot(merged.astype(BF16), wout_ref[...])
    t = i * tm + lax.broadcasted_iota(jnp.int32, (tm, 1), 0)
    o_ref[0] = jnp.where((t >= PAD) & (t < PAD + n_valid), out, 0.0)


def _merge(h, ya, yb2, yb1, yc, gates, wpa, wpb, wpc, wout, *, n_valid, tm=256):
    B, TP, D = h.shape
    kern = functools.partial(_merge_kernel, tm=tm, n_valid=n_valid)
    pair_blk = lambda ns: pl.BlockSpec((1, ns, tm, LANE), lambda b, i: (b, 0, i, 0))
    return pl.pallas_call(
        kern, out_shape=jax.ShapeDtypeStruct((B, TP, D), F32), grid=(B, TP // tm),
        in_specs=[pl.BlockSpec((1, tm, D), lambda b, i: (b, i, 0)),
                  pair_blk(PAIRS),
                  pair_blk(2), pair_blk(1), pair_blk(FOX_HEADS),
                  pl.BlockSpec((1, tm, 3 * D), lambda b, i: (b, i, 0)),
                  _const_spec((PAIRS, LANE, D)),
                  _const_spec((PAIRS, LANE, D)),
                  _const_spec((FOX_HEADS, LANE, D)),
                  _const_spec((D, D))],
        out_specs=pl.BlockSpec((1, tm, D), lambda b, i: (b, i, 0)),
        compiler_params=pltpu.CompilerParams(
            dimension_semantics=("arbitrary", "arbitrary"), vmem_limit_bytes=VMEM_LIMIT),
        name="merge",
    )(h, ya, yb2, yb1, yc, gates, wpa, wpb, wpc, wout)


def _moe_kernel(h_ref, g_ref, wr_ref, br_ref, wg_ref, wu_ref, wd_ref, o_ref, he_ref,
                *, tm, n_valid):
    i = pl.program_id(1)
    x = h_ref[0]
    ms = jnp.mean(x * x, axis=-1, keepdims=True)
    n = x * lax.rsqrt(ms + NORM_EPS) * g_ref[...]

    lg = _dot3(n, wr_ref[...]) + br_ref[...]
    lane = lax.broadcasted_iota(jnp.int32, (tm, LANE), 1)
    big = jnp.int32(LANE)
    is_group = lane < N_GROUPS
    gl = jnp.where(is_group, lg, -jnp.inf)
    gmax = jnp.max(gl, axis=-1, keepdims=True)
    g_idx = jnp.min(jnp.where(is_group & (gl == gmax), lane, big), axis=-1, keepdims=True)
    p_group = 1.0 / jnp.sum(jnp.where(is_group, jnp.exp(gl - gmax), 0.0), axis=-1, keepdims=True)
    lo = N_GROUPS + g_idx * EXPERTS_PER_GROUP
    in_grp = (lane >= lo) & (lane < lo + EXPERTS_PER_GROUP)
    el = jnp.where(in_grp, lg, -jnp.inf)
    top1 = jnp.max(el, axis=-1, keepdims=True)
    i1 = jnp.min(jnp.where(in_grp & (el == top1), lane, big), axis=-1, keepdims=True)
    rest = in_grp & (lane != i1)
    el2 = jnp.where(rest, lg, -jnp.inf)
    top2 = jnp.max(el2, axis=-1, keepdims=True)
    i2 = jnp.min(jnp.where(rest & (el2 == top2), lane, big), axis=-1, keepdims=True)
    e2 = jnp.exp(top2 - top1)
    p1 = 1.0 / (1.0 + e2)
    p2 = e2 / (1.0 + e2)
    comb = p_group * (jnp.where(lane == i1, p1, 0.0) + jnp.where(lane == i2, p2, 0.0))

    n16 = n.astype(BF16)
    for e in range(N_EXPERTS):
        gate = _dot(n16, wg_ref[e])
        up = _dot(n16, wu_ref[e])
        c_e = comb[:, N_GROUPS + e:N_GROUPS + e + 1]
        he_ref[:, e * D_EXPERT:(e + 1) * D_EXPERT] = (
            gate * _sigmoid(gate) * up * c_e).astype(BF16)
    out = x + _dot(he_ref[...], wd_ref[...])
    t = i * tm + lax.broadcasted_iota(jnp.int32, (tm, 1), 0)
    o_ref[0] = jnp.where((t >= PAD) & (t < PAD + n_valid), out, 0.0)


def _moe(h, g2, wr, br, wg, wu, wd, *, n_valid, tm=256):
    B, TP, D = h.shape
    kern = functools.partial(_moe_kernel, tm=tm, n_valid=n_valid)
    EW = N_EXPERTS * D_EXPERT
    return pl.pallas_call(
        kern, out_shape=jax.ShapeDtypeStruct((B, TP, D), F32), grid=(B, TP // tm),
        in_specs=[pl.BlockSpec((1, tm, D), lambda b, i: (b, i, 0)),
                  _const_spec((1, D)), _const_spec((D, LANE)), _const_spec((1, LANE)),
                  _const_spec((N_EXPERTS, D, D_EXPERT)), _const_spec((N_EXPERTS, D, D_EXPERT)),
                  _const_spec((EW, D))],
        out_specs=pl.BlockSpec((1, tm, D), lambda b, i: (b, i, 0)),
        scratch_shapes=[pltpu.VMEM((tm, EW), BF16)],
        compiler_params=pltpu.CompilerParams(
            dimension_semantics=("arbitrary", "arbitrary"), vmem_limit_bytes=VMEM_LIMIT),
        name="moe",
    )(h, g2, wr, br, wg, wu, wd)


def _transpose_kernel(x_ref, o_ref):
    o_ref[...] = x_ref[...].T.astype(BF16)


def _transpose_to_bf16(a, *, tn=LANE):
    N, K = a.shape
    assert N % tn == 0
    return pl.pallas_call(
        _transpose_kernel, out_shape=jax.ShapeDtypeStruct((K, N), BF16), grid=(N // tn,),
        in_specs=[pl.BlockSpec((tn, K), lambda i: (i, 0))],
        out_specs=pl.BlockSpec((K, tn), lambda i: (0, i)),
        compiler_params=pltpu.CompilerParams(dimension_semantics=("arbitrary",)),
        name="transpose_cast",
    )(a)


def _pack_w_in(w_in):
    wt = jnp.swapaxes(w_in, 0, 1)
    rest = wt[SHIFT_W:]
    rows = [wt[:SHIFT_W]]
    off = 0
    for heads in (SB_HEADS,) * 3 + (FOX_HEADS,) * 3:
        rows.append(jnp.pad(rest[off:off + heads * HEAD_DIM],
                            ((0, PAIRS * LANE - heads * HEAD_DIM), (0, 0))))
        off += heads * HEAD_DIM
    rows.append(jnp.pad(rest[off:off + FOX_HEADS], ((0, LANE - FOX_HEADS), (0, 0))))
    rows.append(rest[off + FOX_HEADS:])
    return _transpose_to_bf16(jnp.concatenate(rows, axis=0))


def _per_pair_cols(w):
    return w.reshape(w.shape[0], PAIRS, LANE).transpose(1, 0, 2)


def kernel(x, meta_tokens, norm1_g, w_in, rwkv_mu, rwkv_w_up, rwkv_w0, rwkv_a_up, rwkv_a0,
           rwkv_k_k, rwkv_k_a, rwkv_r_k, rwkv_ln_g, rwkv_ln_b, fox_f_b, fox_q_g, fox_k_g,
           w_p_rwkv, w_p_sb, w_p_fox, w_out, norm2_g, moe_wg, moe_bg, moe_we, moe_be,
           moe_w_gate, moe_w_up, moe_w_down):
    B, S, D = x.shape
    depth = w_in.shape[0]
    L = N_META + S
    TP = -(-(PAD + L) // ROW_TILE) * ROW_TILE
    meta = jnp.broadcast_to(meta_tokens[None].astype(x.dtype), (B, N_META, D))
    h = jnp.concatenate([jnp.zeros((B, PAD, D), x.dtype), meta, x,
                         jnp.zeros((B, TP - PAD - L, D), x.dtype)], axis=1)
    H = RWKV_HEADS
    EW = N_EXPERTS * D_EXPERT
    for l in range(depth):
        fb = jnp.zeros((1, LANE), F32).at[0, :FOX_HEADS].set(fox_f_b[l])
        rkv, wdad, sb, fx, fend, gates = _inproj(
            h, norm1_g[l][None], _pack_w_in(w_in[l]), rwkv_mu[l][None], fb,
            jnp.tile(fox_q_g[l], 2)[None], jnp.tile(fox_k_g[l], 2)[None])
        hv = lambda p: p.reshape(PAIRS, 1, LANE)
        gm, hc, r2, y0, bonus = _rwkv_chunks(
            rkv, wdad, _per_pair_cols(rwkv_w_up[l]), hv(rwkv_w0[l]),
            _per_pair_cols(rwkv_a_up[l]), hv(rwkv_a0[l]),
            hv(rwkv_k_k[l]), hv(rwkv_k_a[l]), hv(rwkv_r_k[l]))
        ya = _rwkv_scan(gm, hc, r2, y0, bonus, hv(rwkv_ln_g[l]), hv(rwkv_ln_b[l]))
        yb2 = _sb_attention(sb, slot0=0, nslots=2, nh=2)
        yb1 = _sb_attention(sb, slot0=2, nslots=1, nh=1)
        shift = (8.0 * LOG2E * jnp.max(jnp.abs(fox_q_g[l])) * jnp.max(jnp.abs(fox_k_g[l]))).reshape(1)
        step = FOX_TILE // INPROJ_TILE
        fe = jnp.transpose(fend[:, step - 1::step, 0, :FOX_HEADS], (0, 2, 1))
        yc = _fox_attention(shift, fe, fx)
        pc = jnp.pad(w_p_fox[l].reshape(FOX_HEADS, HEAD_DIM, D),
                     ((0, 0), (0, LANE - HEAD_DIM), (0, 0))).astype(BF16)
        pp = lambda w, nh: jnp.pad(w, ((0, PAIRS * LANE - nh * HEAD_DIM), (0, 0))).reshape(
            PAIRS, LANE, D).astype(BF16)
        h = _merge(h, ya, yb2, yb1, yc, gates, pp(w_p_rwkv[l], H), pp(w_p_sb[l], SB_HEADS),
                   pc, w_out[l].astype(BF16), n_valid=L)
        wr = jnp.zeros((D, LANE), F32).at[:, :N_GROUPS].set(moe_wg[l])
        wr = wr.at[:, N_GROUPS:N_GROUPS + N_EXPERTS].set(moe_we[l])
        br = jnp.zeros((1, LANE), F32).at[0, :N_GROUPS].set(moe_bg[l])
        br = br.at[0, N_GROUPS:N_GROUPS + N_EXPERTS].set(moe_be[l])
        wg = moe_w_gate[l].astype(BF16)
        wu = moe_w_up[l].astype(BF16)
        wd = moe_w_down[l].reshape(EW, D).astype(BF16)
        h = _moe(h, norm2_g[l][None], wr, br, wg, wu, wd, n_valid=L)
    return h[:, PAD + N_META:PAD + L]
```

```python
import functools
import math

import jax
import jax.numpy as jnp
from jax import lax
from jax.experimental import pallas as pl
from jax.experimental.pallas import tpu as pltpu

D_MODEL = 1024
HEAD_DIM = 64
N_META = 16
FRAME_TILE = 256
PAD = FRAME_TILE - N_META
RWKV_HEADS = 6
SB_HEADS = 5
FOX_HEADS = 5
RWKV_W = RWKV_HEADS * HEAD_DIM
SB_W = SB_HEADS * HEAD_DIM
FOX_W = FOX_HEADS * HEAD_DIM
DECAY_RANK = 64
ICLR_RANK = 64
SHIFT_W = 3 * RWKV_W + DECAY_RANK + ICLR_RANK
N_GROUPS = 4
EXPERTS_PER_GROUP = 4
N_EXPERTS = 16
D_EXPERT = 256
NORM_EPS = 1e-6
RWKV_LN_EPS = 64e-5
NEG_INF = -1e30
ATT_SCALE = 1.0 / math.sqrt(HEAD_DIM)

LANE = 128
ROW_TILE = 512
PAIRS = 3
QKV_W = 3 * PAIRS * LANE
SEG_SB = SHIFT_W
SEG_FOX = SEG_SB + QKV_W
SEG_GATE = SEG_FOX + QKV_W + LANE
D_IN_PAD = SEG_GATE + 3 * D_MODEL
SB_CUTOFF_LOG2 = 160.0
LOG2E = 1.4426950408889634
FOX_FIXED_SHIFT_MAX = 50.0
FOX_SKIP_LOG2 = 150.0
FOX_TILE = 512
FOX_GROUP = 4
INPROJ_TILE = FRAME_TILE

CHUNK = 64
VMEM_LIMIT = 56 * 1024 * 1024

F32 = jnp.float32
BF16 = jnp.bfloat16


def _log_sigmoid(x):
    return jnp.minimum(x, 0.0) - jnp.log1p(jnp.exp(-jnp.abs(x)))


def _sigmoid(x):
    return 1.0 / (1.0 + jnp.exp(-x))


def _dot(a, b, **kw):
    return jnp.dot(a, b, preferred_element_type=F32, **kw)


def _dot_nt(a, b):
    return lax.dot_general(a, b, (((1,), (1,)), ((), ())), preferred_element_type=F32)


def _dot_tn(a, b):
    return lax.dot_general(a, b, (((0,), (0,)), ((), ())), preferred_element_type=F32)


def _dot3(a, b):
    ah = a.astype(BF16)
    al = (a - ah.astype(F32)).astype(BF16)
    bh = b.astype(BF16)
    bl = (b - bh.astype(F32)).astype(BF16)
    return _dot(ah, bh) + (_dot(ah, bl) + _dot(al, bh))


def _stream_specs(first, tm, n_seq):
    if not first:
        return [pl.BlockSpec((1, tm, D_MODEL), lambda b, i: (b, i, 0))]
    return [pl.BlockSpec((1, tm, D_MODEL), lambda b, i: (b, jnp.clip(i - 1, 0, n_seq - 1), 0)),
            pl.BlockSpec((1, tm, D_MODEL), lambda b, i: (b, 0, 0))]


def _stream_tile(i, refs, n_seq):
    if len(refs) == 1:
        return refs[0][0]
    x_ref, head_ref = refs
    return jnp.where(i == 0, head_ref[0], jnp.where(i <= n_seq, x_ref[0], 0.0))


def _const_spec(shape):
    n = len(shape)
    return pl.BlockSpec(shape, lambda *_: (0,) * n, pipeline_mode=pl.Buffered(1))


def _inproj_kernel(*refs, tm, first, n_seq):
    n_stream = 2 if first else 1
    (g_ref, w_ref, mu_ref, fb_ref, fqg_ref, fkg_ref,
     rkv_ref, wdad_ref, sb_ref, fx_ref, fend_ref, gate_ref, carry_u, carry_f) = refs[n_stream:]
    i = pl.program_id(1)

    @pl.when(i == 0)
    def _():
        carry_u[...] = jnp.zeros_like(carry_u)
        carry_f[...] = jnp.zeros_like(carry_f)

    x = _stream_tile(i, refs[:n_stream], n_seq)
    ms = jnp.mean(x * x, axis=-1, keepdims=True)
    n = (x * lax.rsqrt(ms + NORM_EPS) * g_ref[...]).astype(BF16)
    row = lax.broadcasted_iota(jnp.int32, (tm, 1), 0)

    us = _dot(n, w_ref[:, 0:SHIFT_W])
    prev = pltpu.roll(us, 1, axis=0)
    prev = jnp.where(row == 0, carry_u[...], prev)
    carry_u[...] = us[tm - 1:tm, :]
    ush = us + (prev - us) * mu_ref[...]
    for j in range(3 * PAIRS):
        rkv_ref[0, j] = ush[:, j * LANE:(j + 1) * LANE]
    wdad_ref[0] = ush[:, 3 * RWKV_W:SHIFT_W]

    usb = _dot(n, w_ref[:, SEG_SB:SEG_SB + QKV_W])
    for j in range(3 * PAIRS):
        piece = usb[:, j * LANE:(j + 1) * LANE]
        if j < PAIRS:
            piece = piece * (ATT_SCALE * LOG2E)
        sb_ref[0, j] = piece.astype(BF16)

    uf = _dot(n, w_ref[:, SEG_FOX:SEG_FOX + QKV_W + LANE])
    lane = lax.broadcasted_iota(jnp.int32, (tm, LANE), 1)
    first = lane < HEAD_DIM
    t_glob = i * tm + row
    logf = _log_sigmoid(uf[:, QKV_W:QKV_W + LANE] + fb_ref[...])
    logf = jnp.where((lane < FOX_HEADS) & (t_glob >= PAD), logf, 0.0)
    tri = (lax.broadcasted_iota(jnp.int32, (tm, tm), 0)
           >= lax.broadcasted_iota(jnp.int32, (tm, tm), 1)).astype(BF16)
    f_hi = logf.astype(BF16)
    f_mid = (logf - f_hi.astype(F32)).astype(BF16)
    f_lo = (logf - f_hi.astype(F32) - f_mid.astype(F32)).astype(BF16)
    cum = _dot(tri, f_hi) + _dot(tri, f_mid) + _dot(tri, f_lo) + carry_f[...]
    carry_f[...] = cum[tm - 1:tm, :]
    fend_ref[0, 0] = cum[tm - 1:tm, :] * LOG2E

    def split3(hd):
        f2 = jnp.sum(jnp.where(lane == hd, cum, 0.0), axis=-1, keepdims=True) * LOG2E
        hi = f2.astype(BF16).astype(F32)
        mid = (f2 - hi).astype(BF16).astype(F32)
        return hi, mid, f2 - hi - mid

    def tail_cols(vals):
        out = jnp.zeros((tm, LANE), F32)
        for o, val in enumerate(vals):
            out = jnp.where(lane == HEAD_DIM + o, val, out)
        return out

    splits = [split3(hd) for hd in range(FOX_HEADS)]
    for j in range(3 * PAIRS):
        piece = uf[:, j * LANE:(j + 1) * LANE]
        kind = j // PAIRS
        if kind < 2:
            gain = fqg_ref[...] if kind == 0 else fkg_ref[...]
            sq = piece * piece
            ms0 = jnp.sum(jnp.where(first, sq, 0.0), axis=-1, keepdims=True) * (1.0 / HEAD_DIM)
            ms1 = jnp.sum(jnp.where(first, 0.0, sq), axis=-1, keepdims=True) * (1.0 / HEAD_DIM)
            inv = jnp.where(first, lax.rsqrt(ms0 + NORM_EPS), lax.rsqrt(ms1 + NORM_EPS))
            piece = piece * inv * gain
            if kind == 0:
                piece = piece * (ATT_SCALE * LOG2E)
        swapped = pltpu.roll(piece, HEAD_DIM, axis=1)
        for half in range(2):
            hd = 2 * (j % PAIRS) + half
            if hd >= FOX_HEADS:
                continue
            if kind == 2:
                extra = tail_cols([1.0])
            else:
                hi, mid, lo = splits[hd]
                extra = tail_cols([hi, mid, lo, 1.0, 1.0, 1.0] if kind == 0
                                  else [1.0, 1.0, 1.0, -hi, -mid, -lo])
            body = piece if half == 0 else swapped
            fx_ref[0, kind * FOX_HEADS + hd] = jnp.where(first, body, extra).astype(BF16)

    ug = _dot(n, w_ref[:, SEG_GATE:D_IN_PAD])
    gate_ref[0] = _sigmoid(ug).astype(BF16)


def _inproj(stream, g1, w_in_p, mu, fb, fqg, fkg, *, TP, n_seq, tm=INPROJ_TILE):
    first = len(stream) == 2
    B, D = stream[0].shape[0], D_MODEL
    nb = TP // tm
    kern = functools.partial(_inproj_kernel, tm=tm, first=first, n_seq=n_seq)
    out_shape = (
        jax.ShapeDtypeStruct((B, 3 * PAIRS, TP, LANE), F32),
        jax.ShapeDtypeStruct((B, TP, LANE), F32),
        jax.ShapeDtypeStruct((B, 3 * PAIRS, TP, LANE), BF16),
        jax.ShapeDtypeStruct((B, 3 * FOX_HEADS, TP, LANE), BF16),
        jax.ShapeDtypeStruct((B, nb, 1, LANE), F32),
        jax.ShapeDtypeStruct((B, TP, 3 * D_MODEL), BF16),
    )
    in_specs = _stream_specs(first, tm, n_seq) + [
        _const_spec((1, D)),
        _const_spec((D, D_IN_PAD)),
        _const_spec((1, SHIFT_W)),
        _const_spec((1, LANE)),
        _const_spec((1, LANE)),
        _const_spec((1, LANE)),
    ]
    out_specs = (
        pl.BlockSpec((1, 3 * PAIRS, tm, LANE), lambda b, i: (b, 0, i, 0)),
        pl.BlockSpec((1, tm, LANE), lambda b, i: (b, i, 0)),
        pl.BlockSpec((1, 3 * PAIRS, tm, LANE), lambda b, i: (b, 0, i, 0)),
        pl.BlockSpec((1, 3 * FOX_HEADS, tm, LANE), lambda b, i: (b, 0, i, 0)),
        pl.BlockSpec((1, 1, 1, LANE), lambda b, i: (b, i, 0, 0)),
        pl.BlockSpec((1, tm, 3 * D_MODEL), lambda b, i: (b, i, 0)),
    )
    return pl.pallas_call(
        kern, out_shape=out_shape, grid=(B, nb), in_specs=in_specs, out_specs=out_specs,
        scratch_shapes=[pltpu.VMEM((1, SHIFT_W), F32), pltpu.VMEM((1, LANE), F32)],
        compiler_params=pltpu.CompilerParams(
            dimension_semantics=("arbitrary", "arbitrary"), vmem_limit_bytes=VMEM_LIMIT),
        name="inproj",
    )(*stream, g1, w_in_p, mu, fb, fqg, fkg)


def _bdot(a, b):
    return lax.dot_general(a, b, (((2,), (1,)), ((0,), (0,))), preferred_element_type=F32)


def _bdot_nt(a, b):
    return lax.dot_general(a, b, (((2,), (2,)), ((0,), (0,))), preferred_element_type=F32)


def _bdot_tn(a, b):
    return lax.dot_general(a, b, (((1,), (1,)), ((0,), (0,))), preferred_element_type=F32)


def _head_sum(x, first):
    s0 = jnp.sum(jnp.where(first, x, 0.0), axis=-1, keepdims=True)
    s1 = jnp.sum(jnp.where(first, 0.0, x), axis=-1, keepdims=True)
    return jnp.where(first, s0, s1)


def _rwkv_chunk_kernel(r_ref, k_ref, v_ref, wdad_ref, wup_ref, w0_ref, aup_ref, a0_ref,
                       kk_ref, ka_ref, rk_ref,
                       g_ref, hc_ref, r2_ref, y0_ref, bonus_ref, *, rows):
    nc = rows // CHUNK
    r = r_ref[0, 0]
    k = k_ref[0, 0]
    v = v_ref[0, 0]
    wd = wdad_ref[0][:, 0:DECAY_RANK]
    ad = wdad_ref[0][:, DECAY_RANK:DECAY_RANK + ICLR_RANK]
    first = lax.broadcasted_iota(jnp.int32, (rows, LANE), 1) < HEAD_DIM

    pre = w0_ref[0] + _dot3(jnp.tanh(wd), wup_ref[0])
    lw = -jnp.exp(_log_sigmoid(pre) - 0.5)
    iclr = _sigmoid(a0_ref[0] + _dot3(ad, aup_ref[0]))
    kk = k * kk_ref[0]
    kk = kk / jnp.maximum(jnp.sqrt(_head_sum(kk * kk, first)), 1e-12)
    k2 = k * (1.0 + (iclr - 1.0) * ka_ref[0])
    b = kk * iclr
    bonus_ref[0, 0] = _head_sum(r * k2 * rk_ref[0], first) * v

    to3 = lambda x: x.reshape(nc, CHUNK, LANE)
    ri = lax.broadcasted_iota(jnp.int32, (nc, CHUNK, CHUNK), 1)
    ci = lax.broadcasted_iota(jnp.int32, (nc, CHUNK, CHUNK), 2)
    low_incl = ri >= ci
    low_strict = ri > ci
    first3 = lax.broadcasted_iota(jnp.int32, (nc, CHUNK, LANE), 2) < HEAD_DIM

    lw3 = to3(lw)
    tri = low_incl.astype(BF16)
    lw_hi = lw3.astype(BF16)
    cum = _bdot(tri, lw_hi) + _bdot(tri, (lw3 - lw_hi.astype(F32)).astype(BF16))
    cum_end = cum[:, CHUNK - 1:CHUNK, :]
    e_neg = jnp.exp(-cum)
    at = to3(-kk) * jnp.exp(cum - lw3)
    rt = to3(r) * jnp.exp(cum)
    bt = (to3(b) * e_neg).astype(BF16)
    kt = (to3(k2) * e_neg).astype(BF16)
    e_rem = jnp.exp(cum_end - cum)
    bq = (to3(b) * e_rem).astype(BF16)
    kq = (to3(k2) * e_rem).astype(BF16)
    vv = to3(v).astype(BF16)

    heads = range(2)
    sels = [first3, jnp.logical_not(first3)]
    lhs = [jnp.concatenate([jnp.where(sels[hd], at, 0.0), jnp.where(sels[hd], rt, 0.0)],
                           axis=1).astype(BF16) for hd in heads]
    mb = [_bdot_nt(lhs[hd], bt) for hd in heads]
    mk = [_bdot_nt(lhs[hd], kt) for hd in heads]
    m_ak = [jnp.where(low_strict, mk[hd][:, :CHUNK], 0.0).astype(BF16) for hd in heads]
    m_rb = [jnp.where(low_incl, mb[hd][:, CHUNK:], 0.0).astype(BF16) for hd in heads]
    m_rk = [jnp.where(low_incl, mk[hd][:, CHUNK:], 0.0).astype(BF16) for hd in heads]
    p = [jnp.where(low_strict, mb[hd][:, :CHUNK], 0.0) for hd in heads]
    xs = [jnp.concatenate([at, _bdot(m_ak[hd], vv)], axis=-1) for hd in heads]
    for j in range(6):
        p16 = [p[hd].astype(BF16) for hd in heads]
        xs = [xs[hd] + _bdot(p16[hd], xs[hd].astype(BF16)) for hd in heads]
        if j < 5:
            p = [_bdot(p16[hd], p16[hd]) for hd in heads]
    ru = [_bdot(m_rb[hd], xs[hd].astype(BF16)) for hd in heads]
    r2s = [rt + ru[hd][..., :LANE] for hd in heads]
    y0s = [ru[hd][..., LANE:] + _bdot(m_rk[hd], vv) for hd in heads]

    first3w = jnp.concatenate([first3, first3], axis=-1)
    x = jnp.where(first3w, xs[0], xs[1])
    r2_ref[0, 0] = jnp.where(first3, r2s[0], r2s[1]).reshape(rows, LANE)
    y0_ref[0, 0] = jnp.where(first3, y0s[0], y0s[1]).reshape(rows, LANE)

    pg = _bdot_tn(bq, x.astype(BF16))
    ph = pg[..., LANE:] + _bdot_tn(kq, vv)
    rr = lax.broadcasted_iota(jnp.int32, (nc, LANE, LANE), 1)
    cc = lax.broadcasted_iota(jnp.int32, (nc, LANE, LANE), 2)
    same_head = (rr < HEAD_DIM) == (cc < HEAD_DIM)
    g = jnp.where(same_head, pg[..., :LANE], 0.0) + jnp.where(rr == cc, jnp.exp(cum_end), 0.0)
    g_ref[0, 0] = g.reshape(nc * LANE, LANE)
    hc_ref[0, 0] = jnp.where(same_head, ph, 0.0).reshape(nc * LANE, LANE)


def _rwkv_chunks(rkv, wdad, wup, w0, aup, a0, k_k, k_a, r_k):
    B, _, TP, _ = rkv.shape
    rows = next(r for r in (768, 512, 256) if TP % r == 0)
    nb = TP // rows
    kern = functools.partial(_rwkv_chunk_kernel, rows=rows)
    slot_spec = lambda off: pl.BlockSpec((1, 1, rows, LANE), lambda b, p, i: (b, p + off, i, 0))
    par_mat = pl.BlockSpec((1, DECAY_RANK, LANE), lambda b, p, i: (p, 0, 0))
    par_vec = pl.BlockSpec((1, 1, LANE), lambda b, p, i: (p, 0, 0))
    row_out = jax.ShapeDtypeStruct((B, PAIRS, TP, LANE), F32)
    mat_out = jax.ShapeDtypeStruct((B, PAIRS, 2 * TP, LANE), F32)
    row_spec = pl.BlockSpec((1, 1, rows, LANE), lambda b, p, i: (b, p, i, 0))
    mat_spec = pl.BlockSpec((1, 1, 2 * rows, LANE), lambda b, p, i: (b, p, i, 0))
    return pl.pallas_call(
        kern, out_shape=(mat_out, mat_out, row_out, row_out, row_out), grid=(B, PAIRS, nb),
        in_specs=[slot_spec(0), slot_spec(PAIRS), slot_spec(2 * PAIRS),
                  pl.BlockSpec((1, rows, LANE), lambda b, p, i: (b, i, 0)),
                  par_mat, par_vec, par_mat, par_vec, par_vec, par_vec, par_vec],
        out_specs=(mat_spec, mat_spec, row_spec, row_spec, row_spec),
        compiler_params=pltpu.CompilerParams(
            dimension_semantics=("arbitrary",) * 3, vmem_limit_bytes=VMEM_LIMIT),
        name="rwkv_chunks",
    )(rkv, rkv, rkv, wdad, wup, w0, aup, a0, k_k, k_a, r_k)


def _rwkv_scan_kernel(g_ref, hc_ref, r2_ref, y0_ref, bonus_ref, lng_ref, lnb_ref, y_ref,
                      state, *, rows):
    i = pl.program_id(1)

    @pl.when(i == 0)
    def _():
        state[...] = jnp.zeros_like(state)

    first = lax.broadcasted_iota(jnp.int32, (CHUNK, LANE), 1) < HEAD_DIM
    hs = [state[p] for p in range(PAIRS)]
    for c in range(rows // CHUNK):
        sl = slice(c * CHUNK, (c + 1) * CHUNK)
        sm = slice(c * LANE, (c + 1) * LANE)
        for p in range(PAIRS):
            y = _dot3(r2_ref[0, p, sl, :], hs[p]) + y0_ref[0, p, sl, :]
            hs[p] = _dot3(g_ref[0, p, sm, :], hs[p]) + hc_ref[0, p, sm, :]
            yc = y - _head_sum(y, first) * (1.0 / HEAD_DIM)
            var = _head_sum(yc * yc, first) * (1.0 / HEAD_DIM)
            out = yc * lax.rsqrt(var + RWKV_LN_EPS) * lng_ref[p] + lnb_ref[p]
            y_ref[0, p, sl, :] = (out + bonus_ref[0, p, sl, :]).astype(BF16)
    for p in range(PAIRS):
        state[p] = hs[p]


def _rwkv_scan(gm, hc, r2, y0, bonus, ln_g, ln_b, *, rows=512):
    B, P, TP, _ = r2.shape
    nb = TP // rows
    kern = functools.partial(_rwkv_scan_kernel, rows=rows)
    blk = pl.BlockSpec((1, P, rows, LANE), lambda b, i: (b, 0, i, 0))
    mat = pl.BlockSpec((1, P, 2 * rows, LANE), lambda b, i: (b, 0, i, 0))
    par = pl.BlockSpec((P, 1, LANE), lambda b, i: (0, 0, 0))
    return pl.pallas_call(
        kern, out_shape=jax.ShapeDtypeStruct((B, P, TP, LANE), BF16), grid=(B, nb),
        in_specs=[mat, mat, blk, blk, blk, par, par], out_specs=blk,
        scratch_shapes=[pltpu.VMEM((P, LANE, LANE), F32)],
        compiler_params=pltpu.CompilerParams(
            dimension_semantics=("arbitrary", "arbitrary"), vmem_limit_bytes=VMEM_LIMIT),
        name="rwkv_scan",
    )(gm, hc, r2, y0, bonus, ln_g, ln_b)


def _split_pair(qp, nh):
    lane = lax.broadcasted_iota(jnp.int32, qp.shape, 1)
    first = lane < HEAD_DIM
    zero = jnp.zeros_like(qp)
    qs = [jnp.where(first, qp, zero), jnp.where(first, zero, qp)]
    return qs[:nh], first


def _sb_kernel(q_ref, k_ref, v_ref, o_ref, *, t, nh):
    qi = pl.program_id(2)
    qs, first = _split_pair(q_ref[0, 0], nh)
    qpos = qi * t + lax.broadcasted_iota(jnp.int32, (t, 1), 0)
    kloc = lax.broadcasted_iota(jnp.int32, (1, t), 1)
    upper = (lax.broadcasted_iota(jnp.int32, (t, t), 0)
             > lax.broadcasted_iota(jnp.int32, (t, t), 1)).astype(BF16)

    def sweep(blocks, accs, cs):
        units = []
        for kb, mask in blocks:
            start = pl.multiple_of(kb * t, t)
            kblk = k_ref[0, 0, pl.ds(start, t), :]
            vblk = v_ref[0, 0, pl.ds(start, t), :]
            m = None if mask is None else mask(start + kloc)
            units += [(h, kblk, vblk, m) for h in range(nh)]
        zs = [_dot_nt(qs[h], kblk) for h, kblk, _, _ in units]
        sps = [jnp.maximum(z, 0.0) + jnp.log2(1.0 + jnp.exp2(-jnp.abs(z))) for z in zs]
        spms = [sp if u[3] is None else jnp.where(u[3], sp, 0.0) for sp, u in zip(sps, units)]
        laters = [_dot(spm.astype(BF16), upper) for spm in spms]
        accs, cs = list(accs), list(cs)
        weights = []
        for (h, _, _, m), z, sp, spm, later in zip(units, zs, sps, spms, laters):
            a = jnp.exp2(z - sp - later - cs[h])
            weights.append(a if m is None else jnp.where(m, a, 0.0))
            cs[h] = cs[h] + jnp.sum(spm, axis=-1, keepdims=True)
        for (h, _, vblk, _), a in zip(units, weights):
            accs[h] = accs[h] + _dot(a.astype(BF16), vblk)
        return accs, cs

    def block(kb, accs, cs, mask):
        return sweep([(kb, mask)], accs, cs)

    def live(cs):
        low = cs[0]
        for c in cs[1:]:
            low = jnp.minimum(low, c)
        return (jnp.min(low) < SB_CUTOFF_LOG2).astype(jnp.int32)

    zero_acc = [jnp.zeros((t, LANE), F32) for _ in range(nh)]
    zero_c = [jnp.zeros((t, 1), F32) for _ in range(nh)]
    accs, cs = sweep([(qi, lambda kpos: (kpos >= PAD) & (kpos < qpos)),
                      (jnp.maximum(qi - 1, 0), lambda kpos: (kpos >= PAD) & (qi >= 1))],
                     zero_acc, zero_c)

    def cond(carry):
        kb, alive = carry[0], carry[1]
        return (kb >= 1) & (alive > 0)

    def body(carry):
        kb = carry[0]
        accs, cs = block(kb, list(carry[2:2 + nh]), list(carry[2 + nh:]), None)
        return (kb - 1, live(cs), *accs, *cs)

    carry = lax.while_loop(cond, body, (qi - 2, live(cs), *accs, *cs))

    def front(carry):
        accs, cs = block(0, list(carry[2:2 + nh]), list(carry[2 + nh:]), lambda kpos: kpos >= PAD)
        return (carry[0], carry[1], *accs, *cs)

    carry = lax.cond((carry[0] == 0) & (carry[1] > 0), front, lambda c: c, carry)
    accs = carry[2:2 + nh]
    out = jnp.where(first, accs[0], accs[1] if nh == 2 else 0.0)
    o_ref[0, 0] = out.astype(BF16)


def _sb_attention(sb, *, slot0, nslots, nh, t=256):
    B, _, TP, _ = sb.shape
    kern = functools.partial(_sb_kernel, t=t, nh=nh)
    blk = lambda off: pl.BlockSpec((1, 1, t, LANE), lambda b, p, i: (b, slot0 + off + p, i, 0))
    full = lambda off: pl.BlockSpec((1, 1, TP, LANE), lambda b, p, i: (b, slot0 + off + p, 0, 0))
    return pl.pallas_call(
        kern, out_shape=jax.ShapeDtypeStruct((B, nslots, TP, LANE), BF16),
        grid=(B, nslots, TP // t),
        in_specs=[blk(0), full(PAIRS), full(2 * PAIRS)],
        out_specs=pl.BlockSpec((1, 1, t, LANE), lambda b, p, i: (b, p, i, 0)),
        compiler_params=pltpu.CompilerParams(
            dimension_semantics=("arbitrary",) * 3, vmem_limit_bytes=VMEM_LIMIT),
        name="sb_attention",
    )(sb, sb, sb)


def _fox_kernel(shift_ref, fe_ref, q_ref, k_ref, v_ref, o_ref, *, t):
    b = pl.program_id(0)
    h = pl.program_id(1)
    qi = pl.program_id(2)
    q = q_ref[0, 0]
    shift = shift_ref[0]
    thr = FOX_SKIP_LOG2 + 2.0 * shift
    f_q = fe_ref[b, h, jnp.maximum(qi - 1, 0)]
    lo = lax.fori_loop(
        0, qi, lambda kb, n: n + (fe_ref[b, h, kb] - f_q > thr).astype(jnp.int32), jnp.int32(0))
    qpos = qi * t + lax.broadcasted_iota(jnp.int32, (t, 1), 0)
    kloc = lax.broadcasted_iota(jnp.int32, (1, t), 1)

    def block(kb, carry, masked, online):
        acc, m = carry
        start = pl.multiple_of(kb * t, t)
        kblk = k_ref[0, 0, pl.ds(start, t), :]
        vblk = v_ref[0, 0, pl.ds(start, t), :]
        s = _dot_nt(q, kblk)
        if masked:
            kpos = start + kloc
            s = jnp.where((kpos <= qpos) & (kpos >= PAD), s, NEG_INF)
        if online:
            m_new = jnp.maximum(m, jnp.max(s, axis=-1, keepdims=True))
            acc = jnp.exp2(m - m_new) * acc
            m = m_new
        pr = jnp.exp2(s - m)
        return acc + _dot(pr.astype(BF16), vblk), m

    def sweep(online):
        def run(carry):
            carry = lax.cond(lo == 0, lambda c: block(0, c, True, online), lambda c: c, carry)
            start = jnp.maximum(lo, 1)
            n = jnp.maximum(qi - start, 0)

            def group(i, c):
                acc, m = c
                kvs = []
                for j in range(FOX_GROUP):
                    st = pl.multiple_of((start + FOX_GROUP * i + j) * t, t)
                    kvs.append((k_ref[0, 0, pl.ds(st, t), :], v_ref[0, 0, pl.ds(st, t), :]))
                ss = [_dot_nt(q, kblk) for kblk, _ in kvs]
                prs = [jnp.exp2(s - m).astype(BF16) for s in ss]
                for pr, (_, vblk) in zip(prs, kvs):
                    acc = acc + _dot(pr, vblk)
                return acc, m

            grouped = 0 if online else (n // FOX_GROUP) * FOX_GROUP
            if not online:
                carry = lax.fori_loop(0, n // FOX_GROUP, group, carry)
            carry = lax.fori_loop(start + grouped, qi,
                                  lambda kb, c: block(kb, c, False, online), carry)
            return lax.cond(qi > 0, lambda c: block(qi, c, True, online), lambda c: c, carry)
        return run

    acc0 = jnp.zeros((t, LANE), F32)
    fixed = shift <= FOX_FIXED_SHIFT_MAX
    m0 = jnp.where(fixed, jnp.full((t, 1), shift, F32), jnp.full((t, 1), NEG_INF, F32))
    acc, _ = lax.cond(fixed, sweep(False), sweep(True), (acc0, m0))
    lane = lax.broadcasted_iota(jnp.int32, (t, LANE), 1)
    denom = jnp.sum(jnp.where(lane == HEAD_DIM, acc, 0.0), axis=-1, keepdims=True)
    o_ref[0, 0] = (acc / jnp.where(denom > 0.0, denom, 1.0)).astype(BF16)


def _fox_attention(shift, fe, fx, *, t=FOX_TILE):
    B, _, TP, _ = fx.shape
    H = FOX_HEADS
    kern = functools.partial(_fox_kernel, t=t)
    return pl.pallas_call(
        kern, out_shape=jax.ShapeDtypeStruct((B, H, TP, LANE), BF16),
        grid=(B, H, TP // t),
        in_specs=[pl.BlockSpec(memory_space=pltpu.SMEM),
                  pl.BlockSpec(memory_space=pltpu.SMEM),
                  pl.BlockSpec((1, 1, t, LANE), lambda b, h, i: (b, h, i, 0)),
                  pl.BlockSpec((1, 1, TP, LANE), lambda b, h, i: (b, h + H, 0, 0)),
                  pl.BlockSpec((1, 1, TP, LANE), lambda b, h, i: (b, h + 2 * H, 0, 0))],
        out_specs=pl.BlockSpec((1, 1, t, LANE), lambda b, h, i: (b, h, i, 0)),
        compiler_params=pltpu.CompilerParams(
            dimension_semantics=("arbitrary",) * 3, vmem_limit_bytes=VMEM_LIMIT),
        name="fox_attention",
    )(shift, fe, fx, fx, fx)


def _merge_kernel(*refs, tm, n_valid, first, n_seq):
    n_stream = 2 if first else 1
    (ya_ref, yb2_ref, yb1_ref, yc_ref, gate_ref,
     wpa_ref, wpb_ref, wpc_ref, wout_ref, o_ref) = refs[n_stream:]
    i = pl.program_id(1)

    def proj(slots, w_ref):
        return _dot(jnp.concatenate(slots, axis=-1), w_ref[...])

    pa = proj([ya_ref[0, s] for s in range(PAIRS)], wpa_ref)
    pb = proj([yb2_ref[0, 0], yb2_ref[0, 1], yb1_ref[0, 0]], wpb_ref)
    pc = proj([yc_ref[0, s] for s in range(FOX_HEADS)], wpc_ref)
    merged = gate_ref[0, :, 0:D_MODEL].astype(F32) * pa
    merged = merged + gate_ref[0, :, D_MODEL:2 * D_MODEL].astype(F32) * pb
    merged = merged + gate_ref[0, :, 2 * D_MODEL:3 * D_MODEL].astype(F32) * pc
    out = _stream_tile(i, refs[:n_stream], n_seq) + _dot(merged.astype(BF16), wout_ref[...])
    t = i * tm + lax.broadcasted_iota(jnp.int32, (tm, 1), 0)
    o_ref[0] = jnp.where((t >= PAD) & (t < PAD + n_valid), out, 0.0)


def _merge(stream, ya, yb2, yb1, yc, gates, wpa, wpb, wpc, wout, *, n_valid, n_seq, tm=FRAME_TILE):
    first = len(stream) == 2
    B, _, TP, _ = ya.shape
    D = D_MODEL
    kern = functools.partial(_merge_kernel, tm=tm, n_valid=n_valid, first=first, n_seq=n_seq)
    pair_blk = lambda ns: pl.BlockSpec((1, ns, tm, LANE), lambda b, i: (b, 0, i, 0))
    return pl.pallas_call(
        kern, out_shape=jax.ShapeDtypeStruct((B, TP, D), F32), grid=(B, TP // tm),
        in_specs=_stream_specs(first, tm, n_seq) + [
                  pair_blk(PAIRS),
                  pair_blk(2), pair_blk(1), pair_blk(FOX_HEADS),
                  pl.BlockSpec((1, tm, 3 * D), lambda b, i: (b, i, 0)),
                  _const_spec((PAIRS * LANE, D)),
                  _const_spec((PAIRS * LANE, D)),
                  _const_spec((FOX_HEADS * LANE, D)),
                  _const_spec((D, D))],
        out_specs=pl.BlockSpec((1, tm, D), lambda b, i: (b, i, 0)),
        compiler_params=pltpu.CompilerParams(
            dimension_semantics=("arbitrary", "arbitrary"), vmem_limit_bytes=VMEM_LIMIT),
        name="merge",
    )(*stream, ya, yb2, yb1, yc, gates, wpa, wpb, wpc, wout)


def _moe_kernel(h_ref, g_ref, wr_ref, br_ref, wg_ref, wu_ref, wd_ref, o_ref, he_ref,
                *, tm, n_valid, last, n_seq):
    i = pl.program_id(1)
    x = h_ref[0]
    ms = jnp.mean(x * x, axis=-1, keepdims=True)
    n = x * lax.rsqrt(ms + NORM_EPS) * g_ref[...]

    lg = _dot3(n, wr_ref[...]) + br_ref[...]
    lane = lax.broadcasted_iota(jnp.int32, (tm, LANE), 1)
    big = jnp.int32(LANE)
    is_group = lane < N_GROUPS
    gl = jnp.where(is_group, lg, -jnp.inf)
    gmax = jnp.max(gl, axis=-1, keepdims=True)
    g_idx = jnp.min(jnp.where(is_group & (gl == gmax), lane, big), axis=-1, keepdims=True)
    p_group = 1.0 / jnp.sum(jnp.where(is_group, jnp.exp(gl - gmax), 0.0), axis=-1, keepdims=True)
    lo = N_GROUPS + g_idx * EXPERTS_PER_GROUP
    in_grp = (lane >= lo) & (lane < lo + EXPERTS_PER_GROUP)
    el = jnp.where(in_grp, lg, -jnp.inf)
    top1 = jnp.max(el, axis=-1, keepdims=True)
    i1 = jnp.min(jnp.where(in_grp & (el == top1), lane, big), axis=-1, keepdims=True)
    rest = in_grp & (lane != i1)
    el2 = jnp.where(rest, lg, -jnp.inf)
    top2 = jnp.max(el2, axis=-1, keepdims=True)
    i2 = jnp.min(jnp.where(rest & (el2 == top2), lane, big), axis=-1, keepdims=True)
    e2 = jnp.exp(top2 - top1)
    p1 = 1.0 / (1.0 + e2)
    p2 = e2 / (1.0 + e2)
    comb = p_group * (jnp.where(lane == i1, p1, 0.0) + jnp.where(lane == i2, p2, 0.0))

    n16 = n.astype(BF16)
    for e in range(N_EXPERTS):
        gate = _dot(n16, wg_ref[e])
        up = _dot(n16, wu_ref[e])
        c_e = comb[:, N_GROUPS + e:N_GROUPS + e + 1]
        he_ref[:, e * D_EXPERT:(e + 1) * D_EXPERT] = (
            gate * _sigmoid(gate) * up * c_e).astype(BF16)
    out = x + _dot(he_ref[...], wd_ref[...])
    if last:
        @pl.when((i >= 1) & (i <= n_seq))
        def _():
            o_ref[0] = out
    else:
        t = i * tm + lax.broadcasted_iota(jnp.int32, (tm, 1), 0)
        o_ref[0] = jnp.where((t >= PAD) & (t < PAD + n_valid), out, 0.0)


def _moe(h, g2, wr, br, wg, wu, wd, *, n_valid, n_seq, last, tm=FRAME_TILE):
    B, TP, D = h.shape
    kern = functools.partial(_moe_kernel, tm=tm, n_valid=n_valid, last=last, n_seq=n_seq)
    EW = N_EXPERTS * D_EXPERT
    if last:
        out_shape = jax.ShapeDtypeStruct((B, n_seq * tm, D), F32)
        out_spec = pl.BlockSpec((1, tm, D), lambda b, i: (b, jnp.clip(i - 1, 0, n_seq - 1), 0))
    else:
        out_shape = jax.ShapeDtypeStruct((B, TP, D), F32)
        out_spec = pl.BlockSpec((1, tm, D), lambda b, i: (b, i, 0))
    return pl.pallas_call(
        kern, out_shape=out_shape, grid=(B, TP // tm),
        in_specs=[pl.BlockSpec((1, tm, D), lambda b, i: (b, i, 0)),
                  _const_spec((1, D)), _const_spec((D, LANE)), _const_spec((1, LANE)),
                  _const_spec((N_EXPERTS, D, D_EXPERT)), _const_spec((N_EXPERTS, D, D_EXPERT)),
                  _const_spec((EW, D))],
        out_specs=out_spec,
        scratch_shapes=[pltpu.VMEM((tm, EW), BF16)],
        compiler_params=pltpu.CompilerParams(
            dimension_semantics=("arbitrary", "arbitrary"), vmem_limit_bytes=VMEM_LIMIT),
        name="moe",
    )(h, g2, wr, br, wg, wu, wd)


def _transpose_kernel(x_ref, o_ref):
    o_ref[...] = x_ref[...].T.astype(BF16)


def _transpose_to_bf16(a, *, tn=LANE):
    N, K = a.shape
    assert N % tn == 0
    return pl.pallas_call(
        _transpose_kernel, out_shape=jax.ShapeDtypeStruct((K, N), BF16), grid=(N // tn,),
        in_specs=[pl.BlockSpec((tn, K), lambda i: (i, 0))],
        out_specs=pl.BlockSpec((K, tn), lambda i: (0, i)),
        compiler_params=pltpu.CompilerParams(dimension_semantics=("arbitrary",)),
        name="transpose_cast",
    )(a)


def _pack_w_in(w_in):
    wt = jnp.swapaxes(w_in, 0, 1)
    rest = wt[SHIFT_W:]
    rows = [wt[:SHIFT_W]]
    off = 0
    for heads in (SB_HEADS,) * 3 + (FOX_HEADS,) * 3:
        rows.append(jnp.pad(rest[off:off + heads * HEAD_DIM],
                            ((0, PAIRS * LANE - heads * HEAD_DIM), (0, 0))))
        off += heads * HEAD_DIM
    rows.append(jnp.pad(rest[off:off + FOX_HEADS], ((0, LANE - FOX_HEADS), (0, 0))))
    rows.append(rest[off + FOX_HEADS:])
    return _transpose_to_bf16(jnp.concatenate(rows, axis=0))


def _per_pair_cols(w):
    return w.reshape(w.shape[0], PAIRS, LANE).transpose(1, 0, 2)


def kernel(x, meta_tokens, norm1_g, w_in, rwkv_mu, rwkv_w_up, rwkv_w0, rwkv_a_up, rwkv_a0,
           rwkv_k_k, rwkv_k_a, rwkv_r_k, rwkv_ln_g, rwkv_ln_b, fox_f_b, fox_q_g, fox_k_g,
           w_p_rwkv, w_p_sb, w_p_fox, w_out, norm2_g, moe_wg, moe_bg, moe_we, moe_be,
           moe_w_gate, moe_w_up, moe_w_down):
    B, S, D = x.shape
    depth = w_in.shape[0]
    L = N_META + S
    TP = -(-(PAD + L) // ROW_TILE) * ROW_TILE
    assert S % FRAME_TILE == 0
    n_seq = S // FRAME_TILE
    meta = jnp.broadcast_to(meta_tokens[None].astype(x.dtype), (B, N_META, D))
    stream = (x, jnp.concatenate([jnp.zeros((B, PAD, D), x.dtype), meta], axis=1))
    H = RWKV_HEADS
    EW = N_EXPERTS * D_EXPERT
    for l in range(depth):
        fb = jnp.zeros((1, LANE), F32).at[0, :FOX_HEADS].set(fox_f_b[l])
        rkv, wdad, sb, fx, fend, gates = _inproj(
            stream, norm1_g[l][None], _pack_w_in(w_in[l]), rwkv_mu[l][None], fb,
            jnp.tile(fox_q_g[l], 2)[None], jnp.tile(fox_k_g[l], 2)[None], TP=TP, n_seq=n_seq)
        hv = lambda p: p.reshape(PAIRS, 1, LANE)
        gm, hc, r2, y0, bonus = _rwkv_chunks(
            rkv, wdad, _per_pair_cols(rwkv_w_up[l]), hv(rwkv_w0[l]),
            _per_pair_cols(rwkv_a_up[l]), hv(rwkv_a0[l]),
            hv(rwkv_k_k[l]), hv(rwkv_k_a[l]), hv(rwkv_r_k[l]))
        ya = _rwkv_scan(gm, hc, r2, y0, bonus, hv(rwkv_ln_g[l]), hv(rwkv_ln_b[l]))
        yb2 = _sb_attention(sb, slot0=0, nslots=2, nh=2)
        yb1 = _sb_attention(sb, slot0=2, nslots=1, nh=1)
        shift = (8.0 * LOG2E * jnp.max(jnp.abs(fox_q_g[l])) * jnp.max(jnp.abs(fox_k_g[l]))).reshape(1)
        step = FOX_TILE // INPROJ_TILE
        fe = jnp.transpose(fend[:, step - 1::step, 0, :FOX_HEADS], (0, 2, 1))
        yc = _fox_attention(shift, fe, fx)
        pc = jnp.pad(w_p_fox[l].reshape(FOX_HEADS, HEAD_DIM, D),
                     ((0, 0), (0, LANE - HEAD_DIM), (0, 0))).reshape(FOX_HEADS * LANE, D).astype(BF16)
        pp = lambda w, nh: jnp.pad(w, ((0, PAIRS * LANE - nh * HEAD_DIM), (0, 0))).astype(BF16)
        h = _merge(stream, ya, yb2, yb1, yc, gates, pp(w_p_rwkv[l], H), pp(w_p_sb[l], SB_HEADS),
                   pc, w_out[l].astype(BF16), n_valid=L, n_seq=n_seq)
        wr = jnp.zeros((D, LANE), F32).at[:, :N_GROUPS].set(moe_wg[l])
        wr = wr.at[:, N_GROUPS:N_GROUPS + N_EXPERTS].set(moe_we[l])
        br = jnp.zeros((1, LANE), F32).at[0, :N_GROUPS].set(moe_bg[l])
        br = br.at[0, N_GROUPS:N_GROUPS + N_EXPERTS].set(moe_be[l])
        wg = moe_w_gate[l].astype(BF16)
        wu = moe_w_up[l].astype(BF16)
        wd = moe_w_down[l].reshape(EW, D).astype(BF16)
        h = _moe(h, norm2_g[l][None], wr, br, wg, wu, wd, n_valid=L, n_seq=n_seq,
                 last=l == depth - 1)
        stream = (h,)
    return h
```

```python
import functools
import math

import jax
import jax.numpy as jnp
from jax import lax
from jax.experimental import pallas as pl
from jax.experimental.pallas import tpu as pltpu

D_MODEL = 1024
HEAD_DIM = 64
N_META = 16
FRAME_TILE = 256
MOE_TILE = 768
MOE_CAP = 256
PAD = FRAME_TILE - N_META
RWKV_HEADS = 6
SB_HEADS = 5
FOX_HEADS = 5
RWKV_W = RWKV_HEADS * HEAD_DIM
SB_W = SB_HEADS * HEAD_DIM
FOX_W = FOX_HEADS * HEAD_DIM
DECAY_RANK = 64
ICLR_RANK = 64
SHIFT_W = 3 * RWKV_W + DECAY_RANK + ICLR_RANK
N_GROUPS = 4
EXPERTS_PER_GROUP = 4
N_EXPERTS = 16
D_EXPERT = 256
NORM_EPS = 1e-6
RWKV_LN_EPS = 64e-5
NEG_INF = -1e30
ATT_SCALE = 1.0 / math.sqrt(HEAD_DIM)

LANE = 128
ROW_TILE = 512
PAIRS = 3
QKV_W = 3 * PAIRS * LANE
SEG_SB = SHIFT_W
SEG_FOX = SEG_SB + QKV_W
SEG_GATE = SEG_FOX + QKV_W + LANE
D_IN_PAD = SEG_GATE + 3 * D_MODEL
SB_CUTOFF_LOG2 = 160.0
LOG2E = 1.4426950408889634
FOX_FIXED_SHIFT_MAX = 50.0
FOX_SKIP_LOG2 = 150.0
FOX_TILE = 512
FOX_GROUP = 4
INPROJ_TILE = FRAME_TILE

CHUNK = 64
VMEM_LIMIT = 56 * 1024 * 1024

F32 = jnp.float32
BF16 = jnp.bfloat16


def _log_sigmoid(x):
    return jnp.minimum(x, 0.0) - jnp.log1p(jnp.exp(-jnp.abs(x)))


def _sigmoid(x):
    return 1.0 / (1.0 + jnp.exp(-x))


def _dot(a, b, **kw):
    return jnp.dot(a, b, preferred_element_type=F32, **kw)


def _dot_nt(a, b):
    return lax.dot_general(a, b, (((1,), (1,)), ((), ())), preferred_element_type=F32)


def _dot_tn(a, b):
    return lax.dot_general(a, b, (((0,), (0,)), ((), ())), preferred_element_type=F32)


def _dot3(a, b):
    ah = a.astype(BF16)
    al = (a - ah.astype(F32)).astype(BF16)
    bh = b.astype(BF16)
    bl = (b - bh.astype(F32)).astype(BF16)
    return _dot(ah, bh) + (_dot(ah, bl) + _dot(al, bh))


def _stream_specs(first, tm, n_seq):
    if not first:
        return [pl.BlockSpec((1, tm, D_MODEL), lambda b, i: (b, i, 0))]
    return [pl.BlockSpec((1, tm, D_MODEL), lambda b, i: (b, jnp.clip(i - 1, 0, n_seq - 1), 0)),
            pl.BlockSpec((1, tm, D_MODEL), lambda b, i: (b, 0, 0))]


def _stream_tile(i, refs, n_seq):
    if len(refs) == 1:
        return refs[0][0]
    x_ref, head_ref = refs
    return jnp.where(i == 0, head_ref[0], jnp.where(i <= n_seq, x_ref[0], 0.0))


def _const_spec(shape):
    n = len(shape)
    return pl.BlockSpec(shape, lambda *_: (0,) * n, pipeline_mode=pl.Buffered(1))


def _inproj_kernel(*refs, tm, first, n_seq):
    n_stream = 2 if first else 1
    (g_ref, w_ref, mu_ref, fb_ref, fqg_ref, fkg_ref,
     rkv_ref, wdad_ref, sb_ref, fx_ref, fend_ref, gate_ref, carry_u, carry_f) = refs[n_stream:]
    i = pl.program_id(1)

    @pl.when(i == 0)
    def _():
        carry_u[...] = jnp.zeros_like(carry_u)
        carry_f[...] = jnp.zeros_like(carry_f)

    x = _stream_tile(i, refs[:n_stream], n_seq)
    ms = jnp.mean(x * x, axis=-1, keepdims=True)
    n = (x * lax.rsqrt(ms + NORM_EPS) * g_ref[...]).astype(BF16)
    row = lax.broadcasted_iota(jnp.int32, (tm, 1), 0)

    us = _dot(n, w_ref[:, 0:SHIFT_W])
    prev = pltpu.roll(us, 1, axis=0)
    prev = jnp.where(row == 0, carry_u[...], prev)
    carry_u[...] = us[tm - 1:tm, :]
    ush = us + (prev - us) * mu_ref[...]
    for j in range(3 * PAIRS):
        rkv_ref[0, j] = ush[:, j * LANE:(j + 1) * LANE]
    wdad_ref[0] = ush[:, 3 * RWKV_W:SHIFT_W]

    usb = _dot(n, w_ref[:, SEG_SB:SEG_SB + QKV_W])
    for j in range(3 * PAIRS):
        piece = usb[:, j * LANE:(j + 1) * LANE]
        if j < PAIRS:
            piece = piece * (ATT_SCALE * LOG2E)
        sb_ref[0, j] = piece.astype(BF16)

    uf = _dot(n, w_ref[:, SEG_FOX:SEG_FOX + QKV_W + LANE])
    lane = lax.broadcasted_iota(jnp.int32, (tm, LANE), 1)
    first = lane < HEAD_DIM
    t_glob = i * tm + row
    logf = _log_sigmoid(uf[:, QKV_W:QKV_W + LANE] + fb_ref[...])
    logf = jnp.where((lane < FOX_HEADS) & (t_glob >= PAD), logf, 0.0)
    tri = (lax.broadcasted_iota(jnp.int32, (tm, tm), 0)
           >= lax.broadcasted_iota(jnp.int32, (tm, tm), 1)).astype(BF16)
    f_hi = logf.astype(BF16)
    f_mid = (logf - f_hi.astype(F32)).astype(BF16)
    f_lo = (logf - f_hi.astype(F32) - f_mid.astype(F32)).astype(BF16)
    cum = _dot(tri, f_hi) + _dot(tri, f_mid) + _dot(tri, f_lo) + carry_f[...]
    carry_f[...] = cum[tm - 1:tm, :]
    fend_ref[0, 0] = cum[tm - 1:tm, :] * LOG2E

    def split3(hd):
        f2 = jnp.sum(jnp.where(lane == hd, cum, 0.0), axis=-1, keepdims=True) * LOG2E
        hi = f2.astype(BF16).astype(F32)
        mid = (f2 - hi).astype(BF16).astype(F32)
        return hi, mid, f2 - hi - mid

    def tail_cols(vals):
        out = jnp.zeros((tm, LANE), F32)
        for o, val in enumerate(vals):
            out = jnp.where(lane == HEAD_DIM + o, val, out)
        return out

    splits = [split3(hd) for hd in range(FOX_HEADS)]
    for j in range(3 * PAIRS):
        piece = uf[:, j * LANE:(j + 1) * LANE]
        kind = j // PAIRS
        if kind < 2:
            gain = fqg_ref[...] if kind == 0 else fkg_ref[...]
            sq = piece * piece
            ms0 = jnp.sum(jnp.where(first, sq, 0.0), axis=-1, keepdims=True) * (1.0 / HEAD_DIM)
            ms1 = jnp.sum(jnp.where(first, 0.0, sq), axis=-1, keepdims=True) * (1.0 / HEAD_DIM)
            inv = jnp.where(first, lax.rsqrt(ms0 + NORM_EPS), lax.rsqrt(ms1 + NORM_EPS))
            piece = piece * inv * gain
            if kind == 0:
                piece = piece * (ATT_SCALE * LOG2E)
        swapped = pltpu.roll(piece, HEAD_DIM, axis=1)
        for half in range(2):
            hd = 2 * (j % PAIRS) + half
            if hd >= FOX_HEADS:
                continue
            if kind == 2:
                extra = tail_cols([1.0])
            else:
                hi, mid, lo = splits[hd]
                extra = tail_cols([hi, mid, lo, 1.0, 1.0, 1.0] if kind == 0
                                  else [1.0, 1.0, 1.0, -hi, -mid, -lo])
            body = piece if half == 0 else swapped
            fx_ref[0, kind * FOX_HEADS + hd] = jnp.where(first, body, extra).astype(BF16)

    ug = _dot(n, w_ref[:, SEG_GATE:D_IN_PAD])
    gate_ref[0] = _sigmoid(ug).astype(BF16)


def _inproj(stream, g1, w_in_p, mu, fb, fqg, fkg, *, TP, n_seq, tm=INPROJ_TILE):
    first = len(stream) == 2
    B, D = stream[0].shape[0], D_MODEL
    nb = TP // tm
    kern = functools.partial(_inproj_kernel, tm=tm, first=first, n_seq=n_seq)
    out_shape = (
        jax.ShapeDtypeStruct((B, 3 * PAIRS, TP, LANE), F32),
        jax.ShapeDtypeStruct((B, TP, LANE), F32),
        jax.ShapeDtypeStruct((B, 3 * PAIRS, TP, LANE), BF16),
        jax.ShapeDtypeStruct((B, 3 * FOX_HEADS, TP, LANE), BF16),
        jax.ShapeDtypeStruct((B, nb, 1, LANE), F32),
        jax.ShapeDtypeStruct((B, TP, 3 * D_MODEL), BF16),
    )
    in_specs = _stream_specs(first, tm, n_seq) + [
        _const_spec((1, D)),
        _const_spec((D, D_IN_PAD)),
        _const_spec((1, SHIFT_W)),
        _const_spec((1, LANE)),
        _const_spec((1, LANE)),
        _const_spec((1, LANE)),
    ]
    out_specs = (
        pl.BlockSpec((1, 3 * PAIRS, tm, LANE), lambda b, i: (b, 0, i, 0)),
        pl.BlockSpec((1, tm, LANE), lambda b, i: (b, i, 0)),
        pl.BlockSpec((1, 3 * PAIRS, tm, LANE), lambda b, i: (b, 0, i, 0)),
        pl.BlockSpec((1, 3 * FOX_HEADS, tm, LANE), lambda b, i: (b, 0, i, 0)),
        pl.BlockSpec((1, 1, 1, LANE), lambda b, i: (b, i, 0, 0)),
        pl.BlockSpec((1, tm, 3 * D_MODEL), lambda b, i: (b, i, 0)),
    )
    return pl.pallas_call(
        kern, out_shape=out_shape, grid=(B, nb), in_specs=in_specs, out_specs=out_specs,
        scratch_shapes=[pltpu.VMEM((1, SHIFT_W), F32), pltpu.VMEM((1, LANE), F32)],
        compiler_params=pltpu.CompilerParams(
            dimension_semantics=("arbitrary", "arbitrary"), vmem_limit_bytes=VMEM_LIMIT),
        name="inproj",
    )(*stream, g1, w_in_p, mu, fb, fqg, fkg)


def _bdot(a, b):
    return lax.dot_general(a, b, (((2,), (1,)), ((0,), (0,))), preferred_element_type=F32)


def _bdot_nt(a, b):
    return lax.dot_general(a, b, (((2,), (2,)), ((0,), (0,))), preferred_element_type=F32)


def _bdot_tn(a, b):
    return lax.dot_general(a, b, (((1,), (1,)), ((0,), (0,))), preferred_element_type=F32)


def _head_sum(x, first):
    s0 = jnp.sum(jnp.where(first, x, 0.0), axis=-1, keepdims=True)
    s1 = jnp.sum(jnp.where(first, 0.0, x), axis=-1, keepdims=True)
    return jnp.where(first, s0, s1)


def _rwkv_chunk_kernel(r_ref, k_ref, v_ref, wdad_ref, wup_ref, w0_ref, aup_ref, a0_ref,
                       kk_ref, ka_ref, rk_ref,
                       g_ref, hc_ref, r2_ref, y0_ref, bonus_ref, *, rows):
    nc = rows // CHUNK
    r = r_ref[0, 0]
    k = k_ref[0, 0]
    v = v_ref[0, 0]
    wd = wdad_ref[0][:, 0:DECAY_RANK]
    ad = wdad_ref[0][:, DECAY_RANK:DECAY_RANK + ICLR_RANK]
    first = lax.broadcasted_iota(jnp.int32, (rows, LANE), 1) < HEAD_DIM

    pre = w0_ref[0] + _dot3(jnp.tanh(wd), wup_ref[0])
    lw = -jnp.exp(_log_sigmoid(pre) - 0.5)
    iclr = _sigmoid(a0_ref[0] + _dot3(ad, aup_ref[0]))
    kk = k * kk_ref[0]
    kk = kk / jnp.maximum(jnp.sqrt(_head_sum(kk * kk, first)), 1e-12)
    k2 = k * (1.0 + (iclr - 1.0) * ka_ref[0])
    b = kk * iclr
    bonus_ref[0, 0] = _head_sum(r * k2 * rk_ref[0], first) * v

    to3 = lambda x: x.reshape(nc, CHUNK, LANE)
    ri = lax.broadcasted_iota(jnp.int32, (nc, CHUNK, CHUNK), 1)
    ci = lax.broadcasted_iota(jnp.int32, (nc, CHUNK, CHUNK), 2)
    low_incl = ri >= ci
    low_strict = ri > ci
    first3 = lax.broadcasted_iota(jnp.int32, (nc, CHUNK, LANE), 2) < HEAD_DIM

    lw3 = to3(lw)
    tri = low_incl.astype(BF16)
    lw_hi = lw3.astype(BF16)
    cum = _bdot(tri, lw_hi) + _bdot(tri, (lw3 - lw_hi.astype(F32)).astype(BF16))
    cum_end = cum[:, CHUNK - 1:CHUNK, :]
    e_neg = jnp.exp(-cum)
    at = to3(-kk) * jnp.exp(cum - lw3)
    rt = to3(r) * jnp.exp(cum)
    bt = (to3(b) * e_neg).astype(BF16)
    kt = (to3(k2) * e_neg).astype(BF16)
    e_rem = jnp.exp(cum_end - cum)
    bq = (to3(b) * e_rem).astype(BF16)
    kq = (to3(k2) * e_rem).astype(BF16)
    vv = to3(v).astype(BF16)

    heads = range(2)
    sels = [first3, jnp.logical_not(first3)]
    lhs = [jnp.concatenate([jnp.where(sels[hd], at, 0.0), jnp.where(sels[hd], rt, 0.0)],
                           axis=1).astype(BF16) for hd in heads]
    mb = [_bdot_nt(lhs[hd], bt) for hd in heads]
    mk = [_bdot_nt(lhs[hd], kt) for hd in heads]
    m_ak = [jnp.where(low_strict, mk[hd][:, :CHUNK], 0.0).astype(BF16) for hd in heads]
    m_rb = [jnp.where(low_incl, mb[hd][:, CHUNK:], 0.0).astype(BF16) for hd in heads]
    m_rk = [jnp.where(low_incl, mk[hd][:, CHUNK:], 0.0).astype(BF16) for hd in heads]
    p = [jnp.where(low_strict, mb[hd][:, :CHUNK], 0.0) for hd in heads]
    xs = [jnp.concatenate([at, _bdot(m_ak[hd], vv)], axis=-1) for hd in heads]
    for j in range(6):
        p16 = [p[hd].astype(BF16) for hd in heads]
        xs = [xs[hd] + _bdot(p16[hd], xs[hd].astype(BF16)) for hd in heads]
        if j < 5:
            p = [_bdot(p16[hd], p16[hd]) for hd in heads]
    ru = [_bdot(m_rb[hd], xs[hd].astype(BF16)) for hd in heads]
    r2s = [rt + ru[hd][..., :LANE] for hd in heads]
    y0s = [ru[hd][..., LANE:] + _bdot(m_rk[hd], vv) for hd in heads]

    first3w = jnp.concatenate([first3, first3], axis=-1)
    x = jnp.where(first3w, xs[0], xs[1])
    r2_ref[0, 0] = jnp.where(first3, r2s[0], r2s[1]).reshape(rows, LANE)
    y0_ref[0, 0] = jnp.where(first3, y0s[0], y0s[1]).reshape(rows, LANE)

    pg = _bdot_tn(bq, x.astype(BF16))
    ph = pg[..., LANE:] + _bdot_tn(kq, vv)
    rr = lax.broadcasted_iota(jnp.int32, (nc, LANE, LANE), 1)
    cc = lax.broadcasted_iota(jnp.int32, (nc, LANE, LANE), 2)
    same_head = (rr < HEAD_DIM) == (cc < HEAD_DIM)
    g = jnp.where(same_head, pg[..., :LANE], 0.0) + jnp.where(rr == cc, jnp.exp(cum_end), 0.0)
    g_ref[0, 0] = g.reshape(nc * LANE, LANE)
    hc_ref[0, 0] = jnp.where(same_head, ph, 0.0).reshape(nc * LANE, LANE)


def _rwkv_chunks(rkv, wdad, wup, w0, aup, a0, k_k, k_a, r_k):
    B, _, TP, _ = rkv.shape
    rows = next(r for r in (768, 512, 256) if TP % r == 0)
    nb = TP // rows
    kern = functools.partial(_rwkv_chunk_kernel, rows=rows)
    slot_spec = lambda off: pl.BlockSpec((1, 1, rows, LANE), lambda b, p, i: (b, p + off, i, 0))
    par_mat = pl.BlockSpec((1, DECAY_RANK, LANE), lambda b, p, i: (p, 0, 0))
    par_vec = pl.BlockSpec((1, 1, LANE), lambda b, p, i: (p, 0, 0))
    row_out = jax.ShapeDtypeStruct((B, PAIRS, TP, LANE), F32)
    mat_out = jax.ShapeDtypeStruct((B, PAIRS, 2 * TP, LANE), F32)
    row_spec = pl.BlockSpec((1, 1, rows, LANE), lambda b, p, i: (b, p, i, 0))
    mat_spec = pl.BlockSpec((1, 1, 2 * rows, LANE), lambda b, p, i: (b, p, i, 0))
    return pl.pallas_call(
        kern, out_shape=(mat_out, mat_out, row_out, row_out, row_out), grid=(B, PAIRS, nb),
        in_specs=[slot_spec(0), slot_spec(PAIRS), slot_spec(2 * PAIRS),
                  pl.BlockSpec((1, rows, LANE), lambda b, p, i: (b, i, 0)),
                  par_mat, par_vec, par_mat, par_vec, par_vec, par_vec, par_vec],
        out_specs=(mat_spec, mat_spec, row_spec, row_spec, row_spec),
        compiler_params=pltpu.CompilerParams(
            dimension_semantics=("arbitrary",) * 3, vmem_limit_bytes=VMEM_LIMIT),
        name="rwkv_chunks",
    )(rkv, rkv, rkv, wdad, wup, w0, aup, a0, k_k, k_a, r_k)


def _rwkv_scan_kernel(g_ref, hc_ref, r2_ref, y0_ref, bonus_ref, lng_ref, lnb_ref, y_ref,
                      state, *, rows):
    i = pl.program_id(1)

    @pl.when(i == 0)
    def _():
        state[...] = jnp.zeros_like(state)

    first = lax.broadcasted_iota(jnp.int32, (CHUNK, LANE), 1) < HEAD_DIM
    hs = [state[p] for p in range(PAIRS)]
    for c in range(rows // CHUNK):
        sl = slice(c * CHUNK, (c + 1) * CHUNK)
        sm = slice(c * LANE, (c + 1) * LANE)
        for p in range(PAIRS):
            y = _dot3(r2_ref[0, p, sl, :], hs[p]) + y0_ref[0, p, sl, :]
            hs[p] = _dot3(g_ref[0, p, sm, :], hs[p]) + hc_ref[0, p, sm, :]
            yc = y - _head_sum(y, first) * (1.0 / HEAD_DIM)
            var = _head_sum(yc * yc, first) * (1.0 / HEAD_DIM)
            out = yc * lax.rsqrt(var + RWKV_LN_EPS) * lng_ref[p] + lnb_ref[p]
            y_ref[0, p, sl, :] = (out + bonus_ref[0, p, sl, :]).astype(BF16)
    for p in range(PAIRS):
        state[p] = hs[p]


def _rwkv_scan(gm, hc, r2, y0, bonus, ln_g, ln_b, *, rows=512):
    B, P, TP, _ = r2.shape
    nb = TP // rows
    kern = functools.partial(_rwkv_scan_kernel, rows=rows)
    blk = pl.BlockSpec((1, P, rows, LANE), lambda b, i: (b, 0, i, 0))
    mat = pl.BlockSpec((1, P, 2 * rows, LANE), lambda b, i: (b, 0, i, 0))
    par = pl.BlockSpec((P, 1, LANE), lambda b, i: (0, 0, 0))
    return pl.pallas_call(
        kern, out_shape=jax.ShapeDtypeStruct((B, P, TP, LANE), BF16), grid=(B, nb),
        in_specs=[mat, mat, blk, blk, blk, par, par], out_specs=blk,
        scratch_shapes=[pltpu.VMEM((P, LANE, LANE), F32)],
        compiler_params=pltpu.CompilerParams(
            dimension_semantics=("arbitrary", "arbitrary"), vmem_limit_bytes=VMEM_LIMIT),
        name="rwkv_scan",
    )(gm, hc, r2, y0, bonus, ln_g, ln_b)


def _split_pair(qp, nh):
    lane = lax.broadcasted_iota(jnp.int32, qp.shape, 1)
    first = lane < HEAD_DIM
    zero = jnp.zeros_like(qp)
    qs = [jnp.where(first, qp, zero), jnp.where(first, zero, qp)]
    return qs[:nh], first


def _sb_kernel(q_ref, k_ref, v_ref, o_ref, *, t, nh):
    qi = pl.program_id(2)
    qs, first = _split_pair(q_ref[0, 0], nh)
    qpos = qi * t + lax.broadcasted_iota(jnp.int32, (t, 1), 0)
    kloc = lax.broadcasted_iota(jnp.int32, (1, t), 1)
    upper = (lax.broadcasted_iota(jnp.int32, (t, t), 0)
             > lax.broadcasted_iota(jnp.int32, (t, t), 1)).astype(BF16)

    def sweep(blocks, accs, cs):
        units = []
        for kb, mask in blocks:
            start = pl.multiple_of(kb * t, t)
            kblk = k_ref[0, 0, pl.ds(start, t), :]
            vblk = v_ref[0, 0, pl.ds(start, t), :]
            m = None if mask is None else mask(start + kloc)
            units += [(h, kblk, vblk, m) for h in range(nh)]
        zs = [_dot_nt(qs[h], kblk) for h, kblk, _, _ in units]
        sps = [jnp.maximum(z, 0.0) + jnp.log2(1.0 + jnp.exp2(-jnp.abs(z))) for z in zs]
        spms = [sp if u[3] is None else jnp.where(u[3], sp, 0.0) for sp, u in zip(sps, units)]
        laters = [_dot(spm.astype(BF16), upper) for spm in spms]
        accs, cs = list(accs), list(cs)
        weights = []
        for (h, _, _, m), z, sp, spm, later in zip(units, zs, sps, spms, laters):
            a = jnp.exp2(z - sp - later - cs[h])
            weights.append(a if m is None else jnp.where(m, a, 0.0))
            cs[h] = cs[h] + jnp.sum(spm, axis=-1, keepdims=True)
        for (h, _, vblk, _), a in zip(units, weights):
            accs[h] = accs[h] + _dot(a.astype(BF16), vblk)
        return accs, cs

    def block(kb, accs, cs, mask):
        return sweep([(kb, mask)], accs, cs)

    def live(cs):
        low = cs[0]
        for c in cs[1:]:
            low = jnp.minimum(low, c)
        return (jnp.min(low) < SB_CUTOFF_LOG2).astype(jnp.int32)

    zero_acc = [jnp.zeros((t, LANE), F32) for _ in range(nh)]
    zero_c = [jnp.zeros((t, 1), F32) for _ in range(nh)]
    accs, cs = sweep([(qi, lambda kpos: (kpos >= PAD) & (kpos < qpos)),
                      (jnp.maximum(qi - 1, 0), lambda kpos: (kpos >= PAD) & (qi >= 1))],
                     zero_acc, zero_c)

    def cond(carry):
        kb, alive = carry[0], carry[1]
        return (kb >= 1) & (alive > 0)

    def body(carry):
        kb = carry[0]
        accs, cs = block(kb, list(carry[2:2 + nh]), list(carry[2 + nh:]), None)
        return (kb - 1, live(cs), *accs, *cs)

    carry = lax.while_loop(cond, body, (qi - 2, live(cs), *accs, *cs))

    def front(carry):
        accs, cs = block(0, list(carry[2:2 + nh]), list(carry[2 + nh:]), lambda kpos: kpos >= PAD)
        return (carry[0], carry[1], *accs, *cs)

    carry = lax.cond((carry[0] == 0) & (carry[1] > 0), front, lambda c: c, carry)
    accs = carry[2:2 + nh]
    out = jnp.where(first, accs[0], accs[1] if nh == 2 else 0.0)
    o_ref[0, 0] = out.astype(BF16)


def _sb_attention(sb, *, slot0, nslots, nh, t=256):
    B, _, TP, _ = sb.shape
    kern = functools.partial(_sb_kernel, t=t, nh=nh)
    blk = lambda off: pl.BlockSpec((1, 1, t, LANE), lambda b, p, i: (b, slot0 + off + p, i, 0))
    full = lambda off: pl.BlockSpec((1, 1, TP, LANE), lambda b, p, i: (b, slot0 + off + p, 0, 0))
    return pl.pallas_call(
        kern, out_shape=jax.ShapeDtypeStruct((B, nslots, TP, LANE), BF16),
        grid=(B, nslots, TP // t),
        in_specs=[blk(0), full(PAIRS), full(2 * PAIRS)],
        out_specs=pl.BlockSpec((1, 1, t, LANE), lambda b, p, i: (b, p, i, 0)),
        compiler_params=pltpu.CompilerParams(
            dimension_semantics=("arbitrary",) * 3, vmem_limit_bytes=VMEM_LIMIT),
        name="sb_attention",
    )(sb, sb, sb)


def _fox_kernel(shift_ref, fe_ref, q_ref, k_ref, v_ref, o_ref, *, t):
    b = pl.program_id(0)
    h = pl.program_id(1)
    qi = pl.program_id(2)
    q = q_ref[0, 0]
    shift = shift_ref[0]
    thr = FOX_SKIP_LOG2 + 2.0 * shift
    f_q = fe_ref[b, h, jnp.maximum(qi - 1, 0)]
    lo = lax.fori_loop(
        0, qi, lambda kb, n: n + (fe_ref[b, h, kb] - f_q > thr).astype(jnp.int32), jnp.int32(0))
    qpos = qi * t + lax.broadcasted_iota(jnp.int32, (t, 1), 0)
    kloc = lax.broadcasted_iota(jnp.int32, (1, t), 1)

    def block(kb, carry, masked, online):
        acc, m = carry
        start = pl.multiple_of(kb * t, t)
        kblk = k_ref[0, 0, pl.ds(start, t), :]
        vblk = v_ref[0, 0, pl.ds(start, t), :]
        s = _dot_nt(q, kblk)
        if masked:
            kpos = start + kloc
            s = jnp.where((kpos <= qpos) & (kpos >= PAD), s, NEG_INF)
        if online:
            m_new = jnp.maximum(m, jnp.max(s, axis=-1, keepdims=True))
            acc = jnp.exp2(m - m_new) * acc
            m = m_new
        pr = jnp.exp2(s - m)
        return acc + _dot(pr.astype(BF16), vblk), m

    def sweep(online):
        def run(carry):
            carry = lax.cond(lo == 0, lambda c: block(0, c, True, online), lambda c: c, carry)
            start = jnp.maximum(lo, 1)
            n = jnp.maximum(qi - start, 0)

            def group(i, c):
                acc, m = c
                kvs = []
                for j in range(FOX_GROUP):
                    st = pl.multiple_of((start + FOX_GROUP * i + j) * t, t)
                    kvs.append((k_ref[0, 0, pl.ds(st, t), :], v_ref[0, 0, pl.ds(st, t), :]))
                ss = [_dot_nt(q, kblk) for kblk, _ in kvs]
                prs = [jnp.exp2(s - m).astype(BF16) for s in ss]
                for pr, (_, vblk) in zip(prs, kvs):
                    acc = acc + _dot(pr, vblk)
                return acc, m

            grouped = 0 if online else (n // FOX_GROUP) * FOX_GROUP
            if not online:
                carry = lax.fori_loop(0, n // FOX_GROUP, group, carry)
            carry = lax.fori_loop(start + grouped, qi,
                                  lambda kb, c: block(kb, c, False, online), carry)
            return lax.cond(qi > 0, lambda c: block(qi, c, True, online), lambda c: c, carry)
        return run

    acc0 = jnp.zeros((t, LANE), F32)
    fixed = shift <= FOX_FIXED_SHIFT_MAX
    m0 = jnp.where(fixed, jnp.full((t, 1), shift, F32), jnp.full((t, 1), NEG_INF, F32))
    acc, _ = lax.cond(fixed, sweep(False), sweep(True), (acc0, m0))
    lane = lax.broadcasted_iota(jnp.int32, (t, LANE), 1)
    denom = jnp.sum(jnp.where(lane == HEAD_DIM, acc, 0.0), axis=-1, keepdims=True)
    o_ref[0, 0] = (acc / jnp.where(denom > 0.0, denom, 1.0)).astype(BF16)


def _fox_attention(shift, fe, fx, *, t=FOX_TILE):
    B, _, TP, _ = fx.shape
    H = FOX_HEADS
    kern = functools.partial(_fox_kernel, t=t)
    return pl.pallas_call(
        kern, out_shape=jax.ShapeDtypeStruct((B, H, TP, LANE), BF16),
        grid=(B, H, TP // t),
        in_specs=[pl.BlockSpec(memory_space=pltpu.SMEM),
                  pl.BlockSpec(memory_space=pltpu.SMEM),
                  pl.BlockSpec((1, 1, t, LANE), lambda b, h, i: (b, h, i, 0)),
                  pl.BlockSpec((1, 1, TP, LANE), lambda b, h, i: (b, h + H, 0, 0)),
                  pl.BlockSpec((1, 1, TP, LANE), lambda b, h, i: (b, h + 2 * H, 0, 0))],
        out_specs=pl.BlockSpec((1, 1, t, LANE), lambda b, h, i: (b, h, i, 0)),
        compiler_params=pltpu.CompilerParams(
            dimension_semantics=("arbitrary",) * 3, vmem_limit_bytes=VMEM_LIMIT),
        name="fox_attention",
    )(shift, fe, fx, fx, fx)


def _merge_kernel(*refs, tm, n_valid, first, n_seq):
    n_stream = 2 if first else 1
    (ya_ref, yb2_ref, yb1_ref, yc_ref, gate_ref,
     wpa_ref, wpb_ref, wpc_ref, wout_ref, o_ref) = refs[n_stream:]
    i = pl.program_id(1)

    def proj(slots, w_ref):
        return _dot(jnp.concatenate(slots, axis=-1), w_ref[...])

    pa = proj([ya_ref[0, s] for s in range(PAIRS)], wpa_ref)
    pb = proj([yb2_ref[0, 0], yb2_ref[0, 1], yb1_ref[0, 0]], wpb_ref)
    pc = proj([yc_ref[0, s] for s in range(FOX_HEADS)], wpc_ref)
    merged = gate_ref[0, :, 0:D_MODEL].astype(F32) * pa
    merged = merged + gate_ref[0, :, D_MODEL:2 * D_MODEL].astype(F32) * pb
    merged = merged + gate_ref[0, :, 2 * D_MODEL:3 * D_MODEL].astype(F32) * pc
    out = _stream_tile(i, refs[:n_stream], n_seq) + _dot(merged.astype(BF16), wout_ref[...])
    t = i * tm + lax.broadcasted_iota(jnp.int32, (tm, 1), 0)
    o_ref[0] = jnp.where((t >= PAD) & (t < PAD + n_valid), out, 0.0)


def _merge(stream, ya, yb2, yb1, yc, gates, wpa, wpb, wpc, wout, *, n_valid, n_seq, tm=FRAME_TILE):
    first = len(stream) == 2
    B, _, TP, _ = ya.shape
    D = D_MODEL
    kern = functools.partial(_merge_kernel, tm=tm, n_valid=n_valid, first=first, n_seq=n_seq)
    pair_blk = lambda ns: pl.BlockSpec((1, ns, tm, LANE), lambda b, i: (b, 0, i, 0))
    return pl.pallas_call(
        kern, out_shape=jax.ShapeDtypeStruct((B, TP, D), F32), grid=(B, TP // tm),
        in_specs=_stream_specs(first, tm, n_seq) + [
                  pair_blk(PAIRS),
                  pair_blk(2), pair_blk(1), pair_blk(FOX_HEADS),
                  pl.BlockSpec((1, tm, 3 * D), lambda b, i: (b, i, 0)),
                  _const_spec((PAIRS * LANE, D)),
                  _const_spec((PAIRS * LANE, D)),
                  _const_spec((FOX_HEADS * LANE, D)),
                  _const_spec((D, D))],
        out_specs=pl.BlockSpec((1, tm, D), lambda b, i: (b, i, 0)),
        compiler_params=pltpu.CompilerParams(
            dimension_semantics=("arbitrary", "arbitrary"), vmem_limit_bytes=VMEM_LIMIT),
        name="merge",
    )(*stream, ya, yb2, yb1, yc, gates, wpa, wpb, wpc, wout)


def _moe_kernel(h_ref, g_ref, wr_ref, br_ref, wg_ref, wu_ref, wd_ref, o_ref, he_ref, ys_ref,
                *, tm, n_valid):
    i = pl.program_id(1)
    x = h_ref[0]
    ms = jnp.mean(x * x, axis=-1, keepdims=True)
    n = x * lax.rsqrt(ms + NORM_EPS) * g_ref[...]

    lg = _dot3(n, wr_ref[...]) + br_ref[...]
    lane = lax.broadcasted_iota(jnp.int32, (tm, LANE), 1)
    big = jnp.int32(LANE)
    is_group = lane < N_GROUPS
    gl = jnp.where(is_group, lg, -jnp.inf)
    gmax = jnp.max(gl, axis=-1, keepdims=True)
    g_idx = jnp.min(jnp.where(is_group & (gl == gmax), lane, big), axis=-1, keepdims=True)
    p_group = 1.0 / jnp.sum(jnp.where(is_group, jnp.exp(gl - gmax), 0.0), axis=-1, keepdims=True)
    lo = N_GROUPS + g_idx * EXPERTS_PER_GROUP
    in_grp = (lane >= lo) & (lane < lo + EXPERTS_PER_GROUP)
    el = jnp.where(in_grp, lg, -jnp.inf)
    top1 = jnp.max(el, axis=-1, keepdims=True)
    i1 = jnp.min(jnp.where(in_grp & (el == top1), lane, big), axis=-1, keepdims=True)
    rest = in_grp & (lane != i1)
    el2 = jnp.where(rest, lg, -jnp.inf)
    top2 = jnp.max(el2, axis=-1, keepdims=True)
    i2 = jnp.min(jnp.where(rest & (el2 == top2), lane, big), axis=-1, keepdims=True)
    e2 = jnp.exp(top2 - top1)
    p1 = 1.0 / (1.0 + e2)
    p2 = e2 / (1.0 + e2)
    comb = p_group * (jnp.where(lane == i1, p1, 0.0) + jnp.where(lane == i2, p2, 0.0))

    n16 = n.astype(BF16)

    onehot = (lane == g_idx).astype(BF16)
    earlier = (lax.broadcasted_iota(jnp.int32, (tm, tm), 1)
               < lax.broadcasted_iota(jnp.int32, (tm, tm), 0)).astype(BF16)
    cnt = _dot(earlier, onehot)
    rank = jnp.sum(jnp.where(lane == g_idx, cnt, 0.0), axis=-1, keepdims=True).astype(jnp.int32)
    n_pass = (jnp.max(rank) + MOE_CAP) // MOE_CAP
    comb16 = comb.astype(BF16)
    slot_lane = lax.broadcasted_iota(jnp.int32, (tm, N_GROUPS * MOE_CAP), 1)

    def one_pass(p, y):
        r = rank - p * MOE_CAP
        slot = jnp.where((r >= 0) & (r < MOE_CAP), g_idx * MOE_CAP + r, -1)
        place = (slot_lane == slot).astype(BF16)
        xs = _dot_tn(place, n16).astype(BF16)
        cw = _dot_tn(place, comb16)
        for g in range(N_GROUPS):
            rows = slice(g * MOE_CAP, (g + 1) * MOE_CAP)
            for e in range(EXPERTS_PER_GROUP):
                ee = g * EXPERTS_PER_GROUP + e
                gate = _dot(xs[rows], wg_ref[ee])
                up = _dot(xs[rows], wu_ref[ee])
                c_e = cw[rows, N_GROUPS + ee:N_GROUPS + ee + 1]
                he_ref[:, e * D_EXPERT:(e + 1) * D_EXPERT] = (
                    gate * _sigmoid(gate) * up * c_e).astype(BF16)
            gw = EXPERTS_PER_GROUP * D_EXPERT
            ys_ref[rows, :] = _dot(he_ref[...], wd_ref[g * gw:(g + 1) * gw, :]).astype(BF16)
        return y + _dot(place, ys_ref[...])

    out = x + lax.fori_loop(0, n_pass, one_pass, jnp.zeros((tm, D_MODEL), F32))
    t = i * tm + lax.broadcasted_iota(jnp.int32, (tm, 1), 0)
    o_ref[0] = jnp.where((t >= PAD) & (t < PAD + n_valid), out, 0.0)


def _moe(h, g2, wr, br, wg, wu, wd, *, n_valid):
    B, TP, D = h.shape
    tm = next(r for r in (MOE_TILE, FRAME_TILE) if TP % r == 0)
    kern = functools.partial(_moe_kernel, tm=tm, n_valid=n_valid)
    EW = N_EXPERTS * D_EXPERT
    return pl.pallas_call(
        kern, out_shape=jax.ShapeDtypeStruct((B, TP, D), F32), grid=(B, TP // tm),
        in_specs=[pl.BlockSpec((1, tm, D), lambda b, i: (b, i, 0)),
                  _const_spec((1, D)), _const_spec((D, LANE)), _const_spec((1, LANE)),
                  _const_spec((N_EXPERTS, D, D_EXPERT)), _const_spec((N_EXPERTS, D, D_EXPERT)),
                  _const_spec((EW, D))],
        out_specs=pl.BlockSpec((1, tm, D), lambda b, i: (b, i, 0)),
        scratch_shapes=[pltpu.VMEM((MOE_CAP, EXPERTS_PER_GROUP * D_EXPERT), BF16),
                        pltpu.VMEM((N_GROUPS * MOE_CAP, D), BF16)],
        compiler_params=pltpu.CompilerParams(
            dimension_semantics=("arbitrary", "arbitrary"), vmem_limit_bytes=VMEM_LIMIT),
        name="moe",
    )(h, g2, wr, br, wg, wu, wd)


def _transpose_kernel(x_ref, o_ref):
    o_ref[...] = x_ref[...].T.astype(BF16)


def _transpose_to_bf16(a, *, tn=LANE):
    N, K = a.shape
    assert N % tn == 0
    return pl.pallas_call(
        _transpose_kernel, out_shape=jax.ShapeDtypeStruct((K, N), BF16), grid=(N // tn,),
        in_specs=[pl.BlockSpec((tn, K), lambda i: (i, 0))],
        out_specs=pl.BlockSpec((K, tn), lambda i: (0, i)),
        compiler_params=pltpu.CompilerParams(dimension_semantics=("arbitrary",)),
        name="transpose_cast",
    )(a)


def _pack_w_in(w_in):
    wt = jnp.swapaxes(w_in, 0, 1)
    rest = wt[SHIFT_W:]
    rows = [wt[:SHIFT_W]]
    off = 0
    for heads in (SB_HEADS,) * 3 + (FOX_HEADS,) * 3:
        rows.append(jnp.pad(rest[off:off + heads * HEAD_DIM],
                            ((0, PAIRS * LANE - heads * HEAD_DIM), (0, 0))))
        off += heads * HEAD_DIM
    rows.append(jnp.pad(rest[off:off + FOX_HEADS], ((0, LANE - FOX_HEADS), (0, 0))))
    rows.append(rest[off + FOX_HEADS:])
    return _transpose_to_bf16(jnp.concatenate(rows, axis=0))


def _per_pair_cols(w):
    return w.reshape(w.shape[0], PAIRS, LANE).transpose(1, 0, 2)


def kernel(x, meta_tokens, norm1_g, w_in, rwkv_mu, rwkv_w_up, rwkv_w0, rwkv_a_up, rwkv_a0,
           rwkv_k_k, rwkv_k_a, rwkv_r_k, rwkv_ln_g, rwkv_ln_b, fox_f_b, fox_q_g, fox_k_g,
           w_p_rwkv, w_p_sb, w_p_fox, w_out, norm2_g, moe_wg, moe_bg, moe_we, moe_be,
           moe_w_gate, moe_w_up, moe_w_down):
    B, S, D = x.shape
    depth = w_in.shape[0]
    L = N_META + S
    TP = -(-(PAD + L) // ROW_TILE) * ROW_TILE
    assert S % FRAME_TILE == 0
    n_seq = S // FRAME_TILE
    meta = jnp.broadcast_to(meta_tokens[None].astype(x.dtype), (B, N_META, D))
    stream = (x, jnp.concatenate([jnp.zeros((B, PAD, D), x.dtype), meta], axis=1))
    H = RWKV_HEADS
    EW = N_EXPERTS * D_EXPERT
    for l in range(depth):
        fb = jnp.zeros((1, LANE), F32).at[0, :FOX_HEADS].set(fox_f_b[l])
        rkv, wdad, sb, fx, fend, gates = _inproj(
            stream, norm1_g[l][None], _pack_w_in(w_in[l]), rwkv_mu[l][None], fb,
            jnp.tile(fox_q_g[l], 2)[None], jnp.tile(fox_k_g[l], 2)[None], TP=TP, n_seq=n_seq)
        hv = lambda p: p.reshape(PAIRS, 1, LANE)
        gm, hc, r2, y0, bonus = _rwkv_chunks(
            rkv, wdad, _per_pair_cols(rwkv_w_up[l]), hv(rwkv_w0[l]),
            _per_pair_cols(rwkv_a_up[l]), hv(rwkv_a0[l]),
            hv(rwkv_k_k[l]), hv(rwkv_k_a[l]), hv(rwkv_r_k[l]))
        ya = _rwkv_scan(gm, hc, r2, y0, bonus, hv(rwkv_ln_g[l]), hv(rwkv_ln_b[l]))
        yb2 = _sb_attention(sb, slot0=0, nslots=2, nh=2)
        yb1 = _sb_attention(sb, slot0=2, nslots=1, nh=1)
        shift = (8.0 * LOG2E * jnp.max(jnp.abs(fox_q_g[l])) * jnp.max(jnp.abs(fox_k_g[l]))).reshape(1)
        step = FOX_TILE // INPROJ_TILE
        fe = jnp.transpose(fend[:, step - 1::step, 0, :FOX_HEADS], (0, 2, 1))
        yc = _fox_attention(shift, fe, fx)
        pc = jnp.pad(w_p_fox[l].reshape(FOX_HEADS, HEAD_DIM, D),
                     ((0, 0), (0, LANE - HEAD_DIM), (0, 0))).reshape(FOX_HEADS * LANE, D).astype(BF16)
        pp = lambda w, nh: jnp.pad(w, ((0, PAIRS * LANE - nh * HEAD_DIM), (0, 0))).astype(BF16)
        h = _merge(stream, ya, yb2, yb1, yc, gates, pp(w_p_rwkv[l], H), pp(w_p_sb[l], SB_HEADS),
                   pc, w_out[l].astype(BF16), n_valid=L, n_seq=n_seq)
        wr = jnp.zeros((D, LANE), F32).at[:, :N_GROUPS].set(moe_wg[l])
        wr = wr.at[:, N_GROUPS:N_GROUPS + N_EXPERTS].set(moe_we[l])
        br = jnp.zeros((1, LANE), F32).at[0, :N_GROUPS].set(moe_bg[l])
        br = br.at[0, N_GROUPS:N_GROUPS + N_EXPERTS].set(moe_be[l])
        wg = moe_w_gate[l].astype(BF16)
        wu = moe_w_up[l].astype(BF16)
        wd = moe_w_down[l].reshape(EW, D).astype(BF16)
        h = _moe(h, norm2_g[l][None], wr, br, wg, wu, wd, n_valid=L)
        stream = (h,)
    return h[:, PAD + N_META:PAD + L]
```

```python
import functools
import math

import jax
import jax.numpy as jnp
from jax import lax
from jax.experimental import pallas as pl
from jax.experimental.pallas import tpu as pltpu

D_MODEL = 1024
HEAD_DIM = 64
N_META = 16
FRAME_TILE = 256
MOE_TILE = 768
MOE_CAP = 256
PAD = FRAME_TILE - N_META
RWKV_HEADS = 6
SB_HEADS = 5
FOX_HEADS = 5
RWKV_W = RWKV_HEADS * HEAD_DIM
SB_W = SB_HEADS * HEAD_DIM
FOX_W = FOX_HEADS * HEAD_DIM
DECAY_RANK = 64
ICLR_RANK = 64
SHIFT_W = 3 * RWKV_W + DECAY_RANK + ICLR_RANK
N_GROUPS = 4
EXPERTS_PER_GROUP = 4
N_EXPERTS = 16
D_EXPERT = 256
NORM_EPS = 1e-6
RWKV_LN_EPS = 64e-5
NEG_INF = -1e30
ATT_SCALE = 1.0 / math.sqrt(HEAD_DIM)

LANE = 128
ROW_TILE = 512
PAIRS = 3
QKV_W = 3 * PAIRS * LANE
SEG_SB = SHIFT_W
SEG_FOX = SEG_SB + QKV_W
SEG_GATE = SEG_FOX + QKV_W + LANE
D_IN_PAD = SEG_GATE + 3 * D_MODEL
SB_CUTOFF_LOG2 = 160.0
LOG2E = 1.4426950408889634
FOX_FIXED_SHIFT_MAX = 50.0
FOX_SKIP_LOG2 = 150.0
FOX_TILE = 512
FOX_GROUP = 4
INPROJ_TILE = FRAME_TILE

CHUNK = 64
VMEM_LIMIT = 56 * 1024 * 1024

F32 = jnp.float32
BF16 = jnp.bfloat16


def _log_sigmoid(x):
    return jnp.minimum(x, 0.0) - jnp.log1p(jnp.exp(-jnp.abs(x)))


def _sigmoid(x):
    return 1.0 / (1.0 + jnp.exp(-x))


def _dot(a, b, **kw):
    return jnp.dot(a, b, preferred_element_type=F32, **kw)


def _dot_nt(a, b):
    return lax.dot_general(a, b, (((1,), (1,)), ((), ())), preferred_element_type=F32)


def _dot_tn(a, b):
    return lax.dot_general(a, b, (((0,), (0,)), ((), ())), preferred_element_type=F32)


def _dot3(a, b):
    ah = a.astype(BF16)
    al = (a - ah.astype(F32)).astype(BF16)
    bh = b.astype(BF16)
    bl = (b - bh.astype(F32)).astype(BF16)
    return _dot(ah, bh) + (_dot(ah, bl) + _dot(al, bh))


def _stream_specs(first, tm, n_seq):
    if not first:
        return [pl.BlockSpec((1, tm, D_MODEL), lambda b, i: (b, i, 0))]
    return [pl.BlockSpec((1, tm, D_MODEL), lambda b, i: (b, jnp.clip(i - 1, 0, n_seq - 1), 0)),
            pl.BlockSpec((1, tm, D_MODEL), lambda b, i: (b, 0, 0))]


def _stream_tile(i, refs, n_seq):
    if len(refs) == 1:
        return refs[0][0]
    x_ref, head_ref = refs
    return jnp.where(i == 0, head_ref[0], jnp.where(i <= n_seq, x_ref[0], 0.0))


def _const_spec(shape):
    n = len(shape)
    return pl.BlockSpec(shape, lambda *_: (0,) * n, pipeline_mode=pl.Buffered(1))


def _inproj_kernel(*refs, tm, first, n_seq):
    n_stream = 2 if first else 1
    (g_ref, w_ref, mu_ref, fb_ref, fqg_ref, fkg_ref,
     rkv_ref, wdad_ref, sb_ref, fx_ref, fend_ref, gate_ref, carry_u, carry_f) = refs[n_stream:]
    i = pl.program_id(1)

    @pl.when(i == 0)
    def _():
        carry_u[...] = jnp.zeros_like(carry_u)
        carry_f[...] = jnp.zeros_like(carry_f)

    x = _stream_tile(i, refs[:n_stream], n_seq)
    ms = jnp.mean(x * x, axis=-1, keepdims=True)
    n = (x * lax.rsqrt(ms + NORM_EPS) * g_ref[...]).astype(BF16)
    row = lax.broadcasted_iota(jnp.int32, (tm, 1), 0)

    us = _dot(n, w_ref[:, 0:SHIFT_W])
    prev = pltpu.roll(us, 1, axis=0)
    prev = jnp.where(row == 0, carry_u[...], prev)
    carry_u[...] = us[tm - 1:tm, :]
    ush = us + (prev - us) * mu_ref[...]
    for j in range(3 * PAIRS):
        rkv_ref[0, j] = ush[:, j * LANE:(j + 1) * LANE]
    wdad_ref[0] = ush[:, 3 * RWKV_W:SHIFT_W]

    usb = _dot(n, w_ref[:, SEG_SB:SEG_SB + QKV_W])
    for j in range(3 * PAIRS):
        piece = usb[:, j * LANE:(j + 1) * LANE]
        if j < PAIRS:
            piece = piece * (ATT_SCALE * LOG2E)
        sb_ref[0, j] = piece.astype(BF16)

    uf = _dot(n, w_ref[:, SEG_FOX:SEG_FOX + QKV_W + LANE])
    lane = lax.broadcasted_iota(jnp.int32, (tm, LANE), 1)
    first = lane < HEAD_DIM
    t_glob = i * tm + row
    logf = _log_sigmoid(uf[:, QKV_W:QKV_W + LANE] + fb_ref[...])
    logf = jnp.where((lane < FOX_HEADS) & (t_glob >= PAD), logf, 0.0)
    tri = (lax.broadcasted_iota(jnp.int32, (tm, tm), 0)
           >= lax.broadcasted_iota(jnp.int32, (tm, tm), 1)).astype(BF16)
    f_hi = logf.astype(BF16)
    f_mid = (logf - f_hi.astype(F32)).astype(BF16)
    f_lo = (logf - f_hi.astype(F32) - f_mid.astype(F32)).astype(BF16)
    cum = _dot(tri, f_hi) + _dot(tri, f_mid) + _dot(tri, f_lo) + carry_f[...]
    carry_f[...] = cum[tm - 1:tm, :]
    fend_ref[0, 0] = cum[tm - 1:tm, :] * LOG2E

    def split3(hd):
        f2 = jnp.sum(jnp.where(lane == hd, cum, 0.0), axis=-1, keepdims=True) * LOG2E
        hi = f2.astype(BF16).astype(F32)
        mid = (f2 - hi).astype(BF16).astype(F32)
        return hi, mid, f2 - hi - mid

    def tail_cols(vals):
        out = jnp.zeros((tm, LANE), F32)
        for o, val in enumerate(vals):
            out = jnp.where(lane == HEAD_DIM + o, val, out)
        return out

    splits = [split3(hd) for hd in range(FOX_HEADS)]
    for j in range(3 * PAIRS):
        piece = uf[:, j * LANE:(j + 1) * LANE]
        kind = j // PAIRS
        if kind < 2:
            gain = fqg_ref[...] if kind == 0 else fkg_ref[...]
            sq = piece * piece
            ms0 = jnp.sum(jnp.where(first, sq, 0.0), axis=-1, keepdims=True) * (1.0 / HEAD_DIM)
            ms1 = jnp.sum(jnp.where(first, 0.0, sq), axis=-1, keepdims=True) * (1.0 / HEAD_DIM)
            inv = jnp.where(first, lax.rsqrt(ms0 + NORM_EPS), lax.rsqrt(ms1 + NORM_EPS))
            piece = piece * inv * gain
            if kind == 0:
                piece = piece * (ATT_SCALE * LOG2E)
        swapped = pltpu.roll(piece, HEAD_DIM, axis=1)
        for half in range(2):
            hd = 2 * (j % PAIRS) + half
            if hd >= FOX_HEADS:
                continue
            if kind == 2:
                extra = tail_cols([1.0])
            else:
                hi, mid, lo = splits[hd]
                extra = tail_cols([hi, mid, lo, 1.0, 1.0, 1.0] if kind == 0
                                  else [1.0, 1.0, 1.0, -hi, -mid, -lo])
            body = piece if half == 0 else swapped
            fx_ref[0, kind * FOX_HEADS + hd] = jnp.where(first, body, extra).astype(BF16)

    ug = _dot(n, w_ref[:, SEG_GATE:D_IN_PAD])
    gate_ref[0] = _sigmoid(ug).astype(BF16)


def _inproj(stream, g1, w_in_p, mu, fb, fqg, fkg, *, TP, n_seq, tm=INPROJ_TILE):
    first = len(stream) == 2
    B, D = stream[0].shape[0], D_MODEL
    nb = TP // tm
    kern = functools.partial(_inproj_kernel, tm=tm, first=first, n_seq=n_seq)
    out_shape = (
        jax.ShapeDtypeStruct((B, 3 * PAIRS, TP, LANE), F32),
        jax.ShapeDtypeStruct((B, TP, LANE), F32),
        jax.ShapeDtypeStruct((B, 3 * PAIRS, TP, LANE), BF16),
        jax.ShapeDtypeStruct((B, 3 * FOX_HEADS, TP, LANE), BF16),
        jax.ShapeDtypeStruct((B, nb, 1, LANE), F32),
        jax.ShapeDtypeStruct((B, TP, 3 * D_MODEL), BF16),
    )
    in_specs = _stream_specs(first, tm, n_seq) + [
        _const_spec((1, D)),
        _const_spec((D, D_IN_PAD)),
        _const_spec((1, SHIFT_W)),
        _const_spec((1, LANE)),
        _const_spec((1, LANE)),
        _const_spec((1, LANE)),
    ]
    out_specs = (
        pl.BlockSpec((1, 3 * PAIRS, tm, LANE), lambda b, i: (b, 0, i, 0)),
        pl.BlockSpec((1, tm, LANE), lambda b, i: (b, i, 0)),
        pl.BlockSpec((1, 3 * PAIRS, tm, LANE), lambda b, i: (b, 0, i, 0)),
        pl.BlockSpec((1, 3 * FOX_HEADS, tm, LANE), lambda b, i: (b, 0, i, 0)),
        pl.BlockSpec((1, 1, 1, LANE), lambda b, i: (b, i, 0, 0)),
        pl.BlockSpec((1, tm, 3 * D_MODEL), lambda b, i: (b, i, 0)),
    )
    return pl.pallas_call(
        kern, out_shape=out_shape, grid=(B, nb), in_specs=in_specs, out_specs=out_specs,
        scratch_shapes=[pltpu.VMEM((1, SHIFT_W), F32), pltpu.VMEM((1, LANE), F32)],
        compiler_params=pltpu.CompilerParams(
            dimension_semantics=("arbitrary", "arbitrary"), vmem_limit_bytes=VMEM_LIMIT),
        name="inproj",
    )(*stream, g1, w_in_p, mu, fb, fqg, fkg)


def _bdot(a, b):
    return lax.dot_general(a, b, (((2,), (1,)), ((0,), (0,))), preferred_element_type=F32)


def _bdot_nt(a, b):
    return lax.dot_general(a, b, (((2,), (2,)), ((0,), (0,))), preferred_element_type=F32)


def _bdot_tn(a, b):
    return lax.dot_general(a, b, (((1,), (1,)), ((0,), (0,))), preferred_element_type=F32)


def _head_sum(x, first):
    s0 = jnp.sum(jnp.where(first, x, 0.0), axis=-1, keepdims=True)
    s1 = jnp.sum(jnp.where(first, 0.0, x), axis=-1, keepdims=True)
    return jnp.where(first, s0, s1)


def _rwkv_chunk_kernel(r_ref, k_ref, v_ref, wdad_ref, wup_ref, w0_ref, aup_ref, a0_ref,
                       kk_ref, ka_ref, rk_ref,
                       g_ref, hc_ref, r2_ref, y0_ref, bonus_ref, *, rows):
    nc = rows // CHUNK
    r = r_ref[0, 0]
    k = k_ref[0, 0]
    v = v_ref[0, 0]
    wd = wdad_ref[0][:, 0:DECAY_RANK]
    ad = wdad_ref[0][:, DECAY_RANK:DECAY_RANK + ICLR_RANK]
    first = lax.broadcasted_iota(jnp.int32, (rows, LANE), 1) < HEAD_DIM

    pre = w0_ref[0] + _dot3(jnp.tanh(wd), wup_ref[0])
    lw = -jnp.exp(_log_sigmoid(pre) - 0.5)
    iclr = _sigmoid(a0_ref[0] + _dot3(ad, aup_ref[0]))
    kk = k * kk_ref[0]
    kk = kk / jnp.maximum(jnp.sqrt(_head_sum(kk * kk, first)), 1e-12)
    k2 = k * (1.0 + (iclr - 1.0) * ka_ref[0])
    b = kk * iclr
    bonus_ref[0, 0] = _head_sum(r * k2 * rk_ref[0], first) * v

    to3 = lambda x: x.reshape(nc, CHUNK, LANE)
    ri = lax.broadcasted_iota(jnp.int32, (nc, CHUNK, CHUNK), 1)
    ci = lax.broadcasted_iota(jnp.int32, (nc, CHUNK, CHUNK), 2)
    low_incl = ri >= ci
    low_strict = ri > ci
    first3 = lax.broadcasted_iota(jnp.int32, (nc, CHUNK, LANE), 2) < HEAD_DIM

    lw3 = to3(lw)
    tri = low_incl.astype(BF16)
    lw_hi = lw3.astype(BF16)
    cum = _bdot(tri, lw_hi) + _bdot(tri, (lw3 - lw_hi.astype(F32)).astype(BF16))
    cum_end = cum[:, CHUNK - 1:CHUNK, :]
    e_neg = jnp.exp(-cum)
    at = to3(-kk) * jnp.exp(cum - lw3)
    rt = to3(r) * jnp.exp(cum)
    bt = (to3(b) * e_neg).astype(BF16)
    kt = (to3(k2) * e_neg).astype(BF16)
    e_rem = jnp.exp(cum_end - cum)
    bq = (to3(b) * e_rem).astype(BF16)
    kq = (to3(k2) * e_rem).astype(BF16)
    vv = to3(v).astype(BF16)

    heads = range(2)
    sels = [first3, jnp.logical_not(first3)]
    lhs = [jnp.concatenate([jnp.where(sels[hd], at, 0.0), jnp.where(sels[hd], rt, 0.0)],
                           axis=1).astype(BF16) for hd in heads]
    mb = [_bdot_nt(lhs[hd], bt) for hd in heads]
    mk = [_bdot_nt(lhs[hd], kt) for hd in heads]
    m_ak = [jnp.where(low_strict, mk[hd][:, :CHUNK], 0.0).astype(BF16) for hd in heads]
    m_rb = [jnp.where(low_incl, mb[hd][:, CHUNK:], 0.0).astype(BF16) for hd in heads]
    m_rk = [jnp.where(low_incl, mk[hd][:, CHUNK:], 0.0).astype(BF16) for hd in heads]
    p = [jnp.where(low_strict, mb[hd][:, :CHUNK], 0.0) for hd in heads]
    xs = [jnp.concatenate([at, _bdot(m_ak[hd], vv)], axis=-1) for hd in heads]
    for j in range(6):
        p16 = [p[hd].astype(BF16) for hd in heads]
        xs = [xs[hd] + _bdot(p16[hd], xs[hd].astype(BF16)) for hd in heads]
        if j < 5:
            p = [_bdot(p16[hd], p16[hd]) for hd in heads]
    ru = [_bdot(m_rb[hd], xs[hd].astype(BF16)) for hd in heads]
    r2s = [rt + ru[hd][..., :LANE] for hd in heads]
    y0s = [ru[hd][..., LANE:] + _bdot(m_rk[hd], vv) for hd in heads]

    first3w = jnp.concatenate([first3, first3], axis=-1)
    x = jnp.where(first3w, xs[0], xs[1])
    r2_ref[0, 0] = jnp.where(first3, r2s[0], r2s[1]).reshape(rows, LANE)
    y0_ref[0, 0] = jnp.where(first3, y0s[0], y0s[1]).reshape(rows, LANE)

    pg = _bdot_tn(bq, x.astype(BF16))
    ph = pg[..., LANE:] + _bdot_tn(kq, vv)
    rr = lax.broadcasted_iota(jnp.int32, (nc, LANE, LANE), 1)
    cc = lax.broadcasted_iota(jnp.int32, (nc, LANE, LANE), 2)
    same_head = (rr < HEAD_DIM) == (cc < HEAD_DIM)
    g = jnp.where(same_head, pg[..., :LANE], 0.0) + jnp.where(rr == cc, jnp.exp(cum_end), 0.0)
    g_ref[0, 0] = g.reshape(nc * LANE, LANE)
    hc_ref[0, 0] = jnp.where(same_head, ph, 0.0).reshape(nc * LANE, LANE)


def _rwkv_chunks(rkv, wdad, wup, w0, aup, a0, k_k, k_a, r_k):
    B, _, TP, _ = rkv.shape
    rows = next(r for r in (1536, 768, 512, 256) if TP % r == 0)
    nb = TP // rows
    kern = functools.partial(_rwkv_chunk_kernel, rows=rows)
    slot_spec = lambda off: pl.BlockSpec((1, 1, rows, LANE), lambda b, p, i: (b, p + off, i, 0))
    par_mat = pl.BlockSpec((1, DECAY_RANK, LANE), lambda b, p, i: (p, 0, 0))
    par_vec = pl.BlockSpec((1, 1, LANE), lambda b, p, i: (p, 0, 0))
    row_out = jax.ShapeDtypeStruct((B, PAIRS, TP, LANE), F32)
    mat_out = jax.ShapeDtypeStruct((B, PAIRS, 2 * TP, LANE), F32)
    row_spec = pl.BlockSpec((1, 1, rows, LANE), lambda b, p, i: (b, p, i, 0))
    mat_spec = pl.BlockSpec((1, 1, 2 * rows, LANE), lambda b, p, i: (b, p, i, 0))
    return pl.pallas_call(
        kern, out_shape=(mat_out, mat_out, row_out, row_out, row_out), grid=(B, PAIRS, nb),
        in_specs=[slot_spec(0), slot_spec(PAIRS), slot_spec(2 * PAIRS),
                  pl.BlockSpec((1, rows, LANE), lambda b, p, i: (b, i, 0)),
                  par_mat, par_vec, par_mat, par_vec, par_vec, par_vec, par_vec],
        out_specs=(mat_spec, mat_spec, row_spec, row_spec, row_spec),
        compiler_params=pltpu.CompilerParams(
            dimension_semantics=("arbitrary",) * 3, vmem_limit_bytes=VMEM_LIMIT),
        name="rwkv_chunks",
    )(rkv, rkv, rkv, wdad, wup, w0, aup, a0, k_k, k_a, r_k)


def _rwkv_scan_kernel(g_ref, hc_ref, r2_ref, y0_ref, bonus_ref, lng_ref, lnb_ref, y_ref,
                      state, *, rows):
    i = pl.program_id(1)

    @pl.when(i == 0)
    def _():
        state[...] = jnp.zeros_like(state)

    first = lax.broadcasted_iota(jnp.int32, (CHUNK, LANE), 1) < HEAD_DIM
    hs = [state[p] for p in range(PAIRS)]
    for c in range(rows // CHUNK):
        sl = slice(c * CHUNK, (c + 1) * CHUNK)
        sm = slice(c * LANE, (c + 1) * LANE)
        for p in range(PAIRS):
            y = _dot3(r2_ref[0, p, sl, :], hs[p]) + y0_ref[0, p, sl, :]
            hs[p] = _dot3(g_ref[0, p, sm, :], hs[p]) + hc_ref[0, p, sm, :]
            yc = y - _head_sum(y, first) * (1.0 / HEAD_DIM)
            var = _head_sum(yc * yc, first) * (1.0 / HEAD_DIM)
            out = yc * lax.rsqrt(var + RWKV_LN_EPS) * lng_ref[p] + lnb_ref[p]
            y_ref[0, p, sl, :] = (out + bonus_ref[0, p, sl, :]).astype(BF16)
    for p in range(PAIRS):
        state[p] = hs[p]


def _rwkv_scan(gm, hc, r2, y0, bonus, ln_g, ln_b, *, rows=512):
    B, P, TP, _ = r2.shape
    nb = TP // rows
    kern = functools.partial(_rwkv_scan_kernel, rows=rows)
    blk = pl.BlockSpec((1, P, rows, LANE), lambda b, i: (b, 0, i, 0))
    mat = pl.BlockSpec((1, P, 2 * rows, LANE), lambda b, i: (b, 0, i, 0))
    par = pl.BlockSpec((P, 1, LANE), lambda b, i: (0, 0, 0))
    return pl.pallas_call(
        kern, out_shape=jax.ShapeDtypeStruct((B, P, TP, LANE), BF16), grid=(B, nb),
        in_specs=[mat, mat, blk, blk, blk, par, par], out_specs=blk,
        scratch_shapes=[pltpu.VMEM((P, LANE, LANE), F32)],
        compiler_params=pltpu.CompilerParams(
            dimension_semantics=("arbitrary", "arbitrary"), vmem_limit_bytes=VMEM_LIMIT),
        name="rwkv_scan",
    )(gm, hc, r2, y0, bonus, ln_g, ln_b)


def _split_pair(qp, nh):
    lane = lax.broadcasted_iota(jnp.int32, qp.shape, 1)
    first = lane < HEAD_DIM
    zero = jnp.zeros_like(qp)
    qs = [jnp.where(first, qp, zero), jnp.where(first, zero, qp)]
    return qs[:nh], first


def _sb_kernel(q_ref, k_ref, v_ref, o_ref, *, t, nh):
    qi = pl.program_id(2)
    qs, first = _split_pair(q_ref[0, 0], nh)
    qpos = qi * t + lax.broadcasted_iota(jnp.int32, (t, 1), 0)
    kloc = lax.broadcasted_iota(jnp.int32, (1, t), 1)
    upper = (lax.broadcasted_iota(jnp.int32, (t, t), 0)
             > lax.broadcasted_iota(jnp.int32, (t, t), 1)).astype(BF16)

    def sweep(blocks, accs, cs):
        units = []
        for kb, mask in blocks:
            start = pl.multiple_of(kb * t, t)
            kblk = k_ref[0, 0, pl.ds(start, t), :]
            vblk = v_ref[0, 0, pl.ds(start, t), :]
            m = None if mask is None else mask(start + kloc)
            units += [(h, kblk, vblk, m) for h in range(nh)]
        zs = [_dot_nt(qs[h], kblk) for h, kblk, _, _ in units]
        sps = [jnp.maximum(z, 0.0) + jnp.log2(1.0 + jnp.exp2(-jnp.abs(z))) for z in zs]
        spms = [sp if u[3] is None else jnp.where(u[3], sp, 0.0) for sp, u in zip(sps, units)]
        laters = [_dot(spm.astype(BF16), upper) for spm in spms]
        accs, cs = list(accs), list(cs)
        weights = []
        for (h, _, _, m), z, sp, spm, later in zip(units, zs, sps, spms, laters):
            a = jnp.exp2(z - sp - later - cs[h])
            weights.append(a if m is None else jnp.where(m, a, 0.0))
            cs[h] = cs[h] + jnp.sum(spm, axis=-1, keepdims=True)
        for (h, _, vblk, _), a in zip(units, weights):
            accs[h] = accs[h] + _dot(a.astype(BF16), vblk)
        return accs, cs

    def block(kb, accs, cs, mask):
        return sweep([(kb, mask)], accs, cs)

    def live(cs):
        low = cs[0]
        for c in cs[1:]:
            low = jnp.minimum(low, c)
        return (jnp.min(low) < SB_CUTOFF_LOG2).astype(jnp.int32)

    zero_acc = [jnp.zeros((t, LANE), F32) for _ in range(nh)]
    zero_c = [jnp.zeros((t, 1), F32) for _ in range(nh)]
    accs, cs = sweep([(qi, lambda kpos: (kpos >= PAD) & (kpos < qpos)),
                      (jnp.maximum(qi - 1, 0), lambda kpos: (kpos >= PAD) & (qi >= 1))],
                     zero_acc, zero_c)

    def cond(carry):
        kb, alive = carry[0], carry[1]
        return (kb >= 1) & (alive > 0)

    def body(carry):
        kb = carry[0]
        accs, cs = block(kb, list(carry[2:2 + nh]), list(carry[2 + nh:]), None)
        return (kb - 1, live(cs), *accs, *cs)

    carry = lax.while_loop(cond, body, (qi - 2, live(cs), *accs, *cs))

    def front(carry):
        accs, cs = block(0, list(carry[2:2 + nh]), list(carry[2 + nh:]), lambda kpos: kpos >= PAD)
        return (carry[0], carry[1], *accs, *cs)

    carry = lax.cond((carry[0] == 0) & (carry[1] > 0), front, lambda c: c, carry)
    accs = carry[2:2 + nh]
    out = jnp.where(first, accs[0], accs[1] if nh == 2 else 0.0)
    o_ref[0, 0] = out.astype(BF16)


def _sb_attention(sb, *, slot0, nslots, nh, t=256):
    B, _, TP, _ = sb.shape
    kern = functools.partial(_sb_kernel, t=t, nh=nh)
    blk = lambda off: pl.BlockSpec((1, 1, t, LANE), lambda b, p, i: (b, slot0 + off + p, i, 0))
    full = lambda off: pl.BlockSpec((1, 1, TP, LANE), lambda b, p, i: (b, slot0 + off + p, 0, 0))
    return pl.pallas_call(
        kern, out_shape=jax.ShapeDtypeStruct((B, nslots, TP, LANE), BF16),
        grid=(B, nslots, TP // t),
        in_specs=[blk(0), full(PAIRS), full(2 * PAIRS)],
        out_specs=pl.BlockSpec((1, 1, t, LANE), lambda b, p, i: (b, p, i, 0)),
        compiler_params=pltpu.CompilerParams(
            dimension_semantics=("arbitrary",) * 3, vmem_limit_bytes=VMEM_LIMIT),
        name="sb_attention",
    )(sb, sb, sb)


def _fox_kernel(shift_ref, fe_ref, q_ref, k_ref, v_ref, o_ref, lo_ref, *, t):
    b = pl.program_id(0)
    h = pl.program_id(1)
    qi = pl.program_id(2)
    q = q_ref[0, 0]
    shift = shift_ref[0]
    thr = FOX_SKIP_LOG2 + 2.0 * shift
    f_q = fe_ref[b, h, jnp.maximum(qi - 1, 0)]
    lo = lax.while_loop(lambda n: (n < qi) & (fe_ref[b, h, n] - f_q > thr), lambda n: n + 1,
                        jnp.where(qi == 0, 0, lo_ref[0]))
    lo_ref[0] = lo
    qpos = qi * t + lax.broadcasted_iota(jnp.int32, (t, 1), 0)
    kloc = lax.broadcasted_iota(jnp.int32, (1, t), 1)

    def block(kb, carry, masked, online):
        acc, m = carry
        start = pl.multiple_of(kb * t, t)
        kblk = k_ref[0, 0, pl.ds(start, t), :]
        vblk = v_ref[0, 0, pl.ds(start, t), :]
        s = _dot_nt(q, kblk)
        if masked:
            kpos = start + kloc
            s = jnp.where((kpos <= qpos) & (kpos >= PAD), s, NEG_INF)
        if online:
            m_new = jnp.maximum(m, jnp.max(s, axis=-1, keepdims=True))
            acc = jnp.exp2(m - m_new) * acc
            m = m_new
        pr = jnp.exp2(s - m)
        return acc + _dot(pr.astype(BF16), vblk), m

    def sweep(online):
        def run(carry):
            carry = lax.cond(lo == 0, lambda c: block(0, c, True, online), lambda c: c, carry)
            start = jnp.maximum(lo, 1)
            n = jnp.maximum(qi - start, 0)

            def group(i, c):
                acc, m = c
                kvs = []
                for j in range(FOX_GROUP):
                    st = pl.multiple_of((start + FOX_GROUP * i + j) * t, t)
                    kvs.append((k_ref[0, 0, pl.ds(st, t), :], v_ref[0, 0, pl.ds(st, t), :]))
                ss = [_dot_nt(q, kblk) for kblk, _ in kvs]
                prs = [jnp.exp2(s - m).astype(BF16) for s in ss]
                for pr, (_, vblk) in zip(prs, kvs):
                    acc = acc + _dot(pr, vblk)
                return acc, m

            grouped = 0 if online else (n // FOX_GROUP) * FOX_GROUP
            if not online:
                carry = lax.fori_loop(0, n // FOX_GROUP, group, carry)
            carry = lax.fori_loop(start + grouped, qi,
                                  lambda kb, c: block(kb, c, False, online), carry)
            return lax.cond(qi > 0, lambda c: block(qi, c, True, online), lambda c: c, carry)
        return run

    acc0 = jnp.zeros((t, LANE), F32)
    fixed = shift <= FOX_FIXED_SHIFT_MAX
    m0 = jnp.where(fixed, jnp.full((t, 1), shift, F32), jnp.full((t, 1), NEG_INF, F32))
    acc, _ = lax.cond(fixed, sweep(False), sweep(True), (acc0, m0))
    lane = lax.broadcasted_iota(jnp.int32, (t, LANE), 1)
    denom = jnp.sum(jnp.where(lane == HEAD_DIM, acc, 0.0), axis=-1, keepdims=True)
    o_ref[0, 0] = (acc / jnp.where(denom > 0.0, denom, 1.0)).astype(BF16)


def _fox_attention(shift, fe, fx, *, t=FOX_TILE):
    B, _, TP, _ = fx.shape
    H = FOX_HEADS
    kern = functools.partial(_fox_kernel, t=t)
    return pl.pallas_call(
        kern, out_shape=jax.ShapeDtypeStruct((B, H, TP, LANE), BF16),
        grid=(B, H, TP // t),
        in_specs=[pl.BlockSpec(memory_space=pltpu.SMEM),
                  pl.BlockSpec(memory_space=pltpu.SMEM),
                  pl.BlockSpec((1, 1, t, LANE), lambda b, h, i: (b, h, i, 0)),
                  pl.BlockSpec((1, 1, TP, LANE), lambda b, h, i: (b, h + H, 0, 0)),
                  pl.BlockSpec((1, 1, TP, LANE), lambda b, h, i: (b, h + 2 * H, 0, 0))],
        out_specs=pl.BlockSpec((1, 1, t, LANE), lambda b, h, i: (b, h, i, 0)),
        scratch_shapes=[pltpu.SMEM((1,), jnp.int32)],
        compiler_params=pltpu.CompilerParams(
            dimension_semantics=("arbitrary",) * 3, vmem_limit_bytes=VMEM_LIMIT),
        name="fox_attention",
    )(shift, fe, fx, fx, fx)


def _merge_kernel(*refs, tm, n_valid, first, n_seq):
    n_stream = 2 if first else 1
    (ya_ref, yb2_ref, yb1_ref, yc_ref, gate_ref,
     wpa_ref, wpb_ref, wpc_ref, wout_ref, o_ref) = refs[n_stream:]
    i = pl.program_id(1)

    def proj(slots, w_ref):
        return _dot(jnp.concatenate(slots, axis=-1), w_ref[...])

    pa = proj([ya_ref[0, s] for s in range(PAIRS)], wpa_ref)
    pb = proj([yb2_ref[0, 0], yb2_ref[0, 1], yb1_ref[0, 0]], wpb_ref)
    pc = proj([yc_ref[0, s] for s in range(FOX_HEADS)], wpc_ref)
    merged = gate_ref[0, :, 0:D_MODEL].astype(F32) * pa
    merged = merged + gate_ref[0, :, D_MODEL:2 * D_MODEL].astype(F32) * pb
    merged = merged + gate_ref[0, :, 2 * D_MODEL:3 * D_MODEL].astype(F32) * pc
    out = _stream_tile(i, refs[:n_stream], n_seq) + _dot(merged.astype(BF16), wout_ref[...])
    t = i * tm + lax.broadcasted_iota(jnp.int32, (tm, 1), 0)
    o_ref[0] = jnp.where((t >= PAD) & (t < PAD + n_valid), out, 0.0)


def _merge(stream, ya, yb2, yb1, yc, gates, wpa, wpb, wpc, wout, *, n_valid, n_seq, tm=FRAME_TILE):
    first = len(stream) == 2
    B, _, TP, _ = ya.shape
    D = D_MODEL
    kern = functools.partial(_merge_kernel, tm=tm, n_valid=n_valid, first=first, n_seq=n_seq)
    pair_blk = lambda ns: pl.BlockSpec((1, ns, tm, LANE), lambda b, i: (b, 0, i, 0))
    return pl.pallas_call(
        kern, out_shape=jax.ShapeDtypeStruct((B, TP, D), F32), grid=(B, TP // tm),
        in_specs=_stream_specs(first, tm, n_seq) + [
                  pair_blk(PAIRS),
                  pair_blk(2), pair_blk(1), pair_blk(FOX_HEADS),
                  pl.BlockSpec((1, tm, 3 * D), lambda b, i: (b, i, 0)),
                  _const_spec((PAIRS * LANE, D)),
                  _const_spec((PAIRS * LANE, D)),
                  _const_spec((FOX_HEADS * LANE, D)),
                  _const_spec((D, D))],
        out_specs=pl.BlockSpec((1, tm, D), lambda b, i: (b, i, 0)),
        compiler_params=pltpu.CompilerParams(
            dimension_semantics=("arbitrary", "arbitrary"), vmem_limit_bytes=VMEM_LIMIT),
        name="merge",
    )(*stream, ya, yb2, yb1, yc, gates, wpa, wpb, wpc, wout)


def _moe_kernel(h_ref, g_ref, wr_ref, br_ref, wg_ref, wu_ref, wd_ref, o_ref, he_ref, ys_ref,
                earlier_ref, *, tm, n_valid):
    i = pl.program_id(1)
    x = h_ref[0]
    ms = jnp.mean(x * x, axis=-1, keepdims=True)
    n = x * lax.rsqrt(ms + NORM_EPS) * g_ref[...]

    lg = _dot3(n, wr_ref[...]) + br_ref[...]
    lane = lax.broadcasted_iota(jnp.int32, (tm, LANE), 1)
    big = jnp.int32(LANE)
    is_group = lane < N_GROUPS
    gl = jnp.where(is_group, lg, -jnp.inf)
    gmax = jnp.max(gl, axis=-1, keepdims=True)
    g_idx = jnp.min(jnp.where(is_group & (gl == gmax), lane, big), axis=-1, keepdims=True)
    p_group = 1.0 / jnp.sum(jnp.where(is_group, jnp.exp(gl - gmax), 0.0), axis=-1, keepdims=True)
    lo = N_GROUPS + g_idx * EXPERTS_PER_GROUP
    in_grp = (lane >= lo) & (lane < lo + EXPERTS_PER_GROUP)
    el = jnp.where(in_grp, lg, -jnp.inf)
    top1 = jnp.max(el, axis=-1, keepdims=True)
    i1 = jnp.min(jnp.where(in_grp & (el == top1), lane, big), axis=-1, keepdims=True)
    rest = in_grp & (lane != i1)
    el2 = jnp.where(rest, lg, -jnp.inf)
    top2 = jnp.max(el2, axis=-1, keepdims=True)
    i2 = jnp.min(jnp.where(rest & (el2 == top2), lane, big), axis=-1, keepdims=True)
    e2 = jnp.exp(top2 - top1)
    p1 = 1.0 / (1.0 + e2)
    p2 = e2 / (1.0 + e2)
    comb = p_group * (jnp.where(lane == i1, p1, 0.0) + jnp.where(lane == i2, p2, 0.0))

    n16 = n.astype(BF16)

    onehot = (lane == g_idx).astype(BF16)
    @pl.when((pl.program_id(0) == 0) & (i == 0))
    def _():
        earlier_ref[...] = (lax.broadcasted_iota(jnp.int32, (tm, tm), 1)
                            < lax.broadcasted_iota(jnp.int32, (tm, tm), 0)).astype(BF16)

    cnt = _dot(earlier_ref[...], onehot)
    rank = jnp.sum(jnp.where(lane == g_idx, cnt, 0.0), axis=-1, keepdims=True).astype(jnp.int32)
    n_pass = (jnp.max(rank) + MOE_CAP) // MOE_CAP
    comb16 = comb.astype(BF16)
    slot_lane = lax.broadcasted_iota(jnp.int32, (tm, N_GROUPS * MOE_CAP), 1)

    def one_pass(p, y):
        r = rank - p * MOE_CAP
        slot = jnp.where((r >= 0) & (r < MOE_CAP), g_idx * MOE_CAP + r, -1)
        place = (slot_lane == slot).astype(BF16)
        xs = _dot_tn(place, n16).astype(BF16)
        cw = _dot_tn(place, comb16)
        for g in range(N_GROUPS):
            rows = slice(g * MOE_CAP, (g + 1) * MOE_CAP)
            for e in range(EXPERTS_PER_GROUP):
                ee = g * EXPERTS_PER_GROUP + e
                gate = _dot(xs[rows], wg_ref[ee])
                up = _dot(xs[rows], wu_ref[ee])
                c_e = cw[rows, N_GROUPS + ee:N_GROUPS + ee + 1]
                he_ref[:, e * D_EXPERT:(e + 1) * D_EXPERT] = (
                    gate * _sigmoid(gate) * up * c_e).astype(BF16)
            gw = EXPERTS_PER_GROUP * D_EXPERT
            ys_ref[rows, :] = _dot(he_ref[...], wd_ref[g * gw:(g + 1) * gw, :]).astype(BF16)
        return y + _dot(place, ys_ref[...])

    out = x + lax.fori_loop(0, n_pass, one_pass, jnp.zeros((tm, D_MODEL), F32))
    t = i * tm + lax.broadcasted_iota(jnp.int32, (tm, 1), 0)
    o_ref[0] = jnp.where((t >= PAD) & (t < PAD + n_valid), out, 0.0)


def _moe(h, g2, wr, br, wg, wu, wd, *, n_valid):
    B, TP, D = h.shape
    tm = next(r for r in (MOE_TILE, FRAME_TILE) if TP % r == 0)
    kern = functools.partial(_moe_kernel, tm=tm, n_valid=n_valid)
    EW = N_EXPERTS * D_EXPERT
    return pl.pallas_call(
        kern, out_shape=jax.ShapeDtypeStruct((B, TP, D), F32), grid=(B, TP // tm),
        in_specs=[pl.BlockSpec((1, tm, D), lambda b, i: (b, i, 0)),
                  _const_spec((1, D)), _const_spec((D, LANE)), _const_spec((1, LANE)),
                  _const_spec((N_EXPERTS, D, D_EXPERT)), _const_spec((N_EXPERTS, D, D_EXPERT)),
                  _const_spec((EW, D))],
        out_specs=pl.BlockSpec((1, tm, D), lambda b, i: (b, i, 0)),
        scratch_shapes=[pltpu.VMEM((MOE_CAP, EXPERTS_PER_GROUP * D_EXPERT), BF16),
                        pltpu.VMEM((N_GROUPS * MOE_CAP, D), BF16),
                        pltpu.VMEM((tm, tm), BF16)],
        compiler_params=pltpu.CompilerParams(
            dimension_semantics=("arbitrary", "arbitrary"), vmem_limit_bytes=VMEM_LIMIT),
        name="moe",
    )(h, g2, wr, br, wg, wu, wd)


def _transpose_kernel(x_ref, o_ref):
    o_ref[...] = x_ref[...].T.astype(BF16)


def _transpose_to_bf16(a, *, tn=LANE):
    N, K = a.shape
    assert N % tn == 0
    return pl.pallas_call(
        _transpose_kernel, out_shape=jax.ShapeDtypeStruct((K, N), BF16), grid=(N // tn,),
        in_specs=[pl.BlockSpec((tn, K), lambda i: (i, 0))],
        out_specs=pl.BlockSpec((K, tn), lambda i: (0, i)),
        compiler_params=pltpu.CompilerParams(dimension_semantics=("arbitrary",)),
        name="transpose_cast",
    )(a)


def _pack_w_in(w_in):
    wt = jnp.swapaxes(w_in, 0, 1)
    rest = wt[SHIFT_W:]
    rows = [wt[:SHIFT_W]]
    off = 0
    for heads in (SB_HEADS,) * 3 + (FOX_HEADS,) * 3:
        rows.append(jnp.pad(rest[off:off + heads * HEAD_DIM],
                            ((0, PAIRS * LANE - heads * HEAD_DIM), (0, 0))))
        off += heads * HEAD_DIM
    rows.append(jnp.pad(rest[off:off + FOX_HEADS], ((0, LANE - FOX_HEADS), (0, 0))))
    rows.append(rest[off + FOX_HEADS:])
    return _transpose_to_bf16(jnp.concatenate(rows, axis=0))


def _per_pair_cols(w):
    return w.reshape(w.shape[0], PAIRS, LANE).transpose(1, 0, 2)


def kernel(x, meta_tokens, norm1_g, w_in, rwkv_mu, rwkv_w_up, rwkv_w0, rwkv_a_up, rwkv_a0,
           rwkv_k_k, rwkv_k_a, rwkv_r_k, rwkv_ln_g, rwkv_ln_b, fox_f_b, fox_q_g, fox_k_g,
           w_p_rwkv, w_p_sb, w_p_fox, w_out, norm2_g, moe_wg, moe_bg, moe_we, moe_be,
           moe_w_gate, moe_w_up, moe_w_down):
    B, S, D = x.shape
    depth = w_in.shape[0]
    L = N_META + S
    TP = -(-(PAD + L) // ROW_TILE) * ROW_TILE
    assert S % FRAME_TILE == 0
    n_seq = S // FRAME_TILE
    meta = jnp.broadcast_to(meta_tokens[None].astype(x.dtype), (B, N_META, D))
    stream = (x, jnp.concatenate([jnp.zeros((B, PAD, D), x.dtype), meta], axis=1))
    H = RWKV_HEADS
    EW = N_EXPERTS * D_EXPERT
    for l in range(depth):
        fb = jnp.zeros((1, LANE), F32).at[0, :FOX_HEADS].set(fox_f_b[l])
        rkv, wdad, sb, fx, fend, gates = _inproj(
            stream, norm1_g[l][None], _pack_w_in(w_in[l]), rwkv_mu[l][None], fb,
            jnp.tile(fox_q_g[l], 2)[None], jnp.tile(fox_k_g[l], 2)[None], TP=TP, n_seq=n_seq)
        hv = lambda p: p.reshape(PAIRS, 1, LANE)
        gm, hc, r2, y0, bonus = _rwkv_chunks(
            rkv, wdad, _per_pair_cols(rwkv_w_up[l]), hv(rwkv_w0[l]),
            _per_pair_cols(rwkv_a_up[l]), hv(rwkv_a0[l]),
            hv(rwkv_k_k[l]), hv(rwkv_k_a[l]), hv(rwkv_r_k[l]))
        ya = _rwkv_scan(gm, hc, r2, y0, bonus, hv(rwkv_ln_g[l]), hv(rwkv_ln_b[l]))
        yb2 = _sb_attention(sb, slot0=0, nslots=2, nh=2)
        yb1 = _sb_attention(sb, slot0=2, nslots=1, nh=1)
        shift = (8.0 * LOG2E * jnp.max(jnp.abs(fox_q_g[l])) * jnp.max(jnp.abs(fox_k_g[l]))).reshape(1)
        step = FOX_TILE // INPROJ_TILE
        fe = jnp.transpose(fend[:, step - 1::step, 0, :FOX_HEADS], (0, 2, 1))
        yc = _fox_attention(shift, fe, fx)
        pc = jnp.pad(w_p_fox[l].reshape(FOX_HEADS, HEAD_DIM, D),
                     ((0, 0), (0, LANE - HEAD_DIM), (0, 0))).reshape(FOX_HEADS * LANE, D).astype(BF16)
        pp = lambda w, nh: jnp.pad(w, ((0, PAIRS * LANE - nh * HEAD_DIM), (0, 0))).astype(BF16)
        h = _merge(stream, ya, yb2, yb1, yc, gates, pp(w_p_rwkv[l], H), pp(w_p_sb[l], SB_HEADS),
                   pc, w_out[l].astype(BF16), n_valid=L, n_seq=n_seq)
        wr = jnp.zeros((D, LANE), F32).at[:, :N_GROUPS].set(moe_wg[l])
        wr = wr.at[:, N_GROUPS:N_GROUPS + N_EXPERTS].set(moe_we[l])
        br = jnp.zeros((1, LANE), F32).at[0, :N_GROUPS].set(moe_bg[l])
        br = br.at[0, N_GROUPS:N_GROUPS + N_EXPERTS].set(moe_be[l])
        wg = moe_w_gate[l].astype(BF16)
        wu = moe_w_up[l].astype(BF16)
        wd = moe_w_down[l].reshape(EW, D).astype(BF16)
        h = _moe(h, norm2_g[l][None], wr, br, wg, wu, wd, n_valid=L)
        stream = (h,)
    return h[:, PAD + N_META:PAD + L]
```

```python
import functools
import math

import jax
import jax.numpy as jnp
from jax import lax
from jax.experimental import pallas as pl
from jax.experimental.pallas import tpu as pltpu

D_MODEL = 1024
HEAD_DIM = 64
N_META = 16
FRAME_TILE = 256
MOE_TILE = 768
MOE_CAP = 256
PAD = FRAME_TILE - N_META
RWKV_HEADS = 6
SB_HEADS = 5
FOX_HEADS = 5
RWKV_W = RWKV_HEADS * HEAD_DIM
SB_W = SB_HEADS * HEAD_DIM
FOX_W = FOX_HEADS * HEAD_DIM
DECAY_RANK = 64
ICLR_RANK = 64
SHIFT_W = 3 * RWKV_W + DECAY_RANK + ICLR_RANK
N_GROUPS = 4
EXPERTS_PER_GROUP = 4
N_EXPERTS = 16
D_EXPERT = 256
NORM_EPS = 1e-6
RWKV_LN_EPS = 64e-5
NEG_INF = -1e30
ATT_SCALE = 1.0 / math.sqrt(HEAD_DIM)

LANE = 128
ROW_TILE = 512
PAIRS = 3
QKV_W = 3 * PAIRS * LANE
SEG_SB = SHIFT_W
SEG_FOX = SEG_SB + QKV_W
SEG_GATE = SEG_FOX + QKV_W + LANE
D_IN_PAD = SEG_GATE + 3 * D_MODEL
SB_CUTOFF_LOG2 = 160.0
LOG2E = 1.4426950408889634
FOX_FIXED_SHIFT_MAX = 50.0
FOX_SKIP_LOG2 = 150.0
FOX_TILE = 512
FOX_GROUP = 4
INPROJ_TILE = FRAME_TILE

CHUNK = 64
VMEM_LIMIT = 56 * 1024 * 1024

F32 = jnp.float32
BF16 = jnp.bfloat16


def _log_sigmoid(x):
    return jnp.minimum(x, 0.0) - jnp.log1p(jnp.exp(-jnp.abs(x)))


def _sigmoid(x):
    return 1.0 / (1.0 + jnp.exp(-x))


def _dot(a, b, **kw):
    return jnp.dot(a, b, preferred_element_type=F32, **kw)


def _dot_nt(a, b):
    return lax.dot_general(a, b, (((1,), (1,)), ((), ())), preferred_element_type=F32)


def _dot_tn(a, b):
    return lax.dot_general(a, b, (((0,), (0,)), ((), ())), preferred_element_type=F32)


def _dot3(a, b):
    ah = a.astype(BF16)
    al = (a - ah.astype(F32)).astype(BF16)
    bh = b.astype(BF16)
    bl = (b - bh.astype(F32)).astype(BF16)
    return _dot(ah, bh) + (_dot(ah, bl) + _dot(al, bh))


def _stream_specs(first, tm, n_seq):
    if not first:
        return [pl.BlockSpec((1, tm, D_MODEL), lambda b, i: (b, i, 0))]
    return [pl.BlockSpec((1, tm, D_MODEL), lambda b, i: (b, jnp.clip(i - 1, 0, n_seq - 1), 0)),
            pl.BlockSpec((1, tm, D_MODEL), lambda b, i: (b, 0, 0))]


def _stream_tile(i, refs, n_seq):
    if len(refs) == 1:
        return refs[0][0]
    x_ref, head_ref = refs
    return jnp.where(i == 0, head_ref[0], jnp.where(i <= n_seq, x_ref[0], 0.0))


def _const_spec(shape):
    n = len(shape)
    return pl.BlockSpec(shape, lambda *_: (0,) * n, pipeline_mode=pl.Buffered(1))


def _inproj_kernel(*refs, tm, first, n_seq):
    n_stream = 2 if first else 1
    (g_ref, w_ref, mu_ref, fb_ref, fqg_ref, fkg_ref,
     rkv_ref, wdad_ref, sb_ref, fx_ref, fend_ref, gate_ref, carry_u, carry_f) = refs[n_stream:]
    i = pl.program_id(1)

    @pl.when(i == 0)
    def _():
        carry_u[...] = jnp.zeros_like(carry_u)
        carry_f[...] = jnp.zeros_like(carry_f)

    x = _stream_tile(i, refs[:n_stream], n_seq)
    ms = jnp.mean(x * x, axis=-1, keepdims=True)
    n = (x * lax.rsqrt(ms + NORM_EPS) * g_ref[...]).astype(BF16)
    row = lax.broadcasted_iota(jnp.int32, (tm, 1), 0)

    us = _dot(n, w_ref[:, 0:SHIFT_W])
    prev = pltpu.roll(us, 1, axis=0)
    prev = jnp.where(row == 0, carry_u[...], prev)
    carry_u[...] = us[tm - 1:tm, :]
    ush = us + (prev - us) * mu_ref[...]
    for j in range(3 * PAIRS):
        rkv_ref[0, j] = ush[:, j * LANE:(j + 1) * LANE]
    wdad_ref[0] = ush[:, 3 * RWKV_W:SHIFT_W]

    usb = _dot(n, w_ref[:, SEG_SB:SEG_SB + QKV_W])
    for j in range(3 * PAIRS):
        piece = usb[:, j * LANE:(j + 1) * LANE]
        if j < PAIRS:
            piece = piece * (ATT_SCALE * LOG2E)
        sb_ref[0, j] = piece.astype(BF16)

    uf = _dot(n, w_ref[:, SEG_FOX:SEG_FOX + QKV_W + LANE])
    lane = lax.broadcasted_iota(jnp.int32, (tm, LANE), 1)
    first = lane < HEAD_DIM
    t_glob = i * tm + row
    logf = _log_sigmoid(uf[:, QKV_W:QKV_W + LANE] + fb_ref[...])
    logf = jnp.where((lane < FOX_HEADS) & (t_glob >= PAD), logf, 0.0)
    tri = (lax.broadcasted_iota(jnp.int32, (tm, tm), 0)
           >= lax.broadcasted_iota(jnp.int32, (tm, tm), 1)).astype(BF16)
    f_hi = logf.astype(BF16)
    f_mid = (logf - f_hi.astype(F32)).astype(BF16)
    f_lo = (logf - f_hi.astype(F32) - f_mid.astype(F32)).astype(BF16)
    cum = _dot(tri, f_hi) + _dot(tri, f_mid) + _dot(tri, f_lo) + carry_f[...]
    carry_f[...] = cum[tm - 1:tm, :]
    fend_ref[0, 0] = cum[tm - 1:tm, :] * LOG2E

    def split3(hd):
        f2 = jnp.sum(jnp.where(lane == hd, cum, 0.0), axis=-1, keepdims=True) * LOG2E
        hi = f2.astype(BF16).astype(F32)
        mid = (f2 - hi).astype(BF16).astype(F32)
        return hi, mid, f2 - hi - mid

    def tail_cols(vals):
        out = jnp.zeros((tm, LANE), F32)
        for o, val in enumerate(vals):
            out = jnp.where(lane == HEAD_DIM + o, val, out)
        return out

    splits = [split3(hd) for hd in range(FOX_HEADS)]
    for j in range(3 * PAIRS):
        piece = uf[:, j * LANE:(j + 1) * LANE]
        kind = j // PAIRS
        if kind < 2:
            gain = fqg_ref[...] if kind == 0 else fkg_ref[...]
            sq = piece * piece
            ms0 = jnp.sum(jnp.where(first, sq, 0.0), axis=-1, keepdims=True) * (1.0 / HEAD_DIM)
            ms1 = jnp.sum(jnp.where(first, 0.0, sq), axis=-1, keepdims=True) * (1.0 / HEAD_DIM)
            inv = jnp.where(first, lax.rsqrt(ms0 + NORM_EPS), lax.rsqrt(ms1 + NORM_EPS))
            piece = piece * inv * gain
            if kind == 0:
                piece = piece * (ATT_SCALE * LOG2E)
        swapped = pltpu.roll(piece, HEAD_DIM, axis=1)
        for half in range(2):
            hd = 2 * (j % PAIRS) + half
            if hd >= FOX_HEADS:
                continue
            if kind == 2:
                extra = tail_cols([1.0])
            else:
                hi, mid, lo = splits[hd]
                extra = tail_cols([hi, mid, lo, 1.0, 1.0, 1.0] if kind == 0
                                  else [1.0, 1.0, 1.0, -hi, -mid, -lo])
            body = piece if half == 0 else swapped
            fx_ref[0, kind * FOX_HEADS + hd] = jnp.where(first, body, extra).astype(BF16)

    ug = _dot(n, w_ref[:, SEG_GATE:D_IN_PAD])
    gate_ref[0] = _sigmoid(ug).astype(BF16)


def _inproj(stream, g1, w_in_p, mu, fb, fqg, fkg, *, TP, n_seq, tm=INPROJ_TILE):
    first = len(stream) == 2
    B, D = stream[0].shape[0], D_MODEL
    nb = TP // tm
    kern = functools.partial(_inproj_kernel, tm=tm, first=first, n_seq=n_seq)
    out_shape = (
        jax.ShapeDtypeStruct((B, 3 * PAIRS, TP, LANE), F32),
        jax.ShapeDtypeStruct((B, TP, LANE), F32),
        jax.ShapeDtypeStruct((B, 3 * PAIRS, TP, LANE), BF16),
        jax.ShapeDtypeStruct((B, 3 * FOX_HEADS, TP, LANE), BF16),
        jax.ShapeDtypeStruct((B, nb, 1, LANE), F32),
        jax.ShapeDtypeStruct((B, TP, 3 * D_MODEL), BF16),
    )
    in_specs = _stream_specs(first, tm, n_seq) + [
        _const_spec((1, D)),
        _const_spec((D, D_IN_PAD)),
        _const_spec((1, SHIFT_W)),
        _const_spec((1, LANE)),
        _const_spec((1, LANE)),
        _const_spec((1, LANE)),
    ]
    out_specs = (
        pl.BlockSpec((1, 3 * PAIRS, tm, LANE), lambda b, i: (b, 0, i, 0)),
        pl.BlockSpec((1, tm, LANE), lambda b, i: (b, i, 0)),
        pl.BlockSpec((1, 3 * PAIRS, tm, LANE), lambda b, i: (b, 0, i, 0)),
        pl.BlockSpec((1, 3 * FOX_HEADS, tm, LANE), lambda b, i: (b, 0, i, 0)),
        pl.BlockSpec((1, 1, 1, LANE), lambda b, i: (b, i, 0, 0)),
        pl.BlockSpec((1, tm, 3 * D_MODEL), lambda b, i: (b, i, 0)),
    )
    return pl.pallas_call(
        kern, out_shape=out_shape, grid=(B, nb), in_specs=in_specs, out_specs=out_specs,
        scratch_shapes=[pltpu.VMEM((1, SHIFT_W), F32), pltpu.VMEM((1, LANE), F32)],
        compiler_params=pltpu.CompilerParams(
            dimension_semantics=("arbitrary", "arbitrary"), vmem_limit_bytes=VMEM_LIMIT),
        name="inproj",
    )(*stream, g1, w_in_p, mu, fb, fqg, fkg)


def _bdot(a, b):
    return lax.dot_general(a, b, (((2,), (1,)), ((0,), (0,))), preferred_element_type=F32)


def _bdot_nt(a, b):
    return lax.dot_general(a, b, (((2,), (2,)), ((0,), (0,))), preferred_element_type=F32)


def _bdot_tn(a, b):
    return lax.dot_general(a, b, (((1,), (1,)), ((0,), (0,))), preferred_element_type=F32)


def _head_sum(x, first):
    s0 = jnp.sum(jnp.where(first, x, 0.0), axis=-1, keepdims=True)
    s1 = jnp.sum(jnp.where(first, 0.0, x), axis=-1, keepdims=True)
    return jnp.where(first, s0, s1)


def _rwkv_chunk_kernel(r_ref, k_ref, v_ref, wdad_ref, wup_ref, w0_ref, aup_ref, a0_ref,
                       kk_ref, ka_ref, rk_ref,
                       g_ref, hc_ref, r2_ref, y0_ref, bonus_ref, *, rows):
    nc = rows // CHUNK
    r = r_ref[0, 0]
    k = k_ref[0, 0]
    v = v_ref[0, 0]
    wd = wdad_ref[0][:, 0:DECAY_RANK]
    ad = wdad_ref[0][:, DECAY_RANK:DECAY_RANK + ICLR_RANK]
    first = lax.broadcasted_iota(jnp.int32, (rows, LANE), 1) < HEAD_DIM

    pre = w0_ref[0] + _dot3(jnp.tanh(wd), wup_ref[0])
    lw = -jnp.exp(_log_sigmoid(pre) - 0.5)
    iclr = _sigmoid(a0_ref[0] + _dot3(ad, aup_ref[0]))
    kk = k * kk_ref[0]
    kk = kk / jnp.maximum(jnp.sqrt(_head_sum(kk * kk, first)), 1e-12)
    k2 = k * (1.0 + (iclr - 1.0) * ka_ref[0])
    b = kk * iclr
    bonus_ref[0, 0] = _head_sum(r * k2 * rk_ref[0], first) * v

    to3 = lambda x: x.reshape(nc, CHUNK, LANE)
    ri = lax.broadcasted_iota(jnp.int32, (nc, CHUNK, CHUNK), 1)
    ci = lax.broadcasted_iota(jnp.int32, (nc, CHUNK, CHUNK), 2)
    low_incl = ri >= ci
    low_strict = ri > ci
    first3 = lax.broadcasted_iota(jnp.int32, (nc, CHUNK, LANE), 2) < HEAD_DIM

    lw3 = to3(lw)
    tri = low_incl.astype(BF16)
    lw_hi = lw3.astype(BF16)
    cum = _bdot(tri, lw_hi) + _bdot(tri, (lw3 - lw_hi.astype(F32)).astype(BF16))
    cum_end = cum[:, CHUNK - 1:CHUNK, :]
    e_neg = jnp.exp(-cum)
    at = to3(-kk) * jnp.exp(cum - lw3)
    rt = to3(r) * jnp.exp(cum)
    bt = (to3(b) * e_neg).astype(BF16)
    kt = (to3(k2) * e_neg).astype(BF16)
    e_rem = jnp.exp(cum_end - cum)
    bq = (to3(b) * e_rem).astype(BF16)
    kq = (to3(k2) * e_rem).astype(BF16)
    vv = to3(v).astype(BF16)

    heads = range(2)
    sels = [first3, jnp.logical_not(first3)]
    lhs = [jnp.concatenate([jnp.where(sels[hd], at, 0.0), jnp.where(sels[hd], rt, 0.0)],
                           axis=1).astype(BF16) for hd in heads]
    mb = [_bdot_nt(lhs[hd], bt) for hd in heads]
    mk = [_bdot_nt(lhs[hd], kt) for hd in heads]
    m_ak = [jnp.where(low_strict, mk[hd][:, :CHUNK], 0.0).astype(BF16) for hd in heads]
    m_rb = [jnp.where(low_incl, mb[hd][:, CHUNK:], 0.0).astype(BF16) for hd in heads]
    m_rk = [jnp.where(low_incl, mk[hd][:, CHUNK:], 0.0).astype(BF16) for hd in heads]
    p = [jnp.where(low_strict, mb[hd][:, :CHUNK], 0.0) for hd in heads]
    xs = [jnp.concatenate([at, _bdot(m_ak[hd], vv)], axis=-1) for hd in heads]
    for j in range(6):
        p16 = [p[hd].astype(BF16) for hd in heads]
        xs = [xs[hd] + _bdot(p16[hd], xs[hd].astype(BF16)) for hd in heads]
        if j < 5:
            p = [_bdot(p16[hd], p16[hd]) for hd in heads]
    ru = [_bdot(m_rb[hd], xs[hd].astype(BF16)) for hd in heads]
    r2s = [rt + ru[hd][..., :LANE] for hd in heads]
    y0s = [ru[hd][..., LANE:] + _bdot(m_rk[hd], vv) for hd in heads]

    first3w = jnp.concatenate([first3, first3], axis=-1)
    x = jnp.where(first3w, xs[0], xs[1])
    r2_ref[0, 0] = jnp.where(first3, r2s[0], r2s[1]).reshape(rows, LANE)
    y0_ref[0, 0] = jnp.where(first3, y0s[0], y0s[1]).reshape(rows, LANE)

    pg = _bdot_tn(bq, x.astype(BF16))
    ph = pg[..., LANE:] + _bdot_tn(kq, vv)
    rr = lax.broadcasted_iota(jnp.int32, (nc, LANE, LANE), 1)
    cc = lax.broadcasted_iota(jnp.int32, (nc, LANE, LANE), 2)
    same_head = (rr < HEAD_DIM) == (cc < HEAD_DIM)
    g = jnp.where(same_head, pg[..., :LANE], 0.0) + jnp.where(rr == cc, jnp.exp(cum_end), 0.0)
    g_ref[0, 0] = g.reshape(nc * LANE, LANE)
    hc_ref[0, 0] = jnp.where(same_head, ph, 0.0).reshape(nc * LANE, LANE)


def _rwkv_chunks(rkv, wdad, wup, w0, aup, a0, k_k, k_a, r_k):
    B, _, TP, _ = rkv.shape
    rows = next(r for r in (1536, 768, 512, 256) if TP % r == 0)
    nb = TP // rows
    kern = functools.partial(_rwkv_chunk_kernel, rows=rows)
    slot_spec = lambda off: pl.BlockSpec((1, 1, rows, LANE), lambda b, p, i: (b, p + off, i, 0))
    par_mat = pl.BlockSpec((1, DECAY_RANK, LANE), lambda b, p, i: (p, 0, 0))
    par_vec = pl.BlockSpec((1, 1, LANE), lambda b, p, i: (p, 0, 0))
    row_out = jax.ShapeDtypeStruct((B, PAIRS, TP, LANE), F32)
    mat_out = jax.ShapeDtypeStruct((B, PAIRS, 2 * TP, LANE), F32)
    row_spec = pl.BlockSpec((1, 1, rows, LANE), lambda b, p, i: (b, p, i, 0))
    mat_spec = pl.BlockSpec((1, 1, 2 * rows, LANE), lambda b, p, i: (b, p, i, 0))
    return pl.pallas_call(
        kern, out_shape=(mat_out, mat_out, row_out, row_out, row_out), grid=(B, PAIRS, nb),
        in_specs=[slot_spec(0), slot_spec(PAIRS), slot_spec(2 * PAIRS),
                  pl.BlockSpec((1, rows, LANE), lambda b, p, i: (b, i, 0)),
                  par_mat, par_vec, par_mat, par_vec, par_vec, par_vec, par_vec],
        out_specs=(mat_spec, mat_spec, row_spec, row_spec, row_spec),
        compiler_params=pltpu.CompilerParams(
            dimension_semantics=("arbitrary",) * 3, vmem_limit_bytes=VMEM_LIMIT),
        name="rwkv_chunks",
    )(rkv, rkv, rkv, wdad, wup, w0, aup, a0, k_k, k_a, r_k)


def _rwkv_scan_kernel(g_ref, hc_ref, r2_ref, y0_ref, bonus_ref, lng_ref, lnb_ref, y_ref,
                      state, *, rows):
    i = pl.program_id(1)

    @pl.when(i == 0)
    def _():
        state[...] = jnp.zeros_like(state)

    first = lax.broadcasted_iota(jnp.int32, (CHUNK, LANE), 1) < HEAD_DIM
    hs = [state[p] for p in range(PAIRS)]
    for c in range(rows // CHUNK):
        sl = slice(c * CHUNK, (c + 1) * CHUNK)
        sm = slice(c * LANE, (c + 1) * LANE)
        for p in range(PAIRS):
            y = _dot3(r2_ref[0, p, sl, :], hs[p]) + y0_ref[0, p, sl, :]
            hs[p] = _dot3(g_ref[0, p, sm, :], hs[p]) + hc_ref[0, p, sm, :]
            yc = y - _head_sum(y, first) * (1.0 / HEAD_DIM)
            var = _head_sum(yc * yc, first) * (1.0 / HEAD_DIM)
            out = yc * lax.rsqrt(var + RWKV_LN_EPS) * lng_ref[p] + lnb_ref[p]
            y_ref[0, p, sl, :] = (out + bonus_ref[0, p, sl, :]).astype(BF16)
    for p in range(PAIRS):
        state[p] = hs[p]


def _rwkv_scan(gm, hc, r2, y0, bonus, ln_g, ln_b, *, rows=512):
    B, P, TP, _ = r2.shape
    nb = TP // rows
    kern = functools.partial(_rwkv_scan_kernel, rows=rows)
    blk = pl.BlockSpec((1, P, rows, LANE), lambda b, i: (b, 0, i, 0))
    mat = pl.BlockSpec((1, P, 2 * rows, LANE), lambda b, i: (b, 0, i, 0))
    par = pl.BlockSpec((P, 1, LANE), lambda b, i: (0, 0, 0))
    return pl.pallas_call(
        kern, out_shape=jax.ShapeDtypeStruct((B, P, TP, LANE), BF16), grid=(B, nb),
        in_specs=[mat, mat, blk, blk, blk, par, par], out_specs=blk,
        scratch_shapes=[pltpu.VMEM((P, LANE, LANE), F32)],
        compiler_params=pltpu.CompilerParams(
            dimension_semantics=("arbitrary", "arbitrary"), vmem_limit_bytes=VMEM_LIMIT),
        name="rwkv_scan",
    )(gm, hc, r2, y0, bonus, ln_g, ln_b)


def _split_pair(qp, nh):
    lane = lax.broadcasted_iota(jnp.int32, qp.shape, 1)
    first = lane < HEAD_DIM
    zero = jnp.zeros_like(qp)
    qs = [jnp.where(first, qp, zero), jnp.where(first, zero, qp)]
    return qs[:nh], first


def _sb_kernel(q_ref, k_ref, v_ref, o_ref, *, t, nh):
    qi = pl.program_id(2)
    qs, first = _split_pair(q_ref[0, 0], nh)
    qpos = qi * t + lax.broadcasted_iota(jnp.int32, (t, 1), 0)
    kloc = lax.broadcasted_iota(jnp.int32, (1, t), 1)
    upper = (lax.broadcasted_iota(jnp.int32, (t, t), 0)
             > lax.broadcasted_iota(jnp.int32, (t, t), 1)).astype(BF16)

    def sweep(blocks, accs, cs):
        units = []
        for kb, mask in blocks:
            start = pl.multiple_of(kb * t, t)
            kblk = k_ref[0, 0, pl.ds(start, t), :]
            vblk = v_ref[0, 0, pl.ds(start, t), :]
            m = None if mask is None else mask(start + kloc)
            units += [(h, kblk, vblk, m) for h in range(nh)]
        zs = [_dot_nt(qs[h], kblk) for h, kblk, _, _ in units]
        sps = [jnp.maximum(z, 0.0) + jnp.log2(1.0 + jnp.exp2(-jnp.abs(z))) for z in zs]
        spms = [sp if u[3] is None else jnp.where(u[3], sp, 0.0) for sp, u in zip(sps, units)]
        laters = [_dot(spm.astype(BF16), upper) for spm in spms]
        accs, cs = list(accs), list(cs)
        weights = []
        for (h, _, _, m), z, sp, spm, later in zip(units, zs, sps, spms, laters):
            a = jnp.exp2(z - sp - later - cs[h])
            weights.append(a if m is None else jnp.where(m, a, 0.0))
            cs[h] = cs[h] + jnp.sum(spm, axis=-1, keepdims=True)
        for (h, _, vblk, _), a in zip(units, weights):
            accs[h] = accs[h] + _dot(a.astype(BF16), vblk)
        return accs, cs

    def block(kb, accs, cs, mask):
        return sweep([(kb, mask)], accs, cs)

    def live(cs):
        low = cs[0]
        for c in cs[1:]:
            low = jnp.minimum(low, c)
        return (jnp.min(low) < SB_CUTOFF_LOG2).astype(jnp.int32)

    zero_acc = [jnp.zeros((t, LANE), F32) for _ in range(nh)]
    zero_c = [jnp.zeros((t, 1), F32) for _ in range(nh)]
    accs, cs = sweep([(qi, lambda kpos: (kpos >= PAD) & (kpos < qpos)),
                      (jnp.maximum(qi - 1, 0), lambda kpos: (kpos >= PAD) & (qi >= 1))],
                     zero_acc, zero_c)

    def cond(carry):
        kb, alive = carry[0], carry[1]
        return (kb >= 1) & (alive > 0)

    def body(carry):
        kb = carry[0]
        accs, cs = block(kb, list(carry[2:2 + nh]), list(carry[2 + nh:]), None)
        return (kb - 1, live(cs), *accs, *cs)

    carry = lax.while_loop(cond, body, (qi - 2, live(cs), *accs, *cs))

    def front(carry):
        accs, cs = block(0, list(carry[2:2 + nh]), list(carry[2 + nh:]), lambda kpos: kpos >= PAD)
        return (carry[0], carry[1], *accs, *cs)

    carry = lax.cond((carry[0] == 0) & (carry[1] > 0), front, lambda c: c, carry)
    accs = carry[2:2 + nh]
    out = jnp.where(first, accs[0], accs[1] if nh == 2 else 0.0)
    o_ref[0, 0] = out.astype(BF16)


def _sb_attention(sb, *, slot0, nslots, nh, t=256):
    B, _, TP, _ = sb.shape
    kern = functools.partial(_sb_kernel, t=t, nh=nh)
    blk = lambda off: pl.BlockSpec((1, 1, t, LANE), lambda b, p, i: (b, slot0 + off + p, i, 0))
    full = lambda off: pl.BlockSpec((1, 1, TP, LANE), lambda b, p, i: (b, slot0 + off + p, 0, 0))
    return pl.pallas_call(
        kern, out_shape=jax.ShapeDtypeStruct((B, nslots, TP, LANE), BF16),
        grid=(B, nslots, TP // t),
        in_specs=[blk(0), full(PAIRS), full(2 * PAIRS)],
        out_specs=pl.BlockSpec((1, 1, t, LANE), lambda b, p, i: (b, p, i, 0)),
        compiler_params=pltpu.CompilerParams(
            dimension_semantics=("arbitrary",) * 3, vmem_limit_bytes=VMEM_LIMIT),
        name="sb_attention",
    )(sb, sb, sb)


def _fox_kernel(shift_ref, fe_ref, q_ref, k_ref, v_ref, o_ref, lo_ref, *, t):
    b = pl.program_id(0)
    h = pl.program_id(1)
    qi = pl.program_id(2)
    q = q_ref[0, 0]
    shift = shift_ref[0]
    thr = FOX_SKIP_LOG2 + 2.0 * shift
    f_q = fe_ref[b, h, jnp.maximum(qi - 1, 0)]
    lo = lax.while_loop(lambda n: (n < qi) & (fe_ref[b, h, n] - f_q > thr), lambda n: n + 1,
                        jnp.where(qi == 0, 0, lo_ref[0]))
    lo_ref[0] = lo
    qpos = qi * t + lax.broadcasted_iota(jnp.int32, (t, 1), 0)
    kloc = lax.broadcasted_iota(jnp.int32, (1, t), 1)

    def block(kb, carry, masked, online):
        acc, m = carry
        start = pl.multiple_of(kb * t, t)
        kblk = k_ref[0, 0, pl.ds(start, t), :]
        vblk = v_ref[0, 0, pl.ds(start, t), :]
        s = _dot_nt(q, kblk)
        if masked:
            kpos = start + kloc
            s = jnp.where((kpos <= qpos) & (kpos >= PAD), s, NEG_INF)
        if online:
            m_new = jnp.maximum(m, jnp.max(s, axis=-1, keepdims=True))
            acc = jnp.exp2(m - m_new) * acc
            m = m_new
        pr = jnp.exp2(s - m)
        return acc + _dot(pr.astype(BF16), vblk), m

    def sweep(online):
        def run(carry):
            carry = lax.cond(lo == 0, lambda c: block(0, c, True, online), lambda c: c, carry)
            start = jnp.maximum(lo, 1)
            n = jnp.maximum(qi - start, 0)

            def group(i, c):
                acc, m = c
                kvs = []
                for j in range(FOX_GROUP):
                    st = pl.multiple_of((start + FOX_GROUP * i + j) * t, t)
                    kvs.append((k_ref[0, 0, pl.ds(st, t), :], v_ref[0, 0, pl.ds(st, t), :]))
                ss = [_dot_nt(q, kblk) for kblk, _ in kvs]
                prs = [jnp.exp2(s - m).astype(BF16) for s in ss]
                for pr, (_, vblk) in zip(prs, kvs):
                    acc = acc + _dot(pr, vblk)
                return acc, m

            grouped = 0 if online else (n // FOX_GROUP) * FOX_GROUP
            if not online:
                carry = lax.fori_loop(0, n // FOX_GROUP, group, carry)
            carry = lax.fori_loop(start + grouped, qi,
                                  lambda kb, c: block(kb, c, False, online), carry)
            return lax.cond(qi > 0, lambda c: block(qi, c, True, online), lambda c: c, carry)
        return run

    acc0 = jnp.zeros((t, LANE), F32)
    fixed = shift <= FOX_FIXED_SHIFT_MAX
    m0 = jnp.where(fixed, jnp.full((t, 1), shift, F32), jnp.full((t, 1), NEG_INF, F32))
    acc, _ = lax.cond(fixed, sweep(False), sweep(True), (acc0, m0))
    lane = lax.broadcasted_iota(jnp.int32, (t, LANE), 1)
    denom = jnp.sum(jnp.where(lane == HEAD_DIM, acc, 0.0), axis=-1, keepdims=True)
    o_ref[0, 0] = (acc / jnp.where(denom > 0.0, denom, 1.0)).astype(BF16)


def _fox_attention(shift, fe, fx, *, t=FOX_TILE):
    B, _, TP, _ = fx.shape
    H = FOX_HEADS
    kern = functools.partial(_fox_kernel, t=t)
    return pl.pallas_call(
        kern, out_shape=jax.ShapeDtypeStruct((B, H, TP, LANE), BF16),
        grid=(B, H, TP // t),
        in_specs=[pl.BlockSpec(memory_space=pltpu.SMEM),
                  pl.BlockSpec(memory_space=pltpu.SMEM),
                  pl.BlockSpec((1, 1, t, LANE), lambda b, h, i: (b, h, i, 0)),
                  pl.BlockSpec((1, 1, TP, LANE), lambda b, h, i: (b, h + H, 0, 0)),
                  pl.BlockSpec((1, 1, TP, LANE), lambda b, h, i: (b, h + 2 * H, 0, 0))],
        out_specs=pl.BlockSpec((1, 1, t, LANE), lambda b, h, i: (b, h, i, 0)),
        scratch_shapes=[pltpu.SMEM((1,), jnp.int32)],
        compiler_params=pltpu.CompilerParams(
            dimension_semantics=("arbitrary",) * 3, vmem_limit_bytes=VMEM_LIMIT),
        name="fox_attention",
    )(shift, fe, fx, fx, fx)


def _merge_kernel(*refs, tm, n_valid, first, n_seq):
    n_stream = 2 if first else 1
    (ya_ref, yb2_ref, yb1_ref, yc_ref, gate_ref,
     wpa_ref, wpb_ref, wpc_ref, wout_ref, o_ref) = refs[n_stream:]
    i = pl.program_id(1)

    def proj(slots, w_ref):
        return _dot(jnp.concatenate(slots, axis=-1), w_ref[...])

    pa = proj([ya_ref[0, s] for s in range(PAIRS)], wpa_ref)
    pb = proj([yb2_ref[0, 0], yb2_ref[0, 1], yb1_ref[0, 0]], wpb_ref)
    pc = proj([yc_ref[0, s] for s in range(FOX_HEADS)], wpc_ref)
    merged = gate_ref[0, :, 0:D_MODEL].astype(F32) * pa
    merged = merged + gate_ref[0, :, D_MODEL:2 * D_MODEL].astype(F32) * pb
    merged = merged + gate_ref[0, :, 2 * D_MODEL:3 * D_MODEL].astype(F32) * pc
    out = _stream_tile(i, refs[:n_stream], n_seq) + _dot(merged.astype(BF16), wout_ref[...])
    t = i * tm + lax.broadcasted_iota(jnp.int32, (tm, 1), 0)
    o_ref[0] = jnp.where((t >= PAD) & (t < PAD + n_valid), out, 0.0)


def _merge(stream, ya, yb2, yb1, yc, gates, wpa, wpb, wpc, wout, *, n_valid, n_seq, tm=FRAME_TILE):
    first = len(stream) == 2
    B, _, TP, _ = ya.shape
    D = D_MODEL
    kern = functools.partial(_merge_kernel, tm=tm, n_valid=n_valid, first=first, n_seq=n_seq)
    pair_blk = lambda ns: pl.BlockSpec((1, ns, tm, LANE), lambda b, i: (b, 0, i, 0))
    return pl.pallas_call(
        kern, out_shape=jax.ShapeDtypeStruct((B, TP, D), F32), grid=(B, TP // tm),
        in_specs=_stream_specs(first, tm, n_seq) + [
                  pair_blk(PAIRS),
                  pair_blk(2), pair_blk(1), pair_blk(FOX_HEADS),
                  pl.BlockSpec((1, tm, 3 * D), lambda b, i: (b, i, 0)),
                  _const_spec((PAIRS * LANE, D)),
                  _const_spec((PAIRS * LANE, D)),
                  _const_spec((FOX_HEADS * LANE, D)),
                  _const_spec((D, D))],
        out_specs=pl.BlockSpec((1, tm, D), lambda b, i: (b, i, 0)),
        compiler_params=pltpu.CompilerParams(
            dimension_semantics=("arbitrary", "arbitrary"), vmem_limit_bytes=VMEM_LIMIT),
        name="merge",
    )(*stream, ya, yb2, yb1, yc, gates, wpa, wpb, wpc, wout)


def _moe_kernel(h_ref, g_ref, wr_ref, br_ref, wg_ref, wu_ref, wd_ref, o_ref, he_ref, ys_ref,
                *, tm, n_valid):
    i = pl.program_id(1)
    x = h_ref[0]
    ms = jnp.mean(x * x, axis=-1, keepdims=True)
    n = x * lax.rsqrt(ms + NORM_EPS) * g_ref[...]

    lg = _dot3(n, wr_ref[...]) + br_ref[...]
    lane = lax.broadcasted_iota(jnp.int32, (tm, LANE), 1)
    big = jnp.int32(LANE)
    is_group = lane < N_GROUPS
    gl = jnp.where(is_group, lg, -jnp.inf)
    gmax = jnp.max(gl, axis=-1, keepdims=True)
    g_idx = jnp.min(jnp.where(is_group & (gl == gmax), lane, big), axis=-1, keepdims=True)
    p_group = 1.0 / jnp.sum(jnp.where(is_group, jnp.exp(gl - gmax), 0.0), axis=-1, keepdims=True)
    lo = N_GROUPS + g_idx * EXPERTS_PER_GROUP
    in_grp = (lane >= lo) & (lane < lo + EXPERTS_PER_GROUP)
    el = jnp.where(in_grp, lg, -jnp.inf)
    top1 = jnp.max(el, axis=-1, keepdims=True)
    i1 = jnp.min(jnp.where(in_grp & (el == top1), lane, big), axis=-1, keepdims=True)
    rest = in_grp & (lane != i1)
    el2 = jnp.where(rest, lg, -jnp.inf)
    top2 = jnp.max(el2, axis=-1, keepdims=True)
    i2 = jnp.min(jnp.where(rest & (el2 == top2), lane, big), axis=-1, keepdims=True)
    e2 = jnp.exp(top2 - top1)
    p1 = 1.0 / (1.0 + e2)
    p2 = e2 / (1.0 + e2)
    comb = p_group * (jnp.where(lane == i1, p1, 0.0) + jnp.where(lane == i2, p2, 0.0))

    n16 = n.astype(BF16)

    onehot = (lane == g_idx).astype(BF16)
    earlier = (lax.broadcasted_iota(jnp.int32, (tm, tm), 1)
               < lax.broadcasted_iota(jnp.int32, (tm, tm), 0)).astype(BF16)
    cnt = _dot(earlier, onehot)
    rank = jnp.sum(jnp.where(lane == g_idx, cnt, 0.0), axis=-1, keepdims=True).astype(jnp.int32)
    n_pass = (jnp.max(rank) + MOE_CAP) // MOE_CAP
    comb16 = comb.astype(BF16)
    slot_lane = lax.broadcasted_iota(jnp.int32, (tm, N_GROUPS * MOE_CAP), 1)

    def one_pass(p, y):
        r = rank - p * MOE_CAP
        slot = jnp.where((r >= 0) & (r < MOE_CAP), g_idx * MOE_CAP + r, -1)
        place = (slot_lane == slot).astype(BF16)
        xs = _dot_tn(place, n16).astype(BF16)
        cw = _dot_tn(place, comb16)
        for g in range(N_GROUPS):
            rows = slice(g * MOE_CAP, (g + 1) * MOE_CAP)
            for e in range(EXPERTS_PER_GROUP):
                ee = g * EXPERTS_PER_GROUP + e
                gate = _dot(xs[rows], wg_ref[ee])
                up = _dot(xs[rows], wu_ref[ee])
                c_e = cw[rows, N_GROUPS + ee:N_GROUPS + ee + 1]
                he_ref[:, e * D_EXPERT:(e + 1) * D_EXPERT] = (
                    gate * _sigmoid(gate) * up * c_e).astype(BF16)
            gw = EXPERTS_PER_GROUP * D_EXPERT
            ys_ref[rows, :] = _dot(he_ref[...], wd_ref[g * gw:(g + 1) * gw, :]).astype(BF16)
        return y + _dot(place, ys_ref[...])

    out = x + lax.fori_loop(0, n_pass, one_pass, jnp.zeros((tm, D_MODEL), F32))
    t = i * tm + lax.broadcasted_iota(jnp.int32, (tm, 1), 0)
    o_ref[0] = jnp.where((t >= PAD) & (t < PAD + n_valid), out, 0.0)


def _moe(h, g2, wr, br, wg, wu, wd, *, n_valid):
    B, TP, D = h.shape
    tm = next(r for r in (MOE_TILE, FRAME_TILE) if TP % r == 0)
    kern = functools.partial(_moe_kernel, tm=tm, n_valid=n_valid)
    EW = N_EXPERTS * D_EXPERT
    return pl.pallas_call(
        kern, out_shape=jax.ShapeDtypeStruct((B, TP, D), F32), grid=(B, TP // tm),
        in_specs=[pl.BlockSpec((1, tm, D), lambda b, i: (b, i, 0)),
                  _const_spec((1, D)), _const_spec((D, LANE)), _const_spec((1, LANE)),
                  _const_spec((N_EXPERTS, D, D_EXPERT)), _const_spec((N_EXPERTS, D, D_EXPERT)),
                  _const_spec((EW, D))],
        out_specs=pl.BlockSpec((1, tm, D), lambda b, i: (b, i, 0)),
        scratch_shapes=[pltpu.VMEM((MOE_CAP, EXPERTS_PER_GROUP * D_EXPERT), BF16),
                        pltpu.VMEM((N_GROUPS * MOE_CAP, D), BF16)],
        compiler_params=pltpu.CompilerParams(
            dimension_semantics=("arbitrary", "arbitrary"), vmem_limit_bytes=VMEM_LIMIT),
        name="moe",
    )(h, g2, wr, br, wg, wu, wd)


def _transpose_kernel(x_ref, o_ref):
    o_ref[...] = x_ref[...].T.astype(BF16)


def _transpose_to_bf16(a, *, tn=LANE):
    N, K = a.shape
    assert N % tn == 0
    return pl.pallas_call(
        _transpose_kernel, out_shape=jax.ShapeDtypeStruct((K, N), BF16), grid=(N // tn,),
        in_specs=[pl.BlockSpec((tn, K), lambda i: (i, 0))],
        out_specs=pl.BlockSpec((K, tn), lambda i: (0, i)),
        compiler_params=pltpu.CompilerParams(dimension_semantics=("arbitrary",)),
        name="transpose_cast",
    )(a)


def _pack_w_in(w_in):
    wt = jnp.swapaxes(w_in, 0, 1)
    rest = wt[SHIFT_W:]
    rows = [wt[:SHIFT_W]]
    off = 0
    for heads in (SB_HEADS,) * 3 + (FOX_HEADS,) * 3:
        rows.append(jnp.pad(rest[off:off + heads * HEAD_DIM],
                            ((0, PAIRS * LANE - heads * HEAD_DIM), (0, 0))))
        off += heads * HEAD_DIM
    rows.append(jnp.pad(rest[off:off + FOX_HEADS], ((0, LANE - FOX_HEADS), (0, 0))))
    rows.append(rest[off + FOX_HEADS:])
    return _transpose_to_bf16(jnp.concatenate(rows, axis=0))


def _per_pair_cols(w):
    return w.reshape(w.shape[0], PAIRS, LANE).transpose(1, 0, 2)


def kernel(x, meta_tokens, norm1_g, w_in, rwkv_mu, rwkv_w_up, rwkv_w0, rwkv_a_up, rwkv_a0,
           rwkv_k_k, rwkv_k_a, rwkv_r_k, rwkv_ln_g, rwkv_ln_b, fox_f_b, fox_q_g, fox_k_g,
           w_p_rwkv, w_p_sb, w_p_fox, w_out, norm2_g, moe_wg, moe_bg, moe_we, moe_be,
           moe_w_gate, moe_w_up, moe_w_down):
    B, S, D = x.shape
    depth = w_in.shape[0]
    L = N_META + S
    TP = -(-(PAD + L) // ROW_TILE) * ROW_TILE
    assert S % FRAME_TILE == 0
    n_seq = S // FRAME_TILE
    meta = jnp.broadcast_to(meta_tokens[None].astype(x.dtype), (B, N_META, D))
    stream = (x, jnp.concatenate([jnp.zeros((B, PAD, D), x.dtype), meta], axis=1))
    H = RWKV_HEADS
    EW = N_EXPERTS * D_EXPERT
    for l in range(depth):
        fb = jnp.zeros((1, LANE), F32).at[0, :FOX_HEADS].set(fox_f_b[l])
        rkv, wdad, sb, fx, fend, gates = _inproj(
            stream, norm1_g[l][None], _pack_w_in(w_in[l]), rwkv_mu[l][None], fb,
            jnp.tile(fox_q_g[l], 2)[None], jnp.tile(fox_k_g[l], 2)[None], TP=TP, n_seq=n_seq)
        hv = lambda p: p.reshape(PAIRS, 1, LANE)
        gm, hc, r2, y0, bonus = _rwkv_chunks(
            rkv, wdad, _per_pair_cols(rwkv_w_up[l]), hv(rwkv_w0[l]),
            _per_pair_cols(rwkv_a_up[l]), hv(rwkv_a0[l]),
            hv(rwkv_k_k[l]), hv(rwkv_k_a[l]), hv(rwkv_r_k[l]))
        ya = _rwkv_scan(gm, hc, r2, y0, bonus, hv(rwkv_ln_g[l]), hv(rwkv_ln_b[l]))
        yb2 = _sb_attention(sb, slot0=0, nslots=2, nh=2)
        yb1 = _sb_attention(sb, slot0=2, nslots=1, nh=1)
        shift = (8.0 * LOG2E * jnp.max(jnp.abs(fox_q_g[l])) * jnp.max(jnp.abs(fox_k_g[l]))).reshape(1)
        step = FOX_TILE // INPROJ_TILE
        fe = jnp.transpose(fend[:, step - 1::step, 0, :FOX_HEADS], (0, 2, 1))
        yc = _fox_attention(shift, fe, fx)
        pc = jnp.pad(w_p_fox[l].reshape(FOX_HEADS, HEAD_DIM, D),
                     ((0, 0), (0, LANE - HEAD_DIM), (0, 0))).reshape(FOX_HEADS * LANE, D).astype(BF16)
        pp = lambda w, nh: jnp.pad(w, ((0, PAIRS * LANE - nh * HEAD_DIM), (0, 0))).astype(BF16)
        h = _merge(stream, ya, yb2, yb1, yc, gates, pp(w_p_rwkv[l], H), pp(w_p_sb[l], SB_HEADS),
                   pc, w_out[l].astype(BF16), n_valid=L, n_seq=n_seq)
        wr = jnp.zeros((D, LANE), F32).at[:, :N_GROUPS].set(moe_wg[l])
        wr = wr.at[:, N_GROUPS:N_GROUPS + N_EXPERTS].set(moe_we[l])
        br = jnp.zeros((1, LANE), F32).at[0, :N_GROUPS].set(moe_bg[l])
        br = br.at[0, N_GROUPS:N_GROUPS + N_EXPERTS].set(moe_be[l])
        wg = moe_w_gate[l].astype(BF16)
        wu = moe_w_up[l].astype(BF16)
        wd = moe_w_down[l].reshape(EW, D).astype(BF16)
        h = _moe(h, norm2_g[l][None], wr, br, wg, wu, wd, n_valid=L)
        stream = (h,)
    return h[:, PAD + N_META:PAD + L]
```

```python
import functools
import math

import jax
import jax.numpy as jnp
from jax import lax
from jax.experimental import pallas as pl
from jax.experimental.pallas import tpu as pltpu

D_MODEL = 1024
HEAD_DIM = 64
N_META = 16
FRAME_TILE = 256
MOE_TILE = 512
MOE_CAP = 192
PAD = FRAME_TILE - N_META
RWKV_HEADS = 6
SB_HEADS = 5
FOX_HEADS = 5
RWKV_W = RWKV_HEADS * HEAD_DIM
SB_W = SB_HEADS * HEAD_DIM
FOX_W = FOX_HEADS * HEAD_DIM
DECAY_RANK = 64
ICLR_RANK = 64
SHIFT_W = 3 * RWKV_W + DECAY_RANK + ICLR_RANK
N_GROUPS = 4
EXPERTS_PER_GROUP = 4
N_EXPERTS = 16
D_EXPERT = 256
NORM_EPS = 1e-6
RWKV_LN_EPS = 64e-5
NEG_INF = -1e30
ATT_SCALE = 1.0 / math.sqrt(HEAD_DIM)

LANE = 128
ROW_TILE = 512
PAIRS = 3
QKV_W = 3 * PAIRS * LANE
SEG_SB = SHIFT_W
SEG_FOX = SEG_SB + QKV_W
SEG_GATE = SEG_FOX + QKV_W + LANE
D_IN_PAD = SEG_GATE + 3 * D_MODEL
SB_CUTOFF_LOG2 = 160.0
LOG2E = 1.4426950408889634
FOX_FIXED_SHIFT_MAX = 50.0
FOX_SKIP_LOG2 = 150.0
FOX_TILE = 512
FOX_GROUP = 4
INPROJ_TILE = FRAME_TILE

CHUNK = 64
VMEM_LIMIT = 56 * 1024 * 1024

F32 = jnp.float32
BF16 = jnp.bfloat16


def _log_sigmoid(x):
    return jnp.minimum(x, 0.0) - jnp.log1p(jnp.exp(-jnp.abs(x)))


def _sigmoid(x):
    return 1.0 / (1.0 + jnp.exp(-x))


def _dot(a, b, **kw):
    return jnp.dot(a, b, preferred_element_type=F32, **kw)


def _dot_nt(a, b):
    return lax.dot_general(a, b, (((1,), (1,)), ((), ())), preferred_element_type=F32)


def _dot_tn(a, b):
    return lax.dot_general(a, b, (((0,), (0,)), ((), ())), preferred_element_type=F32)


def _dot3(a, b):
    ah = a.astype(BF16)
    al = (a - ah.astype(F32)).astype(BF16)
    bh = b.astype(BF16)
    bl = (b - bh.astype(F32)).astype(BF16)
    return _dot(ah, bh) + (_dot(ah, bl) + _dot(al, bh))


def _stream_specs(first, tm, n_seq):
    if not first:
        return [pl.BlockSpec((1, tm, D_MODEL), lambda b, i: (b, i, 0))]
    return [pl.BlockSpec((1, tm, D_MODEL), lambda b, i: (b, jnp.clip(i - 1, 0, n_seq - 1), 0)),
            pl.BlockSpec((1, tm, D_MODEL), lambda b, i: (b, 0, 0))]


def _stream_tile(i, refs, n_seq):
    if len(refs) == 1:
        return refs[0][0]
    x_ref, head_ref = refs
    return jnp.where(i == 0, head_ref[0], jnp.where(i <= n_seq, x_ref[0], 0.0))


def _const_spec(shape):
    n = len(shape)
    return pl.BlockSpec(shape, lambda *_: (0,) * n, pipeline_mode=pl.Buffered(1))


def _inproj_kernel(*refs, tm, first, n_seq):
    n_stream = 2 if first else 1
    (g_ref, w_ref, mu_ref, fb_ref, fqg_ref, fkg_ref,
     rkv_ref, wdad_ref, sb_ref, fx_ref, fend_ref, gate_ref, carry_u, carry_f) = refs[n_stream:]
    i = pl.program_id(1)

    @pl.when(i == 0)
    def _():
        carry_u[...] = jnp.zeros_like(carry_u)
        carry_f[...] = jnp.zeros_like(carry_f)

    x = _stream_tile(i, refs[:n_stream], n_seq)
    ms = jnp.mean(x * x, axis=-1, keepdims=True)
    n = (x * lax.rsqrt(ms + NORM_EPS) * g_ref[...]).astype(BF16)
    row = lax.broadcasted_iota(jnp.int32, (tm, 1), 0)

    us = _dot(n, w_ref[:, 0:SHIFT_W])
    prev = pltpu.roll(us, 1, axis=0)
    prev = jnp.where(row == 0, carry_u[...], prev)
    carry_u[...] = us[tm - 1:tm, :]
    ush = us + (prev - us) * mu_ref[...]
    for j in range(3 * PAIRS):
        rkv_ref[0, j] = ush[:, j * LANE:(j + 1) * LANE]
    wdad_ref[0] = ush[:, 3 * RWKV_W:SHIFT_W]

    usb = _dot(n, w_ref[:, SEG_SB:SEG_SB + QKV_W])
    for j in range(3 * PAIRS):
        piece = usb[:, j * LANE:(j + 1) * LANE]
        if j < PAIRS:
            piece = piece * (ATT_SCALE * LOG2E)
        sb_ref[0, j] = piece.astype(BF16)

    uf = _dot(n, w_ref[:, SEG_FOX:SEG_FOX + QKV_W + LANE])
    lane = lax.broadcasted_iota(jnp.int32, (tm, LANE), 1)
    first = lane < HEAD_DIM
    t_glob = i * tm + row
    logf = _log_sigmoid(uf[:, QKV_W:QKV_W + LANE] + fb_ref[...])
    logf = jnp.where((lane < FOX_HEADS) & (t_glob >= PAD), logf, 0.0)
    tri = (lax.broadcasted_iota(jnp.int32, (tm, tm), 0)
           >= lax.broadcasted_iota(jnp.int32, (tm, tm), 1)).astype(BF16)
    f_hi = logf.astype(BF16)
    f_mid = (logf - f_hi.astype(F32)).astype(BF16)
    f_lo = (logf - f_hi.astype(F32) - f_mid.astype(F32)).astype(BF16)
    cum = _dot(tri, f_hi) + _dot(tri, f_mid) + _dot(tri, f_lo) + carry_f[...]
    carry_f[...] = cum[tm - 1:tm, :]
    fend_ref[0, 0] = cum[tm - 1:tm, :] * LOG2E

    def split3(hd):
        f2 = jnp.sum(jnp.where(lane == hd, cum, 0.0), axis=-1, keepdims=True) * LOG2E
        hi = f2.astype(BF16).astype(F32)
        mid = (f2 - hi).astype(BF16).astype(F32)
        return hi, mid, f2 - hi - mid

    def tail_cols(vals):
        out = jnp.zeros((tm, LANE), F32)
        for o, val in enumerate(vals):
            out = jnp.where(lane == HEAD_DIM + o, val, out)
        return out

    splits = [split3(hd) for hd in range(FOX_HEADS)]
    for j in range(3 * PAIRS):
        piece = uf[:, j * LANE:(j + 1) * LANE]
        kind = j // PAIRS
        if kind < 2:
            gain = fqg_ref[...] if kind == 0 else fkg_ref[...]
            sq = piece * piece
            ms0 = jnp.sum(jnp.where(first, sq, 0.0), axis=-1, keepdims=True) * (1.0 / HEAD_DIM)
            ms1 = jnp.sum(jnp.where(first, 0.0, sq), axis=-1, keepdims=True) * (1.0 / HEAD_DIM)
            inv = jnp.where(first, lax.rsqrt(ms0 + NORM_EPS), lax.rsqrt(ms1 + NORM_EPS))
            piece = piece * inv * gain
            if kind == 0:
                piece = piece * (ATT_SCALE * LOG2E)
        swapped = pltpu.roll(piece, HEAD_DIM, axis=1)
        for half in range(2):
            hd = 2 * (j % PAIRS) + half
            if hd >= FOX_HEADS:
                continue
            if kind == 2:
                extra = tail_cols([1.0])
            else:
                hi, mid, lo = splits[hd]
                extra = tail_cols([hi, mid, lo, 1.0, 1.0, 1.0] if kind == 0
                                  else [1.0, 1.0, 1.0, -hi, -mid, -lo])
            body = piece if half == 0 else swapped
            fx_ref[0, kind * FOX_HEADS + hd] = jnp.where(first, body, extra).astype(BF16)

    ug = _dot(n, w_ref[:, SEG_GATE:D_IN_PAD])
    gate_ref[0] = _sigmoid(ug).astype(BF16)


def _inproj(stream, g1, w_in_p, mu, fb, fqg, fkg, *, TP, n_seq, tm=INPROJ_TILE):
    first = len(stream) == 2
    B, D = stream[0].shape[0], D_MODEL
    nb = TP // tm
    kern = functools.partial(_inproj_kernel, tm=tm, first=first, n_seq=n_seq)
    out_shape = (
        jax.ShapeDtypeStruct((B, 3 * PAIRS, TP, LANE), F32),
        jax.ShapeDtypeStruct((B, TP, LANE), F32),
        jax.ShapeDtypeStruct((B, 3 * PAIRS, TP, LANE), BF16),
        jax.ShapeDtypeStruct((B, 3 * FOX_HEADS, TP, LANE), BF16),
        jax.ShapeDtypeStruct((B, nb, 1, LANE), F32),
        jax.ShapeDtypeStruct((B, TP, 3 * D_MODEL), BF16),
    )
    in_specs = _stream_specs(first, tm, n_seq) + [
        _const_spec((1, D)),
        _const_spec((D, D_IN_PAD)),
        _const_spec((1, SHIFT_W)),
        _const_spec((1, LANE)),
        _const_spec((1, LANE)),
        _const_spec((1, LANE)),
    ]
    out_specs = (
        pl.BlockSpec((1, 3 * PAIRS, tm, LANE), lambda b, i: (b, 0, i, 0)),
        pl.BlockSpec((1, tm, LANE), lambda b, i: (b, i, 0)),
        pl.BlockSpec((1, 3 * PAIRS, tm, LANE), lambda b, i: (b, 0, i, 0)),
        pl.BlockSpec((1, 3 * FOX_HEADS, tm, LANE), lambda b, i: (b, 0, i, 0)),
        pl.BlockSpec((1, 1, 1, LANE), lambda b, i: (b, i, 0, 0)),
        pl.BlockSpec((1, tm, 3 * D_MODEL), lambda b, i: (b, i, 0)),
    )
    return pl.pallas_call(
        kern, out_shape=out_shape, grid=(B, nb), in_specs=in_specs, out_specs=out_specs,
        scratch_shapes=[pltpu.VMEM((1, SHIFT_W), F32), pltpu.VMEM((1, LANE), F32)],
        compiler_params=pltpu.CompilerParams(
            dimension_semantics=("arbitrary", "arbitrary"), vmem_limit_bytes=VMEM_LIMIT),
        name="inproj",
    )(*stream, g1, w_in_p, mu, fb, fqg, fkg)


def _bdot(a, b):
    return lax.dot_general(a, b, (((2,), (1,)), ((0,), (0,))), preferred_element_type=F32)


def _bdot_nt(a, b):
    return lax.dot_general(a, b, (((2,), (2,)), ((0,), (0,))), preferred_element_type=F32)


def _bdot_tn(a, b):
    return lax.dot_general(a, b, (((1,), (1,)), ((0,), (0,))), preferred_element_type=F32)


def _head_sum(x, first):
    s0 = jnp.sum(jnp.where(first, x, 0.0), axis=-1, keepdims=True)
    s1 = jnp.sum(jnp.where(first, 0.0, x), axis=-1, keepdims=True)
    return jnp.where(first, s0, s1)


def _rwkv_chunk_kernel(r_ref, k_ref, v_ref, wdad_ref, wup_ref, w0_ref, aup_ref, a0_ref,
                       kk_ref, ka_ref, rk_ref,
                       g_ref, hc_ref, r2_ref, y0_ref, bonus_ref, *, rows):
    nc = rows // CHUNK
    r = r_ref[0, 0]
    k = k_ref[0, 0]
    v = v_ref[0, 0]
    wd = wdad_ref[0][:, 0:DECAY_RANK]
    ad = wdad_ref[0][:, DECAY_RANK:DECAY_RANK + ICLR_RANK]
    first = lax.broadcasted_iota(jnp.int32, (rows, LANE), 1) < HEAD_DIM

    pre = w0_ref[0] + _dot3(jnp.tanh(wd), wup_ref[0])
    lw = -jnp.exp(_log_sigmoid(pre) - 0.5)
    iclr = _sigmoid(a0_ref[0] + _dot3(ad, aup_ref[0]))
    kk = k * kk_ref[0]
    kk = kk / jnp.maximum(jnp.sqrt(_head_sum(kk * kk, first)), 1e-12)
    k2 = k * (1.0 + (iclr - 1.0) * ka_ref[0])
    b = kk * iclr
    bonus_ref[0, 0] = _head_sum(r * k2 * rk_ref[0], first) * v

    to3 = lambda x: x.reshape(nc, CHUNK, LANE)
    ri = lax.broadcasted_iota(jnp.int32, (nc, CHUNK, CHUNK), 1)
    ci = lax.broadcasted_iota(jnp.int32, (nc, CHUNK, CHUNK), 2)
    low_incl = ri >= ci
    low_strict = ri > ci
    first3 = lax.broadcasted_iota(jnp.int32, (nc, CHUNK, LANE), 2) < HEAD_DIM

    lw3 = to3(lw)
    tri = low_incl.astype(BF16)
    lw_hi = lw3.astype(BF16)
    cum = _bdot(tri, lw_hi) + _bdot(tri, (lw3 - lw_hi.astype(F32)).astype(BF16))
    cum_end = cum[:, CHUNK - 1:CHUNK, :]
    e_neg = jnp.exp(-cum)
    at = to3(-kk) * jnp.exp(cum - lw3)
    rt = to3(r) * jnp.exp(cum)
    bt = (to3(b) * e_neg).astype(BF16)
    kt = (to3(k2) * e_neg).astype(BF16)
    e_rem = jnp.exp(cum_end - cum)
    bq = (to3(b) * e_rem).astype(BF16)
    kq = (to3(k2) * e_rem).astype(BF16)
    vv = to3(v).astype(BF16)

    heads = range(2)
    sels = [first3, jnp.logical_not(first3)]
    lhs = [jnp.concatenate([jnp.where(sels[hd], at, 0.0), jnp.where(sels[hd], rt, 0.0)],
                           axis=1).astype(BF16) for hd in heads]
    mb = [_bdot_nt(lhs[hd], bt) for hd in heads]
    mk = [_bdot_nt(lhs[hd], kt) for hd in heads]
    m_ak = [jnp.where(low_strict, mk[hd][:, :CHUNK], 0.0).astype(BF16) for hd in heads]
    m_rb = [jnp.where(low_incl, mb[hd][:, CHUNK:], 0.0).astype(BF16) for hd in heads]
    m_rk = [jnp.where(low_incl, mk[hd][:, CHUNK:], 0.0).astype(BF16) for hd in heads]
    p = [jnp.where(low_strict, mb[hd][:, :CHUNK], 0.0) for hd in heads]
    xs = [jnp.concatenate([at, _bdot(m_ak[hd], vv)], axis=-1) for hd in heads]
    for j in range(6):
        p16 = [p[hd].astype(BF16) for hd in heads]
        xs = [xs[hd] + _bdot(p16[hd], xs[hd].astype(BF16)) for hd in heads]
        if j < 5:
            p = [_bdot(p16[hd], p16[hd]) for hd in heads]
    ru = [_bdot(m_rb[hd], xs[hd].astype(BF16)) for hd in heads]
    r2s = [rt + ru[hd][..., :LANE] for hd in heads]
    y0s = [ru[hd][..., LANE:] + _bdot(m_rk[hd], vv) for hd in heads]

    first3w = jnp.concatenate([first3, first3], axis=-1)
    x = jnp.where(first3w, xs[0], xs[1])
    r2_ref[0, 0] = jnp.where(first3, r2s[0], r2s[1]).reshape(rows, LANE)
    y0_ref[0, 0] = jnp.where(first3, y0s[0], y0s[1]).reshape(rows, LANE)

    pg = _bdot_tn(bq, x.astype(BF16))
    ph = pg[..., LANE:] + _bdot_tn(kq, vv)
    rr = lax.broadcasted_iota(jnp.int32, (nc, LANE, LANE), 1)
    cc = lax.broadcasted_iota(jnp.int32, (nc, LANE, LANE), 2)
    same_head = (rr < HEAD_DIM) == (cc < HEAD_DIM)
    g = jnp.where(same_head, pg[..., :LANE], 0.0) + jnp.where(rr == cc, jnp.exp(cum_end), 0.0)
    g_ref[0, 0] = g.reshape(nc * LANE, LANE)
    hc_ref[0, 0] = jnp.where(same_head, ph, 0.0).reshape(nc * LANE, LANE)


def _rwkv_chunks(rkv, wdad, wup, w0, aup, a0, k_k, k_a, r_k):
    B, _, TP, _ = rkv.shape
    rows = next(r for r in (1536, 768, 512, 256) if TP % r == 0)
    nb = TP // rows
    kern = functools.partial(_rwkv_chunk_kernel, rows=rows)
    slot_spec = lambda off: pl.BlockSpec((1, 1, rows, LANE), lambda b, p, i: (b, p + off, i, 0))
    par_mat = pl.BlockSpec((1, DECAY_RANK, LANE), lambda b, p, i: (p, 0, 0))
    par_vec = pl.BlockSpec((1, 1, LANE), lambda b, p, i: (p, 0, 0))
    row_out = jax.ShapeDtypeStruct((B, PAIRS, TP, LANE), F32)
    mat_out = jax.ShapeDtypeStruct((B, PAIRS, 2 * TP, LANE), F32)
    row_spec = pl.BlockSpec((1, 1, rows, LANE), lambda b, p, i: (b, p, i, 0))
    mat_spec = pl.BlockSpec((1, 1, 2 * rows, LANE), lambda b, p, i: (b, p, i, 0))
    return pl.pallas_call(
        kern, out_shape=(mat_out, mat_out, row_out, row_out, row_out), grid=(B, PAIRS, nb),
        in_specs=[slot_spec(0), slot_spec(PAIRS), slot_spec(2 * PAIRS),
                  pl.BlockSpec((1, rows, LANE), lambda b, p, i: (b, i, 0)),
                  par_mat, par_vec, par_mat, par_vec, par_vec, par_vec, par_vec],
        out_specs=(mat_spec, mat_spec, row_spec, row_spec, row_spec),
        compiler_params=pltpu.CompilerParams(
            dimension_semantics=("arbitrary",) * 3, vmem_limit_bytes=VMEM_LIMIT),
        name="rwkv_chunks",
    )(rkv, rkv, rkv, wdad, wup, w0, aup, a0, k_k, k_a, r_k)


def _rwkv_scan_kernel(g_ref, hc_ref, r2_ref, y0_ref, bonus_ref, lng_ref, lnb_ref, y_ref,
                      state, *, rows):
    i = pl.program_id(1)

    @pl.when(i == 0)
    def _():
        state[...] = jnp.zeros_like(state)

    first = lax.broadcasted_iota(jnp.int32, (CHUNK, LANE), 1) < HEAD_DIM
    hs = [state[p] for p in range(PAIRS)]
    for c in range(rows // CHUNK):
        sl = slice(c * CHUNK, (c + 1) * CHUNK)
        sm = slice(c * LANE, (c + 1) * LANE)
        for p in range(PAIRS):
            y = _dot3(r2_ref[0, p, sl, :], hs[p]) + y0_ref[0, p, sl, :]
            hs[p] = _dot3(g_ref[0, p, sm, :], hs[p]) + hc_ref[0, p, sm, :]
            yc = y - _head_sum(y, first) * (1.0 / HEAD_DIM)
            var = _head_sum(yc * yc, first) * (1.0 / HEAD_DIM)
            out = yc * lax.rsqrt(var + RWKV_LN_EPS) * lng_ref[p] + lnb_ref[p]
            y_ref[0, p, sl, :] = (out + bonus_ref[0, p, sl, :]).astype(BF16)
    for p in range(PAIRS):
        state[p] = hs[p]


def _rwkv_scan(gm, hc, r2, y0, bonus, ln_g, ln_b, *, rows=512):
    B, P, TP, _ = r2.shape
    nb = TP // rows
    kern = functools.partial(_rwkv_scan_kernel, rows=rows)
    blk = pl.BlockSpec((1, P, rows, LANE), lambda b, i: (b, 0, i, 0))
    mat = pl.BlockSpec((1, P, 2 * rows, LANE), lambda b, i: (b, 0, i, 0))
    par = pl.BlockSpec((P, 1, LANE), lambda b, i: (0, 0, 0))
    return pl.pallas_call(
        kern, out_shape=jax.ShapeDtypeStruct((B, P, TP, LANE), BF16), grid=(B, nb),
        in_specs=[mat, mat, blk, blk, blk, par, par], out_specs=blk,
        scratch_shapes=[pltpu.VMEM((P, LANE, LANE), F32)],
        compiler_params=pltpu.CompilerParams(
            dimension_semantics=("arbitrary", "arbitrary"), vmem_limit_bytes=VMEM_LIMIT),
        name="rwkv_scan",
    )(gm, hc, r2, y0, bonus, ln_g, ln_b)


def _split_pair(qp, nh):
    lane = lax.broadcasted_iota(jnp.int32, qp.shape, 1)
    first = lane < HEAD_DIM
    zero = jnp.zeros_like(qp)
    qs = [jnp.where(first, qp, zero), jnp.where(first, zero, qp)]
    return qs[:nh], first


def _sb_kernel(q_ref, k_ref, v_ref, o_ref, *, t, nh):
    qi = pl.program_id(2)
    qs, first = _split_pair(q_ref[0, 0], nh)
    qpos = qi * t + lax.broadcasted_iota(jnp.int32, (t, 1), 0)
    kloc = lax.broadcasted_iota(jnp.int32, (1, t), 1)
    upper = (lax.broadcasted_iota(jnp.int32, (t, t), 0)
             > lax.broadcasted_iota(jnp.int32, (t, t), 1)).astype(BF16)

    def sweep(blocks, accs, cs):
        units = []
        for kb, mask in blocks:
            start = pl.multiple_of(kb * t, t)
            kblk = k_ref[0, 0, pl.ds(start, t), :]
            vblk = v_ref[0, 0, pl.ds(start, t), :]
            m = None if mask is None else mask(start + kloc)
            units += [(h, kblk, vblk, m) for h in range(nh)]
        zs = [_dot_nt(qs[h], kblk) for h, kblk, _, _ in units]
        sps = [jnp.maximum(z, 0.0) + jnp.log2(1.0 + jnp.exp2(-jnp.abs(z))) for z in zs]
        spms = [sp if u[3] is None else jnp.where(u[3], sp, 0.0) for sp, u in zip(sps, units)]
        laters = [_dot(spm.astype(BF16), upper) for spm in spms]
        accs, cs = list(accs), list(cs)
        weights = []
        for (h, _, _, m), z, sp, spm, later in zip(units, zs, sps, spms, laters):
            a = jnp.exp2(z - sp - later - cs[h])
            weights.append(a if m is None else jnp.where(m, a, 0.0))
            cs[h] = cs[h] + jnp.sum(spm, axis=-1, keepdims=True)
        for (h, _, vblk, _), a in zip(units, weights):
            accs[h] = accs[h] + _dot(a.astype(BF16), vblk)
        return accs, cs

    def block(kb, accs, cs, mask):
        return sweep([(kb, mask)], accs, cs)

    def live(cs):
        low = cs[0]
        for c in cs[1:]:
            low = jnp.minimum(low, c)
        return (jnp.min(low) < SB_CUTOFF_LOG2).astype(jnp.int32)

    zero_acc = [jnp.zeros((t, LANE), F32) for _ in range(nh)]
    zero_c = [jnp.zeros((t, 1), F32) for _ in range(nh)]
    accs, cs = sweep([(qi, lambda kpos: (kpos >= PAD) & (kpos < qpos)),
                      (jnp.maximum(qi - 1, 0), lambda kpos: (kpos >= PAD) & (qi >= 1))],
                     zero_acc, zero_c)

    def cond(carry):
        kb, alive = carry[0], carry[1]
        return (kb >= 1) & (alive > 0)

    def body(carry):
        kb = carry[0]
        accs, cs = block(kb, list(carry[2:2 + nh]), list(carry[2 + nh:]), None)
        return (kb - 1, live(cs), *accs, *cs)

    carry = lax.while_loop(cond, body, (qi - 2, live(cs), *accs, *cs))

    def front(carry):
        accs, cs = block(0, list(carry[2:2 + nh]), list(carry[2 + nh:]), lambda kpos: kpos >= PAD)
        return (carry[0], carry[1], *accs, *cs)

    carry = lax.cond((carry[0] == 0) & (carry[1] > 0), front, lambda c: c, carry)
    accs = carry[2:2 + nh]
    out = jnp.where(first, accs[0], accs[1] if nh == 2 else 0.0)
    o_ref[0, 0] = out.astype(BF16)


def _sb_attention(sb, *, slot0, nslots, nh, t=256):
    B, _, TP, _ = sb.shape
    kern = functools.partial(_sb_kernel, t=t, nh=nh)
    blk = lambda off: pl.BlockSpec((1, 1, t, LANE), lambda b, p, i: (b, slot0 + off + p, i, 0))
    full = lambda off: pl.BlockSpec((1, 1, TP, LANE), lambda b, p, i: (b, slot0 + off + p, 0, 0))
    return pl.pallas_call(
        kern, out_shape=jax.ShapeDtypeStruct((B, nslots, TP, LANE), BF16),
        grid=(B, nslots, TP // t),
        in_specs=[blk(0), full(PAIRS), full(2 * PAIRS)],
        out_specs=pl.BlockSpec((1, 1, t, LANE), lambda b, p, i: (b, p, i, 0)),
        compiler_params=pltpu.CompilerParams(
            dimension_semantics=("arbitrary",) * 3, vmem_limit_bytes=VMEM_LIMIT),
        name="sb_attention",
    )(sb, sb, sb)


def _fox_kernel(shift_ref, fe_ref, q_ref, k_ref, v_ref, o_ref, lo_ref, *, t):
    b = pl.program_id(0)
    h = pl.program_id(1)
    qi = pl.program_id(2)
    q = q_ref[0, 0]
    shift = shift_ref[0]
    thr = FOX_SKIP_LOG2 + 2.0 * shift
    f_q = fe_ref[b, h, jnp.maximum(qi - 1, 0)]
    lo = lax.while_loop(lambda n: (n < qi) & (fe_ref[b, h, n] - f_q > thr), lambda n: n + 1,
                        jnp.where(qi == 0, 0, lo_ref[0]))
    lo_ref[0] = lo
    qpos = qi * t + lax.broadcasted_iota(jnp.int32, (t, 1), 0)
    kloc = lax.broadcasted_iota(jnp.int32, (1, t), 1)

    def block(kb, carry, masked, online):
        acc, m = carry
        start = pl.multiple_of(kb * t, t)
        kblk = k_ref[0, 0, pl.ds(start, t), :]
        vblk = v_ref[0, 0, pl.ds(start, t), :]
        s = _dot_nt(q, kblk)
        if masked:
            kpos = start + kloc
            s = jnp.where((kpos <= qpos) & (kpos >= PAD), s, NEG_INF)
        if online:
            m_new = jnp.maximum(m, jnp.max(s, axis=-1, keepdims=True))
            acc = jnp.exp2(m - m_new) * acc
            m = m_new
        pr = jnp.exp2(s - m)
        return acc + _dot(pr.astype(BF16), vblk), m

    def sweep(online):
        def run(carry):
            carry = lax.cond(lo == 0, lambda c: block(0, c, True, online), lambda c: c, carry)
            start = jnp.maximum(lo, 1)
            n = jnp.maximum(qi - start, 0)

            def group(i, c):
                acc, m = c
                kvs = []
                for j in range(FOX_GROUP):
                    st = pl.multiple_of((start + FOX_GROUP * i + j) * t, t)
                    kvs.append((k_ref[0, 0, pl.ds(st, t), :], v_ref[0, 0, pl.ds(st, t), :]))
                ss = [_dot_nt(q, kblk) for kblk, _ in kvs]
                prs = [jnp.exp2(s - m).astype(BF16) for s in ss]
                for pr, (_, vblk) in zip(prs, kvs):
                    acc = acc + _dot(pr, vblk)
                return acc, m

            grouped = 0 if online else (n // FOX_GROUP) * FOX_GROUP
            if not online:
                carry = lax.fori_loop(0, n // FOX_GROUP, group, carry)
            carry = lax.fori_loop(start + grouped, qi,
                                  lambda kb, c: block(kb, c, False, online), carry)
            return lax.cond(qi > 0, lambda c: block(qi, c, True, online), lambda c: c, carry)
        return run

    acc0 = jnp.zeros((t, LANE), F32)
    fixed = shift <= FOX_FIXED_SHIFT_MAX
    m0 = jnp.where(fixed, jnp.full((t, 1), shift, F32), jnp.full((t, 1), NEG_INF, F32))
    acc, _ = lax.cond(fixed, sweep(False), sweep(True), (acc0, m0))
    lane = lax.broadcasted_iota(jnp.int32, (t, LANE), 1)
    denom = jnp.sum(jnp.where(lane == HEAD_DIM, acc, 0.0), axis=-1, keepdims=True)
    o_ref[0, 0] = (acc / jnp.where(denom > 0.0, denom, 1.0)).astype(BF16)


def _fox_attention(shift, fe, fx, *, t=FOX_TILE):
    B, _, TP, _ = fx.shape
    H = FOX_HEADS
    kern = functools.partial(_fox_kernel, t=t)
    return pl.pallas_call(
        kern, out_shape=jax.ShapeDtypeStruct((B, H, TP, LANE), BF16),
        grid=(B, H, TP // t),
        in_specs=[pl.BlockSpec(memory_space=pltpu.SMEM),
                  pl.BlockSpec(memory_space=pltpu.SMEM),
                  pl.BlockSpec((1, 1, t, LANE), lambda b, h, i: (b, h, i, 0)),
                  pl.BlockSpec((1, 1, TP, LANE), lambda b, h, i: (b, h + H, 0, 0)),
                  pl.BlockSpec((1, 1, TP, LANE), lambda b, h, i: (b, h + 2 * H, 0, 0))],
        out_specs=pl.BlockSpec((1, 1, t, LANE), lambda b, h, i: (b, h, i, 0)),
        scratch_shapes=[pltpu.SMEM((1,), jnp.int32)],
        compiler_params=pltpu.CompilerParams(
            dimension_semantics=("arbitrary",) * 3, vmem_limit_bytes=VMEM_LIMIT),
        name="fox_attention",
    )(shift, fe, fx, fx, fx)


def _merge_kernel(*refs, tm, n_valid, first, n_seq):
    n_stream = 2 if first else 1
    (ya_ref, yb2_ref, yb1_ref, yc_ref, gate_ref,
     wpa_ref, wpb_ref, wpc_ref, wout_ref, o_ref) = refs[n_stream:]
    i = pl.program_id(1)

    def proj(slots, w_ref):
        return _dot(jnp.concatenate(slots, axis=-1), w_ref[...])

    pa = proj([ya_ref[0, s] for s in range(PAIRS)], wpa_ref)
    pb = proj([yb2_ref[0, 0], yb2_ref[0, 1], yb1_ref[0, 0]], wpb_ref)
    pc = proj([yc_ref[0, s] for s in range(FOX_HEADS)], wpc_ref)
    merged = gate_ref[0, :, 0:D_MODEL].astype(F32) * pa
    merged = merged + gate_ref[0, :, D_MODEL:2 * D_MODEL].astype(F32) * pb
    merged = merged + gate_ref[0, :, 2 * D_MODEL:3 * D_MODEL].astype(F32) * pc
    out = _stream_tile(i, refs[:n_stream], n_seq) + _dot(merged.astype(BF16), wout_ref[...])
    t = i * tm + lax.broadcasted_iota(jnp.int32, (tm, 1), 0)
    o_ref[0] = jnp.where((t >= PAD) & (t < PAD + n_valid), out, 0.0)


def _merge(stream, ya, yb2, yb1, yc, gates, wpa, wpb, wpc, wout, *, n_valid, n_seq, tm=FRAME_TILE):
    first = len(stream) == 2
    B, _, TP, _ = ya.shape
    D = D_MODEL
    kern = functools.partial(_merge_kernel, tm=tm, n_valid=n_valid, first=first, n_seq=n_seq)
    pair_blk = lambda ns: pl.BlockSpec((1, ns, tm, LANE), lambda b, i: (b, 0, i, 0))
    return pl.pallas_call(
        kern, out_shape=jax.ShapeDtypeStruct((B, TP, D), F32), grid=(B, TP // tm),
        in_specs=_stream_specs(first, tm, n_seq) + [
                  pair_blk(PAIRS),
                  pair_blk(2), pair_blk(1), pair_blk(FOX_HEADS),
                  pl.BlockSpec((1, tm, 3 * D), lambda b, i: (b, i, 0)),
                  _const_spec((PAIRS * LANE, D)),
                  _const_spec((PAIRS * LANE, D)),
                  _const_spec((FOX_HEADS * LANE, D)),
                  _const_spec((D, D))],
        out_specs=pl.BlockSpec((1, tm, D), lambda b, i: (b, i, 0)),
        compiler_params=pltpu.CompilerParams(
            dimension_semantics=("arbitrary", "arbitrary"), vmem_limit_bytes=VMEM_LIMIT),
        name="merge",
    )(*stream, ya, yb2, yb1, yc, gates, wpa, wpb, wpc, wout)


def _moe_kernel(*refs, tm, n_valid, last):
    n_stream = 2 if last else 1
    g_ref, wr_ref, br_ref, wg_ref, wu_ref, wd_ref, o_ref, he_ref, ys_ref = refs[n_stream:]
    i = pl.program_id(1)
    x = jnp.concatenate([r[0] for r in refs[:n_stream]], axis=0)
    ms = jnp.mean(x * x, axis=-1, keepdims=True)
    n = x * lax.rsqrt(ms + NORM_EPS) * g_ref[...]

    lg = _dot3(n, wr_ref[...]) + br_ref[...]
    lane = lax.broadcasted_iota(jnp.int32, (tm, LANE), 1)
    big = jnp.int32(LANE)
    is_group = lane < N_GROUPS
    gl = jnp.where(is_group, lg, -jnp.inf)
    gmax = jnp.max(gl, axis=-1, keepdims=True)
    g_idx = jnp.min(jnp.where(is_group & (gl == gmax), lane, big), axis=-1, keepdims=True)
    p_group = 1.0 / jnp.sum(jnp.where(is_group, jnp.exp(gl - gmax), 0.0), axis=-1, keepdims=True)
    lo = N_GROUPS + g_idx * EXPERTS_PER_GROUP
    in_grp = (lane >= lo) & (lane < lo + EXPERTS_PER_GROUP)
    el = jnp.where(in_grp, lg, -jnp.inf)
    top1 = jnp.max(el, axis=-1, keepdims=True)
    i1 = jnp.min(jnp.where(in_grp & (el == top1), lane, big), axis=-1, keepdims=True)
    rest = in_grp & (lane != i1)
    el2 = jnp.where(rest, lg, -jnp.inf)
    top2 = jnp.max(el2, axis=-1, keepdims=True)
    i2 = jnp.min(jnp.where(rest & (el2 == top2), lane, big), axis=-1, keepdims=True)
    e2 = jnp.exp(top2 - top1)
    p1 = 1.0 / (1.0 + e2)
    p2 = e2 / (1.0 + e2)
    comb = p_group * (jnp.where(lane == i1, p1, 0.0) + jnp.where(lane == i2, p2, 0.0))

    n16 = n.astype(BF16)

    onehot = (lane == g_idx).astype(BF16)
    earlier = (lax.broadcasted_iota(jnp.int32, (tm, tm), 1)
               < lax.broadcasted_iota(jnp.int32, (tm, tm), 0)).astype(BF16)
    cnt = _dot(earlier, onehot)
    rank = jnp.sum(jnp.where(lane == g_idx, cnt, 0.0), axis=-1, keepdims=True).astype(jnp.int32)
    n_pass = (jnp.max(rank) + MOE_CAP) // MOE_CAP
    comb16 = comb.astype(BF16)
    slot_lane = lax.broadcasted_iota(jnp.int32, (tm, N_GROUPS * MOE_CAP), 1)

    def one_pass(p, y):
        r = rank - p * MOE_CAP
        slot = jnp.where((r >= 0) & (r < MOE_CAP), g_idx * MOE_CAP + r, -1)
        place = (slot_lane == slot).astype(BF16)
        xs = _dot_tn(place, n16).astype(BF16)
        cw = _dot_tn(place, comb16)
        for g in range(N_GROUPS):
            rows = slice(g * MOE_CAP, (g + 1) * MOE_CAP)
            for e in range(EXPERTS_PER_GROUP):
                ee = g * EXPERTS_PER_GROUP + e
                gate = _dot(xs[rows], wg_ref[ee])
                up = _dot(xs[rows], wu_ref[ee])
                c_e = cw[rows, N_GROUPS + ee:N_GROUPS + ee + 1]
                he_ref[:, e * D_EXPERT:(e + 1) * D_EXPERT] = (
                    gate * _sigmoid(gate) * up * c_e).astype(BF16)
            gw = EXPERTS_PER_GROUP * D_EXPERT
            ys_ref[rows, :] = _dot(he_ref[...], wd_ref[g * gw:(g + 1) * gw, :]).astype(BF16)
        return y + _dot(place, ys_ref[...])

    out = x + lax.fori_loop(0, n_pass, one_pass, jnp.zeros((tm, D_MODEL), F32))
    if last:
        o_ref[0] = out
    else:
        t = i * tm + lax.broadcasted_iota(jnp.int32, (tm, 1), 0)
        o_ref[0] = jnp.where((t >= PAD) & (t < PAD + n_valid), out, 0.0)


def _moe(h, g2, wr, br, wg, wu, wd, *, n_valid, n_seq, last):
    B, TP, D = h.shape
    tm = MOE_TILE
    kern = functools.partial(_moe_kernel, tm=tm, n_valid=n_valid, last=last)
    EW = N_EXPERTS * D_EXPERT
    if last:
        half = tm // FRAME_TILE
        assert half == 2 and n_seq % half == 0
        rows_out = n_seq * FRAME_TILE
        stream_specs = [pl.BlockSpec((1, FRAME_TILE, D), lambda b, i, j=j: (b, half * i + 1 + j, 0))
                        for j in range(half)]
        stream = [h] * half
    else:
        rows_out = TP
        stream_specs = [pl.BlockSpec((1, tm, D), lambda b, i: (b, i, 0))]
        stream = [h]
    return pl.pallas_call(
        kern, out_shape=jax.ShapeDtypeStruct((B, rows_out, D), F32), grid=(B, rows_out // tm),
        in_specs=stream_specs + [
                  _const_spec((1, D)), _const_spec((D, LANE)), _const_spec((1, LANE)),
                  _const_spec((N_EXPERTS, D, D_EXPERT)), _const_spec((N_EXPERTS, D, D_EXPERT)),
                  _const_spec((EW, D))],
        out_specs=pl.BlockSpec((1, tm, D), lambda b, i: (b, i, 0)),
        scratch_shapes=[pltpu.VMEM((MOE_CAP, EXPERTS_PER_GROUP * D_EXPERT), BF16),
                        pltpu.VMEM((N_GROUPS * MOE_CAP, D), BF16)],
        compiler_params=pltpu.CompilerParams(
            dimension_semantics=("arbitrary", "arbitrary"), vmem_limit_bytes=VMEM_LIMIT),
        name="moe",
    )(*stream, g2, wr, br, wg, wu, wd)


def _transpose_kernel(x_ref, o_ref):
    o_ref[...] = x_ref[...].T.astype(BF16)


def _transpose_to_bf16(a, *, tn=LANE):
    N, K = a.shape
    assert N % tn == 0
    return pl.pallas_call(
        _transpose_kernel, out_shape=jax.ShapeDtypeStruct((K, N), BF16), grid=(N // tn,),
        in_specs=[pl.BlockSpec((tn, K), lambda i: (i, 0))],
        out_specs=pl.BlockSpec((K, tn), lambda i: (0, i)),
        compiler_params=pltpu.CompilerParams(dimension_semantics=("arbitrary",)),
        name="transpose_cast",
    )(a)


def _pack_w_in(w_in):
    wt = jnp.swapaxes(w_in, 0, 1)
    rest = wt[SHIFT_W:]
    rows = [wt[:SHIFT_W]]
    off = 0
    for heads in (SB_HEADS,) * 3 + (FOX_HEADS,) * 3:
        rows.append(jnp.pad(rest[off:off + heads * HEAD_DIM],
                            ((0, PAIRS * LANE - heads * HEAD_DIM), (0, 0))))
        off += heads * HEAD_DIM
    rows.append(jnp.pad(rest[off:off + FOX_HEADS], ((0, LANE - FOX_HEADS), (0, 0))))
    rows.append(rest[off + FOX_HEADS:])
    return _transpose_to_bf16(jnp.concatenate(rows, axis=0))


def _per_pair_cols(w):
    return w.reshape(w.shape[0], PAIRS, LANE).transpose(1, 0, 2)


def kernel(x, meta_tokens, norm1_g, w_in, rwkv_mu, rwkv_w_up, rwkv_w0, rwkv_a_up, rwkv_a0,
           rwkv_k_k, rwkv_k_a, rwkv_r_k, rwkv_ln_g, rwkv_ln_b, fox_f_b, fox_q_g, fox_k_g,
           w_p_rwkv, w_p_sb, w_p_fox, w_out, norm2_g, moe_wg, moe_bg, moe_we, moe_be,
           moe_w_gate, moe_w_up, moe_w_down):
    B, S, D = x.shape
    depth = w_in.shape[0]
    L = N_META + S
    TP = -(-(PAD + L) // ROW_TILE) * ROW_TILE
    assert S % FRAME_TILE == 0
    n_seq = S // FRAME_TILE
    meta = jnp.broadcast_to(meta_tokens[None].astype(x.dtype), (B, N_META, D))
    stream = (x, jnp.concatenate([jnp.zeros((B, PAD, D), x.dtype), meta], axis=1))
    H = RWKV_HEADS
    EW = N_EXPERTS * D_EXPERT
    for l in range(depth):
        fb = jnp.zeros((1, LANE), F32).at[0, :FOX_HEADS].set(fox_f_b[l])
        rkv, wdad, sb, fx, fend, gates = _inproj(
            stream, norm1_g[l][None], _pack_w_in(w_in[l]), rwkv_mu[l][None], fb,
            jnp.tile(fox_q_g[l], 2)[None], jnp.tile(fox_k_g[l], 2)[None], TP=TP, n_seq=n_seq)
        hv = lambda p: p.reshape(PAIRS, 1, LANE)
        gm, hc, r2, y0, bonus = _rwkv_chunks(
            rkv, wdad, _per_pair_cols(rwkv_w_up[l]), hv(rwkv_w0[l]),
            _per_pair_cols(rwkv_a_up[l]), hv(rwkv_a0[l]),
            hv(rwkv_k_k[l]), hv(rwkv_k_a[l]), hv(rwkv_r_k[l]))
        ya = _rwkv_scan(gm, hc, r2, y0, bonus, hv(rwkv_ln_g[l]), hv(rwkv_ln_b[l]))
        yb2 = _sb_attention(sb, slot0=0, nslots=2, nh=2)
        yb1 = _sb_attention(sb, slot0=2, nslots=1, nh=1)
        shift = (8.0 * LOG2E * jnp.max(jnp.abs(fox_q_g[l])) * jnp.max(jnp.abs(fox_k_g[l]))).reshape(1)
        step = FOX_TILE // INPROJ_TILE
        fe = jnp.transpose(fend[:, step - 1::step, 0, :FOX_HEADS], (0, 2, 1))
        yc = _fox_attention(shift, fe, fx)
        pc = jnp.pad(w_p_fox[l].reshape(FOX_HEADS, HEAD_DIM, D),
                     ((0, 0), (0, LANE - HEAD_DIM), (0, 0))).reshape(FOX_HEADS * LANE, D).astype(BF16)
        pp = lambda w, nh: jnp.pad(w, ((0, PAIRS * LANE - nh * HEAD_DIM), (0, 0))).astype(BF16)
        h = _merge(stream, ya, yb2, yb1, yc, gates, pp(w_p_rwkv[l], H), pp(w_p_sb[l], SB_HEADS),
                   pc, w_out[l].astype(BF16), n_valid=L, n_seq=n_seq)
        wr = jnp.zeros((D, LANE), F32).at[:, :N_GROUPS].set(moe_wg[l])
        wr = wr.at[:, N_GROUPS:N_GROUPS + N_EXPERTS].set(moe_we[l])
        br = jnp.zeros((1, LANE), F32).at[0, :N_GROUPS].set(moe_bg[l])
        br = br.at[0, N_GROUPS:N_GROUPS + N_EXPERTS].set(moe_be[l])
        wg = moe_w_gate[l].astype(BF16)
        wu = moe_w_up[l].astype(BF16)
        wd = moe_w_down[l].reshape(EW, D).astype(BF16)
        h = _moe(h, norm2_g[l][None], wr, br, wg, wu, wd, n_valid=L, n_seq=n_seq,
                 last=l == depth - 1)
        stream = (h,)
    return h
```

```python
import functools
import math

import jax
import jax.numpy as jnp
from jax import lax
from jax.experimental import pallas as pl
from jax.experimental.pallas import tpu as pltpu

D_MODEL = 1024
HEAD_DIM = 64
N_META = 16
FRAME_TILE = 256
MOE_TILE = 512
MOE_CAP = 192
PAD = FRAME_TILE - N_META
RWKV_HEADS = 6
SB_HEADS = 5
FOX_HEADS = 5
RWKV_W = RWKV_HEADS * HEAD_DIM
SB_W = SB_HEADS * HEAD_DIM
FOX_W = FOX_HEADS * HEAD_DIM
DECAY_RANK = 64
ICLR_RANK = 64
SHIFT_W = 3 * RWKV_W + DECAY_RANK + ICLR_RANK
N_GROUPS = 4
EXPERTS_PER_GROUP = 4
N_EXPERTS = 16
D_EXPERT = 256
NORM_EPS = 1e-6
RWKV_LN_EPS = 64e-5
NEG_INF = -1e30
ATT_SCALE = 1.0 / math.sqrt(HEAD_DIM)

LANE = 128
ROW_TILE = 512
PAIRS = 3
QKV_W = 3 * PAIRS * LANE
SEG_SB = SHIFT_W
SEG_FOX = SEG_SB + QKV_W
SEG_GATE = SEG_FOX + QKV_W + LANE
D_IN_PAD = SEG_GATE + 3 * D_MODEL
SB_CUTOFF_LOG2 = 160.0
SB_TILE = 128
SB_FUSED = 3
LOG2E = 1.4426950408889634
FOX_FIXED_SHIFT_MAX = 50.0
FOX_SKIP_LOG2 = 150.0
FOX_TILE = 512
FOX_GROUP = 4
INPROJ_TILE = FRAME_TILE

CHUNK = 64
VMEM_LIMIT = 56 * 1024 * 1024

F32 = jnp.float32
BF16 = jnp.bfloat16


def _log_sigmoid(x):
    return jnp.minimum(x, 0.0) - jnp.log1p(jnp.exp(-jnp.abs(x)))


def _sigmoid(x):
    return 1.0 / (1.0 + jnp.exp(-x))


def _dot(a, b, **kw):
    return jnp.dot(a, b, preferred_element_type=F32, **kw)


def _dot_nt(a, b):
    return lax.dot_general(a, b, (((1,), (1,)), ((), ())), preferred_element_type=F32)


def _dot_tn(a, b):
    return lax.dot_general(a, b, (((0,), (0,)), ((), ())), preferred_element_type=F32)


def _dot3(a, b):
    ah = a.astype(BF16)
    al = (a - ah.astype(F32)).astype(BF16)
    bh = b.astype(BF16)
    bl = (b - bh.astype(F32)).astype(BF16)
    return _dot(ah, bh) + (_dot(ah, bl) + _dot(al, bh))


def _stream_specs(first, tm, n_seq):
    if not first:
        return [pl.BlockSpec((1, tm, D_MODEL), lambda b, i: (b, i, 0))]
    return [pl.BlockSpec((1, tm, D_MODEL), lambda b, i: (b, jnp.clip(i - 1, 0, n_seq - 1), 0)),
            pl.BlockSpec((1, tm, D_MODEL), lambda b, i: (b, 0, 0))]


def _stream_tile(i, refs, n_seq):
    if len(refs) == 1:
        return refs[0][0]
    x_ref, head_ref = refs
    return jnp.where(i == 0, head_ref[0], jnp.where(i <= n_seq, x_ref[0], 0.0))


def _const_spec(shape):
    n = len(shape)
    return pl.BlockSpec(shape, lambda *_: (0,) * n, pipeline_mode=pl.Buffered(1))


def _inproj_kernel(*refs, tm, first, n_seq):
    n_stream = 2 if first else 1
    (g_ref, w_ref, mu_ref, fb_ref, fqg_ref, fkg_ref,
     rkv_ref, wdad_ref, sb_ref, fx_ref, fend_ref, gate_ref, carry_u, carry_f) = refs[n_stream:]
    i = pl.program_id(1)

    @pl.when(i == 0)
    def _():
        carry_u[...] = jnp.zeros_like(carry_u)
        carry_f[...] = jnp.zeros_like(carry_f)

    x = _stream_tile(i, refs[:n_stream], n_seq)
    ms = jnp.mean(x * x, axis=-1, keepdims=True)
    n = (x * lax.rsqrt(ms + NORM_EPS) * g_ref[...]).astype(BF16)
    row = lax.broadcasted_iota(jnp.int32, (tm, 1), 0)

    us = _dot(n, w_ref[:, 0:SHIFT_W])
    prev = pltpu.roll(us, 1, axis=0)
    prev = jnp.where(row == 0, carry_u[...], prev)
    carry_u[...] = us[tm - 1:tm, :]
    ush = us + (prev - us) * mu_ref[...]
    for j in range(3 * PAIRS):
        rkv_ref[0, j] = ush[:, j * LANE:(j + 1) * LANE]
    wdad_ref[0] = ush[:, 3 * RWKV_W:SHIFT_W]

    usb = _dot(n, w_ref[:, SEG_SB:SEG_SB + QKV_W])
    for j in range(3 * PAIRS):
        piece = usb[:, j * LANE:(j + 1) * LANE]
        if j < PAIRS:
            piece = piece * (ATT_SCALE * LOG2E)
        sb_ref[0, j] = piece.astype(BF16)

    uf = _dot(n, w_ref[:, SEG_FOX:SEG_FOX + QKV_W + LANE])
    lane = lax.broadcasted_iota(jnp.int32, (tm, LANE), 1)
    first = lane < HEAD_DIM
    t_glob = i * tm + row
    logf = _log_sigmoid(uf[:, QKV_W:QKV_W + LANE] + fb_ref[...])
    logf = jnp.where((lane < FOX_HEADS) & (t_glob >= PAD), logf, 0.0)
    tri = (lax.broadcasted_iota(jnp.int32, (tm, tm), 0)
           >= lax.broadcasted_iota(jnp.int32, (tm, tm), 1)).astype(BF16)
    f_hi = logf.astype(BF16)
    f_mid = (logf - f_hi.astype(F32)).astype(BF16)
    f_lo = (logf - f_hi.astype(F32) - f_mid.astype(F32)).astype(BF16)
    cum = _dot(tri, f_hi) + _dot(tri, f_mid) + _dot(tri, f_lo) + carry_f[...]
    carry_f[...] = cum[tm - 1:tm, :]
    fend_ref[0, 0] = cum[tm - 1:tm, :] * LOG2E

    def split3(hd):
        f2 = jnp.sum(jnp.where(lane == hd, cum, 0.0), axis=-1, keepdims=True) * LOG2E
        hi = f2.astype(BF16).astype(F32)
        mid = (f2 - hi).astype(BF16).astype(F32)
        return hi, mid, f2 - hi - mid

    def tail_cols(vals):
        out = jnp.zeros((tm, LANE), F32)
        for o, val in enumerate(vals):
            out = jnp.where(lane == HEAD_DIM + o, val, out)
        return out

    splits = [split3(hd) for hd in range(FOX_HEADS)]
    for j in range(3 * PAIRS):
        piece = uf[:, j * LANE:(j + 1) * LANE]
        kind = j // PAIRS
        if kind < 2:
            gain = fqg_ref[...] if kind == 0 else fkg_ref[...]
            sq = piece * piece
            ms0 = jnp.sum(jnp.where(first, sq, 0.0), axis=-1, keepdims=True) * (1.0 / HEAD_DIM)
            ms1 = jnp.sum(jnp.where(first, 0.0, sq), axis=-1, keepdims=True) * (1.0 / HEAD_DIM)
            inv = jnp.where(first, lax.rsqrt(ms0 + NORM_EPS), lax.rsqrt(ms1 + NORM_EPS))
            piece = piece * inv * gain
            if kind == 0:
                piece = piece * (ATT_SCALE * LOG2E)
        swapped = pltpu.roll(piece, HEAD_DIM, axis=1)
        for half in range(2):
            hd = 2 * (j % PAIRS) + half
            if hd >= FOX_HEADS:
                continue
            if kind == 2:
                extra = tail_cols([1.0])
            else:
                hi, mid, lo = splits[hd]
                extra = tail_cols([hi, mid, lo, 1.0, 1.0, 1.0] if kind == 0
                                  else [1.0, 1.0, 1.0, -hi, -mid, -lo])
            body = piece if half == 0 else swapped
            fx_ref[0, kind * FOX_HEADS + hd] = jnp.where(first, body, extra).astype(BF16)

    ug = _dot(n, w_ref[:, SEG_GATE:D_IN_PAD])
    gate_ref[0] = _sigmoid(ug).astype(BF16)


def _inproj(stream, g1, w_in_p, mu, fb, fqg, fkg, *, TP, n_seq, tm=INPROJ_TILE):
    first = len(stream) == 2
    B, D = stream[0].shape[0], D_MODEL
    nb = TP // tm
    kern = functools.partial(_inproj_kernel, tm=tm, first=first, n_seq=n_seq)
    out_shape = (
        jax.ShapeDtypeStruct((B, 3 * PAIRS, TP, LANE), F32),
        jax.ShapeDtypeStruct((B, TP, LANE), F32),
        jax.ShapeDtypeStruct((B, 3 * PAIRS, TP, LANE), BF16),
        jax.ShapeDtypeStruct((B, 3 * FOX_HEADS, TP, LANE), BF16),
        jax.ShapeDtypeStruct((B, nb, 1, LANE), F32),
        jax.ShapeDtypeStruct((B, TP, 3 * D_MODEL), BF16),
    )
    in_specs = _stream_specs(first, tm, n_seq) + [
        _const_spec((1, D)),
        _const_spec((D, D_IN_PAD)),
        _const_spec((1, SHIFT_W)),
        _const_spec((1, LANE)),
        _const_spec((1, LANE)),
        _const_spec((1, LANE)),
    ]
    out_specs = (
        pl.BlockSpec((1, 3 * PAIRS, tm, LANE), lambda b, i: (b, 0, i, 0)),
        pl.BlockSpec((1, tm, LANE), lambda b, i: (b, i, 0)),
        pl.BlockSpec((1, 3 * PAIRS, tm, LANE), lambda b, i: (b, 0, i, 0)),
        pl.BlockSpec((1, 3 * FOX_HEADS, tm, LANE), lambda b, i: (b, 0, i, 0)),
        pl.BlockSpec((1, 1, 1, LANE), lambda b, i: (b, i, 0, 0)),
        pl.BlockSpec((1, tm, 3 * D_MODEL), lambda b, i: (b, i, 0)),
    )
    return pl.pallas_call(
        kern, out_shape=out_shape, grid=(B, nb), in_specs=in_specs, out_specs=out_specs,
        scratch_shapes=[pltpu.VMEM((1, SHIFT_W), F32), pltpu.VMEM((1, LANE), F32)],
        compiler_params=pltpu.CompilerParams(
            dimension_semantics=("arbitrary", "arbitrary"), vmem_limit_bytes=VMEM_LIMIT),
        name="inproj",
    )(*stream, g1, w_in_p, mu, fb, fqg, fkg)


def _bdot(a, b):
    return lax.dot_general(a, b, (((2,), (1,)), ((0,), (0,))), preferred_element_type=F32)


def _bdot_nt(a, b):
    return lax.dot_general(a, b, (((2,), (2,)), ((0,), (0,))), preferred_element_type=F32)


def _bdot_tn(a, b):
    return lax.dot_general(a, b, (((1,), (1,)), ((0,), (0,))), preferred_element_type=F32)


def _head_sum(x, first):
    s0 = jnp.sum(jnp.where(first, x, 0.0), axis=-1, keepdims=True)
    s1 = jnp.sum(jnp.where(first, 0.0, x), axis=-1, keepdims=True)
    return jnp.where(first, s0, s1)


def _rwkv_chunk_kernel(r_ref, k_ref, v_ref, wdad_ref, wup_ref, w0_ref, aup_ref, a0_ref,
                       kk_ref, ka_ref, rk_ref,
                       g_ref, hc_ref, r2_ref, y0_ref, bonus_ref, *, rows):
    nc = rows // CHUNK
    r = r_ref[0, 0]
    k = k_ref[0, 0]
    v = v_ref[0, 0]
    wd = wdad_ref[0][:, 0:DECAY_RANK]
    ad = wdad_ref[0][:, DECAY_RANK:DECAY_RANK + ICLR_RANK]
    first = lax.broadcasted_iota(jnp.int32, (rows, LANE), 1) < HEAD_DIM

    pre = w0_ref[0] + _dot3(jnp.tanh(wd), wup_ref[0])
    lw = -jnp.exp(_log_sigmoid(pre) - 0.5)
    iclr = _sigmoid(a0_ref[0] + _dot3(ad, aup_ref[0]))
    kk = k * kk_ref[0]
    kk = kk / jnp.maximum(jnp.sqrt(_head_sum(kk * kk, first)), 1e-12)
    k2 = k * (1.0 + (iclr - 1.0) * ka_ref[0])
    b = kk * iclr
    bonus_ref[0, 0] = _head_sum(r * k2 * rk_ref[0], first) * v

    to3 = lambda x: x.reshape(nc, CHUNK, LANE)
    ri = lax.broadcasted_iota(jnp.int32, (nc, CHUNK, CHUNK), 1)
    ci = lax.broadcasted_iota(jnp.int32, (nc, CHUNK, CHUNK), 2)
    low_incl = ri >= ci
    low_strict = ri > ci
    first3 = lax.broadcasted_iota(jnp.int32, (nc, CHUNK, LANE), 2) < HEAD_DIM

    lw3 = to3(lw)
    tri = low_incl.astype(BF16)
    lw_hi = lw3.astype(BF16)
    cum = _bdot(tri, lw_hi) + _bdot(tri, (lw3 - lw_hi.astype(F32)).astype(BF16))
    cum_end = cum[:, CHUNK - 1:CHUNK, :]
    e_neg = jnp.exp(-cum)
    at = to3(-kk) * jnp.exp(cum - lw3)
    rt = to3(r) * jnp.exp(cum)
    bt = (to3(b) * e_neg).astype(BF16)
    kt = (to3(k2) * e_neg).astype(BF16)
    e_rem = jnp.exp(cum_end - cum)
    bq = (to3(b) * e_rem).astype(BF16)
    kq = (to3(k2) * e_rem).astype(BF16)
    vv = to3(v).astype(BF16)

    heads = range(2)
    sels = [first3, jnp.logical_not(first3)]
    lhs = [jnp.concatenate([jnp.where(sels[hd], at, 0.0), jnp.where(sels[hd], rt, 0.0)],
                           axis=1).astype(BF16) for hd in heads]
    mb = [_bdot_nt(lhs[hd], bt) for hd in heads]
    mk = [_bdot_nt(lhs[hd], kt) for hd in heads]
    m_ak = [jnp.where(low_strict, mk[hd][:, :CHUNK], 0.0).astype(BF16) for hd in heads]
    m_rb = [jnp.where(low_incl, mb[hd][:, CHUNK:], 0.0).astype(BF16) for hd in heads]
    m_rk = [jnp.where(low_incl, mk[hd][:, CHUNK:], 0.0).astype(BF16) for hd in heads]
    p = [jnp.where(low_strict, mb[hd][:, :CHUNK], 0.0) for hd in heads]
    xs = [jnp.concatenate([at, _bdot(m_ak[hd], vv)], axis=-1) for hd in heads]
    for j in range(6):
        p16 = [p[hd].astype(BF16) for hd in heads]
        xs = [xs[hd] + _bdot(p16[hd], xs[hd].astype(BF16)) for hd in heads]
        if j < 5:
            p = [_bdot(p16[hd], p16[hd]) for hd in heads]
    ru = [_bdot(m_rb[hd], xs[hd].astype(BF16)) for hd in heads]
    r2s = [rt + ru[hd][..., :LANE] for hd in heads]
    y0s = [ru[hd][..., LANE:] + _bdot(m_rk[hd], vv) for hd in heads]

    first3w = jnp.concatenate([first3, first3], axis=-1)
    x = jnp.where(first3w, xs[0], xs[1])
    r2_ref[0, 0] = jnp.where(first3, r2s[0], r2s[1]).reshape(rows, LANE)
    y0_ref[0, 0] = jnp.where(first3, y0s[0], y0s[1]).reshape(rows, LANE)

    pg = _bdot_tn(bq, x.astype(BF16))
    ph = pg[..., LANE:] + _bdot_tn(kq, vv)
    rr = lax.broadcasted_iota(jnp.int32, (nc, LANE, LANE), 1)
    cc = lax.broadcasted_iota(jnp.int32, (nc, LANE, LANE), 2)
    same_head = (rr < HEAD_DIM) == (cc < HEAD_DIM)
    g = jnp.where(same_head, pg[..., :LANE], 0.0) + jnp.where(rr == cc, jnp.exp(cum_end), 0.0)
    g_ref[0, 0] = g.reshape(nc * LANE, LANE)
    hc_ref[0, 0] = jnp.where(same_head, ph, 0.0).reshape(nc * LANE, LANE)


def _rwkv_chunks(rkv, wdad, wup, w0, aup, a0, k_k, k_a, r_k):
    B, _, TP, _ = rkv.shape
    rows = next(r for r in (1536, 768, 512, 256) if TP % r == 0)
    nb = TP // rows
    kern = functools.partial(_rwkv_chunk_kernel, rows=rows)
    slot_spec = lambda off: pl.BlockSpec((1, 1, rows, LANE), lambda b, p, i: (b, p + off, i, 0))
    par_mat = pl.BlockSpec((1, DECAY_RANK, LANE), lambda b, p, i: (p, 0, 0))
    par_vec = pl.BlockSpec((1, 1, LANE), lambda b, p, i: (p, 0, 0))
    row_out = jax.ShapeDtypeStruct((B, PAIRS, TP, LANE), F32)
    mat_out = jax.ShapeDtypeStruct((B, PAIRS, 2 * TP, LANE), F32)
    row_spec = pl.BlockSpec((1, 1, rows, LANE), lambda b, p, i: (b, p, i, 0))
    mat_spec = pl.BlockSpec((1, 1, 2 * rows, LANE), lambda b, p, i: (b, p, i, 0))
    return pl.pallas_call(
        kern, out_shape=(mat_out, mat_out, row_out, row_out, row_out), grid=(B, PAIRS, nb),
        in_specs=[slot_spec(0), slot_spec(PAIRS), slot_spec(2 * PAIRS),
                  pl.BlockSpec((1, rows, LANE), lambda b, p, i: (b, i, 0)),
                  par_mat, par_vec, par_mat, par_vec, par_vec, par_vec, par_vec],
        out_specs=(mat_spec, mat_spec, row_spec, row_spec, row_spec),
        compiler_params=pltpu.CompilerParams(
            dimension_semantics=("arbitrary",) * 3, vmem_limit_bytes=VMEM_LIMIT),
        name="rwkv_chunks",
    )(rkv, rkv, rkv, wdad, wup, w0, aup, a0, k_k, k_a, r_k)


def _rwkv_scan_kernel(g_ref, hc_ref, r2_ref, y0_ref, bonus_ref, lng_ref, lnb_ref, y_ref,
                      state, *, rows):
    i = pl.program_id(1)

    @pl.when(i == 0)
    def _():
        state[...] = jnp.zeros_like(state)

    first = lax.broadcasted_iota(jnp.int32, (CHUNK, LANE), 1) < HEAD_DIM
    hs = [state[p] for p in range(PAIRS)]
    for c in range(rows // CHUNK):
        sl = slice(c * CHUNK, (c + 1) * CHUNK)
        sm = slice(c * LANE, (c + 1) * LANE)
        for p in range(PAIRS):
            y = _dot3(r2_ref[0, p, sl, :], hs[p]) + y0_ref[0, p, sl, :]
            hs[p] = _dot3(g_ref[0, p, sm, :], hs[p]) + hc_ref[0, p, sm, :]
            yc = y - _head_sum(y, first) * (1.0 / HEAD_DIM)
            var = _head_sum(yc * yc, first) * (1.0 / HEAD_DIM)
            out = yc * lax.rsqrt(var + RWKV_LN_EPS) * lng_ref[p] + lnb_ref[p]
            y_ref[0, p, sl, :] = (out + bonus_ref[0, p, sl, :]).astype(BF16)
    for p in range(PAIRS):
        state[p] = hs[p]


def _rwkv_scan(gm, hc, r2, y0, bonus, ln_g, ln_b, *, rows=512):
    B, P, TP, _ = r2.shape
    nb = TP // rows
    kern = functools.partial(_rwkv_scan_kernel, rows=rows)
    blk = pl.BlockSpec((1, P, rows, LANE), lambda b, i: (b, 0, i, 0))
    mat = pl.BlockSpec((1, P, 2 * rows, LANE), lambda b, i: (b, 0, i, 0))
    par = pl.BlockSpec((P, 1, LANE), lambda b, i: (0, 0, 0))
    return pl.pallas_call(
        kern, out_shape=jax.ShapeDtypeStruct((B, P, TP, LANE), BF16), grid=(B, nb),
        in_specs=[mat, mat, blk, blk, blk, par, par], out_specs=blk,
        scratch_shapes=[pltpu.VMEM((P, LANE, LANE), F32)],
        compiler_params=pltpu.CompilerParams(
            dimension_semantics=("arbitrary", "arbitrary"), vmem_limit_bytes=VMEM_LIMIT),
        name="rwkv_scan",
    )(gm, hc, r2, y0, bonus, ln_g, ln_b)


def _sb_kernel(*refs, t):
    q_refs, k_refs, v_refs = refs[:PAIRS], refs[PAIRS:2 * PAIRS], refs[2 * PAIRS:3 * PAIRS]
    o_ref = refs[3 * PAIRS]
    qi = pl.program_id(1)
    first = lax.broadcasted_iota(jnp.int32, (t, LANE), 1) < HEAD_DIM
    heads = [(hd // 2, hd % 2) for hd in range(SB_HEADS)]
    qs = []
    for slot, half in heads:
        qp = q_refs[slot][0, 0]
        keep = first if half == 0 else jnp.logical_not(first)
        qs.append(jnp.where(keep, qp, jnp.zeros_like(qp)))
    nu = len(heads)
    qpos = qi * t + lax.broadcasted_iota(jnp.int32, (t, 1), 0)
    kloc = lax.broadcasted_iota(jnp.int32, (1, t), 1)
    upper = (lax.broadcasted_iota(jnp.int32, (t, t), 0)
             > lax.broadcasted_iota(jnp.int32, (t, t), 1)).astype(BF16)

    def sweep(blocks, accs, cs):
        units = []
        for kb, mask in blocks:
            start = pl.multiple_of(kb * t, t)
            kblks = [k_refs[s][0, 0, pl.ds(start, t), :] for s in range(PAIRS)]
            vblks = [v_refs[s][0, 0, pl.ds(start, t), :] for s in range(PAIRS)]
            m = None if mask is None else mask(start + kloc)
            units += [(u, kblks[slot], vblks[slot], m) for u, (slot, _) in enumerate(heads)]
        zs = [_dot_nt(qs[u], kblk) for u, kblk, _, _ in units]
        sps = [jnp.maximum(z, 0.0) + jnp.log2(1.0 + jnp.exp2(-jnp.abs(z))) for z in zs]
        spms = [sp if un[3] is None else jnp.where(un[3], sp, 0.0) for sp, un in zip(sps, units)]
        laters = [_dot(spm.astype(BF16), upper) for spm in spms]
        accs, cs = list(accs), list(cs)
        weights = []
        for (u, _, _, m), z, sp, spm, later in zip(units, zs, sps, spms, laters):
            a = jnp.exp2(z - sp - later - cs[u])
            weights.append(a if m is None else jnp.where(m, a, 0.0))
            cs[u] = cs[u] + jnp.sum(spm, axis=-1, keepdims=True)
        for (u, _, vblk, _), a in zip(units, weights):
            accs[u] = accs[u] + _dot(a.astype(BF16), vblk)
        return accs, cs

    def block(kb, accs, cs, mask):
        return sweep([(kb, mask)], accs, cs)

    def live(cs):
        low = cs[0]
        for c in cs[1:]:
            low = jnp.minimum(low, c)
        return (jnp.min(low) < SB_CUTOFF_LOG2).astype(jnp.int32)

    zero_acc = [jnp.zeros((t, LANE), F32) for _ in range(nu)]
    zero_c = [jnp.zeros((t, 1), F32) for _ in range(nu)]
    near = [(qi, lambda kpos: (kpos >= PAD) & (kpos < qpos))]
    for j in range(1, SB_FUSED):
        near.append((jnp.maximum(qi - j, 0), lambda kpos, j=j: (kpos >= PAD) & (qi >= j)))
    accs, cs = sweep(near, zero_acc, zero_c)

    def cond(carry):
        kb, alive = carry[0], carry[1]
        return (kb >= 1) & (alive > 0)

    def body(carry):
        kb = carry[0]
        accs, cs = block(kb, list(carry[2:2 + nu]), list(carry[2 + nu:]), None)
        return (kb - 1, live(cs), *accs, *cs)

    carry = lax.while_loop(cond, body, (qi - SB_FUSED, live(cs), *accs, *cs))

    def front(carry):
        accs, cs = block(0, list(carry[2:2 + nu]), list(carry[2 + nu:]), lambda kpos: kpos >= PAD)
        return (carry[0], carry[1], *accs, *cs)

    carry = lax.cond((carry[0] == 0) & (carry[1] > 0), front, lambda c: c, carry)
    accs = carry[2:2 + nu]
    for slot in range(PAIRS):
        lo = accs[2 * slot]
        hi = accs[2 * slot + 1] if 2 * slot + 1 < nu else 0.0
        o_ref[0, slot] = jnp.where(first, lo, hi).astype(BF16)


def _sb_attention(sb, *, t=SB_TILE):
    B, _, TP, _ = sb.shape
    kern = functools.partial(_sb_kernel, t=t)
    q_specs = [pl.BlockSpec((1, 1, t, LANE), lambda b, i, s=s: (b, s, i, 0)) for s in range(PAIRS)]
    kv_specs = [pl.BlockSpec((1, 1, TP, LANE), lambda b, i, s=s: (b, s, 0, 0),
                             pipeline_mode=pl.Buffered(1)) for s in range(PAIRS, 3 * PAIRS)]
    return pl.pallas_call(
        kern, out_shape=jax.ShapeDtypeStruct((B, PAIRS, TP, LANE), BF16),
        grid=(B, TP // t),
        in_specs=q_specs + kv_specs,
        out_specs=pl.BlockSpec((1, PAIRS, t, LANE), lambda b, i: (b, 0, i, 0)),
        compiler_params=pltpu.CompilerParams(
            dimension_semantics=("arbitrary",) * 2, vmem_limit_bytes=VMEM_LIMIT),
        name="sb_attention",
    )(*([sb] * (3 * PAIRS)))


def _fox_kernel(shift_ref, fe_ref, q_ref, k_ref, v_ref, o_ref, lo_ref, *, t):
    b = pl.program_id(0)
    h = pl.program_id(1)
    qi = pl.program_id(2)
    q = q_ref[0, 0]
    shift = shift_ref[0]
    thr = FOX_SKIP_LOG2 + 2.0 * shift
    f_q = fe_ref[b, h, jnp.maximum(qi - 1, 0)]
    lo = lax.while_loop(lambda n: (n < qi) & (fe_ref[b, h, n] - f_q > thr), lambda n: n + 1,
                        jnp.where(qi == 0, 0, lo_ref[0]))
    lo_ref[0] = lo
    qpos = qi * t + lax.broadcasted_iota(jnp.int32, (t, 1), 0)
    kloc = lax.broadcasted_iota(jnp.int32, (1, t), 1)

    def block(kb, carry, masked, online):
        acc, m = carry
        start = pl.multiple_of(kb * t, t)
        kblk = k_ref[0, 0, pl.ds(start, t), :]
        vblk = v_ref[0, 0, pl.ds(start, t), :]
        s = _dot_nt(q, kblk)
        if masked:
            kpos = start + kloc
            s = jnp.where((kpos <= qpos) & (kpos >= PAD), s, NEG_INF)
        if online:
            m_new = jnp.maximum(m, jnp.max(s, axis=-1, keepdims=True))
            acc = jnp.exp2(m - m_new) * acc
            m = m_new
        pr = jnp.exp2(s - m)
        return acc + _dot(pr.astype(BF16), vblk), m

    def sweep(online):
        def run(carry):
            carry = lax.cond(lo == 0, lambda c: block(0, c, True, online), lambda c: c, carry)
            start = jnp.maximum(lo, 1)
            n = jnp.maximum(qi - start, 0)

            def group(i, c):
                acc, m = c
                kvs = []
                for j in range(FOX_GROUP):
                    st = pl.multiple_of((start + FOX_GROUP * i + j) * t, t)
                    kvs.append((k_ref[0, 0, pl.ds(st, t), :], v_ref[0, 0, pl.ds(st, t), :]))
                ss = [_dot_nt(q, kblk) for kblk, _ in kvs]
                prs = [jnp.exp2(s - m).astype(BF16) for s in ss]
                for pr, (_, vblk) in zip(prs, kvs):
                    acc = acc + _dot(pr, vblk)
                return acc, m

            grouped = 0 if online else (n // FOX_GROUP) * FOX_GROUP
            if not online:
                carry = lax.fori_loop(0, n // FOX_GROUP, group, carry)
            carry = lax.fori_loop(start + grouped, qi,
                                  lambda kb, c: block(kb, c, False, online), carry)
            return lax.cond(qi > 0, lambda c: block(qi, c, True, online), lambda c: c, carry)
        return run

    acc0 = jnp.zeros((t, LANE), F32)
    fixed = shift <= FOX_FIXED_SHIFT_MAX
    m0 = jnp.where(fixed, jnp.full((t, 1), shift, F32), jnp.full((t, 1), NEG_INF, F32))
    acc, _ = lax.cond(fixed, sweep(False), sweep(True), (acc0, m0))
    lane = lax.broadcasted_iota(jnp.int32, (t, LANE), 1)
    denom = jnp.sum(jnp.where(lane == HEAD_DIM, acc, 0.0), axis=-1, keepdims=True)
    o_ref[0, 0] = (acc / jnp.where(denom > 0.0, denom, 1.0)).astype(BF16)


def _fox_attention(shift, fe, fx, *, t=FOX_TILE):
    B, _, TP, _ = fx.shape
    H = FOX_HEADS
    kern = functools.partial(_fox_kernel, t=t)
    return pl.pallas_call(
        kern, out_shape=jax.ShapeDtypeStruct((B, H, TP, LANE), BF16),
        grid=(B, H, TP // t),
        in_specs=[pl.BlockSpec(memory_space=pltpu.SMEM),
                  pl.BlockSpec(memory_space=pltpu.SMEM),
                  pl.BlockSpec((1, 1, t, LANE), lambda b, h, i: (b, h, i, 0)),
                  pl.BlockSpec((1, 1, TP, LANE), lambda b, h, i: (b, h + H, 0, 0)),
                  pl.BlockSpec((1, 1, TP, LANE), lambda b, h, i: (b, h + 2 * H, 0, 0))],
        out_specs=pl.BlockSpec((1, 1, t, LANE), lambda b, h, i: (b, h, i, 0)),
        scratch_shapes=[pltpu.SMEM((1,), jnp.int32)],
        compiler_params=pltpu.CompilerParams(
            dimension_semantics=("arbitrary",) * 3, vmem_limit_bytes=VMEM_LIMIT),
        name="fox_attention",
    )(shift, fe, fx, fx, fx)


def _merge_kernel(*refs, tm, n_valid, first, n_seq):
    n_stream = 2 if first else 1
    (ya_ref, yb_ref, yc_ref, gate_ref,
     wpa_ref, wpb_ref, wpc_ref, wout_ref, o_ref) = refs[n_stream:]
    i = pl.program_id(1)

    def proj(slots, w_ref):
        return _dot(jnp.concatenate(slots, axis=-1), w_ref[...])

    pa = proj([ya_ref[0, s] for s in range(PAIRS)], wpa_ref)
    pb = proj([yb_ref[0, s] for s in range(PAIRS)], wpb_ref)
    pc = proj([yc_ref[0, s] for s in range(FOX_HEADS)], wpc_ref)
    merged = gate_ref[0, :, 0:D_MODEL].astype(F32) * pa
    merged = merged + gate_ref[0, :, D_MODEL:2 * D_MODEL].astype(F32) * pb
    merged = merged + gate_ref[0, :, 2 * D_MODEL:3 * D_MODEL].astype(F32) * pc
    out = _stream_tile(i, refs[:n_stream], n_seq) + _dot(merged.astype(BF16), wout_ref[...])
    t = i * tm + lax.broadcasted_iota(jnp.int32, (tm, 1), 0)
    o_ref[0] = jnp.where((t >= PAD) & (t < PAD + n_valid), out, 0.0)


def _merge(stream, ya, yb, yc, gates, wpa, wpb, wpc, wout, *, n_valid, n_seq, tm=FRAME_TILE):
    first = len(stream) == 2
    B, _, TP, _ = ya.shape
    D = D_MODEL
    kern = functools.partial(_merge_kernel, tm=tm, n_valid=n_valid, first=first, n_seq=n_seq)
    pair_blk = lambda ns: pl.BlockSpec((1, ns, tm, LANE), lambda b, i: (b, 0, i, 0))
    return pl.pallas_call(
        kern, out_shape=jax.ShapeDtypeStruct((B, TP, D), F32), grid=(B, TP // tm),
        in_specs=_stream_specs(first, tm, n_seq) + [
                  pair_blk(PAIRS),
                  pair_blk(PAIRS), pair_blk(FOX_HEADS),
                  pl.BlockSpec((1, tm, 3 * D), lambda b, i: (b, i, 0)),
                  _const_spec((PAIRS * LANE, D)),
                  _const_spec((PAIRS * LANE, D)),
                  _const_spec((FOX_HEADS * LANE, D)),
                  _const_spec((D, D))],
        out_specs=pl.BlockSpec((1, tm, D), lambda b, i: (b, i, 0)),
        compiler_params=pltpu.CompilerParams(
            dimension_semantics=("arbitrary", "arbitrary"), vmem_limit_bytes=VMEM_LIMIT),
        name="merge",
    )(*stream, ya, yb, yc, gates, wpa, wpb, wpc, wout)


def _moe_kernel(*refs, tm, n_valid, last):
    n_stream = 2 if last else 1
    g_ref, wr_ref, br_ref, wg_ref, wu_ref, wd_ref, o_ref, he_ref, ys_ref = refs[n_stream:]
    i = pl.program_id(1)
    x = jnp.concatenate([r[0] for r in refs[:n_stream]], axis=0)
    ms = jnp.mean(x * x, axis=-1, keepdims=True)
    n = x * lax.rsqrt(ms + NORM_EPS) * g_ref[...]

    lg = _dot3(n, wr_ref[...]) + br_ref[...]
    lane = lax.broadcasted_iota(jnp.int32, (tm, LANE), 1)
    big = jnp.int32(LANE)
    is_group = lane < N_GROUPS
    gl = jnp.where(is_group, lg, -jnp.inf)
    gmax = jnp.max(gl, axis=-1, keepdims=True)
    g_idx = jnp.min(jnp.where(is_group & (gl == gmax), lane, big), axis=-1, keepdims=True)
    p_group = 1.0 / jnp.sum(jnp.where(is_group, jnp.exp(gl - gmax), 0.0), axis=-1, keepdims=True)
    lo = N_GROUPS + g_idx * EXPERTS_PER_GROUP
    in_grp = (lane >= lo) & (lane < lo + EXPERTS_PER_GROUP)
    el = jnp.where(in_grp, lg, -jnp.inf)
    top1 = jnp.max(el, axis=-1, keepdims=True)
    i1 = jnp.min(jnp.where(in_grp & (el == top1), lane, big), axis=-1, keepdims=True)
    rest = in_grp & (lane != i1)
    el2 = jnp.where(rest, lg, -jnp.inf)
    top2 = jnp.max(el2, axis=-1, keepdims=True)
    i2 = jnp.min(jnp.where(rest & (el2 == top2), lane, big), axis=-1, keepdims=True)
    e2 = jnp.exp(top2 - top1)
    p1 = 1.0 / (1.0 + e2)
    p2 = e2 / (1.0 + e2)
    comb = p_group * (jnp.where(lane == i1, p1, 0.0) + jnp.where(lane == i2, p2, 0.0))

    n16 = n.astype(BF16)

    onehot = (lane == g_idx).astype(BF16)
    earlier = (lax.broadcasted_iota(jnp.int32, (tm, tm), 1)
               < lax.broadcasted_iota(jnp.int32, (tm, tm), 0)).astype(BF16)
    cnt = _dot(earlier, onehot)
    rank = jnp.sum(jnp.where(lane == g_idx, cnt, 0.0), axis=-1, keepdims=True).astype(jnp.int32)
    n_pass = (jnp.max(rank) + MOE_CAP) // MOE_CAP
    comb16 = comb.astype(BF16)
    slot_lane = lax.broadcasted_iota(jnp.int32, (tm, N_GROUPS * MOE_CAP), 1)

    def one_pass(p, y):
        r = rank - p * MOE_CAP
        slot = jnp.where((r >= 0) & (r < MOE_CAP), g_idx * MOE_CAP + r, -1)
        place = (slot_lane == slot).astype(BF16)
        xs = _dot_tn(place, n16).astype(BF16)
        cw = _dot_tn(place, comb16)
        for g in range(N_GROUPS):
            rows = slice(g * MOE_CAP, (g + 1) * MOE_CAP)
            for e in range(EXPERTS_PER_GROUP):
                ee = g * EXPERTS_PER_GROUP + e
                gate = _dot(xs[rows], wg_ref[ee])
                up = _dot(xs[rows], wu_ref[ee])
                c_e = cw[rows, N_GROUPS + ee:N_GROUPS + ee + 1]
                he_ref[:, e * D_EXPERT:(e + 1) * D_EXPERT] = (
                    gate * _sigmoid(gate) * up * c_e).astype(BF16)
            gw = EXPERTS_PER_GROUP * D_EXPERT
            ys_ref[rows, :] = _dot(he_ref[...], wd_ref[g * gw:(g + 1) * gw, :]).astype(BF16)
        return y + _dot(place, ys_ref[...])

    out = x + lax.fori_loop(0, n_pass, one_pass, jnp.zeros((tm, D_MODEL), F32))
    if last:
        o_ref[0] = out
    else:
        t = i * tm + lax.broadcasted_iota(jnp.int32, (tm, 1), 0)
        o_ref[0] = jnp.where((t >= PAD) & (t < PAD + n_valid), out, 0.0)


def _moe(h, g2, wr, br, wg, wu, wd, *, n_valid, n_seq, last):
    B, TP, D = h.shape
    tm = MOE_TILE
    kern = functools.partial(_moe_kernel, tm=tm, n_valid=n_valid, last=last)
    EW = N_EXPERTS * D_EXPERT
    if last:
        half = tm // FRAME_TILE
        assert half == 2 and n_seq % half == 0
        rows_out = n_seq * FRAME_TILE
        stream_specs = [pl.BlockSpec((1, FRAME_TILE, D), lambda b, i, j=j: (b, half * i + 1 + j, 0))
                        for j in range(half)]
        stream = [h] * half
    else:
        rows_out = TP
        stream_specs = [pl.BlockSpec((1, tm, D), lambda b, i: (b, i, 0))]
        stream = [h]
    return pl.pallas_call(
        kern, out_shape=jax.ShapeDtypeStruct((B, rows_out, D), F32), grid=(B, rows_out // tm),
        in_specs=stream_specs + [
                  _const_spec((1, D)), _const_spec((D, LANE)), _const_spec((1, LANE)),
                  _const_spec((N_EXPERTS, D, D_EXPERT)), _const_spec((N_EXPERTS, D, D_EXPERT)),
                  _const_spec((EW, D))],
        out_specs=pl.BlockSpec((1, tm, D), lambda b, i: (b, i, 0)),
        scratch_shapes=[pltpu.VMEM((MOE_CAP, EXPERTS_PER_GROUP * D_EXPERT), BF16),
                        pltpu.VMEM((N_GROUPS * MOE_CAP, D), BF16)],
        compiler_params=pltpu.CompilerParams(
            dimension_semantics=("arbitrary", "arbitrary"), vmem_limit_bytes=VMEM_LIMIT),
        name="moe",
    )(*stream, g2, wr, br, wg, wu, wd)


def _transpose_kernel(x_ref, o_ref):
    o_ref[...] = x_ref[...].T.astype(BF16)


def _transpose_to_bf16(a, *, tn=LANE):
    N, K = a.shape
    assert N % tn == 0
    return pl.pallas_call(
        _transpose_kernel, out_shape=jax.ShapeDtypeStruct((K, N), BF16), grid=(N // tn,),
        in_specs=[pl.BlockSpec((tn, K), lambda i: (i, 0))],
        out_specs=pl.BlockSpec((K, tn), lambda i: (0, i)),
        compiler_params=pltpu.CompilerParams(dimension_semantics=("arbitrary",)),
        name="transpose_cast",
    )(a)


def _pack_w_in(w_in):
    wt = jnp.swapaxes(w_in, 0, 1)
    rest = wt[SHIFT_W:]
    rows = [wt[:SHIFT_W]]
    off = 0
    for heads in (SB_HEADS,) * 3 + (FOX_HEADS,) * 3:
        rows.append(jnp.pad(rest[off:off + heads * HEAD_DIM],
                            ((0, PAIRS * LANE - heads * HEAD_DIM), (0, 0))))
        off += heads * HEAD_DIM
    rows.append(jnp.pad(rest[off:off + FOX_HEADS], ((0, LANE - FOX_HEADS), (0, 0))))
    rows.append(rest[off + FOX_HEADS:])
    return _transpose_to_bf16(jnp.concatenate(rows, axis=0))


def _per_pair_cols(w):
    return w.reshape(w.shape[0], PAIRS, LANE).transpose(1, 0, 2)


def kernel(x, meta_tokens, norm1_g, w_in, rwkv_mu, rwkv_w_up, rwkv_w0, rwkv_a_up, rwkv_a0,
           rwkv_k_k, rwkv_k_a, rwkv_r_k, rwkv_ln_g, rwkv_ln_b, fox_f_b, fox_q_g, fox_k_g,
           w_p_rwkv, w_p_sb, w_p_fox, w_out, norm2_g, moe_wg, moe_bg, moe_we, moe_be,
           moe_w_gate, moe_w_up, moe_w_down):
    B, S, D = x.shape
    depth = w_in.shape[0]
    L = N_META + S
    TP = -(-(PAD + L) // ROW_TILE) * ROW_TILE
    assert S % FRAME_TILE == 0
    n_seq = S // FRAME_TILE
    meta = jnp.broadcast_to(meta_tokens[None].astype(x.dtype), (B, N_META, D))
    stream = (x, jnp.concatenate([jnp.zeros((B, PAD, D), x.dtype), meta], axis=1))
    H = RWKV_HEADS
    EW = N_EXPERTS * D_EXPERT
    for l in range(depth):
        fb = jnp.zeros((1, LANE), F32).at[0, :FOX_HEADS].set(fox_f_b[l])
        rkv, wdad, sb, fx, fend, gates = _inproj(
            stream, norm1_g[l][None], _pack_w_in(w_in[l]), rwkv_mu[l][None], fb,
            jnp.tile(fox_q_g[l], 2)[None], jnp.tile(fox_k_g[l], 2)[None], TP=TP, n_seq=n_seq)
        hv = lambda p: p.reshape(PAIRS, 1, LANE)
        gm, hc, r2, y0, bonus = _rwkv_chunks(
            rkv, wdad, _per_pair_cols(rwkv_w_up[l]), hv(rwkv_w0[l]),
            _per_pair_cols(rwkv_a_up[l]), hv(rwkv_a0[l]),
            hv(rwkv_k_k[l]), hv(rwkv_k_a[l]), hv(rwkv_r_k[l]))
        ya = _rwkv_scan(gm, hc, r2, y0, bonus, hv(rwkv_ln_g[l]), hv(rwkv_ln_b[l]))
        yb = _sb_attention(sb)
        shift = (8.0 * LOG2E * jnp.max(jnp.abs(fox_q_g[l])) * jnp.max(jnp.abs(fox_k_g[l]))).reshape(1)
        step = FOX_TILE // INPROJ_TILE
        fe = jnp.transpose(fend[:, step - 1::step, 0, :FOX_HEADS], (0, 2, 1))
        yc = _fox_attention(shift, fe, fx)
        pc = jnp.pad(w_p_fox[l].reshape(FOX_HEADS, HEAD_DIM, D),
                     ((0, 0), (0, LANE - HEAD_DIM), (0, 0))).reshape(FOX_HEADS * LANE, D).astype(BF16)
        pp = lambda w, nh: jnp.pad(w, ((0, PAIRS * LANE - nh * HEAD_DIM), (0, 0))).astype(BF16)
        h = _merge(stream, ya, yb, yc, gates, pp(w_p_rwkv[l], H), pp(w_p_sb[l], SB_HEADS),
                   pc, w_out[l].astype(BF16), n_valid=L, n_seq=n_seq)
        wr = jnp.zeros((D, LANE), F32).at[:, :N_GROUPS].set(moe_wg[l])
        wr = wr.at[:, N_GROUPS:N_GROUPS + N_EXPERTS].set(moe_we[l])
        br = jnp.zeros((1, LANE), F32).at[0, :N_GROUPS].set(moe_bg[l])
        br = br.at[0, N_GROUPS:N_GROUPS + N_EXPERTS].set(moe_be[l])
        wg = moe_w_gate[l].astype(BF16)
        wu = moe_w_up[l].astype(BF16)
        wd = moe_w_down[l].reshape(EW, D).astype(BF16)
        h = _moe(h, norm2_g[l][None], wr, br, wg, wu, wd, n_valid=L, n_seq=n_seq,
                 last=l == depth - 1)
        stream = (h,)
    return h
```

```python
import functools
import math

import jax
import jax.numpy as jnp
from jax import lax
from jax.experimental import pallas as pl
from jax.experimental.pallas import tpu as pltpu

D_MODEL = 1024
HEAD_DIM = 64
N_META = 16
FRAME_TILE = 256
MOE_TILE = 512
MOE_CAP = 192
PAD = FRAME_TILE - N_META
RWKV_HEADS = 6
SB_HEADS = 5
FOX_HEADS = 5
RWKV_W = RWKV_HEADS * HEAD_DIM
DECAY_RANK = 64
ICLR_RANK = 64
SHIFT_W = 3 * RWKV_W + DECAY_RANK + ICLR_RANK
N_GROUPS = 4
EXPERTS_PER_GROUP = 4
N_EXPERTS = 16
D_EXPERT = 256
NORM_EPS = 1e-6
RWKV_LN_EPS = 64e-5
NEG_INF = -1e30
ATT_SCALE = 1.0 / math.sqrt(HEAD_DIM)

LANE = 128
ROW_TILE = 512
PAIRS = 3
QKV_W = 3 * PAIRS * LANE
SEG_SB = SHIFT_W
SEG_FOX = SEG_SB + QKV_W
SEG_GATE = SEG_FOX + QKV_W + LANE
D_IN_PAD = SEG_GATE + 3 * D_MODEL
SB_CUTOFF_LOG2 = 160.0
SB_TILE = 128
SB_FUSED = 3
LOG2E = 1.4426950408889634
FOX_FIXED_SHIFT_MAX = 50.0
FOX_SKIP_LOG2 = 150.0
FOX_TILE = 512
FOX_GROUP = 4
INPROJ_TILE = FRAME_TILE

CHUNK = 64
VMEM_LIMIT = 56 * 1024 * 1024

F32 = jnp.float32
BF16 = jnp.bfloat16


def _log_sigmoid(x):
    return jnp.minimum(x, 0.0) - jnp.log1p(jnp.exp(-jnp.abs(x)))


def _sigmoid(x):
    return 1.0 / (1.0 + jnp.exp(-x))


def _dot(a, b, **kw):
    return jnp.dot(a, b, preferred_element_type=F32, **kw)


def _dot_nt(a, b):
    return lax.dot_general(a, b, (((1,), (1,)), ((), ())), preferred_element_type=F32)


def _dot_tn(a, b):
    return lax.dot_general(a, b, (((0,), (0,)), ((), ())), preferred_element_type=F32)


def _dot3(a, b):
    ah = a.astype(BF16)
    al = (a - ah.astype(F32)).astype(BF16)
    bh = b.astype(BF16)
    bl = (b - bh.astype(F32)).astype(BF16)
    return _dot(ah, bh) + (_dot(ah, bl) + _dot(al, bh))


def _stream_specs(first, tm, n_seq):
    if not first:
        return [pl.BlockSpec((1, tm, D_MODEL), lambda b, i: (b, i, 0))]
    return [pl.BlockSpec((1, tm, D_MODEL), lambda b, i: (b, jnp.clip(i - 1, 0, n_seq - 1), 0)),
            pl.BlockSpec((1, tm, D_MODEL), lambda b, i: (b, 0, 0))]


def _stream_tile(i, refs, n_seq):
    if len(refs) == 1:
        return refs[0][0]
    x_ref, head_ref = refs
    return jnp.where(i == 0, head_ref[0], jnp.where(i <= n_seq, x_ref[0], 0.0))


def _const_spec(shape):
    n = len(shape)
    return pl.BlockSpec(shape, lambda *_: (0,) * n, pipeline_mode=pl.Buffered(1))


def _inproj_kernel(*refs, tm, first, n_seq):
    n_stream = 2 if first else 1
    (g_ref, w_ref, mu_ref, fb_ref, fqg_ref, fkg_ref,
     rkv_ref, wdad_ref, sb_ref, fx_ref, fend_ref, gate_ref, carry_u, carry_f) = refs[n_stream:]
    i = pl.program_id(1)

    @pl.when(i == 0)
    def _():
        carry_u[...] = jnp.zeros_like(carry_u)
        carry_f[...] = jnp.zeros_like(carry_f)

    x = _stream_tile(i, refs[:n_stream], n_seq)
    ms = jnp.mean(x * x, axis=-1, keepdims=True)
    n = (x * lax.rsqrt(ms + NORM_EPS) * g_ref[...]).astype(BF16)
    row = lax.broadcasted_iota(jnp.int32, (tm, 1), 0)

    us = _dot(n, w_ref[:, 0:SHIFT_W])
    prev = pltpu.roll(us, 1, axis=0)
    prev = jnp.where(row == 0, carry_u[...], prev)
    carry_u[...] = us[tm - 1:tm, :]
    ush = us + (prev - us) * mu_ref[...]
    for j in range(3 * PAIRS):
        rkv_ref[0, j] = ush[:, j * LANE:(j + 1) * LANE]
    wdad_ref[0] = ush[:, 3 * RWKV_W:SHIFT_W]

    usb = _dot(n, w_ref[:, SEG_SB:SEG_SB + QKV_W])
    for j in range(3 * PAIRS):
        piece = usb[:, j * LANE:(j + 1) * LANE]
        if j < PAIRS:
            piece = piece * (ATT_SCALE * LOG2E)
        sb_ref[0, j] = piece.astype(BF16)

    uf = _dot(n, w_ref[:, SEG_FOX:SEG_FOX + QKV_W + LANE])
    lane = lax.broadcasted_iota(jnp.int32, (tm, LANE), 1)
    first = lane < HEAD_DIM
    t_glob = i * tm + row
    logf = _log_sigmoid(uf[:, QKV_W:QKV_W + LANE] + fb_ref[...])
    logf = jnp.where((lane < FOX_HEADS) & (t_glob >= PAD), logf, 0.0)
    tri = (lax.broadcasted_iota(jnp.int32, (tm, tm), 0)
           >= lax.broadcasted_iota(jnp.int32, (tm, tm), 1)).astype(BF16)
    f_hi = logf.astype(BF16)
    f_mid = (logf - f_hi.astype(F32)).astype(BF16)
    f_lo = (logf - f_hi.astype(F32) - f_mid.astype(F32)).astype(BF16)
    cum = _dot(tri, f_hi) + _dot(tri, f_mid) + _dot(tri, f_lo) + carry_f[...]
    carry_f[...] = cum[tm - 1:tm, :]
    fend_ref[0, 0] = cum[tm - 1:tm, :] * LOG2E

    def split3(hd):
        f2 = jnp.sum(jnp.where(lane == hd, cum, 0.0), axis=-1, keepdims=True) * LOG2E
        hi = f2.astype(BF16).astype(F32)
        mid = (f2 - hi).astype(BF16).astype(F32)
        return hi, mid, f2 - hi - mid

    def tail_cols(vals):
        out = jnp.zeros((tm, LANE), F32)
        for o, val in enumerate(vals):
            out = jnp.where(lane == HEAD_DIM + o, val, out)
        return out

    splits = [split3(hd) for hd in range(FOX_HEADS)]
    for j in range(3 * PAIRS):
        piece = uf[:, j * LANE:(j + 1) * LANE]
        kind = j // PAIRS
        if kind < 2:
            gain = fqg_ref[...] if kind == 0 else fkg_ref[...]
            sq = piece * piece
            ms0 = jnp.sum(jnp.where(first, sq, 0.0), axis=-1, keepdims=True) * (1.0 / HEAD_DIM)
            ms1 = jnp.sum(jnp.where(first, 0.0, sq), axis=-1, keepdims=True) * (1.0 / HEAD_DIM)
            inv = jnp.where(first, lax.rsqrt(ms0 + NORM_EPS), lax.rsqrt(ms1 + NORM_EPS))
            piece = piece * inv * gain
            if kind == 0:
                piece = piece * (ATT_SCALE * LOG2E)
        swapped = pltpu.roll(piece, HEAD_DIM, axis=1)
        for half in range(2):
            hd = 2 * (j % PAIRS) + half
            if hd >= FOX_HEADS:
                continue
            if kind == 2:
                extra = tail_cols([1.0])
            else:
                hi, mid, lo = splits[hd]
                extra = tail_cols([hi, mid, lo, 1.0, 1.0, 1.0] if kind == 0 else
                                  [1.0, 1.0, 1.0, jnp.where(t_glob >= PAD, -hi, NEG_INF), -mid, -lo])
            body = piece if half == 0 else swapped
            fx_ref[0, kind * FOX_HEADS + hd] = jnp.where(first, body, extra).astype(BF16)

    ug = _dot(n, w_ref[:, SEG_GATE:D_IN_PAD])
    gate_ref[0] = _sigmoid(ug).astype(BF16)


def _inproj(stream, g1, w_in_p, mu, fb, fqg, fkg, *, TP, n_seq, tm=INPROJ_TILE):
    first = len(stream) == 2
    B, D = stream[0].shape[0], D_MODEL
    nb = TP // tm
    kern = functools.partial(_inproj_kernel, tm=tm, first=first, n_seq=n_seq)
    out_shape = (
        jax.ShapeDtypeStruct((B, 3 * PAIRS, TP, LANE), F32),
        jax.ShapeDtypeStruct((B, TP, LANE), F32),
        jax.ShapeDtypeStruct((B, 3 * PAIRS, TP, LANE), BF16),
        jax.ShapeDtypeStruct((B, 3 * FOX_HEADS, TP, LANE), BF16),
        jax.ShapeDtypeStruct((B, nb, 1, LANE), F32),
        jax.ShapeDtypeStruct((B, TP, 3 * D_MODEL), BF16),
    )
    in_specs = _stream_specs(first, tm, n_seq) + [
        _const_spec((1, D)),
        _const_spec((D, D_IN_PAD)),
        _const_spec((1, SHIFT_W)),
        _const_spec((1, LANE)),
        _const_spec((1, LANE)),
        _const_spec((1, LANE)),
    ]
    out_specs = (
        pl.BlockSpec((1, 3 * PAIRS, tm, LANE), lambda b, i: (b, 0, i, 0)),
        pl.BlockSpec((1, tm, LANE), lambda b, i: (b, i, 0)),
        pl.BlockSpec((1, 3 * PAIRS, tm, LANE), lambda b, i: (b, 0, i, 0)),
        pl.BlockSpec((1, 3 * FOX_HEADS, tm, LANE), lambda b, i: (b, 0, i, 0)),
        pl.BlockSpec((1, 1, 1, LANE), lambda b, i: (b, i, 0, 0)),
        pl.BlockSpec((1, tm, 3 * D_MODEL), lambda b, i: (b, i, 0)),
    )
    return pl.pallas_call(
        kern, out_shape=out_shape, grid=(B, nb), in_specs=in_specs, out_specs=out_specs,
        scratch_shapes=[pltpu.VMEM((1, SHIFT_W), F32), pltpu.VMEM((1, LANE), F32)],
        compiler_params=pltpu.CompilerParams(
            dimension_semantics=("arbitrary", "arbitrary"), vmem_limit_bytes=VMEM_LIMIT),
        name="inproj",
    )(*stream, g1, w_in_p, mu, fb, fqg, fkg)


def _bdot(a, b):
    return lax.dot_general(a, b, (((2,), (1,)), ((0,), (0,))), preferred_element_type=F32)


def _bdot_nt(a, b):
    return lax.dot_general(a, b, (((2,), (2,)), ((0,), (0,))), preferred_element_type=F32)


def _bdot_tn(a, b):
    return lax.dot_general(a, b, (((1,), (1,)), ((0,), (0,))), preferred_element_type=F32)


def _head_sum(x, first):
    s0 = jnp.sum(jnp.where(first, x, 0.0), axis=-1, keepdims=True)
    s1 = jnp.sum(jnp.where(first, 0.0, x), axis=-1, keepdims=True)
    return jnp.where(first, s0, s1)


def _rwkv_chunk_kernel(r_ref, k_ref, v_ref, wdad_ref, wup_ref, w0_ref, aup_ref, a0_ref,
                       kk_ref, ka_ref, rk_ref,
                       g_ref, hc_ref, r2_ref, y0_ref, bonus_ref, *, rows):
    nc = rows // CHUNK
    r = r_ref[0, 0]
    k = k_ref[0, 0]
    v = v_ref[0, 0]
    wd = wdad_ref[0][:, 0:DECAY_RANK]
    ad = wdad_ref[0][:, DECAY_RANK:DECAY_RANK + ICLR_RANK]
    first = lax.broadcasted_iota(jnp.int32, (rows, LANE), 1) < HEAD_DIM

    pre = w0_ref[0] + _dot3(jnp.tanh(wd), wup_ref[0])
    lw = -jnp.exp(_log_sigmoid(pre) - 0.5)
    iclr = _sigmoid(a0_ref[0] + _dot3(ad, aup_ref[0]))
    kk = k * kk_ref[0]
    kk = kk / jnp.maximum(jnp.sqrt(_head_sum(kk * kk, first)), 1e-12)
    k2 = k * (1.0 + (iclr - 1.0) * ka_ref[0])
    b = kk * iclr
    bonus_ref[0, 0] = _head_sum(r * k2 * rk_ref[0], first) * v

    to3 = lambda x: x.reshape(nc, CHUNK, LANE)
    ri = lax.broadcasted_iota(jnp.int32, (nc, CHUNK, CHUNK), 1)
    ci = lax.broadcasted_iota(jnp.int32, (nc, CHUNK, CHUNK), 2)
    low_incl = ri >= ci
    low_strict = ri > ci
    first3 = lax.broadcasted_iota(jnp.int32, (nc, CHUNK, LANE), 2) < HEAD_DIM

    lw3 = to3(lw)
    tri = low_incl.astype(BF16)
    lw_hi = lw3.astype(BF16)
    cum = _bdot(tri, lw_hi) + _bdot(tri, (lw3 - lw_hi.astype(F32)).astype(BF16))
    cum_end = cum[:, CHUNK - 1:CHUNK, :]
    e_neg = jnp.exp(-cum)
    at = to3(-kk) * jnp.exp(cum - lw3)
    rt = to3(r) * jnp.exp(cum)
    bt = (to3(b) * e_neg).astype(BF16)
    kt = (to3(k2) * e_neg).astype(BF16)
    e_rem = jnp.exp(cum_end - cum)
    bq = (to3(b) * e_rem).astype(BF16)
    kq = (to3(k2) * e_rem).astype(BF16)
    vv = to3(v).astype(BF16)

    heads = range(2)
    sels = [first3, jnp.logical_not(first3)]
    lhs = [jnp.concatenate([jnp.where(sels[hd], at, 0.0), jnp.where(sels[hd], rt, 0.0)],
                           axis=1).astype(BF16) for hd in heads]
    mb = [_bdot_nt(lhs[hd], bt) for hd in heads]
    mk = [_bdot_nt(lhs[hd], kt) for hd in heads]
    m_ak = [jnp.where(low_strict, mk[hd][:, :CHUNK], 0.0).astype(BF16) for hd in heads]
    m_rb = [jnp.where(low_incl, mb[hd][:, CHUNK:], 0.0).astype(BF16) for hd in heads]
    m_rk = [jnp.where(low_incl, mk[hd][:, CHUNK:], 0.0).astype(BF16) for hd in heads]
    p = [jnp.where(low_strict, mb[hd][:, :CHUNK], 0.0) for hd in heads]
    xs = [jnp.concatenate([at, _bdot(m_ak[hd], vv)], axis=-1) for hd in heads]
    for j in range(6):
        p16 = [p[hd].astype(BF16) for hd in heads]
        xs = [xs[hd] + _bdot(p16[hd], xs[hd].astype(BF16)) for hd in heads]
        if j < 5:
            p = [_bdot(p16[hd], p16[hd]) for hd in heads]
    ru = [_bdot(m_rb[hd], xs[hd].astype(BF16)) for hd in heads]
    r2s = [rt + ru[hd][..., :LANE] for hd in heads]
    y0s = [ru[hd][..., LANE:] + _bdot(m_rk[hd], vv) for hd in heads]

    first3w = jnp.concatenate([first3, first3], axis=-1)
    x = jnp.where(first3w, xs[0], xs[1])
    r2_ref[0, 0] = jnp.where(first3, r2s[0], r2s[1]).reshape(rows, LANE)
    y0_ref[0, 0] = jnp.where(first3, y0s[0], y0s[1]).reshape(rows, LANE)

    pg = _bdot_tn(bq, x.astype(BF16))
    ph = pg[..., LANE:] + _bdot_tn(kq, vv)
    rr = lax.broadcasted_iota(jnp.int32, (nc, LANE, LANE), 1)
    cc = lax.broadcasted_iota(jnp.int32, (nc, LANE, LANE), 2)
    same_head = (rr < HEAD_DIM) == (cc < HEAD_DIM)
    g = jnp.where(same_head, pg[..., :LANE], 0.0) + jnp.where(rr == cc, jnp.exp(cum_end), 0.0)
    g_ref[0, 0] = g.reshape(nc * LANE, LANE)
    hc_ref[0, 0] = jnp.where(same_head, ph, 0.0).reshape(nc * LANE, LANE)


def _rwkv_chunks(rkv, wdad, wup, w0, aup, a0, k_k, k_a, r_k):
    B, _, TP, _ = rkv.shape
    rows = next(r for r in (1536, 768, 512, 256) if TP % r == 0)
    nb = TP // rows
    kern = functools.partial(_rwkv_chunk_kernel, rows=rows)
    slot_spec = lambda off: pl.BlockSpec((1, 1, rows, LANE), lambda b, p, i: (b, p + off, i, 0))
    par_mat = pl.BlockSpec((1, DECAY_RANK, LANE), lambda b, p, i: (p, 0, 0))
    par_vec = pl.BlockSpec((1, 1, LANE), lambda b, p, i: (p, 0, 0))
    row_out = jax.ShapeDtypeStruct((B, PAIRS, TP, LANE), F32)
    mat_out = jax.ShapeDtypeStruct((B, PAIRS, 2 * TP, LANE), F32)
    row_spec = pl.BlockSpec((1, 1, rows, LANE), lambda b, p, i: (b, p, i, 0))
    mat_spec = pl.BlockSpec((1, 1, 2 * rows, LANE), lambda b, p, i: (b, p, i, 0))
    return pl.pallas_call(
        kern, out_shape=(mat_out, mat_out, row_out, row_out, row_out), grid=(B, PAIRS, nb),
        in_specs=[slot_spec(0), slot_spec(PAIRS), slot_spec(2 * PAIRS),
                  pl.BlockSpec((1, rows, LANE), lambda b, p, i: (b, i, 0)),
                  par_mat, par_vec, par_mat, par_vec, par_vec, par_vec, par_vec],
        out_specs=(mat_spec, mat_spec, row_spec, row_spec, row_spec),
        compiler_params=pltpu.CompilerParams(
            dimension_semantics=("arbitrary",) * 3, vmem_limit_bytes=VMEM_LIMIT),
        name="rwkv_chunks",
    )(rkv, rkv, rkv, wdad, wup, w0, aup, a0, k_k, k_a, r_k)


def _rwkv_scan_kernel(g_ref, hc_ref, r2_ref, y0_ref, bonus_ref, lng_ref, lnb_ref, y_ref,
                      state, *, rows):
    i = pl.program_id(1)

    @pl.when(i == 0)
    def _():
        state[...] = jnp.zeros_like(state)

    first = lax.broadcasted_iota(jnp.int32, (CHUNK, LANE), 1) < HEAD_DIM
    hs = [state[p] for p in range(PAIRS)]
    for c in range(rows // CHUNK):
        sl = slice(c * CHUNK, (c + 1) * CHUNK)
        sm = slice(c * LANE, (c + 1) * LANE)
        for p in range(PAIRS):
            y = _dot3(r2_ref[0, p, sl, :], hs[p]) + y0_ref[0, p, sl, :]
            hs[p] = _dot3(g_ref[0, p, sm, :], hs[p]) + hc_ref[0, p, sm, :]
            yc = y - _head_sum(y, first) * (1.0 / HEAD_DIM)
            var = _head_sum(yc * yc, first) * (1.0 / HEAD_DIM)
            out = yc * lax.rsqrt(var + RWKV_LN_EPS) * lng_ref[p] + lnb_ref[p]
            y_ref[0, p, sl, :] = (out + bonus_ref[0, p, sl, :]).astype(BF16)
    for p in range(PAIRS):
        state[p] = hs[p]


def _rwkv_scan(gm, hc, r2, y0, bonus, ln_g, ln_b, *, rows=512):
    B, P, TP, _ = r2.shape
    nb = TP // rows
    kern = functools.partial(_rwkv_scan_kernel, rows=rows)
    blk = pl.BlockSpec((1, P, rows, LANE), lambda b, i: (b, 0, i, 0))
    mat = pl.BlockSpec((1, P, 2 * rows, LANE), lambda b, i: (b, 0, i, 0))
    par = pl.BlockSpec((P, 1, LANE), lambda b, i: (0, 0, 0))
    return pl.pallas_call(
        kern, out_shape=jax.ShapeDtypeStruct((B, P, TP, LANE), BF16), grid=(B, nb),
        in_specs=[mat, mat, blk, blk, blk, par, par], out_specs=blk,
        scratch_shapes=[pltpu.VMEM((P, LANE, LANE), F32)],
        compiler_params=pltpu.CompilerParams(
            dimension_semantics=("arbitrary", "arbitrary"), vmem_limit_bytes=VMEM_LIMIT),
        name="rwkv_scan",
    )(gm, hc, r2, y0, bonus, ln_g, ln_b)


def _sb_kernel(*refs, t):
    q_refs, k_refs, v_refs = refs[:PAIRS], refs[PAIRS:2 * PAIRS], refs[2 * PAIRS:3 * PAIRS]
    o_ref = refs[3 * PAIRS]
    qi = pl.program_id(1)
    first = lax.broadcasted_iota(jnp.int32, (t, LANE), 1) < HEAD_DIM
    heads = [(hd // 2, hd % 2) for hd in range(SB_HEADS)]
    qs = []
    for slot, half in heads:
        qp = q_refs[slot][0, 0]
        keep = first if half == 0 else jnp.logical_not(first)
        qs.append(jnp.where(keep, qp, jnp.zeros_like(qp)))
    nu = len(heads)
    qpos = qi * t + lax.broadcasted_iota(jnp.int32, (t, 1), 0)
    kloc = lax.broadcasted_iota(jnp.int32, (1, t), 1)
    upper = (lax.broadcasted_iota(jnp.int32, (t, t), 0)
             > lax.broadcasted_iota(jnp.int32, (t, t), 1)).astype(BF16)

    def sweep(blocks, accs, cs):
        units = []
        for kb, mask in blocks:
            start = pl.multiple_of(kb * t, t)
            kblks = [k_refs[s][0, 0, pl.ds(start, t), :] for s in range(PAIRS)]
            vblks = [v_refs[s][0, 0, pl.ds(start, t), :] for s in range(PAIRS)]
            m = None if mask is None else mask(start + kloc)
            units += [(u, kblks[slot], vblks[slot], m) for u, (slot, _) in enumerate(heads)]
        zs = [_dot_nt(qs[u], kblk) for u, kblk, _, _ in units]
        sps = [jnp.maximum(z, 0.0) + jnp.log2(1.0 + jnp.exp2(-jnp.abs(z))) for z in zs]
        spms = [sp if un[3] is None else jnp.where(un[3], sp, 0.0) for sp, un in zip(sps, units)]
        laters = [_dot(spm.astype(BF16), upper) for spm in spms]
        accs, cs = list(accs), list(cs)
        weights = []
        for (u, _, _, m), z, sp, spm, later in zip(units, zs, sps, spms, laters):
            a = jnp.exp2(z - sp - later - cs[u])
            weights.append(a if m is None else jnp.where(m, a, 0.0))
            cs[u] = cs[u] + jnp.sum(spm, axis=-1, keepdims=True)
        for (u, _, vblk, _), a in zip(units, weights):
            accs[u] = accs[u] + _dot(a.astype(BF16), vblk)
        return accs, cs

    def block(kb, accs, cs, mask):
        return sweep([(kb, mask)], accs, cs)

    def live(cs):
        low = cs[0]
        for c in cs[1:]:
            low = jnp.minimum(low, c)
        return (jnp.min(low) < SB_CUTOFF_LOG2).astype(jnp.int32)

    zero_acc = [jnp.zeros((t, LANE), F32) for _ in range(nu)]
    zero_c = [jnp.zeros((t, 1), F32) for _ in range(nu)]
    near = [(qi, lambda kpos: (kpos >= PAD) & (kpos < qpos))]
    for j in range(1, SB_FUSED):
        near.append((jnp.maximum(qi - j, 0), lambda kpos, j=j: (kpos >= PAD) & (qi >= j)))
    accs, cs = sweep(near, zero_acc, zero_c)

    def cond(carry):
        kb, alive = carry[0], carry[1]
        return (kb >= 1) & (alive > 0)

    def body(carry):
        kb = carry[0]
        accs, cs = block(kb, list(carry[2:2 + nu]), list(carry[2 + nu:]), None)
        return (kb - 1, live(cs), *accs, *cs)

    carry = lax.while_loop(cond, body, (qi - SB_FUSED, live(cs), *accs, *cs))

    def front(carry):
        accs, cs = block(0, list(carry[2:2 + nu]), list(carry[2 + nu:]), lambda kpos: kpos >= PAD)
        return (carry[0], carry[1], *accs, *cs)

    carry = lax.cond((carry[0] == 0) & (carry[1] > 0), front, lambda c: c, carry)
    accs = carry[2:2 + nu]
    for slot in range(PAIRS):
        lo = accs[2 * slot]
        hi = accs[2 * slot + 1] if 2 * slot + 1 < nu else 0.0
        o_ref[0, slot] = jnp.where(first, lo, hi).astype(BF16)


def _sb_attention(sb, *, t=SB_TILE):
    B, _, TP, _ = sb.shape
    kern = functools.partial(_sb_kernel, t=t)
    q_specs = [pl.BlockSpec((1, 1, t, LANE), lambda b, i, s=s: (b, s, i, 0)) for s in range(PAIRS)]
    kv_specs = [pl.BlockSpec((1, 1, TP, LANE), lambda b, i, s=s: (b, s, 0, 0),
                             pipeline_mode=pl.Buffered(1)) for s in range(PAIRS, 3 * PAIRS)]
    return pl.pallas_call(
        kern, out_shape=jax.ShapeDtypeStruct((B, PAIRS, TP, LANE), BF16),
        grid=(B, TP // t),
        in_specs=q_specs + kv_specs,
        out_specs=pl.BlockSpec((1, PAIRS, t, LANE), lambda b, i: (b, 0, i, 0)),
        compiler_params=pltpu.CompilerParams(
            dimension_semantics=("arbitrary",) * 2, vmem_limit_bytes=VMEM_LIMIT),
        name="sb_attention",
    )(*([sb] * (3 * PAIRS)))


def _fox_kernel(shift_ref, fe_ref, q_ref, k_ref, v_ref, o_ref, lo_ref, *, t):
    b = pl.program_id(0)
    h = pl.program_id(1)
    qi = pl.program_id(2)
    q = q_ref[0, 0]
    shift = shift_ref[0]
    thr = FOX_SKIP_LOG2 + 2.0 * shift
    f_q = fe_ref[b, h, jnp.maximum(qi - 1, 0)]
    lo = lax.while_loop(lambda n: (n < qi) & (fe_ref[b, h, n] - f_q > thr), lambda n: n + 1,
                        jnp.where(qi == 0, 0, lo_ref[0]))
    lo_ref[0] = lo
    qpos = qi * t + lax.broadcasted_iota(jnp.int32, (t, 1), 0)
    kloc = lax.broadcasted_iota(jnp.int32, (1, t), 1)

    def block(kb, carry, masked, online):
        acc, m = carry
        start = pl.multiple_of(kb * t, t)
        kblk = k_ref[0, 0, pl.ds(start, t), :]
        vblk = v_ref[0, 0, pl.ds(start, t), :]
        s = _dot_nt(q, kblk)
        if masked:
            kpos = start + kloc
            s = jnp.where(kpos <= qpos, s, NEG_INF)
        if online:
            m_new = jnp.maximum(m, jnp.max(s, axis=-1, keepdims=True))
            acc = jnp.exp2(m - m_new) * acc
            m = m_new
        pr = jnp.exp2(s - m)
        return acc + _dot(pr.astype(BF16), vblk), m

    def sweep(online):
        def run(carry):
            start = lo
            n = qi - lo

            def group(i, c):
                acc, m = c
                kvs = []
                for j in range(FOX_GROUP):
                    st = pl.multiple_of((start + FOX_GROUP * i + j) * t, t)
                    kvs.append((k_ref[0, 0, pl.ds(st, t), :], v_ref[0, 0, pl.ds(st, t), :]))
                ss = [_dot_nt(q, kblk) for kblk, _ in kvs]
                prs = [jnp.exp2(s - m).astype(BF16) for s in ss]
                for pr, (_, vblk) in zip(prs, kvs):
                    acc = acc + _dot(pr, vblk)
                return acc, m

            grouped = 0 if online else (n // FOX_GROUP) * FOX_GROUP
            if not online:
                carry = lax.fori_loop(0, n // FOX_GROUP, group, carry)
            carry = lax.fori_loop(start + grouped, qi,
                                  lambda kb, c: block(kb, c, False, online), carry)
            return block(qi, carry, True, online)
        return run

    acc0 = jnp.zeros((t, LANE), F32)
    fixed = shift <= FOX_FIXED_SHIFT_MAX
    m0 = jnp.where(fixed, jnp.full((t, 1), shift, F32), jnp.full((t, 1), NEG_INF, F32))
    acc, _ = lax.cond(fixed, sweep(False), sweep(True), (acc0, m0))
    lane = lax.broadcasted_iota(jnp.int32, (t, LANE), 1)
    denom = jnp.sum(jnp.where(lane == HEAD_DIM, acc, 0.0), axis=-1, keepdims=True)
    o_ref[0, 0] = (acc / jnp.where(denom > 0.0, denom, 1.0)).astype(BF16)


def _fox_attention(shift, fe, fx, *, t=FOX_TILE):
    B, _, TP, _ = fx.shape
    H = FOX_HEADS
    kern = functools.partial(_fox_kernel, t=t)
    return pl.pallas_call(
        kern, out_shape=jax.ShapeDtypeStruct((B, H, TP, LANE), BF16),
        grid=(B, H, TP // t),
        in_specs=[pl.BlockSpec(memory_space=pltpu.SMEM),
                  pl.BlockSpec(memory_space=pltpu.SMEM),
                  pl.BlockSpec((1, 1, t, LANE), lambda b, h, i: (b, h, i, 0)),
                  pl.BlockSpec((1, 1, TP, LANE), lambda b, h, i: (b, h + H, 0, 0)),
                  pl.BlockSpec((1, 1, TP, LANE), lambda b, h, i: (b, h + 2 * H, 0, 0))],
        out_specs=pl.BlockSpec((1, 1, t, LANE), lambda b, h, i: (b, h, i, 0)),
        scratch_shapes=[pltpu.SMEM((1,), jnp.int32)],
        compiler_params=pltpu.CompilerParams(
            dimension_semantics=("arbitrary",) * 3, vmem_limit_bytes=VMEM_LIMIT),
        name="fox_attention",
    )(shift, fe, fx, fx, fx)


def _merge_kernel(*refs, tm, n_valid, first, n_seq):
    n_stream = 2 if first else 1
    (ya_ref, yb_ref, yc_ref, gate_ref,
     wpa_ref, wpb_ref, wpc_ref, wout_ref, o_ref) = refs[n_stream:]
    i = pl.program_id(1)

    def proj(slots, w_ref):
        return _dot(jnp.concatenate(slots, axis=-1), w_ref[...])

    pa = proj([ya_ref[0, s] for s in range(PAIRS)], wpa_ref)
    pb = proj([yb_ref[0, s] for s in range(PAIRS)], wpb_ref)
    pc = proj([yc_ref[0, s] for s in range(FOX_HEADS)], wpc_ref)
    merged = gate_ref[0, :, 0:D_MODEL].astype(F32) * pa
    merged = merged + gate_ref[0, :, D_MODEL:2 * D_MODEL].astype(F32) * pb
    merged = merged + gate_ref[0, :, 2 * D_MODEL:3 * D_MODEL].astype(F32) * pc
    out = _stream_tile(i, refs[:n_stream], n_seq) + _dot(merged.astype(BF16), wout_ref[...])
    t = i * tm + lax.broadcasted_iota(jnp.int32, (tm, 1), 0)
    o_ref[0] = jnp.where((t >= PAD) & (t < PAD + n_valid), out, 0.0)


def _merge(stream, ya, yb, yc, gates, wpa, wpb, wpc, wout, *, n_valid, n_seq, tm=FRAME_TILE):
    first = len(stream) == 2
    B, _, TP, _ = ya.shape
    D = D_MODEL
    kern = functools.partial(_merge_kernel, tm=tm, n_valid=n_valid, first=first, n_seq=n_seq)
    pair_blk = lambda ns: pl.BlockSpec((1, ns, tm, LANE), lambda b, i: (b, 0, i, 0))
    return pl.pallas_call(
        kern, out_shape=jax.ShapeDtypeStruct((B, TP, D), F32), grid=(B, TP // tm),
        in_specs=_stream_specs(first, tm, n_seq) + [
                  pair_blk(PAIRS),
                  pair_blk(PAIRS), pair_blk(FOX_HEADS),
                  pl.BlockSpec((1, tm, 3 * D), lambda b, i: (b, i, 0)),
                  _const_spec((PAIRS * LANE, D)),
                  _const_spec((PAIRS * LANE, D)),
                  _const_spec((FOX_HEADS * LANE, D)),
                  _const_spec((D, D))],
        out_specs=pl.BlockSpec((1, tm, D), lambda b, i: (b, i, 0)),
        compiler_params=pltpu.CompilerParams(
            dimension_semantics=("arbitrary", "arbitrary"), vmem_limit_bytes=VMEM_LIMIT),
        name="merge",
    )(*stream, ya, yb, yc, gates, wpa, wpb, wpc, wout)


def _moe_kernel(*refs, tm, n_valid, last):
    n_stream = 2 if last else 1
    g_ref, wr_ref, br_ref, wg_ref, wu_ref, wd_ref, o_ref, he_ref, ys_ref = refs[n_stream:]
    i = pl.program_id(1)
    x = jnp.concatenate([r[0] for r in refs[:n_stream]], axis=0)
    ms = jnp.mean(x * x, axis=-1, keepdims=True)
    n = x * lax.rsqrt(ms + NORM_EPS) * g_ref[...]

    lg = _dot3(n, wr_ref[...]) + br_ref[...]
    lane = lax.broadcasted_iota(jnp.int32, (tm, LANE), 1)
    big = jnp.int32(LANE)
    is_group = lane < N_GROUPS
    gl = jnp.where(is_group, lg, -jnp.inf)
    gmax = jnp.max(gl, axis=-1, keepdims=True)
    g_idx = jnp.min(jnp.where(is_group & (gl == gmax), lane, big), axis=-1, keepdims=True)
    p_group = 1.0 / jnp.sum(jnp.where(is_group, jnp.exp(gl - gmax), 0.0), axis=-1, keepdims=True)
    lo = N_GROUPS + g_idx * EXPERTS_PER_GROUP
    in_grp = (lane >= lo) & (lane < lo + EXPERTS_PER_GROUP)
    el = jnp.where(in_grp, lg, -jnp.inf)
    top1 = jnp.max(el, axis=-1, keepdims=True)
    i1 = jnp.min(jnp.where(in_grp & (el == top1), lane, big), axis=-1, keepdims=True)
    rest = in_grp & (lane != i1)
    el2 = jnp.where(rest, lg, -jnp.inf)
    top2 = jnp.max(el2, axis=-1, keepdims=True)
    i2 = jnp.min(jnp.where(rest & (el2 == top2), lane, big), axis=-1, keepdims=True)
    e2 = jnp.exp(top2 - top1)
    p1 = 1.0 / (1.0 + e2)
    p2 = e2 / (1.0 + e2)
    comb = p_group * (jnp.where(lane == i1, p1, 0.0) + jnp.where(lane == i2, p2, 0.0))

    n16 = n.astype(BF16)

    onehot = (lane == g_idx).astype(BF16)
    earlier = (lax.broadcasted_iota(jnp.int32, (tm, tm), 1)
               < lax.broadcasted_iota(jnp.int32, (tm, tm), 0)).astype(BF16)
    cnt = _dot(earlier, onehot)
    rank = jnp.sum(jnp.where(lane == g_idx, cnt, 0.0), axis=-1, keepdims=True).astype(jnp.int32)
    n_pass = (jnp.max(rank) + MOE_CAP) // MOE_CAP
    comb16 = comb.astype(BF16)
    slot_lane = lax.broadcasted_iota(jnp.int32, (tm, N_GROUPS * MOE_CAP), 1)

    def one_pass(p, y):
        r = rank - p * MOE_CAP
        slot = jnp.where((r >= 0) & (r < MOE_CAP), g_idx * MOE_CAP + r, -1)
        place = (slot_lane == slot).astype(BF16)
        xs = _dot_tn(place, n16).astype(BF16)
        cw = _dot_tn(place, comb16)
        for g in range(N_GROUPS):
            rows = slice(g * MOE_CAP, (g + 1) * MOE_CAP)
            for e in range(EXPERTS_PER_GROUP):
                ee = g * EXPERTS_PER_GROUP + e
                gate = _dot(xs[rows], wg_ref[ee])
                up = _dot(xs[rows], wu_ref[ee])
                c_e = cw[rows, N_GROUPS + ee:N_GROUPS + ee + 1]
                he_ref[:, e * D_EXPERT:(e + 1) * D_EXPERT] = (
                    gate * _sigmoid(gate) * up * c_e).astype(BF16)
            gw = EXPERTS_PER_GROUP * D_EXPERT
            ys_ref[rows, :] = _dot(he_ref[...], wd_ref[g * gw:(g + 1) * gw, :]).astype(BF16)
        return y + _dot(place, ys_ref[...])

    out = x + lax.fori_loop(0, n_pass, one_pass, jnp.zeros((tm, D_MODEL), F32))
    if last:
        o_ref[0] = out
    else:
        t = i * tm + lax.broadcasted_iota(jnp.int32, (tm, 1), 0)
        o_ref[0] = jnp.where((t >= PAD) & (t < PAD + n_valid), out, 0.0)


def _moe(h, g2, wr, br, wg, wu, wd, *, n_valid, n_seq, last):
    B, TP, D = h.shape
    tm = MOE_TILE
    kern = functools.partial(_moe_kernel, tm=tm, n_valid=n_valid, last=last)
    EW = N_EXPERTS * D_EXPERT
    if last:
        half = tm // FRAME_TILE
        assert half == 2 and n_seq % half == 0
        rows_out = n_seq * FRAME_TILE
        stream_specs = [pl.BlockSpec((1, FRAME_TILE, D), lambda b, i, j=j: (b, half * i + 1 + j, 0))
                        for j in range(half)]
        stream = [h] * half
    else:
        rows_out = TP
        stream_specs = [pl.BlockSpec((1, tm, D), lambda b, i: (b, i, 0))]
        stream = [h]
    return pl.pallas_call(
        kern, out_shape=jax.ShapeDtypeStruct((B, rows_out, D), F32), grid=(B, rows_out // tm),
        in_specs=stream_specs + [
                  _const_spec((1, D)), _const_spec((D, LANE)), _const_spec((1, LANE)),
                  _const_spec((N_EXPERTS, D, D_EXPERT)), _const_spec((N_EXPERTS, D, D_EXPERT)),
                  _const_spec((EW, D))],
        out_specs=pl.BlockSpec((1, tm, D), lambda b, i: (b, i, 0)),
        scratch_shapes=[pltpu.VMEM((MOE_CAP, EXPERTS_PER_GROUP * D_EXPERT), BF16),
                        pltpu.VMEM((N_GROUPS * MOE_CAP, D), BF16)],
        compiler_params=pltpu.CompilerParams(
            dimension_semantics=("arbitrary", "arbitrary"), vmem_limit_bytes=VMEM_LIMIT),
        name="moe",
    )(*stream, g2, wr, br, wg, wu, wd)


def _transpose_kernel(x_ref, o_ref):
    o_ref[...] = x_ref[...].T.astype(BF16)


def _transpose_to_bf16(a, *, tn=LANE):
    N, K = a.shape
    assert N % tn == 0
    return pl.pallas_call(
        _transpose_kernel, out_shape=jax.ShapeDtypeStruct((K, N), BF16), grid=(N // tn,),
        in_specs=[pl.BlockSpec((tn, K), lambda i: (i, 0))],
        out_specs=pl.BlockSpec((K, tn), lambda i: (0, i)),
        compiler_params=pltpu.CompilerParams(dimension_semantics=("arbitrary",)),
        name="transpose_cast",
    )(a)


def _pack_w_in(w_in):
    wt = jnp.swapaxes(w_in, 0, 1)
    rest = wt[SHIFT_W:]
    rows = [wt[:SHIFT_W]]
    off = 0
    for heads in (SB_HEADS,) * 3 + (FOX_HEADS,) * 3:
        rows.append(jnp.pad(rest[off:off + heads * HEAD_DIM],
                            ((0, PAIRS * LANE - heads * HEAD_DIM), (0, 0))))
        off += heads * HEAD_DIM
    rows.append(jnp.pad(rest[off:off + FOX_HEADS], ((0, LANE - FOX_HEADS), (0, 0))))
    rows.append(rest[off + FOX_HEADS:])
    return _transpose_to_bf16(jnp.concatenate(rows, axis=0))


def _per_pair_cols(w):
    return w.reshape(w.shape[0], PAIRS, LANE).transpose(1, 0, 2)


def kernel(x, meta_tokens, norm1_g, w_in, rwkv_mu, rwkv_w_up, rwkv_w0, rwkv_a_up, rwkv_a0,
           rwkv_k_k, rwkv_k_a, rwkv_r_k, rwkv_ln_g, rwkv_ln_b, fox_f_b, fox_q_g, fox_k_g,
           w_p_rwkv, w_p_sb, w_p_fox, w_out, norm2_g, moe_wg, moe_bg, moe_we, moe_be,
           moe_w_gate, moe_w_up, moe_w_down):
    B, S, D = x.shape
    depth = w_in.shape[0]
    L = N_META + S
    TP = -(-(PAD + L) // ROW_TILE) * ROW_TILE
    assert S % FRAME_TILE == 0
    n_seq = S // FRAME_TILE
    meta = jnp.broadcast_to(meta_tokens[None].astype(x.dtype), (B, N_META, D))
    stream = (x, jnp.concatenate([jnp.zeros((B, PAD, D), x.dtype), meta], axis=1))
    H = RWKV_HEADS
    EW = N_EXPERTS * D_EXPERT
    for l in range(depth):
        fb = jnp.zeros((1, LANE), F32).at[0, :FOX_HEADS].set(fox_f_b[l])
        rkv, wdad, sb, fx, fend, gates = _inproj(
            stream, norm1_g[l][None], _pack_w_in(w_in[l]), rwkv_mu[l][None], fb,
            jnp.tile(fox_q_g[l], 2)[None], jnp.tile(fox_k_g[l], 2)[None], TP=TP, n_seq=n_seq)
        hv = lambda p: p.reshape(PAIRS, 1, LANE)
        gm, hc, r2, y0, bonus = _rwkv_chunks(
            rkv, wdad, _per_pair_cols(rwkv_w_up[l]), hv(rwkv_w0[l]),
            _per_pair_cols(rwkv_a_up[l]), hv(rwkv_a0[l]),
            hv(rwkv_k_k[l]), hv(rwkv_k_a[l]), hv(rwkv_r_k[l]))
        ya = _rwkv_scan(gm, hc, r2, y0, bonus, hv(rwkv_ln_g[l]), hv(rwkv_ln_b[l]))
        yb = _sb_attention(sb)
        shift = (8.0 * LOG2E * jnp.max(jnp.abs(fox_q_g[l])) * jnp.max(jnp.abs(fox_k_g[l]))).reshape(1)
        step = FOX_TILE // INPROJ_TILE
        fe = jnp.transpose(fend[:, step - 1::step, 0, :FOX_HEADS], (0, 2, 1))
        yc = _fox_attention(shift, fe, fx)
        pc = jnp.pad(w_p_fox[l].reshape(FOX_HEADS, HEAD_DIM, D),
                     ((0, 0), (0, LANE - HEAD_DIM), (0, 0))).reshape(FOX_HEADS * LANE, D).astype(BF16)
        pp = lambda w, nh: jnp.pad(w, ((0, PAIRS * LANE - nh * HEAD_DIM), (0, 0))).astype(BF16)
        h = _merge(stream, ya, yb, yc, gates, pp(w_p_rwkv[l], H), pp(w_p_sb[l], SB_HEADS),
                   pc, w_out[l].astype(BF16), n_valid=L, n_seq=n_seq)
        wr = jnp.zeros((D, LANE), F32).at[:, :N_GROUPS].set(moe_wg[l])
        wr = wr.at[:, N_GROUPS:N_GROUPS + N_EXPERTS].set(moe_we[l])
        br = jnp.zeros((1, LANE), F32).at[0, :N_GROUPS].set(moe_bg[l])
        br = br.at[0, N_GROUPS:N_GROUPS + N_EXPERTS].set(moe_be[l])
        wg = moe_w_gate[l].astype(BF16)
        wu = moe_w_up[l].astype(BF16)
        wd = moe_w_down[l].reshape(EW, D).astype(BF16)
        h = _moe(h, norm2_g[l][None], wr, br, wg, wu, wd, n_valid=L, n_seq=n_seq,
                 last=l == depth - 1)
        stream = (h,)
    return h
```

```python
import functools
import math

import jax
import jax.numpy as jnp
from jax import lax
from jax.experimental import pallas as pl
from jax.experimental.pallas import tpu as pltpu

D_MODEL = 1024
HEAD_DIM = 64
N_META = 16
FRAME_TILE = 256
MOE_TILE = 512
MOE_CAP = 192
PAD = FRAME_TILE - N_META
RWKV_HEADS = 6
SB_HEADS = 5
FOX_HEADS = 5
RWKV_W = RWKV_HEADS * HEAD_DIM
DECAY_RANK = 64
ICLR_RANK = 64
SHIFT_W = 3 * RWKV_W + DECAY_RANK + ICLR_RANK
N_GROUPS = 4
EXPERTS_PER_GROUP = 4
N_EXPERTS = 16
D_EXPERT = 256
NORM_EPS = 1e-6
RWKV_LN_EPS = 64e-5
NEG_INF = -1e30
ATT_SCALE = 1.0 / math.sqrt(HEAD_DIM)

LANE = 128
ROW_TILE = 512
PAIRS = 3
QKV_W = 3 * PAIRS * LANE
SEG_SB = SHIFT_W
SEG_FOX = SEG_SB + QKV_W
SEG_GATE = SEG_FOX + QKV_W + LANE
D_IN_PAD = SEG_GATE + 3 * D_MODEL
SB_CUTOFF_LOG2 = 160.0
SB_TILE = 128
SB_FUSED = 3
LOG2E = 1.4426950408889634
FOX_FIXED_SHIFT_MAX = 50.0
FOX_SKIP_LOG2 = 150.0
FOX_TILE = 512
FOX_GROUP = 4
INPROJ_TILE = FRAME_TILE

CHUNK = 64
VMEM_LIMIT = 56 * 1024 * 1024

F32 = jnp.float32
BF16 = jnp.bfloat16


def _log_sigmoid(x):
    return jnp.minimum(x, 0.0) - jnp.log1p(jnp.exp(-jnp.abs(x)))


def _sigmoid(x):
    return 1.0 / (1.0 + jnp.exp(-x))


def _dot(a, b, **kw):
    return jnp.dot(a, b, preferred_element_type=F32, **kw)


def _dot_nt(a, b):
    return lax.dot_general(a, b, (((1,), (1,)), ((), ())), preferred_element_type=F32)


def _dot_tn(a, b):
    return lax.dot_general(a, b, (((0,), (0,)), ((), ())), preferred_element_type=F32)


def _dot3(a, b):
    ah = a.astype(BF16)
    al = (a - ah.astype(F32)).astype(BF16)
    bh = b.astype(BF16)
    bl = (b - bh.astype(F32)).astype(BF16)
    return _dot(ah, bh) + (_dot(ah, bl) + _dot(al, bh))


def _stream_specs(first, tm, n_seq):
    if not first:
        return [pl.BlockSpec((1, tm, D_MODEL), lambda b, i: (b, i, 0))]
    return [pl.BlockSpec((1, tm, D_MODEL), lambda b, i: (b, jnp.clip(i - 1, 0, n_seq - 1), 0)),
            pl.BlockSpec((1, tm, D_MODEL), lambda b, i: (b, 0, 0))]


def _stream_tile(i, refs, n_seq):
    if len(refs) == 1:
        return refs[0][0]
    x_ref, head_ref = refs
    return jnp.where(i == 0, head_ref[0], jnp.where(i <= n_seq, x_ref[0], 0.0))


def _const_spec(shape):
    n = len(shape)
    return pl.BlockSpec(shape, lambda *_: (0,) * n, pipeline_mode=pl.Buffered(1))


def _inproj_kernel(*refs, tm, first, n_seq):
    n_stream = 2 if first else 1
    (g_ref, w_ref, mu_ref, fb_ref, fqg_ref, fkg_ref,
     rkv_ref, wdad_ref, sb_ref, fx_ref, fend_ref, gate_ref, carry_u, carry_f) = refs[n_stream:]
    i = pl.program_id(1)

    @pl.when(i == 0)
    def _():
        carry_u[...] = jnp.zeros_like(carry_u)
        carry_f[...] = jnp.zeros_like(carry_f)

    x = _stream_tile(i, refs[:n_stream], n_seq)
    ms = jnp.mean(x * x, axis=-1, keepdims=True)
    n = (x * lax.rsqrt(ms + NORM_EPS) * g_ref[...]).astype(BF16)
    row = lax.broadcasted_iota(jnp.int32, (tm, 1), 0)

    us = _dot(n, w_ref[:, 0:SHIFT_W])
    prev = pltpu.roll(us, 1, axis=0)
    prev = jnp.where(row == 0, carry_u[...], prev)
    carry_u[...] = us[tm - 1:tm, :]
    ush = us + (prev - us) * mu_ref[...]
    for j in range(3 * PAIRS):
        rkv_ref[0, j] = ush[:, j * LANE:(j + 1) * LANE]
    wdad_ref[0] = ush[:, 3 * RWKV_W:SHIFT_W]

    usb = _dot(n, w_ref[:, SEG_SB:SEG_SB + QKV_W])
    for j in range(3 * PAIRS):
        piece = usb[:, j * LANE:(j + 1) * LANE]
        if j < PAIRS:
            piece = piece * (ATT_SCALE * LOG2E)
        sb_ref[0, j] = piece.astype(BF16)

    uf = _dot(n, w_ref[:, SEG_FOX:SEG_FOX + QKV_W + LANE])
    lane = lax.broadcasted_iota(jnp.int32, (tm, LANE), 1)
    first = lane < HEAD_DIM
    t_glob = i * tm + row
    logf = _log_sigmoid(uf[:, QKV_W:QKV_W + LANE] + fb_ref[...])
    logf = jnp.where((lane < FOX_HEADS) & (t_glob >= PAD), logf, 0.0)
    tri = (lax.broadcasted_iota(jnp.int32, (tm, tm), 0)
           >= lax.broadcasted_iota(jnp.int32, (tm, tm), 1)).astype(BF16)
    f_hi = logf.astype(BF16)
    f_mid = (logf - f_hi.astype(F32)).astype(BF16)
    f_lo = (logf - f_hi.astype(F32) - f_mid.astype(F32)).astype(BF16)
    cum = _dot(tri, f_hi) + _dot(tri, f_mid) + _dot(tri, f_lo) + carry_f[...]
    carry_f[...] = cum[tm - 1:tm, :]
    fend_ref[0, 0] = cum[tm - 1:tm, :] * LOG2E

    def split3(hd):
        f2 = jnp.sum(jnp.where(lane == hd, cum, 0.0), axis=-1, keepdims=True) * LOG2E
        hi = f2.astype(BF16).astype(F32)
        mid = (f2 - hi).astype(BF16).astype(F32)
        return hi, mid, f2 - hi - mid

    def tail_cols(vals):
        out = jnp.zeros((tm, LANE), F32)
        for o, val in enumerate(vals):
            out = jnp.where(lane == HEAD_DIM + o, val, out)
        return out

    splits = [split3(hd) for hd in range(FOX_HEADS)]
    for j in range(3 * PAIRS):
        piece = uf[:, j * LANE:(j + 1) * LANE]
        kind = j // PAIRS
        if kind < 2:
            gain = fqg_ref[...] if kind == 0 else fkg_ref[...]
            sq = piece * piece
            ms0 = jnp.sum(jnp.where(first, sq, 0.0), axis=-1, keepdims=True) * (1.0 / HEAD_DIM)
            ms1 = jnp.sum(jnp.where(first, 0.0, sq), axis=-1, keepdims=True) * (1.0 / HEAD_DIM)
            inv = jnp.where(first, lax.rsqrt(ms0 + NORM_EPS), lax.rsqrt(ms1 + NORM_EPS))
            piece = piece * inv * gain
            if kind == 0:
                piece = piece * (ATT_SCALE * LOG2E)
        swapped = pltpu.roll(piece, HEAD_DIM, axis=1)
        for half in range(2):
            hd = 2 * (j % PAIRS) + half
            if hd >= FOX_HEADS:
                continue
            if kind == 2:
                extra = tail_cols([1.0])
            else:
                hi, mid, lo = splits[hd]
                extra = tail_cols([hi, mid, lo, 1.0, 1.0, 1.0] if kind == 0 else
                                  [1.0, 1.0, 1.0, jnp.where(t_glob >= PAD, -hi, NEG_INF), -mid, -lo])
            body = piece if half == 0 else swapped
            fx_ref[0, kind * FOX_HEADS + hd] = jnp.where(first, body, extra).astype(BF16)

    ug = _dot(n, w_ref[:, SEG_GATE:D_IN_PAD])
    gate_ref[0] = _sigmoid(ug).astype(BF16)


def _inproj(stream, g1, w_in_p, mu, fb, fqg, fkg, *, TP, n_seq, tm=INPROJ_TILE):
    first = len(stream) == 2
    B, D = stream[0].shape[0], D_MODEL
    nb = TP // tm
    kern = functools.partial(_inproj_kernel, tm=tm, first=first, n_seq=n_seq)
    out_shape = (
        jax.ShapeDtypeStruct((B, 3 * PAIRS, TP, LANE), F32),
        jax.ShapeDtypeStruct((B, TP, LANE), F32),
        jax.ShapeDtypeStruct((B, 3 * PAIRS, TP, LANE), BF16),
        jax.ShapeDtypeStruct((B, 3 * FOX_HEADS, TP, LANE), BF16),
        jax.ShapeDtypeStruct((B, nb, 1, LANE), F32),
        jax.ShapeDtypeStruct((B, TP, 3 * D_MODEL), BF16),
    )
    in_specs = _stream_specs(first, tm, n_seq) + [
        _const_spec((1, D)),
        _const_spec((D, D_IN_PAD)),
        _const_spec((1, SHIFT_W)),
        _const_spec((1, LANE)),
        _const_spec((1, LANE)),
        _const_spec((1, LANE)),
    ]
    out_specs = (
        pl.BlockSpec((1, 3 * PAIRS, tm, LANE), lambda b, i: (b, 0, i, 0)),
        pl.BlockSpec((1, tm, LANE), lambda b, i: (b, i, 0)),
        pl.BlockSpec((1, 3 * PAIRS, tm, LANE), lambda b, i: (b, 0, i, 0)),
        pl.BlockSpec((1, 3 * FOX_HEADS, tm, LANE), lambda b, i: (b, 0, i, 0)),
        pl.BlockSpec((1, 1, 1, LANE), lambda b, i: (b, i, 0, 0)),
        pl.BlockSpec((1, tm, 3 * D_MODEL), lambda b, i: (b, i, 0)),
    )
    return pl.pallas_call(
        kern, out_shape=out_shape, grid=(B, nb), in_specs=in_specs, out_specs=out_specs,
        scratch_shapes=[pltpu.VMEM((1, SHIFT_W), F32), pltpu.VMEM((1, LANE), F32)],
        compiler_params=pltpu.CompilerParams(
            dimension_semantics=("arbitrary", "arbitrary"), vmem_limit_bytes=VMEM_LIMIT),
        name="inproj",
    )(*stream, g1, w_in_p, mu, fb, fqg, fkg)


def _bdot(a, b):
    return lax.dot_general(a, b, (((2,), (1,)), ((0,), (0,))), preferred_element_type=F32)


def _bdot_nt(a, b):
    return lax.dot_general(a, b, (((2,), (2,)), ((0,), (0,))), preferred_element_type=F32)


def _bdot_tn(a, b):
    return lax.dot_general(a, b, (((1,), (1,)), ((0,), (0,))), preferred_element_type=F32)


def _head_sum(x, first):
    s0 = jnp.sum(jnp.where(first, x, 0.0), axis=-1, keepdims=True)
    s1 = jnp.sum(jnp.where(first, 0.0, x), axis=-1, keepdims=True)
    return jnp.where(first, s0, s1)


def _rwkv_chunk_kernel(r_ref, k_ref, v_ref, wdad_ref, wup_ref, w0_ref, aup_ref, a0_ref,
                       kk_ref, ka_ref, rk_ref,
                       g_ref, hc_ref, r2_ref, y0_ref, bonus_ref, *, rows):
    nc = rows // CHUNK
    r = r_ref[0, 0]
    k = k_ref[0, 0]
    v = v_ref[0, 0]
    wd = wdad_ref[0][:, 0:DECAY_RANK]
    ad = wdad_ref[0][:, DECAY_RANK:DECAY_RANK + ICLR_RANK]
    first = lax.broadcasted_iota(jnp.int32, (rows, LANE), 1) < HEAD_DIM

    pre = w0_ref[0] + _dot3(jnp.tanh(wd), wup_ref[0])
    lw = -jnp.exp(_log_sigmoid(pre) - 0.5)
    iclr = _sigmoid(a0_ref[0] + _dot3(ad, aup_ref[0]))
    kk = k * kk_ref[0]
    kk = kk / jnp.maximum(jnp.sqrt(_head_sum(kk * kk, first)), 1e-12)
    k2 = k * (1.0 + (iclr - 1.0) * ka_ref[0])
    b = kk * iclr
    bonus_ref[0, 0] = _head_sum(r * k2 * rk_ref[0], first) * v

    to3 = lambda x: x.reshape(nc, CHUNK, LANE)
    ri = lax.broadcasted_iota(jnp.int32, (nc, CHUNK, CHUNK), 1)
    ci = lax.broadcasted_iota(jnp.int32, (nc, CHUNK, CHUNK), 2)
    low_incl = ri >= ci
    low_strict = ri > ci
    first3 = lax.broadcasted_iota(jnp.int32, (nc, CHUNK, LANE), 2) < HEAD_DIM

    lw3 = to3(lw)
    tri = low_incl.astype(BF16)
    lw_hi = lw3.astype(BF16)
    cum = _bdot(tri, lw_hi) + _bdot(tri, (lw3 - lw_hi.astype(F32)).astype(BF16))
    cum_end = cum[:, CHUNK - 1:CHUNK, :]
    e_neg = jnp.exp(-cum)
    at = to3(-kk) * jnp.exp(cum - lw3)
    rt = to3(r) * jnp.exp(cum)
    bt = (to3(b) * e_neg).astype(BF16)
    kt = (to3(k2) * e_neg).astype(BF16)
    e_rem = jnp.exp(cum_end - cum)
    bq = (to3(b) * e_rem).astype(BF16)
    kq = (to3(k2) * e_rem).astype(BF16)
    vv = to3(v).astype(BF16)

    heads = range(2)
    sels = [first3, jnp.logical_not(first3)]
    lhs = [jnp.concatenate([jnp.where(sels[hd], at, 0.0), jnp.where(sels[hd], rt, 0.0)],
                           axis=1).astype(BF16) for hd in heads]
    mb = [_bdot_nt(lhs[hd], bt) for hd in heads]
    mk = [_bdot_nt(lhs[hd], kt) for hd in heads]
    m_ak = [jnp.where(low_strict, mk[hd][:, :CHUNK], 0.0).astype(BF16) for hd in heads]
    m_rb = [jnp.where(low_incl, mb[hd][:, CHUNK:], 0.0).astype(BF16) for hd in heads]
    m_rk = [jnp.where(low_incl, mk[hd][:, CHUNK:], 0.0).astype(BF16) for hd in heads]
    p = [jnp.where(low_strict, mb[hd][:, :CHUNK], 0.0) for hd in heads]
    xs = [jnp.concatenate([at, _bdot(m_ak[hd], vv)], axis=-1) for hd in heads]
    for j in range(6):
        p16 = [p[hd].astype(BF16) for hd in heads]
        xs = [xs[hd] + _bdot(p16[hd], xs[hd].astype(BF16)) for hd in heads]
        if j < 5:
            p = [_bdot(p16[hd], p16[hd]) for hd in heads]
    ru = [_bdot(m_rb[hd], xs[hd].astype(BF16)) for hd in heads]
    r2s = [rt + ru[hd][..., :LANE] for hd in heads]
    y0s = [ru[hd][..., LANE:] + _bdot(m_rk[hd], vv) for hd in heads]

    first3w = jnp.concatenate([first3, first3], axis=-1)
    x = jnp.where(first3w, xs[0], xs[1])
    r2_ref[0, 0] = jnp.where(first3, r2s[0], r2s[1]).reshape(rows, LANE)
    y0_ref[0, 0] = jnp.where(first3, y0s[0], y0s[1]).reshape(rows, LANE)

    pg = _bdot_tn(bq, x.astype(BF16))
    ph = pg[..., LANE:] + _bdot_tn(kq, vv)
    rr = lax.broadcasted_iota(jnp.int32, (nc, LANE, LANE), 1)
    cc = lax.broadcasted_iota(jnp.int32, (nc, LANE, LANE), 2)
    same_head = (rr < HEAD_DIM) == (cc < HEAD_DIM)
    g = jnp.where(same_head, pg[..., :LANE], 0.0) + jnp.where(rr == cc, jnp.exp(cum_end), 0.0)
    g_ref[0, 0] = g.reshape(nc * LANE, LANE)
    hc_ref[0, 0] = jnp.where(same_head, ph, 0.0).reshape(nc * LANE, LANE)


def _rwkv_chunks(rkv, wdad, wup, w0, aup, a0, k_k, k_a, r_k):
    B, _, TP, _ = rkv.shape
    rows = next(r for r in (1536, 768, 512, 256) if TP % r == 0)
    nb = TP // rows
    kern = functools.partial(_rwkv_chunk_kernel, rows=rows)
    slot_spec = lambda off: pl.BlockSpec((1, 1, rows, LANE), lambda b, p, i: (b, p + off, i, 0))
    par_mat = pl.BlockSpec((1, DECAY_RANK, LANE), lambda b, p, i: (p, 0, 0))
    par_vec = pl.BlockSpec((1, 1, LANE), lambda b, p, i: (p, 0, 0))
    row_out = jax.ShapeDtypeStruct((B, PAIRS, TP, LANE), F32)
    mat_out = jax.ShapeDtypeStruct((B, PAIRS, 2 * TP, LANE), F32)
    row_spec = pl.BlockSpec((1, 1, rows, LANE), lambda b, p, i: (b, p, i, 0))
    mat_spec = pl.BlockSpec((1, 1, 2 * rows, LANE), lambda b, p, i: (b, p, i, 0))
    return pl.pallas_call(
        kern, out_shape=(mat_out, mat_out, row_out, row_out, row_out), grid=(B, PAIRS, nb),
        in_specs=[slot_spec(0), slot_spec(PAIRS), slot_spec(2 * PAIRS),
                  pl.BlockSpec((1, rows, LANE), lambda b, p, i: (b, i, 0)),
                  par_mat, par_vec, par_mat, par_vec, par_vec, par_vec, par_vec],
        out_specs=(mat_spec, mat_spec, row_spec, row_spec, row_spec),
        compiler_params=pltpu.CompilerParams(
            dimension_semantics=("arbitrary",) * 3, vmem_limit_bytes=VMEM_LIMIT),
        name="rwkv_chunks",
    )(rkv, rkv, rkv, wdad, wup, w0, aup, a0, k_k, k_a, r_k)


def _rwkv_scan_kernel(g_ref, hc_ref, r2_ref, y0_ref, bonus_ref, lng_ref, lnb_ref, y_ref,
                      state, *, rows):
    i = pl.program_id(1)

    @pl.when(i == 0)
    def _():
        state[...] = jnp.zeros_like(state)

    first = lax.broadcasted_iota(jnp.int32, (CHUNK, LANE), 1) < HEAD_DIM
    hs = [state[p] for p in range(PAIRS)]
    for c in range(rows // CHUNK):
        sl = slice(c * CHUNK, (c + 1) * CHUNK)
        sm = slice(c * LANE, (c + 1) * LANE)
        for p in range(PAIRS):
            y = _dot3(r2_ref[0, p, sl, :], hs[p]) + y0_ref[0, p, sl, :]
            hs[p] = _dot3(g_ref[0, p, sm, :], hs[p]) + hc_ref[0, p, sm, :]
            yc = y - _head_sum(y, first) * (1.0 / HEAD_DIM)
            var = _head_sum(yc * yc, first) * (1.0 / HEAD_DIM)
            out = yc * lax.rsqrt(var + RWKV_LN_EPS) * lng_ref[p] + lnb_ref[p]
            y_ref[0, p, sl, :] = (out + bonus_ref[0, p, sl, :]).astype(BF16)
    for p in range(PAIRS):
        state[p] = hs[p]


def _rwkv_scan(gm, hc, r2, y0, bonus, ln_g, ln_b):
    B, P, TP, _ = r2.shape
    rows = next(r for r in (1536, 512) if TP % r == 0)
    nb = TP // rows
    kern = functools.partial(_rwkv_scan_kernel, rows=rows)
    blk = pl.BlockSpec((1, P, rows, LANE), lambda b, i: (b, 0, i, 0))
    mat = pl.BlockSpec((1, P, 2 * rows, LANE), lambda b, i: (b, 0, i, 0))
    par = pl.BlockSpec((P, 1, LANE), lambda b, i: (0, 0, 0))
    return pl.pallas_call(
        kern, out_shape=jax.ShapeDtypeStruct((B, P, TP, LANE), BF16), grid=(B, nb),
        in_specs=[mat, mat, blk, blk, blk, par, par], out_specs=blk,
        scratch_shapes=[pltpu.VMEM((P, LANE, LANE), F32)],
        compiler_params=pltpu.CompilerParams(
            dimension_semantics=("arbitrary", "arbitrary"), vmem_limit_bytes=VMEM_LIMIT),
        name="rwkv_scan",
    )(gm, hc, r2, y0, bonus, ln_g, ln_b)


def _sb_kernel(*refs, t):
    q_refs, k_refs, v_refs = refs[:PAIRS], refs[PAIRS:2 * PAIRS], refs[2 * PAIRS:3 * PAIRS]
    o_ref = refs[3 * PAIRS]
    qi = pl.program_id(1)
    first = lax.broadcasted_iota(jnp.int32, (t, LANE), 1) < HEAD_DIM
    heads = [(hd // 2, hd % 2) for hd in range(SB_HEADS)]
    qs = []
    for slot, half in heads:
        qp = q_refs[slot][0, 0]
        keep = first if half == 0 else jnp.logical_not(first)
        qs.append(jnp.where(keep, qp, jnp.zeros_like(qp)))
    nu = len(heads)
    qpos = qi * t + lax.broadcasted_iota(jnp.int32, (t, 1), 0)
    kloc = lax.broadcasted_iota(jnp.int32, (1, t), 1)
    upper = (lax.broadcasted_iota(jnp.int32, (t, t), 0)
             > lax.broadcasted_iota(jnp.int32, (t, t), 1)).astype(BF16)

    def sweep(blocks, accs, cs):
        units = []
        for kb, mask in blocks:
            start = pl.multiple_of(kb * t, t)
            kblks = [k_refs[s][0, 0, pl.ds(start, t), :] for s in range(PAIRS)]
            vblks = [v_refs[s][0, 0, pl.ds(start, t), :] for s in range(PAIRS)]
            m = None if mask is None else mask(start + kloc)
            units += [(u, kblks[slot], vblks[slot], m) for u, (slot, _) in enumerate(heads)]
        zs = [_dot_nt(qs[u], kblk) for u, kblk, _, _ in units]
        sps = [jnp.maximum(z, 0.0) + jnp.log2(1.0 + jnp.exp2(-jnp.abs(z))) for z in zs]
        spms = [sp if un[3] is None else jnp.where(un[3], sp, 0.0) for sp, un in zip(sps, units)]
        laters = [_dot(spm.astype(BF16), upper) for spm in spms]
        accs, cs = list(accs), list(cs)
        weights = []
        for (u, _, _, m), z, sp, spm, later in zip(units, zs, sps, spms, laters):
            a = jnp.exp2(z - sp - later - cs[u])
            weights.append(a if m is None else jnp.where(m, a, 0.0))
            cs[u] = cs[u] + jnp.sum(spm, axis=-1, keepdims=True)
        for (u, _, vblk, _), a in zip(units, weights):
            accs[u] = accs[u] + _dot(a.astype(BF16), vblk)
        return accs, cs

    def block(kb, accs, cs, mask):
        return sweep([(kb, mask)], accs, cs)

    def live(cs):
        low = cs[0]
        for c in cs[1:]:
            low = jnp.minimum(low, c)
        return (jnp.min(low) < SB_CUTOFF_LOG2).astype(jnp.int32)

    zero_acc = [jnp.zeros((t, LANE), F32) for _ in range(nu)]
    zero_c = [jnp.zeros((t, 1), F32) for _ in range(nu)]
    near = [(qi, lambda kpos: (kpos >= PAD) & (kpos < qpos))]
    for j in range(1, SB_FUSED):
        near.append((jnp.maximum(qi - j, 0), lambda kpos, j=j: (kpos >= PAD) & (qi >= j)))
    accs, cs = sweep(near, zero_acc, zero_c)

    def cond(carry):
        kb, alive = carry[0], carry[1]
        return (kb >= 1) & (alive > 0)

    def body(carry):
        kb = carry[0]
        accs, cs = block(kb, list(carry[2:2 + nu]), list(carry[2 + nu:]), None)
        return (kb - 1, live(cs), *accs, *cs)

    carry = lax.while_loop(cond, body, (qi - SB_FUSED, live(cs), *accs, *cs))

    def front(carry):
        accs, cs = block(0, list(carry[2:2 + nu]), list(carry[2 + nu:]), lambda kpos: kpos >= PAD)
        return (carry[0], carry[1], *accs, *cs)

    carry = lax.cond((carry[0] == 0) & (carry[1] > 0), front, lambda c: c, carry)
    accs = carry[2:2 + nu]
    for slot in range(PAIRS):
        lo = accs[2 * slot]
        hi = accs[2 * slot + 1] if 2 * slot + 1 < nu else 0.0
        o_ref[0, slot] = jnp.where(first, lo, hi).astype(BF16)


def _sb_attention(sb, *, t=SB_TILE):
    B, _, TP, _ = sb.shape
    kern = functools.partial(_sb_kernel, t=t)
    q_specs = [pl.BlockSpec((1, 1, t, LANE), lambda b, i, s=s: (b, s, i, 0)) for s in range(PAIRS)]
    kv_specs = [pl.BlockSpec((1, 1, TP, LANE), lambda b, i, s=s: (b, s, 0, 0),
                             pipeline_mode=pl.Buffered(1)) for s in range(PAIRS, 3 * PAIRS)]
    return pl.pallas_call(
        kern, out_shape=jax.ShapeDtypeStruct((B, PAIRS, TP, LANE), BF16),
        grid=(B, TP // t),
        in_specs=q_specs + kv_specs,
        out_specs=pl.BlockSpec((1, PAIRS, t, LANE), lambda b, i: (b, 0, i, 0)),
        compiler_params=pltpu.CompilerParams(
            dimension_semantics=("arbitrary",) * 2, vmem_limit_bytes=VMEM_LIMIT),
        name="sb_attention",
    )(*([sb] * (3 * PAIRS)))


def _fox_kernel(shift_ref, fe_ref, q_ref, k_ref, v_ref, o_ref, lo_ref, *, t):
    b = pl.program_id(0)
    h = pl.program_id(1)
    qi = pl.program_id(2)
    q = q_ref[0, 0]
    shift = shift_ref[0]
    thr = FOX_SKIP_LOG2 + 2.0 * shift
    f_q = fe_ref[b, h, jnp.maximum(qi - 1, 0)]
    lo = lax.while_loop(lambda n: (n < qi) & (fe_ref[b, h, n] - f_q > thr), lambda n: n + 1,
                        jnp.where(qi == 0, 0, lo_ref[0]))
    lo_ref[0] = lo
    qpos = qi * t + lax.broadcasted_iota(jnp.int32, (t, 1), 0)
    kloc = lax.broadcasted_iota(jnp.int32, (1, t), 1)

    def block(kb, carry, masked, online):
        acc, m = carry
        start = pl.multiple_of(kb * t, t)
        kblk = k_ref[0, 0, pl.ds(start, t), :]
        vblk = v_ref[0, 0, pl.ds(start, t), :]
        s = _dot_nt(q, kblk)
        if masked:
            kpos = start + kloc
            s = jnp.where(kpos <= qpos, s, NEG_INF)
        if online:
            m_new = jnp.maximum(m, jnp.max(s, axis=-1, keepdims=True))
            acc = jnp.exp2(m - m_new) * acc
            m = m_new
        pr = jnp.exp2(s - m)
        return acc + _dot(pr.astype(BF16), vblk), m

    def sweep(online):
        def run(carry):
            start = lo
            n = qi - lo

            def group(i, c):
                acc, m = c
                kvs = []
                for j in range(FOX_GROUP):
                    st = pl.multiple_of((start + FOX_GROUP * i + j) * t, t)
                    kvs.append((k_ref[0, 0, pl.ds(st, t), :], v_ref[0, 0, pl.ds(st, t), :]))
                ss = [_dot_nt(q, kblk) for kblk, _ in kvs]
                prs = [jnp.exp2(s - m).astype(BF16) for s in ss]
                for pr, (_, vblk) in zip(prs, kvs):
                    acc = acc + _dot(pr, vblk)
                return acc, m

            grouped = 0 if online else (n // FOX_GROUP) * FOX_GROUP
            if not online:
                carry = lax.fori_loop(0, n // FOX_GROUP, group, carry)
            carry = lax.fori_loop(start + grouped, qi,
                                  lambda kb, c: block(kb, c, False, online), carry)
            return block(qi, carry, True, online)
        return run

    acc0 = jnp.zeros((t, LANE), F32)
    fixed = shift <= FOX_FIXED_SHIFT_MAX
    m0 = jnp.where(fixed, jnp.full((t, 1), shift, F32), jnp.full((t, 1), NEG_INF, F32))
    acc, _ = lax.cond(fixed, sweep(False), sweep(True), (acc0, m0))
    lane = lax.broadcasted_iota(jnp.int32, (t, LANE), 1)
    denom = jnp.sum(jnp.where(lane == HEAD_DIM, acc, 0.0), axis=-1, keepdims=True)
    o_ref[0, 0] = (acc / jnp.where(denom > 0.0, denom, 1.0)).astype(BF16)


def _fox_attention(shift, fe, fx, *, t=FOX_TILE):
    B, _, TP, _ = fx.shape
    H = FOX_HEADS
    kern = functools.partial(_fox_kernel, t=t)
    return pl.pallas_call(
        kern, out_shape=jax.ShapeDtypeStruct((B, H, TP, LANE), BF16),
        grid=(B, H, TP // t),
        in_specs=[pl.BlockSpec(memory_space=pltpu.SMEM),
                  pl.BlockSpec(memory_space=pltpu.SMEM),
                  pl.BlockSpec((1, 1, t, LANE), lambda b, h, i: (b, h, i, 0)),
                  pl.BlockSpec((1, 1, TP, LANE), lambda b, h, i: (b, h + H, 0, 0)),
                  pl.BlockSpec((1, 1, TP, LANE), lambda b, h, i: (b, h + 2 * H, 0, 0))],
        out_specs=pl.BlockSpec((1, 1, t, LANE), lambda b, h, i: (b, h, i, 0)),
        scratch_shapes=[pltpu.SMEM((1,), jnp.int32)],
        compiler_params=pltpu.CompilerParams(
            dimension_semantics=("arbitrary",) * 3, vmem_limit_bytes=VMEM_LIMIT),
        name="fox_attention",
    )(shift, fe, fx, fx, fx)


def _merge_kernel(*refs, tm, n_valid, first, n_seq):
    n_stream = 2 if first else 1
    (ya_ref, yb_ref, yc_ref, gate_ref,
     wpa_ref, wpb_ref, wpc_ref, wout_ref, o_ref) = refs[n_stream:]
    i = pl.program_id(1)

    def proj(slots, w_ref):
        return _dot(jnp.concatenate(slots, axis=-1), w_ref[...])

    pa = proj([ya_ref[0, s] for s in range(PAIRS)], wpa_ref)
    pb = proj([yb_ref[0, s] for s in range(PAIRS)], wpb_ref)
    pc = proj([yc_ref[0, s] for s in range(FOX_HEADS)], wpc_ref)
    merged = gate_ref[0, :, 0:D_MODEL].astype(F32) * pa
    merged = merged + gate_ref[0, :, D_MODEL:2 * D_MODEL].astype(F32) * pb
    merged = merged + gate_ref[0, :, 2 * D_MODEL:3 * D_MODEL].astype(F32) * pc
    out = _stream_tile(i, refs[:n_stream], n_seq) + _dot(merged.astype(BF16), wout_ref[...])
    t = i * tm + lax.broadcasted_iota(jnp.int32, (tm, 1), 0)
    o_ref[0] = jnp.where((t >= PAD) & (t < PAD + n_valid), out, 0.0)


def _merge(stream, ya, yb, yc, gates, wpa, wpb, wpc, wout, *, n_valid, n_seq, tm=FRAME_TILE):
    first = len(stream) == 2
    B, _, TP, _ = ya.shape
    D = D_MODEL
    kern = functools.partial(_merge_kernel, tm=tm, n_valid=n_valid, first=first, n_seq=n_seq)
    pair_blk = lambda ns: pl.BlockSpec((1, ns, tm, LANE), lambda b, i: (b, 0, i, 0))
    return pl.pallas_call(
        kern, out_shape=jax.ShapeDtypeStruct((B, TP, D), F32), grid=(B, TP // tm),
        in_specs=_stream_specs(first, tm, n_seq) + [
                  pair_blk(PAIRS),
                  pair_blk(PAIRS), pair_blk(FOX_HEADS),
                  pl.BlockSpec((1, tm, 3 * D), lambda b, i: (b, i, 0)),
                  _const_spec((PAIRS * LANE, D)),
                  _const_spec((PAIRS * LANE, D)),
                  _const_spec((FOX_HEADS * LANE, D)),
                  _const_spec((D, D))],
        out_specs=pl.BlockSpec((1, tm, D), lambda b, i: (b, i, 0)),
        compiler_params=pltpu.CompilerParams(
            dimension_semantics=("arbitrary", "arbitrary"), vmem_limit_bytes=VMEM_LIMIT),
        name="merge",
    )(*stream, ya, yb, yc, gates, wpa, wpb, wpc, wout)


def _moe_kernel(*refs, tm, n_valid, last):
    n_stream = 2 if last else 1
    g_ref, wr_ref, br_ref, wg_ref, wu_ref, wd_ref, o_ref, he_ref, ys_ref = refs[n_stream:]
    i = pl.program_id(1)
    x = jnp.concatenate([r[0] for r in refs[:n_stream]], axis=0)
    ms = jnp.mean(x * x, axis=-1, keepdims=True)
    n = x * lax.rsqrt(ms + NORM_EPS) * g_ref[...]

    lg = _dot3(n, wr_ref[...]) + br_ref[...]
    lane = lax.broadcasted_iota(jnp.int32, (tm, LANE), 1)
    big = jnp.int32(LANE)
    is_group = lane < N_GROUPS
    gl = jnp.where(is_group, lg, -jnp.inf)
    gmax = jnp.max(gl, axis=-1, keepdims=True)
    g_idx = jnp.min(jnp.where(is_group & (gl == gmax), lane, big), axis=-1, keepdims=True)
    p_group = 1.0 / jnp.sum(jnp.where(is_group, jnp.exp(gl - gmax), 0.0), axis=-1, keepdims=True)
    lo = N_GROUPS + g_idx * EXPERTS_PER_GROUP
    in_grp = (lane >= lo) & (lane < lo + EXPERTS_PER_GROUP)
    el = jnp.where(in_grp, lg, -jnp.inf)
    top1 = jnp.max(el, axis=-1, keepdims=True)
    i1 = jnp.min(jnp.where(in_grp & (el == top1), lane, big), axis=-1, keepdims=True)
    rest = in_grp & (lane != i1)
    el2 = jnp.where(rest, lg, -jnp.inf)
    top2 = jnp.max(el2, axis=-1, keepdims=True)
    i2 = jnp.min(jnp.where(rest & (el2 == top2), lane, big), axis=-1, keepdims=True)
    e2 = jnp.exp(top2 - top1)
    p1 = 1.0 / (1.0 + e2)
    p2 = e2 / (1.0 + e2)
    comb = p_group * (jnp.where(lane == i1, p1, 0.0) + jnp.where(lane == i2, p2, 0.0))

    n16 = n.astype(BF16)

    onehot = (lane == g_idx).astype(BF16)
    earlier = (lax.broadcasted_iota(jnp.int32, (tm, tm), 1)
               < lax.broadcasted_iota(jnp.int32, (tm, tm), 0)).astype(BF16)
    cnt = _dot(earlier, onehot)
    rank = jnp.sum(jnp.where(lane == g_idx, cnt, 0.0), axis=-1, keepdims=True).astype(jnp.int32)
    n_pass = (jnp.max(rank) + MOE_CAP) // MOE_CAP
    comb16 = comb.astype(BF16)
    slot_lane = lax.broadcasted_iota(jnp.int32, (tm, N_GROUPS * MOE_CAP), 1)

    def one_pass(p, y):
        r = rank - p * MOE_CAP
        slot = jnp.where((r >= 0) & (r < MOE_CAP), g_idx * MOE_CAP + r, -1)
        place = (slot_lane == slot).astype(BF16)
        xs = _dot_tn(place, n16).astype(BF16)
        cw = _dot_tn(place, comb16)
        for g in range(N_GROUPS):
            rows = slice(g * MOE_CAP, (g + 1) * MOE_CAP)
            for e in range(EXPERTS_PER_GROUP):
                ee = g * EXPERTS_PER_GROUP + e
                gate = _dot(xs[rows], wg_ref[ee])
                up = _dot(xs[rows], wu_ref[ee])
                c_e = cw[rows, N_GROUPS + ee:N_GROUPS + ee + 1]
                he_ref[:, e * D_EXPERT:(e + 1) * D_EXPERT] = (
                    gate * _sigmoid(gate) * up * c_e).astype(BF16)
            gw = EXPERTS_PER_GROUP * D_EXPERT
            ys_ref[rows, :] = _dot(he_ref[...], wd_ref[g * gw:(g + 1) * gw, :]).astype(BF16)
        return y + _dot(place, ys_ref[...])

    out = x + lax.fori_loop(0, n_pass, one_pass, jnp.zeros((tm, D_MODEL), F32))
    if last:
        o_ref[0] = out
    else:
        t = i * tm + lax.broadcasted_iota(jnp.int32, (tm, 1), 0)
        o_ref[0] = jnp.where((t >= PAD) & (t < PAD + n_valid), out, 0.0)


def _moe(h, g2, wr, br, wg, wu, wd, *, n_valid, n_seq, last):
    B, TP, D = h.shape
    tm = MOE_TILE
    kern = functools.partial(_moe_kernel, tm=tm, n_valid=n_valid, last=last)
    EW = N_EXPERTS * D_EXPERT
    if last:
        half = tm // FRAME_TILE
        assert half == 2 and n_seq % half == 0
        rows_out = n_seq * FRAME_TILE
        stream_specs = [pl.BlockSpec((1, FRAME_TILE, D), lambda b, i, j=j: (b, half * i + 1 + j, 0))
                        for j in range(half)]
        stream = [h] * half
    else:
        rows_out = TP
        stream_specs = [pl.BlockSpec((1, tm, D), lambda b, i: (b, i, 0))]
        stream = [h]
    return pl.pallas_call(
        kern, out_shape=jax.ShapeDtypeStruct((B, rows_out, D), F32), grid=(B, rows_out // tm),
        in_specs=stream_specs + [
                  _const_spec((1, D)), _const_spec((D, LANE)), _const_spec((1, LANE)),
                  _const_spec((N_EXPERTS, D, D_EXPERT)), _const_spec((N_EXPERTS, D, D_EXPERT)),
                  _const_spec((EW, D))],
        out_specs=pl.BlockSpec((1, tm, D), lambda b, i: (b, i, 0)),
        scratch_shapes=[pltpu.VMEM((MOE_CAP, EXPERTS_PER_GROUP * D_EXPERT), BF16),
                        pltpu.VMEM((N_GROUPS * MOE_CAP, D), BF16)],
        compiler_params=pltpu.CompilerParams(
            dimension_semantics=("arbitrary", "arbitrary"), vmem_limit_bytes=VMEM_LIMIT),
        name="moe",
    )(*stream, g2, wr, br, wg, wu, wd)


def _transpose_kernel(x_ref, o_ref):
    o_ref[...] = x_ref[...].T.astype(BF16)


def _transpose_to_bf16(a, *, tn=LANE):
    N, K = a.shape
    assert N % tn == 0
    return pl.pallas_call(
        _transpose_kernel, out_shape=jax.ShapeDtypeStruct((K, N), BF16), grid=(N // tn,),
        in_specs=[pl.BlockSpec((tn, K), lambda i: (i, 0))],
        out_specs=pl.BlockSpec((K, tn), lambda i: (0, i)),
        compiler_params=pltpu.CompilerParams(dimension_semantics=("arbitrary",)),
        name="transpose_cast",
    )(a)


def _pack_w_in(w_in):
    wt = jnp.swapaxes(w_in, 0, 1)
    rest = wt[SHIFT_W:]
    rows = [wt[:SHIFT_W]]
    off = 0
    for heads in (SB_HEADS,) * 3 + (FOX_HEADS,) * 3:
        rows.append(jnp.pad(rest[off:off + heads * HEAD_DIM],
                            ((0, PAIRS * LANE - heads * HEAD_DIM), (0, 0))))
        off += heads * HEAD_DIM
    rows.append(jnp.pad(rest[off:off + FOX_HEADS], ((0, LANE - FOX_HEADS), (0, 0))))
    rows.append(rest[off + FOX_HEADS:])
    return _transpose_to_bf16(jnp.concatenate(rows, axis=0))


def _per_pair_cols(w):
    return w.reshape(w.shape[0], PAIRS, LANE).transpose(1, 0, 2)


def kernel(x, meta_tokens, norm1_g, w_in, rwkv_mu, rwkv_w_up, rwkv_w0, rwkv_a_up, rwkv_a0,
           rwkv_k_k, rwkv_k_a, rwkv_r_k, rwkv_ln_g, rwkv_ln_b, fox_f_b, fox_q_g, fox_k_g,
           w_p_rwkv, w_p_sb, w_p_fox, w_out, norm2_g, moe_wg, moe_bg, moe_we, moe_be,
           moe_w_gate, moe_w_up, moe_w_down):
    B, S, D = x.shape
    depth = w_in.shape[0]
    L = N_META + S
    TP = -(-(PAD + L) // ROW_TILE) * ROW_TILE
    assert S % FRAME_TILE == 0
    n_seq = S // FRAME_TILE
    meta = jnp.broadcast_to(meta_tokens[None].astype(x.dtype), (B, N_META, D))
    stream = (x, jnp.concatenate([jnp.zeros((B, PAD, D), x.dtype), meta], axis=1))
    H = RWKV_HEADS
    EW = N_EXPERTS * D_EXPERT
    for l in range(depth):
        fb = jnp.zeros((1, LANE), F32).at[0, :FOX_HEADS].set(fox_f_b[l])
        rkv, wdad, sb, fx, fend, gates = _inproj(
            stream, norm1_g[l][None], _pack_w_in(w_in[l]), rwkv_mu[l][None], fb,
            jnp.tile(fox_q_g[l], 2)[None], jnp.tile(fox_k_g[l], 2)[None], TP=TP, n_seq=n_seq)
        hv = lambda p: p.reshape(PAIRS, 1, LANE)
        gm, hc, r2, y0, bonus = _rwkv_chunks(
            rkv, wdad, _per_pair_cols(rwkv_w_up[l]), hv(rwkv_w0[l]),
            _per_pair_cols(rwkv_a_up[l]), hv(rwkv_a0[l]),
            hv(rwkv_k_k[l]), hv(rwkv_k_a[l]), hv(rwkv_r_k[l]))
        ya = _rwkv_scan(gm, hc, r2, y0, bonus, hv(rwkv_ln_g[l]), hv(rwkv_ln_b[l]))
        yb = _sb_attention(sb)
        shift = (HEAD_DIM * ATT_SCALE * LOG2E
                 * jnp.max(jnp.abs(fox_q_g[l])) * jnp.max(jnp.abs(fox_k_g[l]))).reshape(1)
        step = FOX_TILE // INPROJ_TILE
        fe = jnp.transpose(fend[:, step - 1::step, 0, :FOX_HEADS], (0, 2, 1))
        yc = _fox_attention(shift, fe, fx)
        pc = jnp.pad(w_p_fox[l].reshape(FOX_HEADS, HEAD_DIM, D),
                     ((0, 0), (0, LANE - HEAD_DIM), (0, 0))).reshape(FOX_HEADS * LANE, D).astype(BF16)
        pp = lambda w, nh: jnp.pad(w, ((0, PAIRS * LANE - nh * HEAD_DIM), (0, 0))).astype(BF16)
        h = _merge(stream, ya, yb, yc, gates, pp(w_p_rwkv[l], H), pp(w_p_sb[l], SB_HEADS),
                   pc, w_out[l].astype(BF16), n_valid=L, n_seq=n_seq)
        wr = jnp.zeros((D, LANE), F32).at[:, :N_GROUPS].set(moe_wg[l])
        wr = wr.at[:, N_GROUPS:N_GROUPS + N_EXPERTS].set(moe_we[l])
        br = jnp.zeros((1, LANE), F32).at[0, :N_GROUPS].set(moe_bg[l])
        br = br.at[0, N_GROUPS:N_GROUPS + N_EXPERTS].set(moe_be[l])
        wg = moe_w_gate[l].astype(BF16)
        wu = moe_w_up[l].astype(BF16)
        wd = moe_w_down[l].reshape(EW, D).astype(BF16)
        h = _moe(h, norm2_g[l][None], wr, br, wg, wu, wd, n_valid=L, n_seq=n_seq,
                 last=l == depth - 1)
        stream = (h,)
    return h
```

```python
import functools
import math

import jax
import jax.numpy as jnp
from jax import lax
from jax.experimental import pallas as pl
from jax.experimental.pallas import tpu as pltpu

D_MODEL = 1024
HEAD_DIM = 64
N_META = 16
FRAME_TILE = 256
MOE_TILE = 512
MOE_CAP = 192
PAD = FRAME_TILE - N_META
RWKV_HEADS = 6
SB_HEADS = 5
FOX_HEADS = 5
RWKV_W = RWKV_HEADS * HEAD_DIM
DECAY_RANK = 64
ICLR_RANK = 64
SHIFT_W = 3 * RWKV_W + DECAY_RANK + ICLR_RANK
N_GROUPS = 4
EXPERTS_PER_GROUP = 4
N_EXPERTS = 16
D_EXPERT = 256
NORM_EPS = 1e-6
RWKV_LN_EPS = 64e-5
NEG_INF = -1e30
ATT_SCALE = 1.0 / math.sqrt(HEAD_DIM)

LANE = 128
ROW_TILE = 512
PAIRS = 3
QKV_W = 3 * PAIRS * LANE
SEG_SB = SHIFT_W
SEG_FOX = SEG_SB + QKV_W
SEG_GATE = SEG_FOX + QKV_W + LANE
D_IN_PAD = SEG_GATE + 3 * D_MODEL
SB_CUTOFF_LOG2 = 160.0
SB_TILE = 128
SB_FUSED = 3
LOG2E = 1.4426950408889634
FOX_FIXED_SHIFT_MAX = 50.0
FOX_SKIP_LOG2 = 150.0
FOX_TILE = 512
FOX_GROUP = 4
INPROJ_TILE = FRAME_TILE

CHUNK = 64
VMEM_LIMIT = 56 * 1024 * 1024

F32 = jnp.float32
BF16 = jnp.bfloat16


def _log_sigmoid(x):
    return jnp.minimum(x, 0.0) - jnp.log1p(jnp.exp(-jnp.abs(x)))


def _sigmoid(x):
    return 1.0 / (1.0 + jnp.exp(-x))


def _dot(a, b, **kw):
    return jnp.dot(a, b, preferred_element_type=F32, **kw)


def _dot_nt(a, b):
    return lax.dot_general(a, b, (((1,), (1,)), ((), ())), preferred_element_type=F32)


def _dot_tn(a, b):
    return lax.dot_general(a, b, (((0,), (0,)), ((), ())), preferred_element_type=F32)


def _dot3(a, b):
    ah = a.astype(BF16)
    al = (a - ah.astype(F32)).astype(BF16)
    bh = b.astype(BF16)
    bl = (b - bh.astype(F32)).astype(BF16)
    return _dot(ah, bh) + (_dot(ah, bl) + _dot(al, bh))


def _stream_specs(first, tm, n_seq):
    if not first:
        return [pl.BlockSpec((1, tm, D_MODEL), lambda b, i: (b, i, 0))]
    return [pl.BlockSpec((1, tm, D_MODEL), lambda b, i: (b, jnp.clip(i - 1, 0, n_seq - 1), 0)),
            pl.BlockSpec((1, tm, D_MODEL), lambda b, i: (b, 0, 0))]


def _stream_tile(i, refs, n_seq):
    if len(refs) == 1:
        return refs[0][0]
    x_ref, head_ref = refs
    return jnp.where(i == 0, head_ref[0], jnp.where(i <= n_seq, x_ref[0], 0.0))


def _const_spec(shape):
    n = len(shape)
    return pl.BlockSpec(shape, lambda *_: (0,) * n, pipeline_mode=pl.Buffered(1))


def _inproj_kernel(*refs, tm, first, n_seq):
    n_stream = 2 if first else 1
    (g_ref, w_ref, mu_ref, fb_ref, fqg_ref, fkg_ref,
     rkv_ref, wdad_ref, sb_ref, fx_ref, fend_ref, gate_ref, carry_u, carry_f) = refs[n_stream:]
    i = pl.program_id(1)

    @pl.when(i == 0)
    def _():
        carry_u[...] = jnp.zeros_like(carry_u)
        carry_f[...] = jnp.zeros_like(carry_f)

    x = _stream_tile(i, refs[:n_stream], n_seq)
    ms = jnp.mean(x * x, axis=-1, keepdims=True)
    n = (x * lax.rsqrt(ms + NORM_EPS) * g_ref[...]).astype(BF16)
    row = lax.broadcasted_iota(jnp.int32, (tm, 1), 0)

    us = _dot(n, w_ref[:, 0:SHIFT_W])
    prev = pltpu.roll(us, 1, axis=0)
    prev = jnp.where(row == 0, carry_u[...], prev)
    carry_u[...] = us[tm - 1:tm, :]
    ush = us + (prev - us) * mu_ref[...]
    for j in range(3 * PAIRS):
        rkv_ref[0, j] = ush[:, j * LANE:(j + 1) * LANE]
    wdad_ref[0] = ush[:, 3 * RWKV_W:SHIFT_W]

    usb = _dot(n, w_ref[:, SEG_SB:SEG_SB + QKV_W])
    for j in range(3 * PAIRS):
        piece = usb[:, j * LANE:(j + 1) * LANE]
        if j < PAIRS:
            piece = piece * (ATT_SCALE * LOG2E)
        sb_ref[0, j] = piece.astype(BF16)

    uf = _dot(n, w_ref[:, SEG_FOX:SEG_FOX + QKV_W + LANE])
    lane = lax.broadcasted_iota(jnp.int32, (tm, LANE), 1)
    first = lane < HEAD_DIM
    t_glob = i * tm + row
    logf = _log_sigmoid(uf[:, QKV_W:QKV_W + LANE] + fb_ref[...])
    logf = jnp.where((lane < FOX_HEADS) & (t_glob >= PAD), logf, 0.0)
    tri = (lax.broadcasted_iota(jnp.int32, (tm, tm), 0)
           >= lax.broadcasted_iota(jnp.int32, (tm, tm), 1)).astype(BF16)
    f_hi = logf.astype(BF16)
    f_mid = (logf - f_hi.astype(F32)).astype(BF16)
    f_lo = (logf - f_hi.astype(F32) - f_mid.astype(F32)).astype(BF16)
    cum = _dot(tri, f_hi) + _dot(tri, f_mid) + _dot(tri, f_lo) + carry_f[...]
    carry_f[...] = cum[tm - 1:tm, :]
    fend_ref[0, 0] = cum[tm - 1:tm, :] * LOG2E

    def split3(hd):
        f2 = jnp.sum(jnp.where(lane == hd, cum, 0.0), axis=-1, keepdims=True) * LOG2E
        hi = f2.astype(BF16).astype(F32)
        mid = (f2 - hi).astype(BF16).astype(F32)
        return hi, mid, f2 - hi - mid

    def tail_cols(vals):
        out = jnp.zeros((tm, LANE), F32)
        for o, val in enumerate(vals):
            out = jnp.where(lane == HEAD_DIM + o, val, out)
        return out

    splits = [split3(hd) for hd in range(FOX_HEADS)]
    for j in range(3 * PAIRS):
        piece = uf[:, j * LANE:(j + 1) * LANE]
        kind = j // PAIRS
        if kind < 2:
            gain = fqg_ref[...] if kind == 0 else fkg_ref[...]
            sq = piece * piece
            ms0 = jnp.sum(jnp.where(first, sq, 0.0), axis=-1, keepdims=True) * (1.0 / HEAD_DIM)
            ms1 = jnp.sum(jnp.where(first, 0.0, sq), axis=-1, keepdims=True) * (1.0 / HEAD_DIM)
            inv = jnp.where(first, lax.rsqrt(ms0 + NORM_EPS), lax.rsqrt(ms1 + NORM_EPS))
            piece = piece * inv * gain
            if kind == 0:
                piece = piece * (ATT_SCALE * LOG2E)
        swapped = pltpu.roll(piece, HEAD_DIM, axis=1)
        for half in range(2):
            hd = 2 * (j % PAIRS) + half
            if hd >= FOX_HEADS:
                continue
            if kind == 2:
                extra = tail_cols([1.0])
            else:
                hi, mid, lo = splits[hd]
                extra = tail_cols([hi, mid, lo, 1.0, 1.0, 1.0] if kind == 0 else
                                  [1.0, 1.0, 1.0, jnp.where(t_glob >= PAD, -hi, NEG_INF), -mid, -lo])
            body = piece if half == 0 else swapped
            fx_ref[0, kind * FOX_HEADS + hd] = jnp.where(first, body, extra).astype(BF16)

    ug = _dot(n, w_ref[:, SEG_GATE:D_IN_PAD])
    gate_ref[0] = _sigmoid(ug).astype(BF16)


def _inproj(stream, g1, w_in_p, mu, fb, fqg, fkg, *, TP, n_seq, tm=INPROJ_TILE):
    first = len(stream) == 2
    B, D = stream[0].shape[0], D_MODEL
    nb = TP // tm
    kern = functools.partial(_inproj_kernel, tm=tm, first=first, n_seq=n_seq)
    out_shape = (
        jax.ShapeDtypeStruct((B, 3 * PAIRS, TP, LANE), F32),
        jax.ShapeDtypeStruct((B, TP, LANE), F32),
        jax.ShapeDtypeStruct((B, 3 * PAIRS, TP, LANE), BF16),
        jax.ShapeDtypeStruct((B, 3 * FOX_HEADS, TP, LANE), BF16),
        jax.ShapeDtypeStruct((B, nb, 1, LANE), F32),
        jax.ShapeDtypeStruct((B, TP, 3 * D_MODEL), BF16),
    )
    in_specs = _stream_specs(first, tm, n_seq) + [
        _const_spec((1, D)),
        _const_spec((D, D_IN_PAD)),
        _const_spec((1, SHIFT_W)),
        _const_spec((1, LANE)),
        _const_spec((1, LANE)),
        _const_spec((1, LANE)),
    ]
    out_specs = (
        pl.BlockSpec((1, 3 * PAIRS, tm, LANE), lambda b, i: (b, 0, i, 0)),
        pl.BlockSpec((1, tm, LANE), lambda b, i: (b, i, 0)),
        pl.BlockSpec((1, 3 * PAIRS, tm, LANE), lambda b, i: (b, 0, i, 0)),
        pl.BlockSpec((1, 3 * FOX_HEADS, tm, LANE), lambda b, i: (b, 0, i, 0)),
        pl.BlockSpec((1, 1, 1, LANE), lambda b, i: (b, i, 0, 0)),
        pl.BlockSpec((1, tm, 3 * D_MODEL), lambda b, i: (b, i, 0)),
    )
    return pl.pallas_call(
        kern, out_shape=out_shape, grid=(B, nb), in_specs=in_specs, out_specs=out_specs,
        scratch_shapes=[pltpu.VMEM((1, SHIFT_W), F32), pltpu.VMEM((1, LANE), F32)],
        compiler_params=pltpu.CompilerParams(
            dimension_semantics=("arbitrary", "arbitrary"), vmem_limit_bytes=VMEM_LIMIT),
        name="inproj",
    )(*stream, g1, w_in_p, mu, fb, fqg, fkg)


def _bdot(a, b):
    return lax.dot_general(a, b, (((2,), (1,)), ((0,), (0,))), preferred_element_type=F32)


def _bdot_nt(a, b):
    return lax.dot_general(a, b, (((2,), (2,)), ((0,), (0,))), preferred_element_type=F32)


def _bdot_tn(a, b):
    return lax.dot_general(a, b, (((1,), (1,)), ((0,), (0,))), preferred_element_type=F32)


def _head_sum(x, first):
    s0 = jnp.sum(jnp.where(first, x, 0.0), axis=-1, keepdims=True)
    s1 = jnp.sum(jnp.where(first, 0.0, x), axis=-1, keepdims=True)
    return jnp.where(first, s0, s1)


def _rwkv_chunk_kernel(r_ref, k_ref, v_ref, wdad_ref, wup_ref, w0_ref, aup_ref, a0_ref,
                       kk_ref, ka_ref, rk_ref,
                       g_ref, hc_ref, r2_ref, y0_ref, bonus_ref, *, rows):
    nc = rows // CHUNK
    r = r_ref[0, 0]
    k = k_ref[0, 0]
    v = v_ref[0, 0]
    wd = wdad_ref[0][:, 0:DECAY_RANK]
    ad = wdad_ref[0][:, DECAY_RANK:DECAY_RANK + ICLR_RANK]
    first = lax.broadcasted_iota(jnp.int32, (rows, LANE), 1) < HEAD_DIM

    pre = w0_ref[0] + _dot3(jnp.tanh(wd), wup_ref[0])
    lw = -jnp.exp(_log_sigmoid(pre) - 0.5)
    iclr = _sigmoid(a0_ref[0] + _dot3(ad, aup_ref[0]))
    kk = k * kk_ref[0]
    kk = kk / jnp.maximum(jnp.sqrt(_head_sum(kk * kk, first)), 1e-12)
    k2 = k * (1.0 + (iclr - 1.0) * ka_ref[0])
    b = kk * iclr
    bonus_ref[0, 0] = _head_sum(r * k2 * rk_ref[0], first) * v

    to3 = lambda x: x.reshape(nc, CHUNK, LANE)
    ri = lax.broadcasted_iota(jnp.int32, (nc, CHUNK, CHUNK), 1)
    ci = lax.broadcasted_iota(jnp.int32, (nc, CHUNK, CHUNK), 2)
    low_incl = ri >= ci
    low_strict = ri > ci
    first3 = lax.broadcasted_iota(jnp.int32, (nc, CHUNK, LANE), 2) < HEAD_DIM

    lw3 = to3(lw)
    tri = low_incl.astype(BF16)
    lw_hi = lw3.astype(BF16)
    cum = _bdot(tri, lw_hi) + _bdot(tri, (lw3 - lw_hi.astype(F32)).astype(BF16))
    cum_end = cum[:, CHUNK - 1:CHUNK, :]
    e_neg = jnp.exp(-cum)
    at = to3(-kk) * jnp.exp(cum - lw3)
    rt = to3(r) * jnp.exp(cum)
    bt = (to3(b) * e_neg).astype(BF16)
    kt = (to3(k2) * e_neg).astype(BF16)
    e_rem = jnp.exp(cum_end - cum)
    bq = (to3(b) * e_rem).astype(BF16)
    kq = (to3(k2) * e_rem).astype(BF16)
    vv = to3(v).astype(BF16)

    heads = range(2)
    sels = [first3, jnp.logical_not(first3)]
    lhs = [jnp.concatenate([jnp.where(sels[hd], at, 0.0), jnp.where(sels[hd], rt, 0.0)],
                           axis=1).astype(BF16) for hd in heads]
    mb = [_bdot_nt(lhs[hd], bt) for hd in heads]
    mk = [_bdot_nt(lhs[hd], kt) for hd in heads]
    m_ak = [jnp.where(low_strict, mk[hd][:, :CHUNK], 0.0).astype(BF16) for hd in heads]
    m_rb = [jnp.where(low_incl, mb[hd][:, CHUNK:], 0.0).astype(BF16) for hd in heads]
    m_rk = [jnp.where(low_incl, mk[hd][:, CHUNK:], 0.0).astype(BF16) for hd in heads]
    p = [jnp.where(low_strict, mb[hd][:, :CHUNK], 0.0) for hd in heads]
    xs = [jnp.concatenate([at, _bdot(m_ak[hd], vv)], axis=-1) for hd in heads]
    for j in range(6):
        p16 = [p[hd].astype(BF16) for hd in heads]
        xs = [xs[hd] + _bdot(p16[hd], xs[hd].astype(BF16)) for hd in heads]
        if j < 5:
            p = [_bdot(p16[hd], p16[hd]) for hd in heads]
    ru = [_bdot(m_rb[hd], xs[hd].astype(BF16)) for hd in heads]
    r2s = [rt + ru[hd][..., :LANE] for hd in heads]
    y0s = [ru[hd][..., LANE:] + _bdot(m_rk[hd], vv) for hd in heads]

    first3w = jnp.concatenate([first3, first3], axis=-1)
    x = jnp.where(first3w, xs[0], xs[1])
    r2_ref[0, 0] = jnp.where(first3, r2s[0], r2s[1]).reshape(rows, LANE)
    y0_ref[0, 0] = jnp.where(first3, y0s[0], y0s[1]).reshape(rows, LANE)

    pg = _bdot_tn(bq, x.astype(BF16))
    ph = pg[..., LANE:] + _bdot_tn(kq, vv)
    rr = lax.broadcasted_iota(jnp.int32, (nc, LANE, LANE), 1)
    cc = lax.broadcasted_iota(jnp.int32, (nc, LANE, LANE), 2)
    same_head = (rr < HEAD_DIM) == (cc < HEAD_DIM)
    g = jnp.where(same_head, pg[..., :LANE], 0.0) + jnp.where(rr == cc, jnp.exp(cum_end), 0.0)
    g_ref[0, 0] = g.reshape(nc * LANE, LANE)
    hc_ref[0, 0] = jnp.where(same_head, ph, 0.0).reshape(nc * LANE, LANE)


def _rwkv_chunks(rkv, wdad, wup, w0, aup, a0, k_k, k_a, r_k):
    B, _, TP, _ = rkv.shape
    rows = next(r for r in (1536, 768, 512, 256) if TP % r == 0)
    nb = TP // rows
    kern = functools.partial(_rwkv_chunk_kernel, rows=rows)
    slot_spec = lambda off: pl.BlockSpec((1, 1, rows, LANE), lambda b, p, i: (b, p + off, i, 0))
    par_mat = pl.BlockSpec((1, DECAY_RANK, LANE), lambda b, p, i: (p, 0, 0))
    par_vec = pl.BlockSpec((1, 1, LANE), lambda b, p, i: (p, 0, 0))
    row_out = jax.ShapeDtypeStruct((B, PAIRS, TP, LANE), F32)
    mat_out = jax.ShapeDtypeStruct((B, PAIRS, 2 * TP, LANE), F32)
    row_spec = pl.BlockSpec((1, 1, rows, LANE), lambda b, p, i: (b, p, i, 0))
    mat_spec = pl.BlockSpec((1, 1, 2 * rows, LANE), lambda b, p, i: (b, p, i, 0))
    return pl.pallas_call(
        kern, out_shape=(mat_out, mat_out, row_out, row_out, row_out), grid=(B, PAIRS, nb),
        in_specs=[slot_spec(0), slot_spec(PAIRS), slot_spec(2 * PAIRS),
                  pl.BlockSpec((1, rows, LANE), lambda b, p, i: (b, i, 0)),
                  par_mat, par_vec, par_mat, par_vec, par_vec, par_vec, par_vec],
        out_specs=(mat_spec, mat_spec, row_spec, row_spec, row_spec),
        compiler_params=pltpu.CompilerParams(
            dimension_semantics=("arbitrary",) * 3, vmem_limit_bytes=VMEM_LIMIT),
        name="rwkv_chunks",
    )(rkv, rkv, rkv, wdad, wup, w0, aup, a0, k_k, k_a, r_k)


def _rwkv_scan_kernel(g_ref, hc_ref, r2_ref, y0_ref, bonus_ref, lng_ref, lnb_ref, y_ref,
                      state, *, rows):
    i = pl.program_id(1)

    @pl.when(i == 0)
    def _():
        state[...] = jnp.zeros_like(state)

    first = lax.broadcasted_iota(jnp.int32, (CHUNK, LANE), 1) < HEAD_DIM
    hs = [state[p] for p in range(PAIRS)]
    for c in range(rows // CHUNK):
        sl = slice(c * CHUNK, (c + 1) * CHUNK)
        sm = slice(c * LANE, (c + 1) * LANE)
        for p in range(PAIRS):
            y = _dot3(r2_ref[0, p, sl, :], hs[p]) + y0_ref[0, p, sl, :]
            hs[p] = _dot3(g_ref[0, p, sm, :], hs[p]) + hc_ref[0, p, sm, :]
            yc = y - _head_sum(y, first) * (1.0 / HEAD_DIM)
            var = _head_sum(yc * yc, first) * (1.0 / HEAD_DIM)
            out = yc * lax.rsqrt(var + RWKV_LN_EPS) * lng_ref[p] + lnb_ref[p]
            y_ref[0, p, sl, :] = (out + bonus_ref[0, p, sl, :]).astype(BF16)
    for p in range(PAIRS):
        state[p] = hs[p]


def _rwkv_scan(gm, hc, r2, y0, bonus, ln_g, ln_b):
    B, P, TP, _ = r2.shape
    rows = next(r for r in (1536, 512) if TP % r == 0)
    nb = TP // rows
    kern = functools.partial(_rwkv_scan_kernel, rows=rows)
    blk = pl.BlockSpec((1, P, rows, LANE), lambda b, i: (b, 0, i, 0))
    mat = pl.BlockSpec((1, P, 2 * rows, LANE), lambda b, i: (b, 0, i, 0))
    par = pl.BlockSpec((P, 1, LANE), lambda b, i: (0, 0, 0))
    return pl.pallas_call(
        kern, out_shape=jax.ShapeDtypeStruct((B, P, TP, LANE), BF16), grid=(B, nb),
        in_specs=[mat, mat, blk, blk, blk, par, par], out_specs=blk,
        scratch_shapes=[pltpu.VMEM((P, LANE, LANE), F32)],
        compiler_params=pltpu.CompilerParams(
            dimension_semantics=("arbitrary", "arbitrary"), vmem_limit_bytes=VMEM_LIMIT),
        name="rwkv_scan",
    )(gm, hc, r2, y0, bonus, ln_g, ln_b)


def _sb_kernel(*refs, t):
    q_refs, k_refs, v_refs = refs[:PAIRS], refs[PAIRS:2 * PAIRS], refs[2 * PAIRS:3 * PAIRS]
    o_ref = refs[3 * PAIRS]
    qi = pl.program_id(1)
    first = lax.broadcasted_iota(jnp.int32, (t, LANE), 1) < HEAD_DIM
    heads = [(hd // 2, hd % 2) for hd in range(SB_HEADS)]
    qs = []
    for slot, half in heads:
        qp = q_refs[slot][0, 0]
        keep = first if half == 0 else jnp.logical_not(first)
        qs.append(jnp.where(keep, qp, jnp.zeros_like(qp)))
    nu = len(heads)
    qpos = qi * t + lax.broadcasted_iota(jnp.int32, (t, 1), 0)
    kloc = lax.broadcasted_iota(jnp.int32, (1, t), 1)
    upper = (lax.broadcasted_iota(jnp.int32, (t, t), 0)
             > lax.broadcasted_iota(jnp.int32, (t, t), 1)).astype(BF16)

    def sweep(blocks, accs, cs):
        units = []
        for kb, mask in blocks:
            start = pl.multiple_of(kb * t, t)
            kblks = [k_refs[s][0, 0, pl.ds(start, t), :] for s in range(PAIRS)]
            vblks = [v_refs[s][0, 0, pl.ds(start, t), :] for s in range(PAIRS)]
            m = None if mask is None else mask(start + kloc)
            units += [(u, kblks[slot], vblks[slot], m) for u, (slot, _) in enumerate(heads)]
        zs = [_dot_nt(qs[u], kblk) for u, kblk, _, _ in units]
        sps = [jnp.maximum(z, 0.0) + jnp.log2(1.0 + jnp.exp2(-jnp.abs(z))) for z in zs]
        spms = [sp if un[3] is None else jnp.where(un[3], sp, 0.0) for sp, un in zip(sps, units)]
        laters = [_dot(spm.astype(BF16), upper) for spm in spms]
        accs, cs = list(accs), list(cs)
        weights = []
        for (u, _, _, m), z, sp, spm, later in zip(units, zs, sps, spms, laters):
            a = jnp.exp2(z - sp - later - cs[u])
            weights.append(a if m is None else jnp.where(m, a, 0.0))
            cs[u] = cs[u] + jnp.sum(spm, axis=-1, keepdims=True)
        for (u, _, vblk, _), a in zip(units, weights):
            accs[u] = accs[u] + _dot(a.astype(BF16), vblk)
        return accs, cs

    def block(kb, accs, cs, mask):
        return sweep([(kb, mask)], accs, cs)

    def live(cs):
        low = cs[0]
        for c in cs[1:]:
            low = jnp.minimum(low, c)
        return (jnp.min(low) < SB_CUTOFF_LOG2).astype(jnp.int32)

    zero_acc = [jnp.zeros((t, LANE), F32) for _ in range(nu)]
    zero_c = [jnp.zeros((t, 1), F32) for _ in range(nu)]
    near = [(qi, lambda kpos: (kpos >= PAD) & (kpos < qpos))]
    for j in range(1, SB_FUSED):
        near.append((jnp.maximum(qi - j, 0), lambda kpos, j=j: (kpos >= PAD) & (qi >= j)))
    accs, cs = sweep(near, zero_acc, zero_c)

    def cond(carry):
        kb, alive = carry[0], carry[1]
        return (kb >= 1) & (alive > 0)

    def body(carry):
        kb = carry[0]
        accs, cs = block(kb, list(carry[2:2 + nu]), list(carry[2 + nu:]), None)
        return (kb - 1, live(cs), *accs, *cs)

    carry = lax.while_loop(cond, body, (qi - SB_FUSED, live(cs), *accs, *cs))

    def front(carry):
        accs, cs = block(0, list(carry[2:2 + nu]), list(carry[2 + nu:]), lambda kpos: kpos >= PAD)
        return (carry[0], carry[1], *accs, *cs)

    carry = lax.cond((carry[0] == 0) & (carry[1] > 0), front, lambda c: c, carry)
    accs = carry[2:2 + nu]
    for slot in range(PAIRS):
        lo = accs[2 * slot]
        hi = accs[2 * slot + 1] if 2 * slot + 1 < nu else 0.0
        o_ref[0, slot] = jnp.where(first, lo, hi).astype(BF16)


def _sb_attention(sb, *, t=SB_TILE):
    B, _, TP, _ = sb.shape
    kern = functools.partial(_sb_kernel, t=t)
    q_specs = [pl.BlockSpec((1, 1, t, LANE), lambda b, i, s=s: (b, s, i, 0)) for s in range(PAIRS)]
    kv_specs = [pl.BlockSpec((1, 1, TP, LANE), lambda b, i, s=s: (b, s, 0, 0),
                             pipeline_mode=pl.Buffered(1)) for s in range(PAIRS, 3 * PAIRS)]
    return pl.pallas_call(
        kern, out_shape=jax.ShapeDtypeStruct((B, PAIRS, TP, LANE), BF16),
        grid=(B, TP // t),
        in_specs=q_specs + kv_specs,
        out_specs=pl.BlockSpec((1, PAIRS, t, LANE), lambda b, i: (b, 0, i, 0)),
        compiler_params=pltpu.CompilerParams(
            dimension_semantics=("arbitrary",) * 2, vmem_limit_bytes=VMEM_LIMIT),
        name="sb_attention",
    )(*([sb] * (3 * PAIRS)))


def _fox_kernel(shift_ref, fe_ref, q_ref, k_ref, v_ref, o_ref, lo_ref, *, t, online):
    b = pl.program_id(0)
    h = pl.program_id(1)
    qi = pl.program_id(2)
    q = q_ref[0, 0]
    shift = shift_ref[0]
    thr = FOX_SKIP_LOG2 + 2.0 * shift
    f_q = fe_ref[b, h, jnp.maximum(qi - 1, 0)]
    lo = lax.while_loop(lambda n: (n < qi) & (fe_ref[b, h, n] - f_q > thr), lambda n: n + 1,
                        jnp.where(qi == 0, 0, lo_ref[0]))
    lo_ref[0] = lo
    qpos = qi * t + lax.broadcasted_iota(jnp.int32, (t, 1), 0)
    kloc = lax.broadcasted_iota(jnp.int32, (1, t), 1)

    def block(kb, carry, masked, online):
        acc, m = carry
        start = pl.multiple_of(kb * t, t)
        kblk = k_ref[0, 0, pl.ds(start, t), :]
        vblk = v_ref[0, 0, pl.ds(start, t), :]
        s = _dot_nt(q, kblk)
        if masked:
            kpos = start + kloc
            s = jnp.where(kpos <= qpos, s, NEG_INF)
        if online:
            m_new = jnp.maximum(m, jnp.max(s, axis=-1, keepdims=True))
            acc = jnp.exp2(m - m_new) * acc
            m = m_new
        pr = jnp.exp2(s - m)
        return acc + _dot(pr.astype(BF16), vblk), m

    def sweep(online):
        def run(carry):
            start = lo
            n = qi - lo

            def group(i, c):
                acc, m = c
                kvs = []
                for j in range(FOX_GROUP):
                    st = pl.multiple_of((start + FOX_GROUP * i + j) * t, t)
                    kvs.append((k_ref[0, 0, pl.ds(st, t), :], v_ref[0, 0, pl.ds(st, t), :]))
                ss = [_dot_nt(q, kblk) for kblk, _ in kvs]
                prs = [jnp.exp2(s - m).astype(BF16) for s in ss]
                for pr, (_, vblk) in zip(prs, kvs):
                    acc = acc + _dot(pr, vblk)
                return acc, m

            grouped = 0 if online else (n // FOX_GROUP) * FOX_GROUP
            if not online:
                carry = lax.fori_loop(0, n // FOX_GROUP, group, carry)
            carry = lax.fori_loop(start + grouped, qi,
                                  lambda kb, c: block(kb, c, False, online), carry)
            return block(qi, carry, True, online)
        return run

    acc0 = jnp.zeros((t, LANE), F32)
    m0 = jnp.full((t, 1), NEG_INF, F32) if online else jnp.full((t, 1), shift, F32)
    acc, _ = sweep(online)((acc0, m0))
    lane = lax.broadcasted_iota(jnp.int32, (t, LANE), 1)
    denom = jnp.sum(jnp.where(lane == HEAD_DIM, acc, 0.0), axis=-1, keepdims=True)
    o_ref[0, 0] = (acc / jnp.where(denom > 0.0, denom, 1.0)).astype(BF16)


def _fox_attention(shift, fe, fx, *, online, t=FOX_TILE):
    B, _, TP, _ = fx.shape
    H = FOX_HEADS
    kern = functools.partial(_fox_kernel, t=t, online=online)
    return pl.pallas_call(
        kern, out_shape=jax.ShapeDtypeStruct((B, H, TP, LANE), BF16),
        grid=(B, H, TP // t),
        in_specs=[pl.BlockSpec(memory_space=pltpu.SMEM),
                  pl.BlockSpec(memory_space=pltpu.SMEM),
                  pl.BlockSpec((1, 1, t, LANE), lambda b, h, i: (b, h, i, 0)),
                  pl.BlockSpec((1, 1, TP, LANE), lambda b, h, i: (b, h + H, 0, 0)),
                  pl.BlockSpec((1, 1, TP, LANE), lambda b, h, i: (b, h + 2 * H, 0, 0))],
        out_specs=pl.BlockSpec((1, 1, t, LANE), lambda b, h, i: (b, h, i, 0)),
        scratch_shapes=[pltpu.SMEM((1,), jnp.int32)],
        compiler_params=pltpu.CompilerParams(
            dimension_semantics=("arbitrary",) * 3, vmem_limit_bytes=VMEM_LIMIT),
        name="fox_attention",
    )(shift, fe, fx, fx, fx)


def _merge_kernel(*refs, tm, n_valid, first, n_seq):
    n_stream = 2 if first else 1
    (ya_ref, yb_ref, yc_ref, gate_ref,
     wpa_ref, wpb_ref, wpc_ref, wout_ref, o_ref) = refs[n_stream:]
    i = pl.program_id(1)

    def proj(slots, w_ref):
        return _dot(jnp.concatenate(slots, axis=-1), w_ref[...])

    pa = proj([ya_ref[0, s] for s in range(PAIRS)], wpa_ref)
    pb = proj([yb_ref[0, s] for s in range(PAIRS)], wpb_ref)
    pc = proj([yc_ref[0, s] for s in range(FOX_HEADS)], wpc_ref)
    merged = gate_ref[0, :, 0:D_MODEL].astype(F32) * pa
    merged = merged + gate_ref[0, :, D_MODEL:2 * D_MODEL].astype(F32) * pb
    merged = merged + gate_ref[0, :, 2 * D_MODEL:3 * D_MODEL].astype(F32) * pc
    out = _stream_tile(i, refs[:n_stream], n_seq) + _dot(merged.astype(BF16), wout_ref[...])
    t = i * tm + lax.broadcasted_iota(jnp.int32, (tm, 1), 0)
    o_ref[0] = jnp.where((t >= PAD) & (t < PAD + n_valid), out, 0.0)


def _merge(stream, ya, yb, yc, gates, wpa, wpb, wpc, wout, *, n_valid, n_seq, tm=FRAME_TILE):
    first = len(stream) == 2
    B, _, TP, _ = ya.shape
    D = D_MODEL
    kern = functools.partial(_merge_kernel, tm=tm, n_valid=n_valid, first=first, n_seq=n_seq)
    pair_blk = lambda ns: pl.BlockSpec((1, ns, tm, LANE), lambda b, i: (b, 0, i, 0))
    return pl.pallas_call(
        kern, out_shape=jax.ShapeDtypeStruct((B, TP, D), F32), grid=(B, TP // tm),
        in_specs=_stream_specs(first, tm, n_seq) + [
                  pair_blk(PAIRS),
                  pair_blk(PAIRS), pair_blk(FOX_HEADS),
                  pl.BlockSpec((1, tm, 3 * D), lambda b, i: (b, i, 0)),
                  _const_spec((PAIRS * LANE, D)),
                  _const_spec((PAIRS * LANE, D)),
                  _const_spec((FOX_HEADS * LANE, D)),
                  _const_spec((D, D))],
        out_specs=pl.BlockSpec((1, tm, D), lambda b, i: (b, i, 0)),
        compiler_params=pltpu.CompilerParams(
            dimension_semantics=("arbitrary", "arbitrary"), vmem_limit_bytes=VMEM_LIMIT),
        name="merge",
    )(*stream, ya, yb, yc, gates, wpa, wpb, wpc, wout)


def _moe_kernel(*refs, tm, n_valid, last):
    n_stream = 2 if last else 1
    g_ref, wr_ref, br_ref, wg_ref, wu_ref, wd_ref, o_ref, he_ref, ys_ref = refs[n_stream:]
    i = pl.program_id(1)
    x = jnp.concatenate([r[0] for r in refs[:n_stream]], axis=0)
    ms = jnp.mean(x * x, axis=-1, keepdims=True)
    n = x * lax.rsqrt(ms + NORM_EPS) * g_ref[...]

    lg = _dot3(n, wr_ref[...]) + br_ref[...]
    lane = lax.broadcasted_iota(jnp.int32, (tm, LANE), 1)
    big = jnp.int32(LANE)
    is_group = lane < N_GROUPS
    gl = jnp.where(is_group, lg, -jnp.inf)
    gmax = jnp.max(gl, axis=-1, keepdims=True)
    g_idx = jnp.min(jnp.where(is_group & (gl == gmax), lane, big), axis=-1, keepdims=True)
    p_group = 1.0 / jnp.sum(jnp.where(is_group, jnp.exp(gl - gmax), 0.0), axis=-1, keepdims=True)
    lo = N_GROUPS + g_idx * EXPERTS_PER_GROUP
    in_grp = (lane >= lo) & (lane < lo + EXPERTS_PER_GROUP)
    el = jnp.where(in_grp, lg, -jnp.inf)
    top1 = jnp.max(el, axis=-1, keepdims=True)
    i1 = jnp.min(jnp.where(in_grp & (el == top1), lane, big), axis=-1, keepdims=True)
    rest = in_grp & (lane != i1)
    el2 = jnp.where(rest, lg, -jnp.inf)
    top2 = jnp.max(el2, axis=-1, keepdims=True)
    i2 = jnp.min(jnp.where(rest & (el2 == top2), lane, big), axis=-1, keepdims=True)
    e2 = jnp.exp(top2 - top1)
    p1 = 1.0 / (1.0 + e2)
    p2 = e2 / (1.0 + e2)
    comb = p_group * (jnp.where(lane == i1, p1, 0.0) + jnp.where(lane == i2, p2, 0.0))

    n16 = n.astype(BF16)

    onehot = (lane == g_idx).astype(BF16)
    earlier = (lax.broadcasted_iota(jnp.int32, (tm, tm), 1)
               < lax.broadcasted_iota(jnp.int32, (tm, tm), 0)).astype(BF16)
    cnt = _dot(earlier, onehot)
    rank = jnp.sum(jnp.where(lane == g_idx, cnt, 0.0), axis=-1, keepdims=True).astype(jnp.int32)
    n_pass = (jnp.max(rank) + MOE_CAP) // MOE_CAP
    comb16 = comb.astype(BF16)
    slot_lane = lax.broadcasted_iota(jnp.int32, (tm, N_GROUPS * MOE_CAP), 1)

    def one_pass(p, y):
        r = rank - p * MOE_CAP
        slot = jnp.where((r >= 0) & (r < MOE_CAP), g_idx * MOE_CAP + r, -1)
        place = (slot_lane == slot).astype(BF16)
        xs = _dot_tn(place, n16).astype(BF16)
        cw = _dot_tn(place, comb16)
        for g in range(N_GROUPS):
            rows = slice(g * MOE_CAP, (g + 1) * MOE_CAP)
            for e in range(EXPERTS_PER_GROUP):
                ee = g * EXPERTS_PER_GROUP + e
                gate = _dot(xs[rows], wg_ref[ee])
                up = _dot(xs[rows], wu_ref[ee])
                c_e = cw[rows, N_GROUPS + ee:N_GROUPS + ee + 1]
                he_ref[:, e * D_EXPERT:(e + 1) * D_EXPERT] = (
                    gate * _sigmoid(gate) * up * c_e).astype(BF16)
            gw = EXPERTS_PER_GROUP * D_EXPERT
            ys_ref[rows, :] = _dot(he_ref[...], wd_ref[g * gw:(g + 1) * gw, :]).astype(BF16)
        return y + _dot(place, ys_ref[...])

    out = x + lax.fori_loop(0, n_pass, one_pass, jnp.zeros((tm, D_MODEL), F32))
    if last:
        o_ref[0] = out
    else:
        t = i * tm + lax.broadcasted_iota(jnp.int32, (tm, 1), 0)
        o_ref[0] = jnp.where((t >= PAD) & (t < PAD + n_valid), out, 0.0)


def _moe(h, g2, wr, br, wg, wu, wd, *, n_valid, n_seq, last):
    B, TP, D = h.shape
    tm = MOE_TILE
    kern = functools.partial(_moe_kernel, tm=tm, n_valid=n_valid, last=last)
    EW = N_EXPERTS * D_EXPERT
    if last:
        half = tm // FRAME_TILE
        assert half == 2 and n_seq % half == 0
        rows_out = n_seq * FRAME_TILE
        stream_specs = [pl.BlockSpec((1, FRAME_TILE, D), lambda b, i, j=j: (b, half * i + 1 + j, 0))
                        for j in range(half)]
        stream = [h] * half
    else:
        rows_out = TP
        stream_specs = [pl.BlockSpec((1, tm, D), lambda b, i: (b, i, 0))]
        stream = [h]
    return pl.pallas_call(
        kern, out_shape=jax.ShapeDtypeStruct((B, rows_out, D), F32), grid=(B, rows_out // tm),
        in_specs=stream_specs + [
                  _const_spec((1, D)), _const_spec((D, LANE)), _const_spec((1, LANE)),
                  _const_spec((N_EXPERTS, D, D_EXPERT)), _const_spec((N_EXPERTS, D, D_EXPERT)),
                  _const_spec((EW, D))],
        out_specs=pl.BlockSpec((1, tm, D), lambda b, i: (b, i, 0)),
        scratch_shapes=[pltpu.VMEM((MOE_CAP, EXPERTS_PER_GROUP * D_EXPERT), BF16),
                        pltpu.VMEM((N_GROUPS * MOE_CAP, D), BF16)],
        compiler_params=pltpu.CompilerParams(
            dimension_semantics=("arbitrary", "arbitrary"), vmem_limit_bytes=VMEM_LIMIT),
        name="moe",
    )(*stream, g2, wr, br, wg, wu, wd)


def _transpose_kernel(x_ref, o_ref):
    o_ref[...] = x_ref[...].T.astype(BF16)


def _transpose_to_bf16(a, *, tn=LANE):
    N, K = a.shape
    assert N % tn == 0
    return pl.pallas_call(
        _transpose_kernel, out_shape=jax.ShapeDtypeStruct((K, N), BF16), grid=(N // tn,),
        in_specs=[pl.BlockSpec((tn, K), lambda i: (i, 0))],
        out_specs=pl.BlockSpec((K, tn), lambda i: (0, i)),
        compiler_params=pltpu.CompilerParams(dimension_semantics=("arbitrary",)),
        name="transpose_cast",
    )(a)


def _pack_w_in(w_in):
    wt = jnp.swapaxes(w_in, 0, 1)
    rest = wt[SHIFT_W:]
    rows = [wt[:SHIFT_W]]
    off = 0
    for heads in (SB_HEADS,) * 3 + (FOX_HEADS,) * 3:
        rows.append(jnp.pad(rest[off:off + heads * HEAD_DIM],
                            ((0, PAIRS * LANE - heads * HEAD_DIM), (0, 0))))
        off += heads * HEAD_DIM
    rows.append(jnp.pad(rest[off:off + FOX_HEADS], ((0, LANE - FOX_HEADS), (0, 0))))
    rows.append(rest[off + FOX_HEADS:])
    return _transpose_to_bf16(jnp.concatenate(rows, axis=0))


def _per_pair_cols(w):
    return w.reshape(w.shape[0], PAIRS, LANE).transpose(1, 0, 2)


def kernel(x, meta_tokens, norm1_g, w_in, rwkv_mu, rwkv_w_up, rwkv_w0, rwkv_a_up, rwkv_a0,
           rwkv_k_k, rwkv_k_a, rwkv_r_k, rwkv_ln_g, rwkv_ln_b, fox_f_b, fox_q_g, fox_k_g,
           w_p_rwkv, w_p_sb, w_p_fox, w_out, norm2_g, moe_wg, moe_bg, moe_we, moe_be,
           moe_w_gate, moe_w_up, moe_w_down):
    B, S, D = x.shape
    depth = w_in.shape[0]
    L = N_META + S
    TP = -(-(PAD + L) // ROW_TILE) * ROW_TILE
    assert S % FRAME_TILE == 0
    n_seq = S // FRAME_TILE
    meta = jnp.broadcast_to(meta_tokens[None].astype(x.dtype), (B, N_META, D))
    stream = (x, jnp.concatenate([jnp.zeros((B, PAD, D), x.dtype), meta], axis=1))
    H = RWKV_HEADS
    EW = N_EXPERTS * D_EXPERT
    for l in range(depth):
        fb = jnp.zeros((1, LANE), F32).at[0, :FOX_HEADS].set(fox_f_b[l])
        rkv, wdad, sb, fx, fend, gates = _inproj(
            stream, norm1_g[l][None], _pack_w_in(w_in[l]), rwkv_mu[l][None], fb,
            jnp.tile(fox_q_g[l], 2)[None], jnp.tile(fox_k_g[l], 2)[None], TP=TP, n_seq=n_seq)
        hv = lambda p: p.reshape(PAIRS, 1, LANE)
        gm, hc, r2, y0, bonus = _rwkv_chunks(
            rkv, wdad, _per_pair_cols(rwkv_w_up[l]), hv(rwkv_w0[l]),
            _per_pair_cols(rwkv_a_up[l]), hv(rwkv_a0[l]),
            hv(rwkv_k_k[l]), hv(rwkv_k_a[l]), hv(rwkv_r_k[l]))
        ya = _rwkv_scan(gm, hc, r2, y0, bonus, hv(rwkv_ln_g[l]), hv(rwkv_ln_b[l]))
        yb = _sb_attention(sb)
        shift = (HEAD_DIM * ATT_SCALE * LOG2E
                 * jnp.max(jnp.abs(fox_q_g[l])) * jnp.max(jnp.abs(fox_k_g[l]))).reshape(1)
        step = FOX_TILE // INPROJ_TILE
        fe = jnp.transpose(fend[:, step - 1::step, 0, :FOX_HEADS], (0, 2, 1))
        yc = lax.cond(shift[0] <= FOX_FIXED_SHIFT_MAX,
                      functools.partial(_fox_attention, online=False),
                      functools.partial(_fox_attention, online=True), shift, fe, fx)
        pc = jnp.pad(w_p_fox[l].reshape(FOX_HEADS, HEAD_DIM, D),
                     ((0, 0), (0, LANE - HEAD_DIM), (0, 0))).reshape(FOX_HEADS * LANE, D).astype(BF16)
        pp = lambda w, nh: jnp.pad(w, ((0, PAIRS * LANE - nh * HEAD_DIM), (0, 0))).astype(BF16)
        h = _merge(stream, ya, yb, yc, gates, pp(w_p_rwkv[l], H), pp(w_p_sb[l], SB_HEADS),
                   pc, w_out[l].astype(BF16), n_valid=L, n_seq=n_seq)
        wr = jnp.zeros((D, LANE), F32).at[:, :N_GROUPS].set(moe_wg[l])
        wr = wr.at[:, N_GROUPS:N_GROUPS + N_EXPERTS].set(moe_we[l])
        br = jnp.zeros((1, LANE), F32).at[0, :N_GROUPS].set(moe_bg[l])
        br = br.at[0, N_GROUPS:N_GROUPS + N_EXPERTS].set(moe_be[l])
        wg = moe_w_gate[l].astype(BF16)
        wu = moe_w_up[l].astype(BF16)
        wd = moe_w_down[l].reshape(EW, D).astype(BF16)
        h = _moe(h, norm2_g[l][None], wr, br, wg, wu, wd, n_valid=L, n_seq=n_seq,
                 last=l == depth - 1)
        stream = (h,)
    return h
```

```python
import functools
import math

import jax
import jax.numpy as jnp
from jax import lax
from jax.experimental import pallas as pl
from jax.experimental.pallas import tpu as pltpu

D_MODEL = 1024
HEAD_DIM = 64
N_META = 16
FRAME_TILE = 256
MOE_TILE = 512
MOE_CAP = 192
PAD = FRAME_TILE - N_META
RWKV_HEADS = 6
SB_HEADS = 5
FOX_HEADS = 5
RWKV_W = RWKV_HEADS * HEAD_DIM
DECAY_RANK = 64
ICLR_RANK = 64
SHIFT_W = 3 * RWKV_W + DECAY_RANK + ICLR_RANK
N_GROUPS = 4
EXPERTS_PER_GROUP = 4
N_EXPERTS = 16
D_EXPERT = 256
NORM_EPS = 1e-6
RWKV_LN_EPS = 64e-5
NEG_INF = -1e30
ATT_SCALE = 1.0 / math.sqrt(HEAD_DIM)

LANE = 128
ROW_TILE = 512
PAIRS = 3
QKV_W = 3 * PAIRS * LANE
SEG_SB = SHIFT_W
SEG_FOX = SEG_SB + QKV_W
SEG_GATE = SEG_FOX + QKV_W + LANE
D_IN_PAD = SEG_GATE + 3 * D_MODEL
SB_CUTOFF_LOG2 = 160.0
SB_TILE = 128
SB_FUSED = 3
LOG2E = 1.4426950408889634
FOX_FIXED_SHIFT_MAX = 50.0
FOX_SKIP_LOG2 = 150.0
FOX_TILE = 512
FOX_GROUP = 4
INPROJ_TILE = FRAME_TILE

CHUNK = 64
VMEM_LIMIT = 56 * 1024 * 1024

F32 = jnp.float32
BF16 = jnp.bfloat16


def _log_sigmoid(x):
    return jnp.minimum(x, 0.0) - jnp.log1p(jnp.exp(-jnp.abs(x)))


def _sigmoid(x):
    return 1.0 / (1.0 + jnp.exp(-x))


def _dot(a, b, **kw):
    return jnp.dot(a, b, preferred_element_type=F32, **kw)


def _dot_nt(a, b):
    return lax.dot_general(a, b, (((1,), (1,)), ((), ())), preferred_element_type=F32)


def _dot_tn(a, b):
    return lax.dot_general(a, b, (((0,), (0,)), ((), ())), preferred_element_type=F32)


def _dot3(a, b):
    ah = a.astype(BF16)
    al = (a - ah.astype(F32)).astype(BF16)
    bh = b.astype(BF16)
    bl = (b - bh.astype(F32)).astype(BF16)
    return _dot(ah, bh) + (_dot(ah, bl) + _dot(al, bh))


def _stream_specs(first, tm, n_seq):
    if not first:
        return [pl.BlockSpec((1, tm, D_MODEL), lambda b, i: (b, i, 0))]
    return [pl.BlockSpec((1, tm, D_MODEL), lambda b, i: (b, jnp.clip(i - 1, 0, n_seq - 1), 0)),
            pl.BlockSpec((1, tm, D_MODEL), lambda b, i: (b, 0, 0))]


def _stream_tile(i, refs, n_seq):
    if len(refs) == 1:
        return refs[0][0]
    x_ref, head_ref = refs
    return jnp.where(i == 0, head_ref[0], jnp.where(i <= n_seq, x_ref[0], 0.0))


def _const_spec(shape):
    n = len(shape)
    return pl.BlockSpec(shape, lambda *_: (0,) * n, pipeline_mode=pl.Buffered(1))


def _inproj_kernel(*refs, tm, first, n_seq):
    n_stream = 2 if first else 1
    (g_ref, w_ref, mu_ref, fb_ref, fqg_ref, fkg_ref,
     rkv_ref, wdad_ref, sb_ref, fx_ref, fend_ref, gate_ref, carry_u, carry_f) = refs[n_stream:]
    i = pl.program_id(1)

    @pl.when(i == 0)
    def _():
        carry_u[...] = jnp.zeros_like(carry_u)
        carry_f[...] = jnp.zeros_like(carry_f)

    x = _stream_tile(i, refs[:n_stream], n_seq)
    ms = jnp.mean(x * x, axis=-1, keepdims=True)
    n = (x * lax.rsqrt(ms + NORM_EPS) * g_ref[...]).astype(BF16)
    row = lax.broadcasted_iota(jnp.int32, (tm, 1), 0)

    us = _dot(n, w_ref[:, 0:SHIFT_W])
    prev = pltpu.roll(us, 1, axis=0)
    prev = jnp.where(row == 0, carry_u[...], prev)
    carry_u[...] = us[tm - 1:tm, :]
    ush = us + (prev - us) * mu_ref[...]
    for j in range(3 * PAIRS):
        rkv_ref[0, j] = ush[:, j * LANE:(j + 1) * LANE]
    wdad_ref[0] = ush[:, 3 * RWKV_W:SHIFT_W]

    usb = _dot(n, w_ref[:, SEG_SB:SEG_SB + QKV_W])
    for j in range(3 * PAIRS):
        piece = usb[:, j * LANE:(j + 1) * LANE]
        if j < PAIRS:
            piece = piece * (ATT_SCALE * LOG2E)
        sb_ref[0, j] = piece.astype(BF16)

    uf = _dot(n, w_ref[:, SEG_FOX:SEG_FOX + QKV_W + LANE])
    lane = lax.broadcasted_iota(jnp.int32, (tm, LANE), 1)
    first = lane < HEAD_DIM
    t_glob = i * tm + row
    logf = _log_sigmoid(uf[:, QKV_W:QKV_W + LANE] + fb_ref[...])
    logf = jnp.where((lane < FOX_HEADS) & (t_glob >= PAD), logf, 0.0)
    tri = (lax.broadcasted_iota(jnp.int32, (tm, tm), 0)
           >= lax.broadcasted_iota(jnp.int32, (tm, tm), 1)).astype(BF16)
    f_hi = logf.astype(BF16)
    f_mid = (logf - f_hi.astype(F32)).astype(BF16)
    f_lo = (logf - f_hi.astype(F32) - f_mid.astype(F32)).astype(BF16)
    cum = _dot(tri, f_hi) + _dot(tri, f_mid) + _dot(tri, f_lo) + carry_f[...]
    carry_f[...] = cum[tm - 1:tm, :]
    fend_ref[0, 0] = cum[tm - 1:tm, :] * LOG2E

    def split3(hd):
        f2 = jnp.sum(jnp.where(lane == hd, cum, 0.0), axis=-1, keepdims=True) * LOG2E
        hi = f2.astype(BF16).astype(F32)
        mid = (f2 - hi).astype(BF16).astype(F32)
        return hi, mid, f2 - hi - mid

    def tail_cols(vals):
        out = jnp.zeros((tm, LANE), F32)
        for o, val in enumerate(vals):
            out = jnp.where(lane == HEAD_DIM + o, val, out)
        return out

    splits = [split3(hd) for hd in range(FOX_HEADS)]
    for j in range(3 * PAIRS):
        piece = uf[:, j * LANE:(j + 1) * LANE]
        kind = j // PAIRS
        if kind < 2:
            gain = fqg_ref[...] if kind == 0 else fkg_ref[...]
            sq = piece * piece
            ms0 = jnp.sum(jnp.where(first, sq, 0.0), axis=-1, keepdims=True) * (1.0 / HEAD_DIM)
            ms1 = jnp.sum(jnp.where(first, 0.0, sq), axis=-1, keepdims=True) * (1.0 / HEAD_DIM)
            inv = jnp.where(first, lax.rsqrt(ms0 + NORM_EPS), lax.rsqrt(ms1 + NORM_EPS))
            piece = piece * inv * gain
            if kind == 0:
                piece = piece * (ATT_SCALE * LOG2E)
        swapped = pltpu.roll(piece, HEAD_DIM, axis=1)
        for half in range(2):
            hd = 2 * (j % PAIRS) + half
            if hd >= FOX_HEADS:
                continue
            if kind == 2:
                extra = tail_cols([1.0])
            else:
                hi, mid, lo = splits[hd]
                extra = tail_cols([hi, mid, lo, 1.0, 1.0, 1.0] if kind == 0 else
                                  [1.0, 1.0, 1.0, jnp.where(t_glob >= PAD, -hi, NEG_INF), -mid, -lo])
            body = piece if half == 0 else swapped
            fx_ref[0, kind * FOX_HEADS + hd] = jnp.where(first, body, extra).astype(BF16)

    ug = _dot(n, w_ref[:, SEG_GATE:D_IN_PAD])
    gate_ref[0] = _sigmoid(ug).astype(BF16)


def _inproj(stream, g1, w_in_p, mu, fb, fqg, fkg, *, TP, n_seq, tm=INPROJ_TILE):
    first = len(stream) == 2
    B, D = stream[0].shape[0], D_MODEL
    nb = TP // tm
    kern = functools.partial(_inproj_kernel, tm=tm, first=first, n_seq=n_seq)
    out_shape = (
        jax.ShapeDtypeStruct((B, 3 * PAIRS, TP, LANE), F32),
        jax.ShapeDtypeStruct((B, TP, LANE), F32),
        jax.ShapeDtypeStruct((B, 3 * PAIRS, TP, LANE), BF16),
        jax.ShapeDtypeStruct((B, 3 * FOX_HEADS, TP, LANE), BF16),
        jax.ShapeDtypeStruct((B, nb, 1, LANE), F32),
        jax.ShapeDtypeStruct((B, TP, 3 * D_MODEL), BF16),
    )
    in_specs = _stream_specs(first, tm, n_seq) + [
        _const_spec((1, D)),
        _const_spec((D, D_IN_PAD)),
        _const_spec((1, SHIFT_W)),
        _const_spec((1, LANE)),
        _const_spec((1, LANE)),
        _const_spec((1, LANE)),
    ]
    out_specs = (
        pl.BlockSpec((1, 3 * PAIRS, tm, LANE), lambda b, i: (b, 0, i, 0)),
        pl.BlockSpec((1, tm, LANE), lambda b, i: (b, i, 0)),
        pl.BlockSpec((1, 3 * PAIRS, tm, LANE), lambda b, i: (b, 0, i, 0)),
        pl.BlockSpec((1, 3 * FOX_HEADS, tm, LANE), lambda b, i: (b, 0, i, 0)),
        pl.BlockSpec((1, 1, 1, LANE), lambda b, i: (b, i, 0, 0)),
        pl.BlockSpec((1, tm, 3 * D_MODEL), lambda b, i: (b, i, 0)),
    )
    return pl.pallas_call(
        kern, out_shape=out_shape, grid=(B, nb), in_specs=in_specs, out_specs=out_specs,
        scratch_shapes=[pltpu.VMEM((1, SHIFT_W), F32), pltpu.VMEM((1, LANE), F32)],
        compiler_params=pltpu.CompilerParams(
            dimension_semantics=("arbitrary", "arbitrary"), vmem_limit_bytes=VMEM_LIMIT),
        name="inproj",
    )(*stream, g1, w_in_p, mu, fb, fqg, fkg)


def _bdot(a, b):
    return lax.dot_general(a, b, (((2,), (1,)), ((0,), (0,))), preferred_element_type=F32)


def _bdot_nt(a, b):
    return lax.dot_general(a, b, (((2,), (2,)), ((0,), (0,))), preferred_element_type=F32)


def _bdot_tn(a, b):
    return lax.dot_general(a, b, (((1,), (1,)), ((0,), (0,))), preferred_element_type=F32)


def _head_sum(x, first):
    s0 = jnp.sum(jnp.where(first, x, 0.0), axis=-1, keepdims=True)
    s1 = jnp.sum(jnp.where(first, 0.0, x), axis=-1, keepdims=True)
    return jnp.where(first, s0, s1)


def _rwkv_chunk_kernel(r_ref, k_ref, v_ref, wdad_ref, wup_ref, w0_ref, aup_ref, a0_ref,
                       kk_ref, ka_ref, rk_ref,
                       g_ref, hc_ref, r2_ref, y0_ref, bonus_ref, *, rows):
    nc = rows // CHUNK
    r = r_ref[0, 0]
    k = k_ref[0, 0]
    v = v_ref[0, 0]
    wd = wdad_ref[0][:, 0:DECAY_RANK]
    ad = wdad_ref[0][:, DECAY_RANK:DECAY_RANK + ICLR_RANK]
    first = lax.broadcasted_iota(jnp.int32, (rows, LANE), 1) < HEAD_DIM

    pre = w0_ref[0] + _dot3(jnp.tanh(wd), wup_ref[0])
    lw = -jnp.exp(_log_sigmoid(pre) - 0.5)
    iclr = _sigmoid(a0_ref[0] + _dot3(ad, aup_ref[0]))
    kk = k * kk_ref[0]
    kk = kk / jnp.maximum(jnp.sqrt(_head_sum(kk * kk, first)), 1e-12)
    k2 = k * (1.0 + (iclr - 1.0) * ka_ref[0])
    b = kk * iclr
    bonus_ref[0, 0] = _head_sum(r * k2 * rk_ref[0], first) * v

    to3 = lambda x: x.reshape(nc, CHUNK, LANE)
    ri = lax.broadcasted_iota(jnp.int32, (nc, CHUNK, CHUNK), 1)
    ci = lax.broadcasted_iota(jnp.int32, (nc, CHUNK, CHUNK), 2)
    low_incl = ri >= ci
    low_strict = ri > ci
    first3 = lax.broadcasted_iota(jnp.int32, (nc, CHUNK, LANE), 2) < HEAD_DIM

    lw3 = to3(lw)
    tri = low_incl.astype(BF16)
    lw_hi = lw3.astype(BF16)
    cum = _bdot(tri, lw_hi) + _bdot(tri, (lw3 - lw_hi.astype(F32)).astype(BF16))
    cum_end = cum[:, CHUNK - 1:CHUNK, :]
    e_neg = jnp.exp(-cum)
    at = to3(-kk) * jnp.exp(cum - lw3)
    rt = to3(r) * jnp.exp(cum)
    bt = (to3(b) * e_neg).astype(BF16)
    kt = (to3(k2) * e_neg).astype(BF16)
    e_rem = jnp.exp(cum_end - cum)
    bq = (to3(b) * e_rem).astype(BF16)
    kq = (to3(k2) * e_rem).astype(BF16)
    vv = to3(v).astype(BF16)

    heads = range(2)
    sels = [first3, jnp.logical_not(first3)]
    lhs = [jnp.concatenate([jnp.where(sels[hd], at, 0.0), jnp.where(sels[hd], rt, 0.0)],
                           axis=1).astype(BF16) for hd in heads]
    mb = [_bdot_nt(lhs[hd], bt) for hd in heads]
    mk = [_bdot_nt(lhs[hd], kt) for hd in heads]
    m_ak = [jnp.where(low_strict, mk[hd][:, :CHUNK], 0.0).astype(BF16) for hd in heads]
    m_rb = [jnp.where(low_incl, mb[hd][:, CHUNK:], 0.0).astype(BF16) for hd in heads]
    m_rk = [jnp.where(low_incl, mk[hd][:, CHUNK:], 0.0).astype(BF16) for hd in heads]
    p = [jnp.where(low_strict, mb[hd][:, :CHUNK], 0.0) for hd in heads]
    xs = [jnp.concatenate([at, _bdot(m_ak[hd], vv)], axis=-1) for hd in heads]
    for j in range(6):
        p16 = [p[hd].astype(BF16) for hd in heads]
        xs = [xs[hd] + _bdot(p16[hd], xs[hd].astype(BF16)) for hd in heads]
        if j < 5:
            p = [_bdot(p16[hd], p16[hd]) for hd in heads]
    ru = [_bdot(m_rb[hd], xs[hd].astype(BF16)) for hd in heads]
    r2s = [rt + ru[hd][..., :LANE] for hd in heads]
    y0s = [ru[hd][..., LANE:] + _bdot(m_rk[hd], vv) for hd in heads]

    first3w = jnp.concatenate([first3, first3], axis=-1)
    x = jnp.where(first3w, xs[0], xs[1])
    r2_ref[0, 0] = jnp.where(first3, r2s[0], r2s[1]).reshape(rows, LANE)
    y0_ref[0, 0] = jnp.where(first3, y0s[0], y0s[1]).reshape(rows, LANE)

    pg = _bdot_tn(bq, x.astype(BF16))
    ph = pg[..., LANE:] + _bdot_tn(kq, vv)
    rr = lax.broadcasted_iota(jnp.int32, (nc, LANE, LANE), 1)
    cc = lax.broadcasted_iota(jnp.int32, (nc, LANE, LANE), 2)
    same_head = (rr < HEAD_DIM) == (cc < HEAD_DIM)
    g = jnp.where(same_head, pg[..., :LANE], 0.0) + jnp.where(rr == cc, jnp.exp(cum_end), 0.0)
    g_ref[0, 0] = g.reshape(nc * LANE, LANE)
    hc_ref[0, 0] = jnp.where(same_head, ph, 0.0).reshape(nc * LANE, LANE)


def _rwkv_chunks(rkv, wdad, wup, w0, aup, a0, k_k, k_a, r_k):
    B, _, TP, _ = rkv.shape
    rows = next(r for r in (1536, 768, 512, 256) if TP % r == 0)
    nb = TP // rows
    kern = functools.partial(_rwkv_chunk_kernel, rows=rows)
    slot_spec = lambda off: pl.BlockSpec((1, 1, rows, LANE), lambda b, p, i: (b, p + off, i, 0))
    par_mat = pl.BlockSpec((1, DECAY_RANK, LANE), lambda b, p, i: (p, 0, 0))
    par_vec = pl.BlockSpec((1, 1, LANE), lambda b, p, i: (p, 0, 0))
    row_out = jax.ShapeDtypeStruct((B, PAIRS, TP, LANE), F32)
    mat_out = jax.ShapeDtypeStruct((B, PAIRS, 2 * TP, LANE), F32)
    row_spec = pl.BlockSpec((1, 1, rows, LANE), lambda b, p, i: (b, p, i, 0))
    mat_spec = pl.BlockSpec((1, 1, 2 * rows, LANE), lambda b, p, i: (b, p, i, 0))
    return pl.pallas_call(
        kern, out_shape=(mat_out, mat_out, row_out, row_out, row_out), grid=(B, PAIRS, nb),
        in_specs=[slot_spec(0), slot_spec(PAIRS), slot_spec(2 * PAIRS),
                  pl.BlockSpec((1, rows, LANE), lambda b, p, i: (b, i, 0)),
                  par_mat, par_vec, par_mat, par_vec, par_vec, par_vec, par_vec],
        out_specs=(mat_spec, mat_spec, row_spec, row_spec, row_spec),
        compiler_params=pltpu.CompilerParams(
            dimension_semantics=("arbitrary",) * 3, vmem_limit_bytes=VMEM_LIMIT),
        name="rwkv_chunks",
    )(rkv, rkv, rkv, wdad, wup, w0, aup, a0, k_k, k_a, r_k)


def _rwkv_scan_kernel(g_ref, hc_ref, r2_ref, y0_ref, bonus_ref, lng_ref, lnb_ref, y_ref,
                      state, *, rows):
    i = pl.program_id(1)

    @pl.when(i == 0)
    def _():
        state[...] = jnp.zeros_like(state)

    first = lax.broadcasted_iota(jnp.int32, (CHUNK, LANE), 1) < HEAD_DIM
    hs = [state[p] for p in range(PAIRS)]
    for c in range(rows // CHUNK):
        sl = slice(c * CHUNK, (c + 1) * CHUNK)
        sm = slice(c * LANE, (c + 1) * LANE)
        for p in range(PAIRS):
            y = _dot3(r2_ref[0, p, sl, :], hs[p]) + y0_ref[0, p, sl, :]
            hs[p] = _dot3(g_ref[0, p, sm, :], hs[p]) + hc_ref[0, p, sm, :]
            yc = y - _head_sum(y, first) * (1.0 / HEAD_DIM)
            var = _head_sum(yc * yc, first) * (1.0 / HEAD_DIM)
            out = yc * lax.rsqrt(var + RWKV_LN_EPS) * lng_ref[p] + lnb_ref[p]
            y_ref[0, p, sl, :] = (out + bonus_ref[0, p, sl, :]).astype(BF16)
    for p in range(PAIRS):
        state[p] = hs[p]


def _rwkv_scan(gm, hc, r2, y0, bonus, ln_g, ln_b):
    B, P, TP, _ = r2.shape
    rows = next(r for r in (1536, 512) if TP % r == 0)
    nb = TP // rows
    kern = functools.partial(_rwkv_scan_kernel, rows=rows)
    blk = pl.BlockSpec((1, P, rows, LANE), lambda b, i: (b, 0, i, 0))
    mat = pl.BlockSpec((1, P, 2 * rows, LANE), lambda b, i: (b, 0, i, 0))
    par = pl.BlockSpec((P, 1, LANE), lambda b, i: (0, 0, 0))
    return pl.pallas_call(
        kern, out_shape=jax.ShapeDtypeStruct((B, P, TP, LANE), BF16), grid=(B, nb),
        in_specs=[mat, mat, blk, blk, blk, par, par], out_specs=blk,
        scratch_shapes=[pltpu.VMEM((P, LANE, LANE), F32)],
        compiler_params=pltpu.CompilerParams(
            dimension_semantics=("arbitrary", "arbitrary"), vmem_limit_bytes=VMEM_LIMIT),
        name="rwkv_scan",
    )(gm, hc, r2, y0, bonus, ln_g, ln_b)


def _sb_kernel(*refs, t):
    q_refs, k_refs, v_refs = refs[:PAIRS], refs[PAIRS:2 * PAIRS], refs[2 * PAIRS:3 * PAIRS]
    o_ref = refs[3 * PAIRS]
    qi = pl.program_id(1)
    first = lax.broadcasted_iota(jnp.int32, (t, LANE), 1) < HEAD_DIM
    heads = [(hd // 2, hd % 2) for hd in range(SB_HEADS)]
    qs = []
    for slot, half in heads:
        qp = q_refs[slot][0, 0]
        keep = first if half == 0 else jnp.logical_not(first)
        qs.append(jnp.where(keep, qp, jnp.zeros_like(qp)))
    nu = len(heads)
    qpos = qi * t + lax.broadcasted_iota(jnp.int32, (t, 1), 0)
    kloc = lax.broadcasted_iota(jnp.int32, (1, t), 1)
    upper = (lax.broadcasted_iota(jnp.int32, (t, t), 0)
             > lax.broadcasted_iota(jnp.int32, (t, t), 1)).astype(BF16)

    def sweep(blocks, accs, cs):
        units = []
        for kb, mask in blocks:
            start = pl.multiple_of(kb * t, t)
            kblks = [k_refs[s][0, 0, pl.ds(start, t), :] for s in range(PAIRS)]
            vblks = [v_refs[s][0, 0, pl.ds(start, t), :] for s in range(PAIRS)]
            m = None if mask is None else mask(start + kloc)
            units += [(u, kblks[slot], vblks[slot], m) for u, (slot, _) in enumerate(heads)]
        zs = [_dot_nt(qs[u], kblk) for u, kblk, _, _ in units]
        sps = [jnp.maximum(z, 0.0) + jnp.log2(1.0 + jnp.exp2(-jnp.abs(z))) for z in zs]
        spms = [sp if un[3] is None else jnp.where(un[3], sp, 0.0) for sp, un in zip(sps, units)]
        laters = [_dot(spm.astype(BF16), upper) for spm in spms]
        accs, cs = list(accs), list(cs)
        weights = []
        for (u, _, _, m), z, sp, spm, later in zip(units, zs, sps, spms, laters):
            a = jnp.exp2(z - sp - later - cs[u])
            weights.append(a if m is None else jnp.where(m, a, 0.0))
            cs[u] = cs[u] + jnp.sum(spm, axis=-1, keepdims=True)
        for (u, _, vblk, _), a in zip(units, weights):
            accs[u] = accs[u] + _dot(a.astype(BF16), vblk)
        return accs, cs

    def block(kb, accs, cs, mask):
        return sweep([(kb, mask)], accs, cs)

    def live(cs):
        low = cs[0]
        for c in cs[1:]:
            low = jnp.minimum(low, c)
        return (jnp.min(low) < SB_CUTOFF_LOG2).astype(jnp.int32)

    zero_acc = [jnp.zeros((t, LANE), F32) for _ in range(nu)]
    zero_c = [jnp.zeros((t, 1), F32) for _ in range(nu)]
    near = [(qi, lambda kpos: (kpos >= PAD) & (kpos < qpos))]
    for j in range(1, SB_FUSED):
        near.append((jnp.maximum(qi - j, 0), lambda kpos, j=j: (kpos >= PAD) & (qi >= j)))
    accs, cs = sweep(near, zero_acc, zero_c)

    def cond(carry):
        kb, alive = carry[0], carry[1]
        return (kb >= 1) & (alive > 0)

    def body(carry):
        kb = carry[0]
        accs, cs = block(kb, list(carry[2:2 + nu]), list(carry[2 + nu:]), None)
        return (kb - 1, live(cs), *accs, *cs)

    carry = lax.while_loop(cond, body, (qi - SB_FUSED, live(cs), *accs, *cs))

    def front(carry):
        accs, cs = block(0, list(carry[2:2 + nu]), list(carry[2 + nu:]), lambda kpos: kpos >= PAD)
        return (carry[0], carry[1], *accs, *cs)

    carry = lax.cond((carry[0] == 0) & (carry[1] > 0), front, lambda c: c, carry)
    accs = carry[2:2 + nu]
    for slot in range(PAIRS):
        lo = accs[2 * slot]
        hi = accs[2 * slot + 1] if 2 * slot + 1 < nu else 0.0
        o_ref[0, slot] = jnp.where(first, lo, hi).astype(BF16)


def _sb_attention(sb, *, t=SB_TILE):
    B, _, TP, _ = sb.shape
    kern = functools.partial(_sb_kernel, t=t)
    q_specs = [pl.BlockSpec((1, 1, t, LANE), lambda b, i, s=s: (b, s, i, 0)) for s in range(PAIRS)]
    kv_specs = [pl.BlockSpec((1, 1, TP, LANE), lambda b, i, s=s: (b, s, 0, 0),
                             pipeline_mode=pl.Buffered(1)) for s in range(PAIRS, 3 * PAIRS)]
    return pl.pallas_call(
        kern, out_shape=jax.ShapeDtypeStruct((B, PAIRS, TP, LANE), BF16),
        grid=(B, TP // t),
        in_specs=q_specs + kv_specs,
        out_specs=pl.BlockSpec((1, PAIRS, t, LANE), lambda b, i: (b, 0, i, 0)),
        compiler_params=pltpu.CompilerParams(
            dimension_semantics=("arbitrary",) * 2, vmem_limit_bytes=VMEM_LIMIT),
        name="sb_attention",
    )(*([sb] * (3 * PAIRS)))


def _fox_kernel(shift_ref, fe_ref, q_ref, k_ref, v_ref, o_ref, lo_ref, *, t, online):
    b = pl.program_id(0)
    h = pl.program_id(1)
    qi = pl.program_id(2)
    q = q_ref[0, 0]
    shift = shift_ref[0]
    thr = FOX_SKIP_LOG2 + 2.0 * shift
    f_q = fe_ref[b, h, jnp.maximum(qi - 1, 0)]
    lo = lax.while_loop(lambda n: (n < qi) & (fe_ref[b, h, n] - f_q > thr), lambda n: n + 1,
                        jnp.where(qi == 0, 0, lo_ref[0]))
    lo_ref[0] = lo
    qpos = qi * t + lax.broadcasted_iota(jnp.int32, (t, 1), 0)
    kloc = lax.broadcasted_iota(jnp.int32, (1, t), 1)

    def block(kb, carry, masked, online):
        acc, m = carry
        start = pl.multiple_of(kb * t, t)
        kblk = k_ref[0, 0, pl.ds(start, t), :]
        vblk = v_ref[0, 0, pl.ds(start, t), :]
        s = _dot_nt(q, kblk)
        if masked:
            kpos = start + kloc
            s = jnp.where(kpos <= qpos, s, NEG_INF)
        if online:
            m_new = jnp.maximum(m, jnp.max(s, axis=-1, keepdims=True))
            acc = jnp.exp2(m - m_new) * acc
            m = m_new
        pr = jnp.exp2(s - m)
        return acc + _dot(pr.astype(BF16), vblk), m

    def sweep(online):
        def run(carry):
            start = lo
            n = qi - lo

            def group(i, c):
                acc, m = c
                kvs = []
                for j in range(FOX_GROUP):
                    st = pl.multiple_of((start + FOX_GROUP * i + j) * t, t)
                    kvs.append((k_ref[0, 0, pl.ds(st, t), :], v_ref[0, 0, pl.ds(st, t), :]))
                ss = [_dot_nt(q, kblk) for kblk, _ in kvs]
                prs = [jnp.exp2(s - m).astype(BF16) for s in ss]
                for pr, (_, vblk) in zip(prs, kvs):
                    acc = acc + _dot(pr, vblk)
                return acc, m

            grouped = 0 if online else (n // FOX_GROUP) * FOX_GROUP
            if not online:
                carry = lax.fori_loop(0, n // FOX_GROUP, group, carry)
            carry = lax.fori_loop(start + grouped, qi,
                                  lambda kb, c: block(kb, c, False, online), carry)
            return block(qi, carry, True, online)
        return run

    acc0 = jnp.zeros((t, LANE), F32)
    m0 = jnp.full((t, 1), NEG_INF, F32) if online else jnp.full((t, 1), shift, F32)
    acc, _ = sweep(online)((acc0, m0))
    lane = lax.broadcasted_iota(jnp.int32, (t, LANE), 1)
    denom = jnp.sum(jnp.where(lane == HEAD_DIM, acc, 0.0), axis=-1, keepdims=True)
    o_ref[0, 0] = (acc / jnp.where(denom > 0.0, denom, 1.0)).astype(BF16)


def _fox_attention(shift, fe, fx, *, online, t=FOX_TILE):
    B, _, TP, _ = fx.shape
    H = FOX_HEADS
    kern = functools.partial(_fox_kernel, t=t, online=online)
    return pl.pallas_call(
        kern, out_shape=jax.ShapeDtypeStruct((B, H, TP, LANE), BF16),
        grid=(B, H, TP // t),
        in_specs=[pl.BlockSpec(memory_space=pltpu.SMEM),
                  pl.BlockSpec(memory_space=pltpu.SMEM),
                  pl.BlockSpec((1, 1, t, LANE), lambda b, h, i: (b, h, i, 0)),
                  pl.BlockSpec((1, 1, TP, LANE), lambda b, h, i: (b, h + H, 0, 0)),
                  pl.BlockSpec((1, 1, TP, LANE), lambda b, h, i: (b, h + 2 * H, 0, 0))],
        out_specs=pl.BlockSpec((1, 1, t, LANE), lambda b, h, i: (b, h, i, 0)),
        scratch_shapes=[pltpu.SMEM((1,), jnp.int32)],
        compiler_params=pltpu.CompilerParams(
            dimension_semantics=("arbitrary",) * 3, vmem_limit_bytes=VMEM_LIMIT),
        name="fox_attention",
    )(shift, fe, fx, fx, fx)


def _merge_kernel(*refs, tm, n_valid, first, n_seq):
    n_stream = 2 if first else 1
    (ya_ref, yb_ref, yc_ref, gate_ref,
     wpa_ref, wpb_ref, wpc_ref, wout_ref, o_ref) = refs[n_stream:]
    i = pl.program_id(1)

    def proj(slots, w_ref):
        return _dot(jnp.concatenate(slots, axis=-1), w_ref[...])

    pa = proj([ya_ref[0, s] for s in range(PAIRS)], wpa_ref)
    pb = proj([yb_ref[0, s] for s in range(PAIRS)], wpb_ref)
    pc = proj([yc_ref[0, s] for s in range(FOX_HEADS)], wpc_ref)
    merged = gate_ref[0, :, 0:D_MODEL].astype(F32) * pa
    merged = merged + gate_ref[0, :, D_MODEL:2 * D_MODEL].astype(F32) * pb
    merged = merged + gate_ref[0, :, 2 * D_MODEL:3 * D_MODEL].astype(F32) * pc
    out = _stream_tile(i, refs[:n_stream], n_seq) + _dot(merged.astype(BF16), wout_ref[...])
    t = i * tm + lax.broadcasted_iota(jnp.int32, (tm, 1), 0)
    o_ref[0] = jnp.where((t >= PAD) & (t < PAD + n_valid), out, 0.0)


def _merge(stream, ya, yb, yc, gates, wpa, wpb, wpc, wout, *, n_valid, n_seq, tm=FRAME_TILE):
    first = len(stream) == 2
    B, _, TP, _ = ya.shape
    D = D_MODEL
    kern = functools.partial(_merge_kernel, tm=tm, n_valid=n_valid, first=first, n_seq=n_seq)
    pair_blk = lambda ns: pl.BlockSpec((1, ns, tm, LANE), lambda b, i: (b, 0, i, 0))
    return pl.pallas_call(
        kern, out_shape=jax.ShapeDtypeStruct((B, TP, D), F32), grid=(B, TP // tm),
        in_specs=_stream_specs(first, tm, n_seq) + [
                  pair_blk(PAIRS),
                  pair_blk(PAIRS), pair_blk(FOX_HEADS),
                  pl.BlockSpec((1, tm, 3 * D), lambda b, i: (b, i, 0)),
                  _const_spec((PAIRS * LANE, D)),
                  _const_spec((PAIRS * LANE, D)),
                  _const_spec((FOX_HEADS * LANE, D)),
                  _const_spec((D, D))],
        out_specs=pl.BlockSpec((1, tm, D), lambda b, i: (b, i, 0)),
        compiler_params=pltpu.CompilerParams(
            dimension_semantics=("arbitrary", "arbitrary"), vmem_limit_bytes=VMEM_LIMIT),
        name="merge",
    )(*stream, ya, yb, yc, gates, wpa, wpb, wpc, wout)


def _moe_kernel(*refs, tm, last):
    n_stream = 2 if last else 1
    g_ref, wr_ref, br_ref, wg_ref, wu_ref, wd_ref, o_ref, he_ref, ys_ref = refs[n_stream:]
    x = jnp.concatenate([r[0] for r in refs[:n_stream]], axis=0)
    ms = jnp.mean(x * x, axis=-1, keepdims=True)
    n = x * lax.rsqrt(ms + NORM_EPS) * g_ref[...]

    lg = _dot3(n, wr_ref[...]) + br_ref[...]
    lane = lax.broadcasted_iota(jnp.int32, (tm, LANE), 1)
    big = jnp.int32(LANE)
    is_group = lane < N_GROUPS
    gl = jnp.where(is_group, lg, -jnp.inf)
    gmax = jnp.max(gl, axis=-1, keepdims=True)
    g_idx = jnp.min(jnp.where(is_group & (gl == gmax), lane, big), axis=-1, keepdims=True)
    p_group = 1.0 / jnp.sum(jnp.where(is_group, jnp.exp(gl - gmax), 0.0), axis=-1, keepdims=True)
    lo = N_GROUPS + g_idx * EXPERTS_PER_GROUP
    in_grp = (lane >= lo) & (lane < lo + EXPERTS_PER_GROUP)
    el = jnp.where(in_grp, lg, -jnp.inf)
    top1 = jnp.max(el, axis=-1, keepdims=True)
    i1 = jnp.min(jnp.where(in_grp & (el == top1), lane, big), axis=-1, keepdims=True)
    rest = in_grp & (lane != i1)
    el2 = jnp.where(rest, lg, -jnp.inf)
    top2 = jnp.max(el2, axis=-1, keepdims=True)
    i2 = jnp.min(jnp.where(rest & (el2 == top2), lane, big), axis=-1, keepdims=True)
    e2 = jnp.exp(top2 - top1)
    p1 = 1.0 / (1.0 + e2)
    p2 = e2 / (1.0 + e2)
    comb = p_group * (jnp.where(lane == i1, p1, 0.0) + jnp.where(lane == i2, p2, 0.0))

    n16 = n.astype(BF16)

    onehot = (lane == g_idx).astype(BF16)
    earlier = (lax.broadcasted_iota(jnp.int32, (tm, tm), 1)
               < lax.broadcasted_iota(jnp.int32, (tm, tm), 0)).astype(BF16)
    cnt = _dot(earlier, onehot)
    rank = jnp.sum(jnp.where(lane == g_idx, cnt, 0.0), axis=-1, keepdims=True).astype(jnp.int32)
    n_pass = (jnp.max(rank) + MOE_CAP) // MOE_CAP
    comb16 = comb.astype(BF16)
    slot_lane = lax.broadcasted_iota(jnp.int32, (tm, N_GROUPS * MOE_CAP), 1)

    def one_pass(p, carry):
        r = rank - p * MOE_CAP
        slot = jnp.where((r >= 0) & (r < MOE_CAP), g_idx * MOE_CAP + r, -1)
        place = (slot_lane == slot).astype(BF16)
        xs = _dot_tn(place, n16).astype(BF16)
        cw = _dot_tn(place, comb16)
        for g in range(N_GROUPS):
            rows = slice(g * MOE_CAP, (g + 1) * MOE_CAP)
            for e in range(EXPERTS_PER_GROUP):
                ee = g * EXPERTS_PER_GROUP + e
                gate = _dot(xs[rows], wg_ref[ee])
                up = _dot(xs[rows], wu_ref[ee])
                c_e = cw[rows, N_GROUPS + ee:N_GROUPS + ee + 1]
                he_ref[:, e * D_EXPERT:(e + 1) * D_EXPERT] = (
                    gate * _sigmoid(gate) * up * c_e).astype(BF16)
            gw = EXPERTS_PER_GROUP * D_EXPERT
            ys_ref[rows, :] = _dot(he_ref[...], wd_ref[g * gw:(g + 1) * gw, :]).astype(BF16)
        o_ref[0] = o_ref[0] + _dot(place, ys_ref[...])
        return carry

    o_ref[0] = x
    lax.fori_loop(0, n_pass, one_pass, 0)


def _moe(h, g2, wr, br, wg, wu, wd, *, n_seq, last):
    B, TP, D = h.shape
    tm = MOE_TILE
    kern = functools.partial(_moe_kernel, tm=tm, last=last)
    EW = N_EXPERTS * D_EXPERT
    if last:
        half = tm // FRAME_TILE
        assert half == 2 and n_seq % half == 0
        rows_out = n_seq * FRAME_TILE
        stream_specs = [pl.BlockSpec((1, FRAME_TILE, D), lambda b, i, j=j: (b, half * i + 1 + j, 0))
                        for j in range(half)]
        stream = [h] * half
    else:
        rows_out = TP
        stream_specs = [pl.BlockSpec((1, tm, D), lambda b, i: (b, i, 0))]
        stream = [h]
    return pl.pallas_call(
        kern, out_shape=jax.ShapeDtypeStruct((B, rows_out, D), F32), grid=(B, rows_out // tm),
        in_specs=stream_specs + [
                  _const_spec((1, D)), _const_spec((D, LANE)), _const_spec((1, LANE)),
                  _const_spec((N_EXPERTS, D, D_EXPERT)), _const_spec((N_EXPERTS, D, D_EXPERT)),
                  _const_spec((EW, D))],
        out_specs=pl.BlockSpec((1, tm, D), lambda b, i: (b, i, 0)),
        scratch_shapes=[pltpu.VMEM((MOE_CAP, EXPERTS_PER_GROUP * D_EXPERT), BF16),
                        pltpu.VMEM((N_GROUPS * MOE_CAP, D), BF16)],
        compiler_params=pltpu.CompilerParams(
            dimension_semantics=("arbitrary", "arbitrary"), vmem_limit_bytes=VMEM_LIMIT),
        name="moe",
    )(*stream, g2, wr, br, wg, wu, wd)


def _transpose_kernel(x_ref, o_ref):
    o_ref[...] = x_ref[...].T.astype(BF16)


def _transpose_to_bf16(a, *, tn=LANE):
    N, K = a.shape
    assert N % tn == 0
    return pl.pallas_call(
        _transpose_kernel, out_shape=jax.ShapeDtypeStruct((K, N), BF16), grid=(N // tn,),
        in_specs=[pl.BlockSpec((tn, K), lambda i: (i, 0))],
        out_specs=pl.BlockSpec((K, tn), lambda i: (0, i)),
        compiler_params=pltpu.CompilerParams(dimension_semantics=("arbitrary",)),
        name="transpose_cast",
    )(a)


def _pack_w_in(w_in):
    wt = jnp.swapaxes(w_in, 0, 1)
    rest = wt[SHIFT_W:]
    rows = [wt[:SHIFT_W]]
    off = 0
    for heads in (SB_HEADS,) * 3 + (FOX_HEADS,) * 3:
        rows.append(jnp.pad(rest[off:off + heads * HEAD_DIM],
                            ((0, PAIRS * LANE - heads * HEAD_DIM), (0, 0))))
        off += heads * HEAD_DIM
    rows.append(jnp.pad(rest[off:off + FOX_HEADS], ((0, LANE - FOX_HEADS), (0, 0))))
    rows.append(rest[off + FOX_HEADS:])
    return _transpose_to_bf16(jnp.concatenate(rows, axis=0))


def _per_pair_cols(w):
    return w.reshape(w.shape[0], PAIRS, LANE).transpose(1, 0, 2)


def kernel(x, meta_tokens, norm1_g, w_in, rwkv_mu, rwkv_w_up, rwkv_w0, rwkv_a_up, rwkv_a0,
           rwkv_k_k, rwkv_k_a, rwkv_r_k, rwkv_ln_g, rwkv_ln_b, fox_f_b, fox_q_g, fox_k_g,
           w_p_rwkv, w_p_sb, w_p_fox, w_out, norm2_g, moe_wg, moe_bg, moe_we, moe_be,
           moe_w_gate, moe_w_up, moe_w_down):
    B, S, D = x.shape
    depth = w_in.shape[0]
    L = N_META + S
    TP = -(-(PAD + L) // ROW_TILE) * ROW_TILE
    assert S % FRAME_TILE == 0
    n_seq = S // FRAME_TILE
    meta = jnp.broadcast_to(meta_tokens[None].astype(x.dtype), (B, N_META, D))
    stream = (x, jnp.concatenate([jnp.zeros((B, PAD, D), x.dtype), meta], axis=1))
    H = RWKV_HEADS
    EW = N_EXPERTS * D_EXPERT
    for l in range(depth):
        fb = jnp.zeros((1, LANE), F32).at[0, :FOX_HEADS].set(fox_f_b[l])
        rkv, wdad, sb, fx, fend, gates = _inproj(
            stream, norm1_g[l][None], _pack_w_in(w_in[l]), rwkv_mu[l][None], fb,
            jnp.tile(fox_q_g[l], 2)[None], jnp.tile(fox_k_g[l], 2)[None], TP=TP, n_seq=n_seq)
        hv = lambda p: p.reshape(PAIRS, 1, LANE)
        gm, hc, r2, y0, bonus = _rwkv_chunks(
            rkv, wdad, _per_pair_cols(rwkv_w_up[l]), hv(rwkv_w0[l]),
            _per_pair_cols(rwkv_a_up[l]), hv(rwkv_a0[l]),
            hv(rwkv_k_k[l]), hv(rwkv_k_a[l]), hv(rwkv_r_k[l]))
        ya = _rwkv_scan(gm, hc, r2, y0, bonus, hv(rwkv_ln_g[l]), hv(rwkv_ln_b[l]))
        yb = _sb_attention(sb)
        shift = (HEAD_DIM * ATT_SCALE * LOG2E
                 * jnp.max(jnp.abs(fox_q_g[l])) * jnp.max(jnp.abs(fox_k_g[l]))).reshape(1)
        step = FOX_TILE // INPROJ_TILE
        fe = jnp.transpose(fend[:, step - 1::step, 0, :FOX_HEADS], (0, 2, 1))
        yc = lax.cond(shift[0] <= FOX_FIXED_SHIFT_MAX,
                      functools.partial(_fox_attention, online=False),
                      functools.partial(_fox_attention, online=True), shift, fe, fx)
        pc = jnp.pad(w_p_fox[l].reshape(FOX_HEADS, HEAD_DIM, D),
                     ((0, 0), (0, LANE - HEAD_DIM), (0, 0))).reshape(FOX_HEADS * LANE, D).astype(BF16)
        pp = lambda w, nh: jnp.pad(w, ((0, PAIRS * LANE - nh * HEAD_DIM), (0, 0))).astype(BF16)
        h = _merge(stream, ya, yb, yc, gates, pp(w_p_rwkv[l], H), pp(w_p_sb[l], SB_HEADS),
                   pc, w_out[l].astype(BF16), n_valid=L, n_seq=n_seq)
        wr = jnp.zeros((D, LANE), F32).at[:, :N_GROUPS].set(moe_wg[l])
        wr = wr.at[:, N_GROUPS:N_GROUPS + N_EXPERTS].set(moe_we[l])
        br = jnp.zeros((1, LANE), F32).at[0, :N_GROUPS].set(moe_bg[l])
        br = br.at[0, N_GROUPS:N_GROUPS + N_EXPERTS].set(moe_be[l])
        wg = moe_w_gate[l].astype(BF16)
        wu = moe_w_up[l].astype(BF16)
        wd = moe_w_down[l].reshape(EW, D).astype(BF16)
        h = _moe(h, norm2_g[l][None], wr, br, wg, wu, wd, n_seq=n_seq, last=l == depth - 1)
        stream = (h,)
    return h
```

```python
import functools
import math

import jax
import jax.numpy as jnp
from jax import lax
from jax.experimental import pallas as pl
from jax.experimental.pallas import tpu as pltpu

D_MODEL = 1024
HEAD_DIM = 64
N_META = 16
FRAME_TILE = 256
MOE_TILE = 512
MOE_CAP = 192
PAD = FRAME_TILE - N_META
RWKV_HEADS = 6
SB_HEADS = 5
FOX_HEADS = 5
RWKV_W = RWKV_HEADS * HEAD_DIM
DECAY_RANK = 64
ICLR_RANK = 64
SHIFT_W = 3 * RWKV_W + DECAY_RANK + ICLR_RANK
N_GROUPS = 4
EXPERTS_PER_GROUP = 4
N_EXPERTS = 16
D_EXPERT = 256
NORM_EPS = 1e-6
RWKV_LN_EPS = 64e-5
NEG_INF = -1e30
ATT_SCALE = 1.0 / math.sqrt(HEAD_DIM)

LANE = 128
ROW_TILE = 512
PAIRS = 3
QKV_W = 3 * PAIRS * LANE
SEG_SB = SHIFT_W
SEG_FOX = SEG_SB + QKV_W
SEG_GATE = SEG_FOX + QKV_W + LANE
D_IN_PAD = SEG_GATE + 3 * D_MODEL
SB_CUTOFF_LOG2 = 160.0
SB_TILE = 128
SB_FUSED = 3
LOG2E = 1.4426950408889634
FOX_FIXED_SHIFT_MAX = 50.0
FOX_SKIP_LOG2 = 150.0
FOX_TILE = 512
FOX_GROUP = 4
INPROJ_TILE = FRAME_TILE

CHUNK = 64
VMEM_LIMIT = 56 * 1024 * 1024

F32 = jnp.float32
BF16 = jnp.bfloat16


def _log_sigmoid(x):
    return jnp.minimum(x, 0.0) - jnp.log1p(jnp.exp(-jnp.abs(x)))


def _sigmoid(x):
    return 1.0 / (1.0 + jnp.exp(-x))


def _dot(a, b, **kw):
    return jnp.dot(a, b, preferred_element_type=F32, **kw)


def _dot_nt(a, b):
    return lax.dot_general(a, b, (((1,), (1,)), ((), ())), preferred_element_type=F32)


def _dot_tn(a, b):
    return lax.dot_general(a, b, (((0,), (0,)), ((), ())), preferred_element_type=F32)


def _dot3(a, b):
    ah = a.astype(BF16)
    al = (a - ah.astype(F32)).astype(BF16)
    bh = b.astype(BF16)
    bl = (b - bh.astype(F32)).astype(BF16)
    return _dot(ah, bh) + (_dot(ah, bl) + _dot(al, bh))


def _stream_specs(first, tm, n_seq):
    if not first:
        return [pl.BlockSpec((1, tm, D_MODEL), lambda b, i: (b, i, 0))]
    return [pl.BlockSpec((1, tm, D_MODEL), lambda b, i: (b, jnp.clip(i - 1, 0, n_seq - 1), 0)),
            pl.BlockSpec((1, tm, D_MODEL), lambda b, i: (b, 0, 0))]


def _stream_tile(i, refs, n_seq):
    if len(refs) == 1:
        return refs[0][0]
    x_ref, head_ref = refs
    return jnp.where(i == 0, head_ref[0], jnp.where(i <= n_seq, x_ref[0], 0.0))


def _const_spec(shape):
    n = len(shape)
    return pl.BlockSpec(shape, lambda *_: (0,) * n, pipeline_mode=pl.Buffered(1))


def _inproj_kernel(*refs, tm, first, n_seq):
    n_stream = 2 if first else 1
    (g_ref, w_ref, mu_ref, fb_ref, fqg_ref, fkg_ref,
     rkv_ref, wdad_ref, sb_ref, fx_ref, fend_ref, gate_ref, carry_u, carry_f) = refs[n_stream:]
    i = pl.program_id(1)

    @pl.when(i == 0)
    def _():
        carry_u[...] = jnp.zeros_like(carry_u)
        carry_f[...] = jnp.zeros_like(carry_f)

    x = _stream_tile(i, refs[:n_stream], n_seq)
    ms = jnp.mean(x * x, axis=-1, keepdims=True)
    n = (x * lax.rsqrt(ms + NORM_EPS) * g_ref[...]).astype(BF16)
    row = lax.broadcasted_iota(jnp.int32, (tm, 1), 0)

    us = _dot(n, w_ref[:, 0:SHIFT_W])
    prev = pltpu.roll(us, 1, axis=0)
    prev = jnp.where(row == 0, carry_u[...], prev)
    carry_u[...] = us[tm - 1:tm, :]
    ush = us + (prev - us) * mu_ref[...]
    for j in range(3 * PAIRS):
        rkv_ref[0, j] = ush[:, j * LANE:(j + 1) * LANE]
    wdad_ref[0] = ush[:, 3 * RWKV_W:SHIFT_W]

    usb = _dot(n, w_ref[:, SEG_SB:SEG_SB + QKV_W])
    for j in range(3 * PAIRS):
        piece = usb[:, j * LANE:(j + 1) * LANE]
        if j < PAIRS:
            piece = piece * (ATT_SCALE * LOG2E)
        sb_ref[0, j] = piece.astype(BF16)

    uf = _dot(n, w_ref[:, SEG_FOX:SEG_FOX + QKV_W + LANE])
    lane = lax.broadcasted_iota(jnp.int32, (tm, LANE), 1)
    first = lane < HEAD_DIM
    t_glob = i * tm + row
    logf = _log_sigmoid(uf[:, QKV_W:QKV_W + LANE] + fb_ref[...])
    logf = jnp.where((lane < FOX_HEADS) & (t_glob >= PAD), logf, 0.0)
    tri = (lax.broadcasted_iota(jnp.int32, (tm, tm), 0)
           >= lax.broadcasted_iota(jnp.int32, (tm, tm), 1)).astype(BF16)
    f_hi = logf.astype(BF16)
    f_mid = (logf - f_hi.astype(F32)).astype(BF16)
    f_lo = (logf - f_hi.astype(F32) - f_mid.astype(F32)).astype(BF16)
    cum = _dot(tri, f_hi) + _dot(tri, f_mid) + _dot(tri, f_lo) + carry_f[...]
    carry_f[...] = cum[tm - 1:tm, :]
    fend_ref[0, 0] = cum[tm - 1:tm, :] * LOG2E

    def split3(hd):
        f2 = jnp.sum(jnp.where(lane == hd, cum, 0.0), axis=-1, keepdims=True) * LOG2E
        hi = f2.astype(BF16).astype(F32)
        mid = (f2 - hi).astype(BF16).astype(F32)
        return hi, mid, f2 - hi - mid

    def tail_cols(vals):
        out = jnp.zeros((tm, LANE), F32)
        for o, val in enumerate(vals):
            out = jnp.where(lane == HEAD_DIM + o, val, out)
        return out

    splits = [split3(hd) for hd in range(FOX_HEADS)]
    for j in range(3 * PAIRS):
        piece = uf[:, j * LANE:(j + 1) * LANE]
        kind = j // PAIRS
        if kind < 2:
            gain = fqg_ref[...] if kind == 0 else fkg_ref[...]
            sq = piece * piece
            ms0 = jnp.sum(jnp.where(first, sq, 0.0), axis=-1, keepdims=True) * (1.0 / HEAD_DIM)
            ms1 = jnp.sum(jnp.where(first, 0.0, sq), axis=-1, keepdims=True) * (1.0 / HEAD_DIM)
            inv = jnp.where(first, lax.rsqrt(ms0 + NORM_EPS), lax.rsqrt(ms1 + NORM_EPS))
            piece = piece * inv * gain
            if kind == 0:
                piece = piece * (ATT_SCALE * LOG2E)
        swapped = pltpu.roll(piece, HEAD_DIM, axis=1)
        for half in range(2):
            hd = 2 * (j % PAIRS) + half
            if hd >= FOX_HEADS:
                continue
            if kind == 2:
                extra = tail_cols([1.0])
            else:
                hi, mid, lo = splits[hd]
                extra = tail_cols([hi, mid, lo, 1.0, 1.0, 1.0] if kind == 0 else
                                  [1.0, 1.0, 1.0, jnp.where(t_glob >= PAD, -hi, NEG_INF), -mid, -lo])
            body = piece if half == 0 else swapped
            fx_ref[0, kind * FOX_HEADS + hd] = jnp.where(first, body, extra).astype(BF16)

    ug = _dot(n, w_ref[:, SEG_GATE:D_IN_PAD])
    gate_ref[0] = _sigmoid(ug).astype(BF16)


def _inproj(stream, g1, w_in_p, mu, fb, fqg, fkg, *, TP, n_seq, tm=INPROJ_TILE):
    first = len(stream) == 2
    B, D = stream[0].shape[0], D_MODEL
    nb = TP // tm
    kern = functools.partial(_inproj_kernel, tm=tm, first=first, n_seq=n_seq)
    out_shape = (
        jax.ShapeDtypeStruct((B, 3 * PAIRS, TP, LANE), F32),
        jax.ShapeDtypeStruct((B, TP, LANE), F32),
        jax.ShapeDtypeStruct((B, 3 * PAIRS, TP, LANE), BF16),
        jax.ShapeDtypeStruct((B, 3 * FOX_HEADS, TP, LANE), BF16),
        jax.ShapeDtypeStruct((B, nb, 1, LANE), F32),
        jax.ShapeDtypeStruct((B, TP, 3 * D_MODEL), BF16),
    )
    in_specs = _stream_specs(first, tm, n_seq) + [
        _const_spec((1, D)),
        _const_spec((D, D_IN_PAD)),
        _const_spec((1, SHIFT_W)),
        _const_spec((1, LANE)),
        _const_spec((1, LANE)),
        _const_spec((1, LANE)),
    ]
    out_specs = (
        pl.BlockSpec((1, 3 * PAIRS, tm, LANE), lambda b, i: (b, 0, i, 0)),
        pl.BlockSpec((1, tm, LANE), lambda b, i: (b, i, 0)),
        pl.BlockSpec((1, 3 * PAIRS, tm, LANE), lambda b, i: (b, 0, i, 0)),
        pl.BlockSpec((1, 3 * FOX_HEADS, tm, LANE), lambda b, i: (b, 0, i, 0)),
        pl.BlockSpec((1, 1, 1, LANE), lambda b, i: (b, i, 0, 0)),
        pl.BlockSpec((1, tm, 3 * D_MODEL), lambda b, i: (b, i, 0)),
    )
    return pl.pallas_call(
        kern, out_shape=out_shape, grid=(B, nb), in_specs=in_specs, out_specs=out_specs,
        scratch_shapes=[pltpu.VMEM((1, SHIFT_W), F32), pltpu.VMEM((1, LANE), F32)],
        compiler_params=pltpu.CompilerParams(
            dimension_semantics=("arbitrary", "arbitrary"), vmem_limit_bytes=VMEM_LIMIT),
        name="inproj",
    )(*stream, g1, w_in_p, mu, fb, fqg, fkg)


def _bdot(a, b):
    return lax.dot_general(a, b, (((2,), (1,)), ((0,), (0,))), preferred_element_type=F32)


def _bdot_nt(a, b):
    return lax.dot_general(a, b, (((2,), (2,)), ((0,), (0,))), preferred_element_type=F32)


def _bdot_tn(a, b):
    return lax.dot_general(a, b, (((1,), (1,)), ((0,), (0,))), preferred_element_type=F32)


def _head_sum(x, first):
    s0 = jnp.sum(jnp.where(first, x, 0.0), axis=-1, keepdims=True)
    s1 = jnp.sum(jnp.where(first, 0.0, x), axis=-1, keepdims=True)
    return jnp.where(first, s0, s1)


def _rwkv_chunk_kernel(r_ref, k_ref, v_ref, wdad_ref, wup_ref, w0_ref, aup_ref, a0_ref,
                       kk_ref, ka_ref, rk_ref,
                       g_ref, hc_ref, r2_ref, y0_ref, bonus_ref, *, rows):
    nc = rows // CHUNK
    r = r_ref[0, 0]
    k = k_ref[0, 0]
    v = v_ref[0, 0]
    wd = wdad_ref[0][:, 0:DECAY_RANK]
    ad = wdad_ref[0][:, DECAY_RANK:DECAY_RANK + ICLR_RANK]
    first = lax.broadcasted_iota(jnp.int32, (rows, LANE), 1) < HEAD_DIM

    pre = w0_ref[0] + _dot3(jnp.tanh(wd), wup_ref[0])
    lw = -jnp.exp(_log_sigmoid(pre) - 0.5)
    iclr = _sigmoid(a0_ref[0] + _dot3(ad, aup_ref[0]))
    kk = k * kk_ref[0]
    kk = kk / jnp.maximum(jnp.sqrt(_head_sum(kk * kk, first)), 1e-12)
    k2 = k * (1.0 + (iclr - 1.0) * ka_ref[0])
    b = kk * iclr
    bonus_ref[0, 0] = _head_sum(r * k2 * rk_ref[0], first) * v

    to3 = lambda x: x.reshape(nc, CHUNK, LANE)
    ri = lax.broadcasted_iota(jnp.int32, (nc, CHUNK, CHUNK), 1)
    ci = lax.broadcasted_iota(jnp.int32, (nc, CHUNK, CHUNK), 2)
    low_incl = ri >= ci
    low_strict = ri > ci
    first3 = lax.broadcasted_iota(jnp.int32, (nc, CHUNK, LANE), 2) < HEAD_DIM

    lw3 = to3(lw)
    tri = low_incl.astype(BF16)
    lw_hi = lw3.astype(BF16)
    cum = _bdot(tri, lw_hi) + _bdot(tri, (lw3 - lw_hi.astype(F32)).astype(BF16))
    cum_end = cum[:, CHUNK - 1:CHUNK, :]
    e_neg = jnp.exp(-cum)
    at = to3(-kk) * jnp.exp(cum - lw3)
    rt = to3(r) * jnp.exp(cum)
    bt = (to3(b) * e_neg).astype(BF16)
    kt = (to3(k2) * e_neg).astype(BF16)
    e_rem = jnp.exp(cum_end - cum)
    bq = (to3(b) * e_rem).astype(BF16)
    kq = (to3(k2) * e_rem).astype(BF16)
    vv = to3(v).astype(BF16)

    heads = range(2)
    sels = [first3, jnp.logical_not(first3)]
    lhs = [jnp.concatenate([jnp.where(sels[hd], at, 0.0), jnp.where(sels[hd], rt, 0.0)],
                           axis=1).astype(BF16) for hd in heads]
    mb = [_bdot_nt(lhs[hd], bt) for hd in heads]
    mk = [_bdot_nt(lhs[hd], kt) for hd in heads]
    m_ak = [jnp.where(low_strict, mk[hd][:, :CHUNK], 0.0).astype(BF16) for hd in heads]
    m_rb = [jnp.where(low_incl, mb[hd][:, CHUNK:], 0.0).astype(BF16) for hd in heads]
    m_rk = [jnp.where(low_incl, mk[hd][:, CHUNK:], 0.0).astype(BF16) for hd in heads]
    p = [jnp.where(low_strict, mb[hd][:, :CHUNK], 0.0) for hd in heads]
    xs = [jnp.concatenate([at, _bdot(m_ak[hd], vv)], axis=-1) for hd in heads]
    for j in range(6):
        p16 = [p[hd].astype(BF16) for hd in heads]
        xs = [xs[hd] + _bdot(p16[hd], xs[hd].astype(BF16)) for hd in heads]
        if j < 5:
            p = [_bdot(p16[hd], p16[hd]) for hd in heads]
    ru = [_bdot(m_rb[hd], xs[hd].astype(BF16)) for hd in heads]
    r2s = [rt + ru[hd][..., :LANE] for hd in heads]
    y0s = [ru[hd][..., LANE:] + _bdot(m_rk[hd], vv) for hd in heads]

    first3w = jnp.concatenate([first3, first3], axis=-1)
    x = jnp.where(first3w, xs[0], xs[1])
    r2_ref[0, 0] = jnp.where(first3, r2s[0], r2s[1]).reshape(rows, LANE)
    y0_ref[0, 0] = jnp.where(first3, y0s[0], y0s[1]).reshape(rows, LANE)

    pg = _bdot_tn(bq, x.astype(BF16))
    ph = pg[..., LANE:] + _bdot_tn(kq, vv)
    rr = lax.broadcasted_iota(jnp.int32, (nc, LANE, LANE), 1)
    cc = lax.broadcasted_iota(jnp.int32, (nc, LANE, LANE), 2)
    same_head = (rr < HEAD_DIM) == (cc < HEAD_DIM)
    g = jnp.where(same_head, pg[..., :LANE], 0.0) + jnp.where(rr == cc, jnp.exp(cum_end), 0.0)
    g_ref[0, 0] = g.reshape(nc * LANE, LANE)
    hc_ref[0, 0] = jnp.where(same_head, ph, 0.0).reshape(nc * LANE, LANE)


def _rwkv_chunks(rkv, wdad, wup, w0, aup, a0, k_k, k_a, r_k):
    B, _, TP, _ = rkv.shape
    rows = next(r for r in (1536, 768, 512, 256) if TP % r == 0)
    nb = TP // rows
    kern = functools.partial(_rwkv_chunk_kernel, rows=rows)
    slot_spec = lambda off: pl.BlockSpec((1, 1, rows, LANE), lambda b, p, i: (b, p + off, i, 0))
    par_mat = pl.BlockSpec((1, DECAY_RANK, LANE), lambda b, p, i: (p, 0, 0))
    par_vec = pl.BlockSpec((1, 1, LANE), lambda b, p, i: (p, 0, 0))
    row_out = jax.ShapeDtypeStruct((B, PAIRS, TP, LANE), F32)
    mat_out = jax.ShapeDtypeStruct((B, PAIRS, 2 * TP, LANE), F32)
    row_spec = pl.BlockSpec((1, 1, rows, LANE), lambda b, p, i: (b, p, i, 0))
    mat_spec = pl.BlockSpec((1, 1, 2 * rows, LANE), lambda b, p, i: (b, p, i, 0))
    return pl.pallas_call(
        kern, out_shape=(mat_out, mat_out, row_out, row_out, row_out), grid=(B, PAIRS, nb),
        in_specs=[slot_spec(0), slot_spec(PAIRS), slot_spec(2 * PAIRS),
                  pl.BlockSpec((1, rows, LANE), lambda b, p, i: (b, i, 0)),
                  par_mat, par_vec, par_mat, par_vec, par_vec, par_vec, par_vec],
        out_specs=(mat_spec, mat_spec, row_spec, row_spec, row_spec),
        compiler_params=pltpu.CompilerParams(
            dimension_semantics=("arbitrary",) * 3, vmem_limit_bytes=VMEM_LIMIT),
        name="rwkv_chunks",
    )(rkv, rkv, rkv, wdad, wup, w0, aup, a0, k_k, k_a, r_k)


def _rwkv_scan_kernel(g_ref, hc_ref, r2_ref, y0_ref, bonus_ref, lng_ref, lnb_ref, y_ref,
                      state, *, rows):
    i = pl.program_id(1)

    @pl.when(i == 0)
    def _():
        state[...] = jnp.zeros_like(state)

    first = lax.broadcasted_iota(jnp.int32, (CHUNK, LANE), 1) < HEAD_DIM
    hs = [state[p] for p in range(PAIRS)]
    for c in range(rows // CHUNK):
        sl = slice(c * CHUNK, (c + 1) * CHUNK)
        sm = slice(c * LANE, (c + 1) * LANE)
        for p in range(PAIRS):
            y = _dot3(r2_ref[0, p, sl, :], hs[p]) + y0_ref[0, p, sl, :]
            hs[p] = _dot3(g_ref[0, p, sm, :], hs[p]) + hc_ref[0, p, sm, :]
            yc = y - _head_sum(y, first) * (1.0 / HEAD_DIM)
            var = _head_sum(yc * yc, first) * (1.0 / HEAD_DIM)
            out = yc * lax.rsqrt(var + RWKV_LN_EPS) * lng_ref[p] + lnb_ref[p]
            y_ref[0, p, sl, :] = (out + bonus_ref[0, p, sl, :]).astype(BF16)
    for p in range(PAIRS):
        state[p] = hs[p]


def _rwkv_scan(gm, hc, r2, y0, bonus, ln_g, ln_b):
    B, P, TP, _ = r2.shape
    rows = next(r for r in (1536, 512) if TP % r == 0)
    nb = TP // rows
    kern = functools.partial(_rwkv_scan_kernel, rows=rows)
    blk = pl.BlockSpec((1, P, rows, LANE), lambda b, i: (b, 0, i, 0))
    mat = pl.BlockSpec((1, P, 2 * rows, LANE), lambda b, i: (b, 0, i, 0))
    par = pl.BlockSpec((P, 1, LANE), lambda b, i: (0, 0, 0))
    return pl.pallas_call(
        kern, out_shape=jax.ShapeDtypeStruct((B, P, TP, LANE), BF16), grid=(B, nb),
        in_specs=[mat, mat, blk, blk, blk, par, par], out_specs=blk,
        scratch_shapes=[pltpu.VMEM((P, LANE, LANE), F32)],
        compiler_params=pltpu.CompilerParams(
            dimension_semantics=("arbitrary", "arbitrary"), vmem_limit_bytes=VMEM_LIMIT),
        name="rwkv_scan",
    )(gm, hc, r2, y0, bonus, ln_g, ln_b)


def _sb_kernel(*refs, t):
    q_refs, k_refs, v_refs = refs[:PAIRS], refs[PAIRS:2 * PAIRS], refs[2 * PAIRS:3 * PAIRS]
    o_ref = refs[3 * PAIRS]
    qi = pl.program_id(1)
    first = lax.broadcasted_iota(jnp.int32, (t, LANE), 1) < HEAD_DIM
    heads = [(hd // 2, hd % 2) for hd in range(SB_HEADS)]
    qs = []
    for slot, half in heads:
        qp = q_refs[slot][0, 0]
        keep = first if half == 0 else jnp.logical_not(first)
        qs.append(jnp.where(keep, qp, jnp.zeros_like(qp)))
    nu = len(heads)
    qpos = qi * t + lax.broadcasted_iota(jnp.int32, (t, 1), 0)
    kloc = lax.broadcasted_iota(jnp.int32, (1, t), 1)
    upper = (lax.broadcasted_iota(jnp.int32, (t, t), 0)
             > lax.broadcasted_iota(jnp.int32, (t, t), 1)).astype(BF16)

    def sweep(blocks, accs, cs):
        units = []
        for kb, mask in blocks:
            start = pl.multiple_of(kb * t, t)
            kblks = [k_refs[s][0, 0, pl.ds(start, t), :] for s in range(PAIRS)]
            vblks = [v_refs[s][0, 0, pl.ds(start, t), :] for s in range(PAIRS)]
            m = None if mask is None else mask(start + kloc)
            units += [(u, kblks[slot], vblks[slot], m) for u, (slot, _) in enumerate(heads)]
        zs = [_dot_nt(qs[u], kblk) for u, kblk, _, _ in units]
        sps = [jnp.maximum(z, 0.0) + jnp.log2(1.0 + jnp.exp2(-jnp.abs(z))) for z in zs]
        spms = [sp if un[3] is None else jnp.where(un[3], sp, 0.0) for sp, un in zip(sps, units)]
        laters = [_dot(spm.astype(BF16), upper) for spm in spms]
        accs, cs = list(accs), list(cs)
        weights = []
        for (u, _, _, m), z, sp, spm, later in zip(units, zs, sps, spms, laters):
            a = jnp.exp2(z - sp - later - cs[u])
            weights.append(a if m is None else jnp.where(m, a, 0.0))
            cs[u] = cs[u] + jnp.sum(spm, axis=-1, keepdims=True)
        for (u, _, vblk, _), a in zip(units, weights):
            accs[u] = accs[u] + _dot(a.astype(BF16), vblk)
        return accs, cs

    def block(kb, accs, cs, mask):
        return sweep([(kb, mask)], accs, cs)

    def live(cs):
        low = cs[0]
        for c in cs[1:]:
            low = jnp.minimum(low, c)
        return (jnp.min(low) < SB_CUTOFF_LOG2).astype(jnp.int32)

    zero_acc = [jnp.zeros((t, LANE), F32) for _ in range(nu)]
    zero_c = [jnp.zeros((t, 1), F32) for _ in range(nu)]
    near = [(qi, lambda kpos: (kpos >= PAD) & (kpos < qpos))]
    for j in range(1, SB_FUSED):
        near.append((jnp.maximum(qi - j, 0), lambda kpos, j=j: (kpos >= PAD) & (qi >= j)))
    accs, cs = sweep(near, zero_acc, zero_c)

    def cond(carry):
        kb, alive = carry[0], carry[1]
        return (kb >= 1) & (alive > 0)

    def body(carry):
        kb = carry[0]
        accs, cs = block(kb, list(carry[2:2 + nu]), list(carry[2 + nu:]), None)
        return (kb - 1, live(cs), *accs, *cs)

    carry = lax.while_loop(cond, body, (qi - SB_FUSED, live(cs), *accs, *cs))

    def front(carry):
        accs, cs = block(0, list(carry[2:2 + nu]), list(carry[2 + nu:]), lambda kpos: kpos >= PAD)
        return (carry[0], carry[1], *accs, *cs)

    carry = lax.cond((carry[0] == 0) & (carry[1] > 0), front, lambda c: c, carry)
    accs = carry[2:2 + nu]
    for slot in range(PAIRS):
        lo = accs[2 * slot]
        hi = accs[2 * slot + 1] if 2 * slot + 1 < nu else 0.0
        o_ref[0, slot] = jnp.where(first, lo, hi).astype(BF16)


def _sb_attention(sb, *, t=SB_TILE):
    B, _, TP, _ = sb.shape
    kern = functools.partial(_sb_kernel, t=t)
    q_specs = [pl.BlockSpec((1, 1, t, LANE), lambda b, i, s=s: (b, s, i, 0)) for s in range(PAIRS)]
    kv_specs = [pl.BlockSpec((1, 1, TP, LANE), lambda b, i, s=s: (b, s, 0, 0),
                             pipeline_mode=pl.Buffered(1)) for s in range(PAIRS, 3 * PAIRS)]
    return pl.pallas_call(
        kern, out_shape=jax.ShapeDtypeStruct((B, PAIRS, TP, LANE), BF16),
        grid=(B, TP // t),
        in_specs=q_specs + kv_specs,
        out_specs=pl.BlockSpec((1, PAIRS, t, LANE), lambda b, i: (b, 0, i, 0)),
        compiler_params=pltpu.CompilerParams(
            dimension_semantics=("arbitrary",) * 2, vmem_limit_bytes=VMEM_LIMIT),
        name="sb_attention",
    )(*([sb] * (3 * PAIRS)))


def _fox_kernel(shift_ref, fe_ref, q_ref, k_ref, v_ref, o_ref, lo_ref, acc_ref, *, t, online):
    b = pl.program_id(0)
    h = pl.program_id(1)
    qi = pl.program_id(2)
    q = q_ref[0, 0]
    shift = shift_ref[0]
    thr = FOX_SKIP_LOG2 + 2.0 * shift
    f_q = fe_ref[b, h, jnp.maximum(qi - 1, 0)]
    lo = lax.while_loop(lambda n: (n < qi) & (fe_ref[b, h, n] - f_q > thr), lambda n: n + 1,
                        jnp.where(qi == 0, 0, lo_ref[0]))
    lo_ref[0] = lo
    qpos = qi * t + lax.broadcasted_iota(jnp.int32, (t, 1), 0)
    kloc = lax.broadcasted_iota(jnp.int32, (1, t), 1)

    def scores(kb, masked):
        start = pl.multiple_of(kb * t, t)
        s = _dot_nt(q, k_ref[0, 0, pl.ds(start, t), :])
        if masked:
            s = jnp.where(start + kloc <= qpos, s, NEG_INF)
        return s, v_ref[0, 0, pl.ds(start, t), :]

    if online:
        def block(kb, carry, masked):
            acc, m = carry
            s, vblk = scores(kb, masked)
            m_new = jnp.maximum(m, jnp.max(s, axis=-1, keepdims=True))
            pr = jnp.exp2(s - m_new)
            return jnp.exp2(m - m_new) * acc + _dot(pr.astype(BF16), vblk), m_new

        carry = (jnp.zeros((t, LANE), F32), jnp.full((t, 1), NEG_INF, F32))
        carry = lax.fori_loop(lo, qi, lambda kb, c: block(kb, c, False), carry)
        acc, _ = block(qi, carry, True)
    else:
        def add_blocks(kbs, masked):
            parts = [scores(kb, masked) for kb in kbs]
            prs = [jnp.exp2(s - shift).astype(BF16) for s, _ in parts]
            total = _dot(prs[0], parts[0][1])
            for pr, (_, vblk) in zip(prs[1:], parts[1:]):
                total = total + _dot(pr, vblk)
            acc_ref[...] += total

        def group(i, c):
            add_blocks([lo + FOX_GROUP * i + j for j in range(FOX_GROUP)], False)
            return c

        def single(kb, c):
            add_blocks([kb], False)
            return c

        n = qi - lo
        acc_ref[...] = jnp.zeros((t, LANE), F32)
        lax.fori_loop(0, n // FOX_GROUP, group, 0)
        lax.fori_loop(lo + (n // FOX_GROUP) * FOX_GROUP, qi, single, 0)
        add_blocks([qi], True)
        acc = acc_ref[...]
    lane = lax.broadcasted_iota(jnp.int32, (t, LANE), 1)
    denom = jnp.sum(jnp.where(lane == HEAD_DIM, acc, 0.0), axis=-1, keepdims=True)
    o_ref[0, 0] = (acc / jnp.where(denom > 0.0, denom, 1.0)).astype(BF16)


def _fox_attention(shift, fe, fx, *, online, t=FOX_TILE):
    B, _, TP, _ = fx.shape
    H = FOX_HEADS
    kern = functools.partial(_fox_kernel, t=t, online=online)
    return pl.pallas_call(
        kern, out_shape=jax.ShapeDtypeStruct((B, H, TP, LANE), BF16),
        grid=(B, H, TP // t),
        in_specs=[pl.BlockSpec(memory_space=pltpu.SMEM),
                  pl.BlockSpec(memory_space=pltpu.SMEM),
                  pl.BlockSpec((1, 1, t, LANE), lambda b, h, i: (b, h, i, 0)),
                  pl.BlockSpec((1, 1, TP, LANE), lambda b, h, i: (b, h + H, 0, 0)),
                  pl.BlockSpec((1, 1, TP, LANE), lambda b, h, i: (b, h + 2 * H, 0, 0))],
        out_specs=pl.BlockSpec((1, 1, t, LANE), lambda b, h, i: (b, h, i, 0)),
        scratch_shapes=[pltpu.SMEM((1,), jnp.int32), pltpu.VMEM((t, LANE), F32)],
        compiler_params=pltpu.CompilerParams(
            dimension_semantics=("arbitrary",) * 3, vmem_limit_bytes=VMEM_LIMIT),
        name="fox_attention",
    )(shift, fe, fx, fx, fx)


def _merge_kernel(*refs, tm, n_valid, first, n_seq):
    n_stream = 2 if first else 1
    (ya_ref, yb_ref, yc_ref, gate_ref,
     wpa_ref, wpb_ref, wpc_ref, wout_ref, o_ref) = refs[n_stream:]
    i = pl.program_id(1)

    def proj(slots, w_ref):
        return _dot(jnp.concatenate(slots, axis=-1), w_ref[...])

    pa = proj([ya_ref[0, s] for s in range(PAIRS)], wpa_ref)
    pb = proj([yb_ref[0, s] for s in range(PAIRS)], wpb_ref)
    pc = proj([yc_ref[0, s] for s in range(FOX_HEADS)], wpc_ref)
    merged = gate_ref[0, :, 0:D_MODEL].astype(F32) * pa
    merged = merged + gate_ref[0, :, D_MODEL:2 * D_MODEL].astype(F32) * pb
    merged = merged + gate_ref[0, :, 2 * D_MODEL:3 * D_MODEL].astype(F32) * pc
    out = _stream_tile(i, refs[:n_stream], n_seq) + _dot(merged.astype(BF16), wout_ref[...])
    t = i * tm + lax.broadcasted_iota(jnp.int32, (tm, 1), 0)
    o_ref[0] = jnp.where((t >= PAD) & (t < PAD + n_valid), out, 0.0)


def _merge(stream, ya, yb, yc, gates, wpa, wpb, wpc, wout, *, n_valid, n_seq, tm=FRAME_TILE):
    first = len(stream) == 2
    B, _, TP, _ = ya.shape
    D = D_MODEL
    kern = functools.partial(_merge_kernel, tm=tm, n_valid=n_valid, first=first, n_seq=n_seq)
    pair_blk = lambda ns: pl.BlockSpec((1, ns, tm, LANE), lambda b, i: (b, 0, i, 0))
    return pl.pallas_call(
        kern, out_shape=jax.ShapeDtypeStruct((B, TP, D), F32), grid=(B, TP // tm),
        in_specs=_stream_specs(first, tm, n_seq) + [
                  pair_blk(PAIRS),
                  pair_blk(PAIRS), pair_blk(FOX_HEADS),
                  pl.BlockSpec((1, tm, 3 * D), lambda b, i: (b, i, 0)),
                  _const_spec((PAIRS * LANE, D)),
                  _const_spec((PAIRS * LANE, D)),
                  _const_spec((FOX_HEADS * LANE, D)),
                  _const_spec((D, D))],
        out_specs=pl.BlockSpec((1, tm, D), lambda b, i: (b, i, 0)),
        compiler_params=pltpu.CompilerParams(
            dimension_semantics=("arbitrary", "arbitrary"), vmem_limit_bytes=VMEM_LIMIT),
        name="merge",
    )(*stream, ya, yb, yc, gates, wpa, wpb, wpc, wout)


def _moe_kernel(*refs, tm, last):
    n_stream = 2 if last else 1
    g_ref, wr_ref, br_ref, wg_ref, wu_ref, wd_ref, o_ref, he_ref, ys_ref = refs[n_stream:]
    x = jnp.concatenate([r[0] for r in refs[:n_stream]], axis=0)
    ms = jnp.mean(x * x, axis=-1, keepdims=True)
    n = x * lax.rsqrt(ms + NORM_EPS) * g_ref[...]

    lg = _dot3(n, wr_ref[...]) + br_ref[...]
    lane = lax.broadcasted_iota(jnp.int32, (tm, LANE), 1)
    big = jnp.int32(LANE)
    is_group = lane < N_GROUPS
    gl = jnp.where(is_group, lg, -jnp.inf)
    gmax = jnp.max(gl, axis=-1, keepdims=True)
    g_idx = jnp.min(jnp.where(is_group & (gl == gmax), lane, big), axis=-1, keepdims=True)
    p_group = 1.0 / jnp.sum(jnp.where(is_group, jnp.exp(gl - gmax), 0.0), axis=-1, keepdims=True)
    lo = N_GROUPS + g_idx * EXPERTS_PER_GROUP
    in_grp = (lane >= lo) & (lane < lo + EXPERTS_PER_GROUP)
    el = jnp.where(in_grp, lg, -jnp.inf)
    top1 = jnp.max(el, axis=-1, keepdims=True)
    i1 = jnp.min(jnp.where(in_grp & (el == top1), lane, big), axis=-1, keepdims=True)
    rest = in_grp & (lane != i1)
    el2 = jnp.where(rest, lg, -jnp.inf)
    top2 = jnp.max(el2, axis=-1, keepdims=True)
    i2 = jnp.min(jnp.where(rest & (el2 == top2), lane, big), axis=-1, keepdims=True)
    e2 = jnp.exp(top2 - top1)
    p1 = 1.0 / (1.0 + e2)
    p2 = e2 / (1.0 + e2)
    comb = p_group * (jnp.where(lane == i1, p1, 0.0) + jnp.where(lane == i2, p2, 0.0))

    n16 = n.astype(BF16)

    onehot = (lane == g_idx).astype(BF16)
    earlier = (lax.broadcasted_iota(jnp.int32, (tm, tm), 1)
               < lax.broadcasted_iota(jnp.int32, (tm, tm), 0)).astype(BF16)
    cnt = _dot(earlier, onehot)
    rank = jnp.sum(jnp.where(lane == g_idx, cnt, 0.0), axis=-1, keepdims=True).astype(jnp.int32)
    n_pass = (jnp.max(rank) + MOE_CAP) // MOE_CAP
    comb16 = comb.astype(BF16)
    slot_lane = lax.broadcasted_iota(jnp.int32, (tm, N_GROUPS * MOE_CAP), 1)

    def one_pass(p, carry):
        r = rank - p * MOE_CAP
        slot = jnp.where((r >= 0) & (r < MOE_CAP), g_idx * MOE_CAP + r, -1)
        place = (slot_lane == slot).astype(BF16)
        xs = _dot_tn(place, n16).astype(BF16)
        cw = _dot_tn(place, comb16)
        for g in range(N_GROUPS):
            rows = slice(g * MOE_CAP, (g + 1) * MOE_CAP)
            for e in range(EXPERTS_PER_GROUP):
                ee = g * EXPERTS_PER_GROUP + e
                gate = _dot(xs[rows], wg_ref[ee])
                up = _dot(xs[rows], wu_ref[ee])
                c_e = cw[rows, N_GROUPS + ee:N_GROUPS + ee + 1]
                he_ref[:, e * D_EXPERT:(e + 1) * D_EXPERT] = (
                    gate * _sigmoid(gate) * up * c_e).astype(BF16)
            gw = EXPERTS_PER_GROUP * D_EXPERT
            ys_ref[rows, :] = _dot(he_ref[...], wd_ref[g * gw:(g + 1) * gw, :]).astype(BF16)
        o_ref[0] = o_ref[0] + _dot(place, ys_ref[...])
        return carry

    o_ref[0] = x
    lax.fori_loop(0, n_pass, one_pass, 0)


def _moe(h, g2, wr, br, wg, wu, wd, *, n_seq, last):
    B, TP, D = h.shape
    tm = MOE_TILE
    kern = functools.partial(_moe_kernel, tm=tm, last=last)
    EW = N_EXPERTS * D_EXPERT
    if last:
        half = tm // FRAME_TILE
        assert half == 2 and n_seq % half == 0
        rows_out = n_seq * FRAME_TILE
        stream_specs = [pl.BlockSpec((1, FRAME_TILE, D), lambda b, i, j=j: (b, half * i + 1 + j, 0))
                        for j in range(half)]
        stream = [h] * half
    else:
        rows_out = TP
        stream_specs = [pl.BlockSpec((1, tm, D), lambda b, i: (b, i, 0))]
        stream = [h]
    return pl.pallas_call(
        kern, out_shape=jax.ShapeDtypeStruct((B, rows_out, D), F32), grid=(B, rows_out // tm),
        in_specs=stream_specs + [
                  _const_spec((1, D)), _const_spec((D, LANE)), _const_spec((1, LANE)),
                  _const_spec((N_EXPERTS, D, D_EXPERT)), _const_spec((N_EXPERTS, D, D_EXPERT)),
                  _const_spec((EW, D))],
        out_specs=pl.BlockSpec((1, tm, D), lambda b, i: (b, i, 0)),
        scratch_shapes=[pltpu.VMEM((MOE_CAP, EXPERTS_PER_GROUP * D_EXPERT), BF16),
                        pltpu.VMEM((N_GROUPS * MOE_CAP, D), BF16)],
        compiler_params=pltpu.CompilerParams(
            dimension_semantics=("arbitrary", "arbitrary"), vmem_limit_bytes=VMEM_LIMIT),
        name="moe",
    )(*stream, g2, wr, br, wg, wu, wd)


def _transpose_kernel(x_ref, o_ref):
    o_ref[...] = x_ref[...].T.astype(BF16)


def _transpose_to_bf16(a, *, tn=LANE):
    N, K = a.shape
    assert N % tn == 0
    return pl.pallas_call(
        _transpose_kernel, out_shape=jax.ShapeDtypeStruct((K, N), BF16), grid=(N // tn,),
        in_specs=[pl.BlockSpec((tn, K), lambda i: (i, 0))],
        out_specs=pl.BlockSpec((K, tn), lambda i: (0, i)),
        compiler_params=pltpu.CompilerParams(dimension_semantics=("arbitrary",)),
        name="transpose_cast",
    )(a)


def _pack_w_in(w_in):
    wt = jnp.swapaxes(w_in, 0, 1)
    rest = wt[SHIFT_W:]
    rows = [wt[:SHIFT_W]]
    off = 0
    for heads in (SB_HEADS,) * 3 + (FOX_HEADS,) * 3:
        rows.append(jnp.pad(rest[off:off + heads * HEAD_DIM],
                            ((0, PAIRS * LANE - heads * HEAD_DIM), (0, 0))))
        off += heads * HEAD_DIM
    rows.append(jnp.pad(rest[off:off + FOX_HEADS], ((0, LANE - FOX_HEADS), (0, 0))))
    rows.append(rest[off + FOX_HEADS:])
    return _transpose_to_bf16(jnp.concatenate(rows, axis=0))


def _per_pair_cols(w):
    return w.reshape(w.shape[0], PAIRS, LANE).transpose(1, 0, 2)


def kernel(x, meta_tokens, norm1_g, w_in, rwkv_mu, rwkv_w_up, rwkv_w0, rwkv_a_up, rwkv_a0,
           rwkv_k_k, rwkv_k_a, rwkv_r_k, rwkv_ln_g, rwkv_ln_b, fox_f_b, fox_q_g, fox_k_g,
           w_p_rwkv, w_p_sb, w_p_fox, w_out, norm2_g, moe_wg, moe_bg, moe_we, moe_be,
           moe_w_gate, moe_w_up, moe_w_down):
    B, S, D = x.shape
    depth = w_in.shape[0]
    L = N_META + S
    TP = -(-(PAD + L) // ROW_TILE) * ROW_TILE
    assert S % FRAME_TILE == 0
    n_seq = S // FRAME_TILE
    meta = jnp.broadcast_to(meta_tokens[None].astype(x.dtype), (B, N_META, D))
    stream = (x, jnp.concatenate([jnp.zeros((B, PAD, D), x.dtype), meta], axis=1))
    H = RWKV_HEADS
    EW = N_EXPERTS * D_EXPERT
    for l in range(depth):
        fb = jnp.zeros((1, LANE), F32).at[0, :FOX_HEADS].set(fox_f_b[l])
        rkv, wdad, sb, fx, fend, gates = _inproj(
            stream, norm1_g[l][None], _pack_w_in(w_in[l]), rwkv_mu[l][None], fb,
            jnp.tile(fox_q_g[l], 2)[None], jnp.tile(fox_k_g[l], 2)[None], TP=TP, n_seq=n_seq)
        hv = lambda p: p.reshape(PAIRS, 1, LANE)
        gm, hc, r2, y0, bonus = _rwkv_chunks(
            rkv, wdad, _per_pair_cols(rwkv_w_up[l]), hv(rwkv_w0[l]),
            _per_pair_cols(rwkv_a_up[l]), hv(rwkv_a0[l]),
            hv(rwkv_k_k[l]), hv(rwkv_k_a[l]), hv(rwkv_r_k[l]))
        ya = _rwkv_scan(gm, hc, r2, y0, bonus, hv(rwkv_ln_g[l]), hv(rwkv_ln_b[l]))
        yb = _sb_attention(sb)
        shift = (HEAD_DIM * ATT_SCALE * LOG2E
                 * jnp.max(jnp.abs(fox_q_g[l])) * jnp.max(jnp.abs(fox_k_g[l]))).reshape(1)
        step = FOX_TILE // INPROJ_TILE
        fe = jnp.transpose(fend[:, step - 1::step, 0, :FOX_HEADS], (0, 2, 1))
        yc = lax.cond(shift[0] <= FOX_FIXED_SHIFT_MAX,
                      functools.partial(_fox_attention, online=False),
                      functools.partial(_fox_attention, online=True), shift, fe, fx)
        pc = jnp.pad(w_p_fox[l].reshape(FOX_HEADS, HEAD_DIM, D),
                     ((0, 0), (0, LANE - HEAD_DIM), (0, 0))).reshape(FOX_HEADS * LANE, D).astype(BF16)
        pp = lambda w, nh: jnp.pad(w, ((0, PAIRS * LANE - nh * HEAD_DIM), (0, 0))).astype(BF16)
        h = _merge(stream, ya, yb, yc, gates, pp(w_p_rwkv[l], H), pp(w_p_sb[l], SB_HEADS),
                   pc, w_out[l].astype(BF16), n_valid=L, n_seq=n_seq)
        wr = jnp.zeros((D, LANE), F32).at[:, :N_GROUPS].set(moe_wg[l])
        wr = wr.at[:, N_GROUPS:N_GROUPS + N_EXPERTS].set(moe_we[l])
        br = jnp.zeros((1, LANE), F32).at[0, :N_GROUPS].set(moe_bg[l])
        br = br.at[0, N_GROUPS:N_GROUPS + N_EXPERTS].set(moe_be[l])
        wg = moe_w_gate[l].astype(BF16)
        wu = moe_w_up[l].astype(BF16)
        wd = moe_w_down[l].reshape(EW, D).astype(BF16)
        h = _moe(h, norm2_g[l][None], wr, br, wg, wu, wd, n_seq=n_seq, last=l == depth - 1)
        stream = (h,)
    return h
```

```python
import functools
import math

import jax
import jax.numpy as jnp
from jax import lax
from jax.experimental import pallas as pl
from jax.experimental.pallas import tpu as pltpu

D_MODEL = 1024
HEAD_DIM = 64
N_META = 16
FRAME_TILE = 256
MOE_TILE = 512
MOE_CAP = 192
PAD = FRAME_TILE - N_META
RWKV_HEADS = 6
SB_HEADS = 5
FOX_HEADS = 5
RWKV_W = RWKV_HEADS * HEAD_DIM
DECAY_RANK = 64
ICLR_RANK = 64
SHIFT_W = 3 * RWKV_W + DECAY_RANK + ICLR_RANK
N_GROUPS = 4
EXPERTS_PER_GROUP = 4
N_EXPERTS = 16
D_EXPERT = 256
NORM_EPS = 1e-6
RWKV_LN_EPS = 64e-5
NEG_INF = -1e30
ATT_SCALE = 1.0 / math.sqrt(HEAD_DIM)

LANE = 128
ROW_TILE = 512
PAIRS = 3
QKV_W = 3 * PAIRS * LANE
SEG_SB = SHIFT_W
SEG_FOX = SEG_SB + QKV_W
SEG_GATE = SEG_FOX + QKV_W + LANE
D_IN_PAD = SEG_GATE + 3 * D_MODEL
SB_CUTOFF_LOG2 = 160.0
SB_TILE = 128
SB_FUSED = 3
LOG2E = 1.4426950408889634
FOX_FIXED_SHIFT_MAX = 50.0
FOX_SKIP_LOG2 = 150.0
FOX_TILE = 512
FOX_GROUP = 4
INPROJ_TILE = FRAME_TILE

CHUNK = 64
VMEM_LIMIT = 56 * 1024 * 1024

F32 = jnp.float32
BF16 = jnp.bfloat16


def _log_sigmoid(x):
    return jnp.minimum(x, 0.0) - jnp.log1p(jnp.exp(-jnp.abs(x)))


def _sigmoid(x):
    return 1.0 / (1.0 + jnp.exp(-x))


def _dot(a, b, **kw):
    return jnp.dot(a, b, preferred_element_type=F32, **kw)


def _dot_nt(a, b):
    return lax.dot_general(a, b, (((1,), (1,)), ((), ())), preferred_element_type=F32)


def _dot_tn(a, b):
    return lax.dot_general(a, b, (((0,), (0,)), ((), ())), preferred_element_type=F32)


def _dot3(a, b):
    ah = a.astype(BF16)
    al = (a - ah.astype(F32)).astype(BF16)
    bh = b.astype(BF16)
    bl = (b - bh.astype(F32)).astype(BF16)
    return _dot(ah, bh) + (_dot(ah, bl) + _dot(al, bh))


def _stream_specs(first, tm, n_seq):
    if not first:
        return [pl.BlockSpec((1, tm, D_MODEL), lambda b, i: (b, i, 0))]
    return [pl.BlockSpec((1, tm, D_MODEL), lambda b, i: (b, jnp.clip(i - 1, 0, n_seq - 1), 0)),
            pl.BlockSpec((1, tm, D_MODEL), lambda b, i: (b, 0, 0))]


def _stream_tile(i, refs, n_seq):
    if len(refs) == 1:
        return refs[0][0]
    x_ref, head_ref = refs
    return jnp.where(i == 0, head_ref[0], jnp.where(i <= n_seq, x_ref[0], 0.0))


def _const_spec(shape):
    n = len(shape)
    return pl.BlockSpec(shape, lambda *_: (0,) * n, pipeline_mode=pl.Buffered(1))


def _inproj_kernel(*refs, tm, first, n_seq):
    n_stream = 2 if first else 1
    (g_ref, w_ref, mu_ref, fb_ref, fqg_ref, fkg_ref,
     rkv_ref, wdad_ref, sb_ref, fx_ref, fend_ref, gate_ref, carry_u, carry_f) = refs[n_stream:]
    i = pl.program_id(1)

    @pl.when(i == 0)
    def _():
        carry_u[...] = jnp.zeros_like(carry_u)
        carry_f[...] = jnp.zeros_like(carry_f)

    x = _stream_tile(i, refs[:n_stream], n_seq)
    ms = jnp.mean(x * x, axis=-1, keepdims=True)
    n = (x * lax.rsqrt(ms + NORM_EPS) * g_ref[...]).astype(BF16)
    row = lax.broadcasted_iota(jnp.int32, (tm, 1), 0)

    us = _dot(n, w_ref[:, 0:SHIFT_W])
    prev = pltpu.roll(us, 1, axis=0)
    prev = jnp.where(row == 0, carry_u[...], prev)
    carry_u[...] = us[tm - 1:tm, :]
    ush = us + (prev - us) * mu_ref[...]
    for j in range(3 * PAIRS):
        rkv_ref[0, j] = ush[:, j * LANE:(j + 1) * LANE]
    wdad_ref[0] = ush[:, 3 * RWKV_W:SHIFT_W]

    usb = _dot(n, w_ref[:, SEG_SB:SEG_SB + QKV_W])
    for j in range(3 * PAIRS):
        piece = usb[:, j * LANE:(j + 1) * LANE]
        if j < PAIRS:
            piece = piece * (ATT_SCALE * LOG2E)
        sb_ref[0, j] = piece.astype(BF16)

    uf = _dot(n, w_ref[:, SEG_FOX:SEG_FOX + QKV_W + LANE])
    lane = lax.broadcasted_iota(jnp.int32, (tm, LANE), 1)
    first = lane < HEAD_DIM
    t_glob = i * tm + row
    logf = _log_sigmoid(uf[:, QKV_W:QKV_W + LANE] + fb_ref[...])
    logf = jnp.where((lane < FOX_HEADS) & (t_glob >= PAD), logf, 0.0)
    tri = (lax.broadcasted_iota(jnp.int32, (tm, tm), 0)
           >= lax.broadcasted_iota(jnp.int32, (tm, tm), 1)).astype(BF16)
    f_hi = logf.astype(BF16)
    f_mid = (logf - f_hi.astype(F32)).astype(BF16)
    f_lo = (logf - f_hi.astype(F32) - f_mid.astype(F32)).astype(BF16)
    cum = _dot(tri, f_hi) + _dot(tri, f_mid) + _dot(tri, f_lo) + carry_f[...]
    carry_f[...] = cum[tm - 1:tm, :]
    fend_ref[0, 0] = cum[tm - 1:tm, :] * LOG2E

    def split3(hd):
        f2 = jnp.sum(jnp.where(lane == hd, cum, 0.0), axis=-1, keepdims=True) * LOG2E
        hi = f2.astype(BF16).astype(F32)
        mid = (f2 - hi).astype(BF16).astype(F32)
        return hi, mid, f2 - hi - mid

    def tail_cols(vals):
        out = jnp.zeros((tm, LANE), F32)
        for o, val in enumerate(vals):
            out = jnp.where(lane == HEAD_DIM + o, val, out)
        return out

    splits = [split3(hd) for hd in range(FOX_HEADS)]
    for j in range(3 * PAIRS):
        piece = uf[:, j * LANE:(j + 1) * LANE]
        kind = j // PAIRS
        if kind < 2:
            gain = fqg_ref[...] if kind == 0 else fkg_ref[...]
            sq = piece * piece
            ms0 = jnp.sum(jnp.where(first, sq, 0.0), axis=-1, keepdims=True) * (1.0 / HEAD_DIM)
            ms1 = jnp.sum(jnp.where(first, 0.0, sq), axis=-1, keepdims=True) * (1.0 / HEAD_DIM)
            inv = jnp.where(first, lax.rsqrt(ms0 + NORM_EPS), lax.rsqrt(ms1 + NORM_EPS))
            piece = piece * inv * gain
            if kind == 0:
                piece = piece * (ATT_SCALE * LOG2E)
        swapped = pltpu.roll(piece, HEAD_DIM, axis=1)
        for half in range(2):
            hd = 2 * (j % PAIRS) + half
            if hd >= FOX_HEADS:
                continue
            if kind == 2:
                extra = tail_cols([1.0])
            else:
                hi, mid, lo = splits[hd]
                extra = tail_cols([hi, mid, lo, 1.0, 1.0, 1.0] if kind == 0 else
                                  [1.0, 1.0, 1.0, jnp.where(t_glob >= PAD, -hi, NEG_INF), -mid, -lo])
            body = piece if half == 0 else swapped
            fx_ref[0, kind * FOX_HEADS + hd] = jnp.where(first, body, extra).astype(BF16)

    ug = _dot(n, w_ref[:, SEG_GATE:D_IN_PAD])
    gate_ref[0] = _sigmoid(ug).astype(BF16)


def _inproj(stream, g1, w_in_p, mu, fb, fqg, fkg, *, TP, n_seq, tm=INPROJ_TILE):
    first = len(stream) == 2
    B, D = stream[0].shape[0], D_MODEL
    nb = TP // tm
    kern = functools.partial(_inproj_kernel, tm=tm, first=first, n_seq=n_seq)
    out_shape = (
        jax.ShapeDtypeStruct((B, 3 * PAIRS, TP, LANE), F32),
        jax.ShapeDtypeStruct((B, TP, LANE), F32),
        jax.ShapeDtypeStruct((B, 3 * PAIRS, TP, LANE), BF16),
        jax.ShapeDtypeStruct((B, 3 * FOX_HEADS, TP, LANE), BF16),
        jax.ShapeDtypeStruct((B, nb, 1, LANE), F32),
        jax.ShapeDtypeStruct((B, TP, 3 * D_MODEL), BF16),
    )
    in_specs = _stream_specs(first, tm, n_seq) + [
        _const_spec((1, D)),
        _const_spec((D, D_IN_PAD)),
        _const_spec((1, SHIFT_W)),
        _const_spec((1, LANE)),
        _const_spec((1, LANE)),
        _const_spec((1, LANE)),
    ]
    out_specs = (
        pl.BlockSpec((1, 3 * PAIRS, tm, LANE), lambda b, i: (b, 0, i, 0)),
        pl.BlockSpec((1, tm, LANE), lambda b, i: (b, i, 0)),
        pl.BlockSpec((1, 3 * PAIRS, tm, LANE), lambda b, i: (b, 0, i, 0)),
        pl.BlockSpec((1, 3 * FOX_HEADS, tm, LANE), lambda b, i: (b, 0, i, 0)),
        pl.BlockSpec((1, 1, 1, LANE), lambda b, i: (b, i, 0, 0)),
        pl.BlockSpec((1, tm, 3 * D_MODEL), lambda b, i: (b, i, 0)),
    )
    return pl.pallas_call(
        kern, out_shape=out_shape, grid=(B, nb), in_specs=in_specs, out_specs=out_specs,
        scratch_shapes=[pltpu.VMEM((1, SHIFT_W), F32), pltpu.VMEM((1, LANE), F32)],
        compiler_params=pltpu.CompilerParams(
            dimension_semantics=("arbitrary", "arbitrary"), vmem_limit_bytes=VMEM_LIMIT),
        name="inproj",
    )(*stream, g1, w_in_p, mu, fb, fqg, fkg)


def _bdot(a, b):
    return lax.dot_general(a, b, (((2,), (1,)), ((0,), (0,))), preferred_element_type=F32)


def _bdot_nt(a, b):
    return lax.dot_general(a, b, (((2,), (2,)), ((0,), (0,))), preferred_element_type=F32)


def _bdot_tn(a, b):
    return lax.dot_general(a, b, (((1,), (1,)), ((0,), (0,))), preferred_element_type=F32)


def _head_sum(x, first):
    s0 = jnp.sum(jnp.where(first, x, 0.0), axis=-1, keepdims=True)
    s1 = jnp.sum(jnp.where(first, 0.0, x), axis=-1, keepdims=True)
    return jnp.where(first, s0, s1)


def _rwkv_chunk_kernel(r_ref, k_ref, v_ref, wdad_ref, wup_ref, w0_ref, aup_ref, a0_ref,
                       kk_ref, ka_ref, rk_ref,
                       g_ref, hc_ref, r2_ref, y0_ref, bonus_ref, *, rows):
    nc = rows // CHUNK
    r = r_ref[0, 0]
    k = k_ref[0, 0]
    v = v_ref[0, 0]
    wd = wdad_ref[0][:, 0:DECAY_RANK]
    ad = wdad_ref[0][:, DECAY_RANK:DECAY_RANK + ICLR_RANK]
    first = lax.broadcasted_iota(jnp.int32, (rows, LANE), 1) < HEAD_DIM

    pre = w0_ref[0] + _dot3(jnp.tanh(wd), wup_ref[0])
    lw = -jnp.exp(_log_sigmoid(pre) - 0.5)
    iclr = _sigmoid(a0_ref[0] + _dot3(ad, aup_ref[0]))
    kk = k * kk_ref[0]
    kk = kk / jnp.maximum(jnp.sqrt(_head_sum(kk * kk, first)), 1e-12)
    k2 = k * (1.0 + (iclr - 1.0) * ka_ref[0])
    b = kk * iclr
    bonus_ref[0, 0] = _head_sum(r * k2 * rk_ref[0], first) * v

    to3 = lambda x: x.reshape(nc, CHUNK, LANE)
    ri = lax.broadcasted_iota(jnp.int32, (nc, CHUNK, CHUNK), 1)
    ci = lax.broadcasted_iota(jnp.int32, (nc, CHUNK, CHUNK), 2)
    low_incl = ri >= ci
    low_strict = ri > ci
    first3 = lax.broadcasted_iota(jnp.int32, (nc, CHUNK, LANE), 2) < HEAD_DIM

    lw3 = to3(lw)
    tri = low_incl.astype(BF16)
    lw_hi = lw3.astype(BF16)
    cum = _bdot(tri, lw_hi) + _bdot(tri, (lw3 - lw_hi.astype(F32)).astype(BF16))
    cum_end = cum[:, CHUNK - 1:CHUNK, :]
    e_neg = jnp.exp(-cum)
    at = to3(-kk) * jnp.exp(cum - lw3)
    rt = to3(r) * jnp.exp(cum)
    bt = (to3(b) * e_neg).astype(BF16)
    kt = (to3(k2) * e_neg).astype(BF16)
    e_rem = jnp.exp(cum_end - cum)
    bq = (to3(b) * e_rem).astype(BF16)
    kq = (to3(k2) * e_rem).astype(BF16)
    vv = to3(v).astype(BF16)

    heads = range(2)
    sels = [first3, jnp.logical_not(first3)]
    lhs = [jnp.concatenate([jnp.where(sels[hd], at, 0.0), jnp.where(sels[hd], rt, 0.0)],
                           axis=1).astype(BF16) for hd in heads]
    mb = [_bdot_nt(lhs[hd], bt) for hd in heads]
    mk = [_bdot_nt(lhs[hd], kt) for hd in heads]
    m_ak = [jnp.where(low_strict, mk[hd][:, :CHUNK], 0.0).astype(BF16) for hd in heads]
    m_rb = [jnp.where(low_incl, mb[hd][:, CHUNK:], 0.0).astype(BF16) for hd in heads]
    m_rk = [jnp.where(low_incl, mk[hd][:, CHUNK:], 0.0).astype(BF16) for hd in heads]
    p = [jnp.where(low_strict, mb[hd][:, :CHUNK], 0.0) for hd in heads]
    xs = [jnp.concatenate([at, _bdot(m_ak[hd], vv)], axis=-1) for hd in heads]
    for j in range(6):
        p16 = [p[hd].astype(BF16) for hd in heads]
        xs = [xs[hd] + _bdot(p16[hd], xs[hd].astype(BF16)) for hd in heads]
        if j < 5:
            p = [_bdot(p16[hd], p16[hd]) for hd in heads]
    ru = [_bdot(m_rb[hd], xs[hd].astype(BF16)) for hd in heads]
    r2s = [rt + ru[hd][..., :LANE] for hd in heads]
    y0s = [ru[hd][..., LANE:] + _bdot(m_rk[hd], vv) for hd in heads]

    first3w = jnp.concatenate([first3, first3], axis=-1)
    x = jnp.where(first3w, xs[0], xs[1])
    r2_ref[0, 0] = jnp.where(first3, r2s[0], r2s[1]).reshape(rows, LANE)
    y0_ref[0, 0] = jnp.where(first3, y0s[0], y0s[1]).reshape(rows, LANE)

    pg = _bdot_tn(bq, x.astype(BF16))
    ph = pg[..., LANE:] + _bdot_tn(kq, vv)
    rr = lax.broadcasted_iota(jnp.int32, (nc, LANE, LANE), 1)
    cc = lax.broadcasted_iota(jnp.int32, (nc, LANE, LANE), 2)
    same_head = (rr < HEAD_DIM) == (cc < HEAD_DIM)
    g = jnp.where(same_head, pg[..., :LANE], 0.0) + jnp.where(rr == cc, jnp.exp(cum_end), 0.0)
    g_ref[0, 0] = g.reshape(nc * LANE, LANE)
    hc_ref[0, 0] = jnp.where(same_head, ph, 0.0).reshape(nc * LANE, LANE)


def _rwkv_chunks(rkv, wdad, wup, w0, aup, a0, k_k, k_a, r_k):
    B, _, TP, _ = rkv.shape
    rows = next(r for r in (1536, 768, 512, 256) if TP % r == 0)
    nb = TP // rows
    kern = functools.partial(_rwkv_chunk_kernel, rows=rows)
    slot_spec = lambda off: pl.BlockSpec((1, 1, rows, LANE), lambda b, p, i: (b, p + off, i, 0))
    par_mat = pl.BlockSpec((1, DECAY_RANK, LANE), lambda b, p, i: (p, 0, 0))
    par_vec = pl.BlockSpec((1, 1, LANE), lambda b, p, i: (p, 0, 0))
    row_out = jax.ShapeDtypeStruct((B, PAIRS, TP, LANE), F32)
    mat_out = jax.ShapeDtypeStruct((B, PAIRS, 2 * TP, LANE), F32)
    row_spec = pl.BlockSpec((1, 1, rows, LANE), lambda b, p, i: (b, p, i, 0))
    mat_spec = pl.BlockSpec((1, 1, 2 * rows, LANE), lambda b, p, i: (b, p, i, 0))
    return pl.pallas_call(
        kern, out_shape=(mat_out, mat_out, row_out, row_out, row_out), grid=(B, PAIRS, nb),
        in_specs=[slot_spec(0), slot_spec(PAIRS), slot_spec(2 * PAIRS),
                  pl.BlockSpec((1, rows, LANE), lambda b, p, i: (b, i, 0)),
                  par_mat, par_vec, par_mat, par_vec, par_vec, par_vec, par_vec],
        out_specs=(mat_spec, mat_spec, row_spec, row_spec, row_spec),
        compiler_params=pltpu.CompilerParams(
            dimension_semantics=("arbitrary",) * 3, vmem_limit_bytes=VMEM_LIMIT),
        name="rwkv_chunks",
    )(rkv, rkv, rkv, wdad, wup, w0, aup, a0, k_k, k_a, r_k)


def _rwkv_scan_kernel(g_ref, hc_ref, r2_ref, y0_ref, bonus_ref, lng_ref, lnb_ref, y_ref,
                      state, *, rows):
    i = pl.program_id(1)

    @pl.when(i == 0)
    def _():
        state[...] = jnp.zeros_like(state)

    first = lax.broadcasted_iota(jnp.int32, (CHUNK, LANE), 1) < HEAD_DIM
    hs = [state[p] for p in range(PAIRS)]
    for c in range(rows // CHUNK):
        sl = slice(c * CHUNK, (c + 1) * CHUNK)
        sm = slice(c * LANE, (c + 1) * LANE)
        for p in range(PAIRS):
            y = _dot3(r2_ref[0, p, sl, :], hs[p]) + y0_ref[0, p, sl, :]
            hs[p] = _dot3(g_ref[0, p, sm, :], hs[p]) + hc_ref[0, p, sm, :]
            yc = y - _head_sum(y, first) * (1.0 / HEAD_DIM)
            var = _head_sum(yc * yc, first) * (1.0 / HEAD_DIM)
            out = yc * lax.rsqrt(var + RWKV_LN_EPS) * lng_ref[p] + lnb_ref[p]
            y_ref[0, p, sl, :] = (out + bonus_ref[0, p, sl, :]).astype(BF16)
    for p in range(PAIRS):
        state[p] = hs[p]


def _rwkv_scan(gm, hc, r2, y0, bonus, ln_g, ln_b):
    B, P, TP, _ = r2.shape
    rows = next(r for r in (1536, 512) if TP % r == 0)
    nb = TP // rows
    kern = functools.partial(_rwkv_scan_kernel, rows=rows)
    blk = pl.BlockSpec((1, P, rows, LANE), lambda b, i: (b, 0, i, 0))
    mat = pl.BlockSpec((1, P, 2 * rows, LANE), lambda b, i: (b, 0, i, 0))
    par = pl.BlockSpec((P, 1, LANE), lambda b, i: (0, 0, 0))
    return pl.pallas_call(
        kern, out_shape=jax.ShapeDtypeStruct((B, P, TP, LANE), BF16), grid=(B, nb),
        in_specs=[mat, mat, blk, blk, blk, par, par], out_specs=blk,
        scratch_shapes=[pltpu.VMEM((P, LANE, LANE), F32)],
        compiler_params=pltpu.CompilerParams(
            dimension_semantics=("arbitrary", "arbitrary"), vmem_limit_bytes=VMEM_LIMIT),
        name="rwkv_scan",
    )(gm, hc, r2, y0, bonus, ln_g, ln_b)


def _sb_kernel(*refs, t):
    q_refs, k_refs, v_refs = refs[:PAIRS], refs[PAIRS:2 * PAIRS], refs[2 * PAIRS:3 * PAIRS]
    o_ref, acc_ref, cs_ref = refs[3 * PAIRS:]
    qi = pl.program_id(1)
    first = lax.broadcasted_iota(jnp.int32, (t, LANE), 1) < HEAD_DIM
    heads = [(hd // 2, hd % 2) for hd in range(SB_HEADS)]
    qs = []
    for slot, half in heads:
        qp = q_refs[slot][0, 0]
        keep = first if half == 0 else jnp.logical_not(first)
        qs.append(jnp.where(keep, qp, jnp.zeros_like(qp)))
    nu = len(heads)
    qpos = qi * t + lax.broadcasted_iota(jnp.int32, (t, 1), 0)
    kloc = lax.broadcasted_iota(jnp.int32, (1, t), 1)
    upper = (lax.broadcasted_iota(jnp.int32, (t, t), 0)
             > lax.broadcasted_iota(jnp.int32, (t, t), 1)).astype(BF16)

    def sweep(blocks, accs, cs):
        units = []
        for kb, mask in blocks:
            start = pl.multiple_of(kb * t, t)
            kblks = [k_refs[s][0, 0, pl.ds(start, t), :] for s in range(PAIRS)]
            vblks = [v_refs[s][0, 0, pl.ds(start, t), :] for s in range(PAIRS)]
            m = None if mask is None else mask(start + kloc)
            units += [(u, kblks[slot], vblks[slot], m) for u, (slot, _) in enumerate(heads)]
        zs = [_dot_nt(qs[u], kblk) for u, kblk, _, _ in units]
        sps = [jnp.maximum(z, 0.0) + jnp.log2(1.0 + jnp.exp2(-jnp.abs(z))) for z in zs]
        spms = [sp if un[3] is None else jnp.where(un[3], sp, 0.0) for sp, un in zip(sps, units)]
        laters = [_dot(spm.astype(BF16), upper) for spm in spms]
        accs, cs = list(accs), list(cs)
        weights = []
        for (u, _, _, m), z, sp, spm, later in zip(units, zs, sps, spms, laters):
            a = jnp.exp2(z - sp - later - cs[u])
            weights.append(a if m is None else jnp.where(m, a, 0.0))
            cs[u] = cs[u] + jnp.sum(spm, axis=-1, keepdims=True)
        for (u, _, vblk, _), a in zip(units, weights):
            accs[u] = accs[u] + _dot(a.astype(BF16), vblk)
        return accs, cs

    def block(kb, accs, cs, mask):
        return sweep([(kb, mask)], accs, cs)

    def live(cs):
        low = cs[0]
        for c in cs[1:]:
            low = jnp.minimum(low, c)
        return (jnp.min(low) < SB_CUTOFF_LOG2).astype(jnp.int32)

    zero_acc = [jnp.zeros((t, LANE), F32) for _ in range(nu)]
    zero_c = [jnp.zeros((t, 1), F32) for _ in range(nu)]
    near = [(qi, lambda kpos: (kpos >= PAD) & (kpos < qpos))]
    for j in range(1, SB_FUSED):
        near.append((jnp.maximum(qi - j, 0), lambda kpos, j=j: (kpos >= PAD) & (qi >= j)))
    accs, cs = sweep(near, zero_acc, zero_c)

    def cond(carry):
        kb, alive = carry[0], carry[1]
        return (kb >= 1) & (alive > 0)

    def put(accs, cs):
        for u in range(nu):
            acc_ref[u] = accs[u]
            cs_ref[u] = cs[u]

    def step(kb, mask):
        accs, cs = block(kb, [acc_ref[u] for u in range(nu)], [cs_ref[u] for u in range(nu)], mask)
        put(accs, cs)
        return live(cs)

    put(accs, cs)

    def body(carry):
        kb = carry[0]
        return (kb - 1, step(kb, None))

    kb, alive = lax.while_loop(cond, body, (qi - SB_FUSED, live(cs)))

    @pl.when((kb == 0) & (alive > 0))
    def _():
        step(0, lambda kpos: kpos >= PAD)

    accs = [acc_ref[u] for u in range(nu)]
    for slot in range(PAIRS):
        lo = accs[2 * slot]
        hi = accs[2 * slot + 1] if 2 * slot + 1 < nu else 0.0
        o_ref[0, slot] = jnp.where(first, lo, hi).astype(BF16)


def _sb_attention(sb, *, t=SB_TILE):
    B, _, TP, _ = sb.shape
    kern = functools.partial(_sb_kernel, t=t)
    q_specs = [pl.BlockSpec((1, 1, t, LANE), lambda b, i, s=s: (b, s, i, 0)) for s in range(PAIRS)]
    kv_specs = [pl.BlockSpec((1, 1, TP, LANE), lambda b, i, s=s: (b, s, 0, 0),
                             pipeline_mode=pl.Buffered(1)) for s in range(PAIRS, 3 * PAIRS)]
    return pl.pallas_call(
        kern, out_shape=jax.ShapeDtypeStruct((B, PAIRS, TP, LANE), BF16),
        grid=(B, TP // t),
        in_specs=q_specs + kv_specs,
        out_specs=pl.BlockSpec((1, PAIRS, t, LANE), lambda b, i: (b, 0, i, 0)),
        scratch_shapes=[pltpu.VMEM((SB_HEADS, t, LANE), F32), pltpu.VMEM((SB_HEADS, t, 1), F32)],
        compiler_params=pltpu.CompilerParams(
            dimension_semantics=("arbitrary",) * 2, vmem_limit_bytes=VMEM_LIMIT),
        name="sb_attention",
    )(*([sb] * (3 * PAIRS)))


def _fox_kernel(shift_ref, fe_ref, q_ref, k_ref, v_ref, o_ref, lo_ref, acc_ref, *, t, online):
    b = pl.program_id(0)
    h = pl.program_id(1)
    qi = pl.program_id(2)
    q = q_ref[0, 0]
    shift = shift_ref[0]
    thr = FOX_SKIP_LOG2 + 2.0 * shift
    f_q = fe_ref[b, h, jnp.maximum(qi - 1, 0)]
    lo = lax.while_loop(lambda n: (n < qi) & (fe_ref[b, h, n] - f_q > thr), lambda n: n + 1,
                        jnp.where(qi == 0, 0, lo_ref[0]))
    lo_ref[0] = lo
    qpos = qi * t + lax.broadcasted_iota(jnp.int32, (t, 1), 0)
    kloc = lax.broadcasted_iota(jnp.int32, (1, t), 1)

    def scores(kb, masked):
        start = pl.multiple_of(kb * t, t)
        s = _dot_nt(q, k_ref[0, 0, pl.ds(start, t), :])
        if masked:
            s = jnp.where(start + kloc <= qpos, s, NEG_INF)
        return s, v_ref[0, 0, pl.ds(start, t), :]

    if online:
        def block(kb, carry, masked):
            acc, m = carry
            s, vblk = scores(kb, masked)
            m_new = jnp.maximum(m, jnp.max(s, axis=-1, keepdims=True))
            pr = jnp.exp2(s - m_new)
            return jnp.exp2(m - m_new) * acc + _dot(pr.astype(BF16), vblk), m_new

        carry = (jnp.zeros((t, LANE), F32), jnp.full((t, 1), NEG_INF, F32))
        carry = lax.fori_loop(lo, qi, lambda kb, c: block(kb, c, False), carry)
        acc, _ = block(qi, carry, True)
    else:
        def add_blocks(kbs, masked):
            parts = [scores(kb, masked) for kb in kbs]
            prs = [jnp.exp2(s - shift).astype(BF16) for s, _ in parts]
            total = _dot(prs[0], parts[0][1])
            for pr, (_, vblk) in zip(prs[1:], parts[1:]):
                total = total + _dot(pr, vblk)
            acc_ref[...] += total

        def group(i, c):
            add_blocks([lo + FOX_GROUP * i + j for j in range(FOX_GROUP)], False)
            return c

        def single(kb, c):
            add_blocks([kb], False)
            return c

        n = qi - lo
        acc_ref[...] = jnp.zeros((t, LANE), F32)
        lax.fori_loop(0, n // FOX_GROUP, group, 0)
        lax.fori_loop(lo + (n // FOX_GROUP) * FOX_GROUP, qi, single, 0)
        add_blocks([qi], True)
        acc = acc_ref[...]
    lane = lax.broadcasted_iota(jnp.int32, (t, LANE), 1)
    denom = jnp.sum(jnp.where(lane == HEAD_DIM, acc, 0.0), axis=-1, keepdims=True)
    o_ref[0, 0] = (acc / jnp.where(denom > 0.0, denom, 1.0)).astype(BF16)


def _fox_attention(shift, fe, fx, *, online, t=FOX_TILE):
    B, _, TP, _ = fx.shape
    H = FOX_HEADS
    kern = functools.partial(_fox_kernel, t=t, online=online)
    return pl.pallas_call(
        kern, out_shape=jax.ShapeDtypeStruct((B, H, TP, LANE), BF16),
        grid=(B, H, TP // t),
        in_specs=[pl.BlockSpec(memory_space=pltpu.SMEM),
                  pl.BlockSpec(memory_space=pltpu.SMEM),
                  pl.BlockSpec((1, 1, t, LANE), lambda b, h, i: (b, h, i, 0)),
                  pl.BlockSpec((1, 1, TP, LANE), lambda b, h, i: (b, h + H, 0, 0)),
                  pl.BlockSpec((1, 1, TP, LANE), lambda b, h, i: (b, h + 2 * H, 0, 0))],
        out_specs=pl.BlockSpec((1, 1, t, LANE), lambda b, h, i: (b, h, i, 0)),
        scratch_shapes=[pltpu.SMEM((1,), jnp.int32), pltpu.VMEM((t, LANE), F32)],
        compiler_params=pltpu.CompilerParams(
            dimension_semantics=("arbitrary",) * 3, vmem_limit_bytes=VMEM_LIMIT),
        name="fox_attention",
    )(shift, fe, fx, fx, fx)


def _merge_kernel(*refs, tm, n_valid, first, n_seq):
    n_stream = 2 if first else 1
    (ya_ref, yb_ref, yc_ref, gate_ref,
     wpa_ref, wpb_ref, wpc_ref, wout_ref, o_ref) = refs[n_stream:]
    i = pl.program_id(1)

    def proj(slots, w_ref):
        return _dot(jnp.concatenate(slots, axis=-1), w_ref[...])

    pa = proj([ya_ref[0, s] for s in range(PAIRS)], wpa_ref)
    pb = proj([yb_ref[0, s] for s in range(PAIRS)], wpb_ref)
    pc = proj([yc_ref[0, s] for s in range(FOX_HEADS)], wpc_ref)
    merged = gate_ref[0, :, 0:D_MODEL].astype(F32) * pa
    merged = merged + gate_ref[0, :, D_MODEL:2 * D_MODEL].astype(F32) * pb
    merged = merged + gate_ref[0, :, 2 * D_MODEL:3 * D_MODEL].astype(F32) * pc
    out = _stream_tile(i, refs[:n_stream], n_seq) + _dot(merged.astype(BF16), wout_ref[...])
    t = i * tm + lax.broadcasted_iota(jnp.int32, (tm, 1), 0)
    o_ref[0] = jnp.where((t >= PAD) & (t < PAD + n_valid), out, 0.0)


def _merge(stream, ya, yb, yc, gates, wpa, wpb, wpc, wout, *, n_valid, n_seq, tm=FRAME_TILE):
    first = len(stream) == 2
    B, _, TP, _ = ya.shape
    D = D_MODEL
    kern = functools.partial(_merge_kernel, tm=tm, n_valid=n_valid, first=first, n_seq=n_seq)
    pair_blk = lambda ns: pl.BlockSpec((1, ns, tm, LANE), lambda b, i: (b, 0, i, 0))
    return pl.pallas_call(
        kern, out_shape=jax.ShapeDtypeStruct((B, TP, D), F32), grid=(B, TP // tm),
        in_specs=_stream_specs(first, tm, n_seq) + [
                  pair_blk(PAIRS),
                  pair_blk(PAIRS), pair_blk(FOX_HEADS),
                  pl.BlockSpec((1, tm, 3 * D), lambda b, i: (b, i, 0)),
                  _const_spec((PAIRS * LANE, D)),
                  _const_spec((PAIRS * LANE, D)),
                  _const_spec((FOX_HEADS * LANE, D)),
                  _const_spec((D, D))],
        out_specs=pl.BlockSpec((1, tm, D), lambda b, i: (b, i, 0)),
        compiler_params=pltpu.CompilerParams(
            dimension_semantics=("arbitrary", "arbitrary"), vmem_limit_bytes=VMEM_LIMIT),
        name="merge",
    )(*stream, ya, yb, yc, gates, wpa, wpb, wpc, wout)


def _moe_kernel(*refs, tm, last):
    n_stream = 2 if last else 1
    g_ref, wr_ref, br_ref, wg_ref, wu_ref, wd_ref, o_ref, he_ref, ys_ref = refs[n_stream:]
    x = jnp.concatenate([r[0] for r in refs[:n_stream]], axis=0)
    ms = jnp.mean(x * x, axis=-1, keepdims=True)
    n = x * lax.rsqrt(ms + NORM_EPS) * g_ref[...]

    lg = _dot3(n, wr_ref[...]) + br_ref[...]
    lane = lax.broadcasted_iota(jnp.int32, (tm, LANE), 1)
    big = jnp.int32(LANE)
    is_group = lane < N_GROUPS
    gl = jnp.where(is_group, lg, -jnp.inf)
    gmax = jnp.max(gl, axis=-1, keepdims=True)
    g_idx = jnp.min(jnp.where(is_group & (gl == gmax), lane, big), axis=-1, keepdims=True)
    p_group = 1.0 / jnp.sum(jnp.where(is_group, jnp.exp(gl - gmax), 0.0), axis=-1, keepdims=True)
    lo = N_GROUPS + g_idx * EXPERTS_PER_GROUP
    in_grp = (lane >= lo) & (lane < lo + EXPERTS_PER_GROUP)
    el = jnp.where(in_grp, lg, -jnp.inf)
    top1 = jnp.max(el, axis=-1, keepdims=True)
    i1 = jnp.min(jnp.where(in_grp & (el == top1), lane, big), axis=-1, keepdims=True)
    rest = in_grp & (lane != i1)
    el2 = jnp.where(rest, lg, -jnp.inf)
    top2 = jnp.max(el2, axis=-1, keepdims=True)
    i2 = jnp.min(jnp.where(rest & (el2 == top2), lane, big), axis=-1, keepdims=True)
    e2 = jnp.exp(top2 - top1)
    p1 = 1.0 / (1.0 + e2)
    p2 = e2 / (1.0 + e2)
    comb = p_group * (jnp.where(lane == i1, p1, 0.0) + jnp.where(lane == i2, p2, 0.0))

    n16 = n.astype(BF16)

    onehot = (lane == g_idx).astype(BF16)
    earlier = (lax.broadcasted_iota(jnp.int32, (tm, tm), 1)
               < lax.broadcasted_iota(jnp.int32, (tm, tm), 0)).astype(BF16)
    cnt = _dot(earlier, onehot)
    rank = jnp.sum(jnp.where(lane == g_idx, cnt, 0.0), axis=-1, keepdims=True).astype(jnp.int32)
    n_pass = (jnp.max(rank) + MOE_CAP) // MOE_CAP
    comb16 = comb.astype(BF16)
    slot_lane = lax.broadcasted_iota(jnp.int32, (tm, N_GROUPS * MOE_CAP), 1)

    def one_pass(p, carry):
        r = rank - p * MOE_CAP
        slot = jnp.where((r >= 0) & (r < MOE_CAP), g_idx * MOE_CAP + r, -1)
        place = (slot_lane == slot).astype(BF16)
        xs = _dot_tn(place, n16).astype(BF16)
        cw = _dot_tn(place, comb16)
        for g in range(N_GROUPS):
            rows = slice(g * MOE_CAP, (g + 1) * MOE_CAP)
            for e in range(EXPERTS_PER_GROUP):
                ee = g * EXPERTS_PER_GROUP + e
                gate = _dot(xs[rows], wg_ref[ee])
                up = _dot(xs[rows], wu_ref[ee])
                c_e = cw[rows, N_GROUPS + ee:N_GROUPS + ee + 1]
                he_ref[:, e * D_EXPERT:(e + 1) * D_EXPERT] = (
                    gate * _sigmoid(gate) * up * c_e).astype(BF16)
            gw = EXPERTS_PER_GROUP * D_EXPERT
            ys_ref[rows, :] = _dot(he_ref[...], wd_ref[g * gw:(g + 1) * gw, :]).astype(BF16)
        o_ref[0] = o_ref[0] + _dot(place, ys_ref[...])
        return carry

    o_ref[0] = x
    lax.fori_loop(0, n_pass, one_pass, 0)


def _moe(h, g2, wr, br, wg, wu, wd, *, n_seq, last):
    B, TP, D = h.shape
    tm = MOE_TILE
    kern = functools.partial(_moe_kernel, tm=tm, last=last)
    EW = N_EXPERTS * D_EXPERT
    if last:
        half = tm // FRAME_TILE
        assert half == 2 and n_seq % half == 0
        rows_out = n_seq * FRAME_TILE
        stream_specs = [pl.BlockSpec((1, FRAME_TILE, D), lambda b, i, j=j: (b, half * i + 1 + j, 0))
                        for j in range(half)]
        stream = [h] * half
    else:
        rows_out = TP
        stream_specs = [pl.BlockSpec((1, tm, D), lambda b, i: (b, i, 0))]
        stream = [h]
    return pl.pallas_call(
        kern, out_shape=jax.ShapeDtypeStruct((B, rows_out, D), F32), grid=(B, rows_out // tm),
        in_specs=stream_specs + [
                  _const_spec((1, D)), _const_spec((D, LANE)), _const_spec((1, LANE)),
                  _const_spec((N_EXPERTS, D, D_EXPERT)), _const_spec((N_EXPERTS, D, D_EXPERT)),
                  _const_spec((EW, D))],
        out_specs=pl.BlockSpec((1, tm, D), lambda b, i: (b, i, 0)),
        scratch_shapes=[pltpu.VMEM((MOE_CAP, EXPERTS_PER_GROUP * D_EXPERT), BF16),
                        pltpu.VMEM((N_GROUPS * MOE_CAP, D), BF16)],
        compiler_params=pltpu.CompilerParams(
            dimension_semantics=("arbitrary", "arbitrary"), vmem_limit_bytes=VMEM_LIMIT),
        name="moe",
    )(*stream, g2, wr, br, wg, wu, wd)


def _transpose_kernel(x_ref, o_ref):
    o_ref[...] = x_ref[...].T.astype(BF16)


def _transpose_to_bf16(a, *, tn=LANE):
    N, K = a.shape
    assert N % tn == 0
    return pl.pallas_call(
        _transpose_kernel, out_shape=jax.ShapeDtypeStruct((K, N), BF16), grid=(N // tn,),
        in_specs=[pl.BlockSpec((tn, K), lambda i: (i, 0))],
        out_specs=pl.BlockSpec((K, tn), lambda i: (0, i)),
        compiler_params=pltpu.CompilerParams(dimension_semantics=("arbitrary",)),
        name="transpose_cast",
    )(a)


def _pack_w_in(w_in):
    wt = jnp.swapaxes(w_in, 0, 1)
    rest = wt[SHIFT_W:]
    rows = [wt[:SHIFT_W]]
    off = 0
    for heads in (SB_HEADS,) * 3 + (FOX_HEADS,) * 3:
        rows.append(jnp.pad(rest[off:off + heads * HEAD_DIM],
                            ((0, PAIRS * LANE - heads * HEAD_DIM), (0, 0))))
        off += heads * HEAD_DIM
    rows.append(jnp.pad(rest[off:off + FOX_HEADS], ((0, LANE - FOX_HEADS), (0, 0))))
    rows.append(rest[off + FOX_HEADS:])
    return _transpose_to_bf16(jnp.concatenate(rows, axis=0))


def _per_pair_cols(w):
    return w.reshape(w.shape[0], PAIRS, LANE).transpose(1, 0, 2)


def kernel(x, meta_tokens, norm1_g, w_in, rwkv_mu, rwkv_w_up, rwkv_w0, rwkv_a_up, rwkv_a0,
           rwkv_k_k, rwkv_k_a, rwkv_r_k, rwkv_ln_g, rwkv_ln_b, fox_f_b, fox_q_g, fox_k_g,
           w_p_rwkv, w_p_sb, w_p_fox, w_out, norm2_g, moe_wg, moe_bg, moe_we, moe_be,
           moe_w_gate, moe_w_up, moe_w_down):
    B, S, D = x.shape
    depth = w_in.shape[0]
    L = N_META + S
    TP = -(-(PAD + L) // ROW_TILE) * ROW_TILE
    assert S % FRAME_TILE == 0
    n_seq = S // FRAME_TILE
    meta = jnp.broadcast_to(meta_tokens[None].astype(x.dtype), (B, N_META, D))
    stream = (x, jnp.concatenate([jnp.zeros((B, PAD, D), x.dtype), meta], axis=1))
    H = RWKV_HEADS
    EW = N_EXPERTS * D_EXPERT
    for l in range(depth):
        fb = jnp.zeros((1, LANE), F32).at[0, :FOX_HEADS].set(fox_f_b[l])
        rkv, wdad, sb, fx, fend, gates = _inproj(
            stream, norm1_g[l][None], _pack_w_in(w_in[l]), rwkv_mu[l][None], fb,
            jnp.tile(fox_q_g[l], 2)[None], jnp.tile(fox_k_g[l], 2)[None], TP=TP, n_seq=n_seq)
        hv = lambda p: p.reshape(PAIRS, 1, LANE)
        gm, hc, r2, y0, bonus = _rwkv_chunks(
            rkv, wdad, _per_pair_cols(rwkv_w_up[l]), hv(rwkv_w0[l]),
            _per_pair_cols(rwkv_a_up[l]), hv(rwkv_a0[l]),
            hv(rwkv_k_k[l]), hv(rwkv_k_a[l]), hv(rwkv_r_k[l]))
        ya = _rwkv_scan(gm, hc, r2, y0, bonus, hv(rwkv_ln_g[l]), hv(rwkv_ln_b[l]))
        yb = _sb_attention(sb)
        shift = (HEAD_DIM * ATT_SCALE * LOG2E
                 * jnp.max(jnp.abs(fox_q_g[l])) * jnp.max(jnp.abs(fox_k_g[l]))).reshape(1)
        step = FOX_TILE // INPROJ_TILE
        fe = jnp.transpose(fend[:, step - 1::step, 0, :FOX_HEADS], (0, 2, 1))
        yc = lax.cond(shift[0] <= FOX_FIXED_SHIFT_MAX,
                      functools.partial(_fox_attention, online=False),
                      functools.partial(_fox_attention, online=True), shift, fe, fx)
        pc = jnp.pad(w_p_fox[l].reshape(FOX_HEADS, HEAD_DIM, D),
                     ((0, 0), (0, LANE - HEAD_DIM), (0, 0))).reshape(FOX_HEADS * LANE, D).astype(BF16)
        pp = lambda w, nh: jnp.pad(w, ((0, PAIRS * LANE - nh * HEAD_DIM), (0, 0))).astype(BF16)
        h = _merge(stream, ya, yb, yc, gates, pp(w_p_rwkv[l], H), pp(w_p_sb[l], SB_HEADS),
                   pc, w_out[l].astype(BF16), n_valid=L, n_seq=n_seq)
        wr = jnp.zeros((D, LANE), F32).at[:, :N_GROUPS].set(moe_wg[l])
        wr = wr.at[:, N_GROUPS:N_GROUPS + N_EXPERTS].set(moe_we[l])
        br = jnp.zeros((1, LANE), F32).at[0, :N_GROUPS].set(moe_bg[l])
        br = br.at[0, N_GROUPS:N_GROUPS + N_EXPERTS].set(moe_be[l])
        wg = moe_w_gate[l].astype(BF16)
        wu = moe_w_up[l].astype(BF16)
        wd = moe_w_down[l].reshape(EW, D).astype(BF16)
        h = _moe(h, norm2_g[l][None], wr, br, wg, wu, wd, n_seq=n_seq, last=l == depth - 1)
        stream = (h,)
    return h
```
